```python
import jax, jax.numpy as jnp
from jax import lax
import numpy as np

D_MODEL = 1024
BATCH = 8
SEQ = 4096
DEPTH = 1
DEC_BATCH = 128
DEC_SEQ = 1
PAST_LEN = 8192
PAGE_SIZE = 128

HEAD_DIM = D_MODEL // 8
HEADS_PER_GROUP = 4
ATTN_PATTERNS = ((128, 1), (512, 4), (2048, 16))
N_ATTN_GROUPS = len(ATTN_PATTERNS)
QKV_WIDTH = N_ATTN_GROUPS * HEADS_PER_GROUP * HEAD_DIM
ATTN_WIDTH = HEADS_PER_GROUP * HEAD_DIM
POOL_WINDOWS = (2, 4, 8, 16)
N_POOL_GROUPS = len(POOL_WINDOWS)
POOL_GROUP_DIM = D_MODEL // 8
POOL_WIDTH = N_POOL_GROUPS * POOL_GROUP_DIM
POOL_STATE = max(POOL_WINDOWS) - 1
Q_BLOCK = 128
IN_WIDTH = 3 * QKV_WIDTH + ATTN_WIDTH + 2 * POOL_WIDTH + 2 * D_MODEL
EPS = 1e-6

kernel_name = 'dilated_attn_pool_gated_hybrid_step'


def rms_norm(x, gain):
    xf = x.astype(jnp.float32)
    return xf * lax.rsqrt(jnp.mean(xf * xf, axis=-1, keepdims=True) + EPS) * gain.astype(jnp.float32)


def mixer_inputs(x, c, norm_g, w_ada, b_ada, w_in, q_gain, k_gain):
    B, L, _ = x.shape
    shift, scale, gate = jnp.split(jax.nn.silu(c) @ w_ada + b_ada, 3, axis=-1)
    h = (rms_norm(x, norm_g) * (1.0 + scale[:, None].astype(jnp.float32))
         + shift[:, None].astype(jnp.float32)).astype(x.dtype)
    proj = h @ w_in
    sizes = (QKV_WIDTH, QKV_WIDTH, QKV_WIDTH, ATTN_WIDTH, POOL_WIDTH, POOL_WIDTH, D_MODEL, D_MODEL)
    cuts = [sum(sizes[:i + 1]) for i in range(len(sizes) - 1)]
    q, k, v, z_a, u_b, z_b, g_a, g_b = jnp.split(proj, cuts, axis=-1)
    heads = (B, L, N_ATTN_GROUPS, HEADS_PER_GROUP, HEAD_DIM)
    q = rms_norm(q.reshape(heads), q_gain[:, None, :]) * (HEAD_DIM ** -0.5)
    k = rms_norm(k.reshape(heads), k_gain[:, None, :])
    v = v.reshape(heads).astype(jnp.float32)
    return q, k, v, z_a, u_b, z_b, g_a, g_b, gate


def dilated_attn_prompt(q, k, v, window, dilation):
    B, L, H, E = q.shape
    n_back = window // dilation
    span = dilation * Q_BLOCK
    Lp = -(-L // span) * span
    M = Lp // dilation
    nb = M // Q_BLOCK

    def to_blocks(t):
        t = jnp.pad(t, ((0, 0), (0, Lp - L), (0, 0), (0, 0)))
        t = t.reshape(B, M, dilation, H, E).transpose(0, 2, 1, 3, 4)
        return t.reshape(B, dilation, nb, Q_BLOCK, H, E)

    def with_prev(t):
        prev = jnp.pad(t, ((0, 0), (0, 0), (1, 0), (0, 0), (0, 0), (0, 0)))[:, :, :nb]
        return jnp.concatenate([prev, t], axis=3)

    qb = to_blocks(q)
    kc = with_prev(to_blocks(k))
    vc = with_prev(to_blocks(v))
    s = jnp.einsum('brnqhe,brnkhe->brnhqk', qb, kc)
    i = jnp.arange(Q_BLOCK)[:, None]
    j = jnp.arange(2 * Q_BLOCK)[None, :]
    dist = Q_BLOCK + i - j
    blk = jnp.arange(nb)[:, None, None]
    valid = (dist >= 0) & (dist <= n_back) & ((blk > 0) | (j >= Q_BLOCK))
    s = jnp.where(valid[:, None], s, -jnp.inf)
    lse = jax.nn.logsumexp(s, axis=-1)
    p = jnp.exp(s - lse[..., None])
    o = jnp.einsum('brnhqk,brnkhe->brnqhe', p, vc)
    o = o.reshape(B, dilation, M, H, E).transpose(0, 2, 1, 3, 4).reshape(B, Lp, H, E)[:, :L]
    lse = lse.transpose(0, 1, 2, 4, 3).reshape(B, dilation, M, H).transpose(0, 2, 1, 3).reshape(B, Lp, H)[:, :L]
    return o, lse


def dilated_attn_sample(q, k_new, v_new, kv_buf, window, dilation):
    B, S, H, E = q.shape
    Lb = kv_buf.shape[1]
    n_back = window // dilation
    k_all = jnp.concatenate([kv_buf[:, :, 0].astype(jnp.float32), k_new], axis=1)
    v_all = jnp.concatenate([kv_buf[:, :, 1].astype(jnp.float32), v_new], axis=1)
    idx = Lb + jnp.arange(S)[:, None] - dilation * jnp.arange(n_back + 1)[None, :]
    valid = idx >= 0
    idx = jnp.maximum(idx, 0)
    kg = k_all[:, idx]
    vg = v_all[:, idx]
    s = jnp.einsum('bshe,bsjhe->bshj', q, kg)
    s = jnp.where(valid[:, None, :], s, -jnp.inf)
    lse = jax.nn.logsumexp(s, axis=-1)
    p = jnp.exp(s - lse[..., None])
    o = jnp.einsum('bshj,bsjhe->bshe', p, vg)
    keep = min(window, Lb + S)
    new_buf = jnp.stack([k_all, v_all], axis=2)[:, Lb + S - keep:].astype(kv_buf.dtype)
    return o, lse, new_buf


def combine_groups(outs, lses):
    o = jnp.stack(outs, axis=2)
    w = jax.nn.softmax(jnp.stack(lses, axis=2), axis=2)
    a = jnp.einsum('blgh,blghe->blhe', w, o)
    return a.reshape(a.shape[0], a.shape[1], ATTN_WIDTH)


def pool_mix(u_full, n_new, pos0, w_pool, b_pool, pool_scale):
    B, N, _ = u_full.shape
    u = u_full.astype(jnp.float32).reshape(B, N, N_POOL_GROUPS, POOL_GROUP_DIM)
    cs = jnp.cumsum(u, axis=1)
    pos = pos0 + jnp.arange(n_new)
    means = []
    for g, w in enumerate(POOL_WINDOWS):
        csg = cs[:, :, g]
        lag = jnp.pad(csg, ((0, 0), (w, 0), (0, 0)))[:, :N]
        wsum = (csg - lag)[:, N - n_new:]
        cnt = jnp.minimum(w, pos + 1).astype(jnp.float32)
        means.append(wsum / cnt[None, :, None])
    mixed = jnp.stack(means, axis=2) - u[:, N - n_new:]
    y = jnp.einsum('blgc,gcd->blgd', mixed, w_pool.astype(jnp.float32)) + b_pool.astype(jnp.float32)
    return y.reshape(B, n_new, POOL_WIDTH) * pool_scale.astype(jnp.float32)


def mixer_output(x, a, p, z_a, z_b, g_a, g_b, gate, w_a_out, w_b_out, w_out):
    ya = (a.astype(x.dtype) * jax.nn.silu(z_a)) @ w_a_out
    yb = (p.astype(x.dtype) * jax.nn.silu(z_b)) @ w_b_out
    m = jax.nn.sigmoid(g_a) * ya + jax.nn.sigmoid(g_b) * yb
    return x + gate[:, None] * (m @ w_out)


def layer_prompt(x, c, norm_g, w_ada, b_ada, w_in, q_gain, k_gain, w_pool, b_pool, pool_scale,
                 w_a_out, w_b_out, w_out):
    L = x.shape[1]
    q, k, v, z_a, u_b, z_b, g_a, g_b, gate = mixer_inputs(x, c, norm_g, w_ada, b_ada, w_in, q_gain, k_gain)
    outs, lses, kv_new = [], [], []
    for g, (window, dilation) in enumerate(ATTN_PATTERNS):
        o, lse = dilated_attn_prompt(q[:, :, g], k[:, :, g], v[:, :, g], window, dilation)
        outs.append(o)
        lses.append(lse)
        kv_new.append(jnp.stack([k[:, :, g], v[:, :, g]], axis=2)[:, L - min(window, L):].astype(x.dtype))
    a = combine_groups(outs, lses)
    p = pool_mix(u_b, L, 0, w_pool, b_pool, pool_scale)
    pool_new = u_b[:, max(L - POOL_STATE, 0):]
    y = mixer_output(x, a, p, z_a, z_b, g_a, g_b, gate, w_a_out, w_b_out, w_out)
    return y, kv_new[0], kv_new[1], kv_new[2], pool_new


def layer_sample(x, c, kv_w128, kv_w512, kv_w2048, pool_buf, norm_g, w_ada, b_ada, w_in, q_gain, k_gain,
                 w_pool, b_pool, pool_scale, w_a_out, w_b_out, w_out):
    S = x.shape[1]
    q, k, v, z_a, u_b, z_b, g_a, g_b, gate = mixer_inputs(x, c, norm_g, w_ada, b_ada, w_in, q_gain, k_gain)
    outs, lses, kv_new = [], [], []
    for g, ((window, dilation), buf) in enumerate(zip(ATTN_PATTERNS, (kv_w128, kv_w512, kv_w2048))):
        o, lse, nbuf = dilated_attn_sample(q[:, :, g], k[:, :, g], v[:, :, g], buf, window, dilation)
        outs.append(o)
        lses.append(lse)
        kv_new.append(nbuf)
    a = combine_groups(outs, lses)
    u_full = jnp.concatenate([pool_buf.astype(u_b.dtype), u_b], axis=1)
    p = pool_mix(u_full, S, PAST_LEN, w_pool, b_pool, pool_scale)
    pool_new = u_full[:, u_full.shape[1] - POOL_STATE:].astype(pool_buf.dtype)
    y = mixer_output(x, a, p, z_a, z_b, g_a, g_b, gate, w_a_out, w_b_out, w_out)
    return y, kv_new[0], kv_new[1], kv_new[2], pool_new


def setup_inputs(seed: int = 0) -> dict:
    key = jax.random.key(seed)
    ks = jax.random.split(key, 24)
    f32 = jnp.float32
    nrm = lambda k, shape, s: jax.random.normal(k, shape, f32) * s
    kv_shape = lambda w: (DEPTH, DEC_BATCH, min(w, PAST_LEN), 2, HEADS_PER_GROUP, HEAD_DIM)
    return {
        'x_prompt': nrm(ks[0], (BATCH, SEQ, D_MODEL), 1.0),
        'x_sample': nrm(ks[1], (DEC_BATCH, DEC_SEQ, D_MODEL), 1.0),
        'state_kv_w128': nrm(ks[2], kv_shape(ATTN_PATTERNS[0][0]), 1.0),
        'state_kv_w512': nrm(ks[3], kv_shape(ATTN_PATTERNS[1][0]), 1.0),
        'state_kv_w2048': nrm(ks[4], kv_shape(ATTN_PATTERNS[2][0]), 1.0),
        'state_pool': nrm(ks[5], (DEPTH, DEC_BATCH, POOL_STATE, POOL_WIDTH), 1.0),
        'c_prompt': nrm(ks[6], (BATCH, D_MODEL), 1.0),
        'c_sample': nrm(ks[7], (DEC_BATCH, D_MODEL), 1.0),
        'norm_g': 1.0 + nrm(ks[8], (DEPTH, D_MODEL), 0.02),
        'w_ada': nrm(ks[9], (DEPTH, D_MODEL, 3 * D_MODEL), D_MODEL ** -0.5),
        'b_ada': nrm(ks[10], (DEPTH, 3 * D_MODEL), 0.02),
        'w_in': nrm(ks[11], (DEPTH, D_MODEL, IN_WIDTH), D_MODEL ** -0.5),
        'q_gain': 1.0 + nrm(ks[12], (DEPTH, N_ATTN_GROUPS, HEAD_DIM), 0.02),
        'k_gain': 1.0 + nrm(ks[13], (DEPTH, N_ATTN_GROUPS, HEAD_DIM), 0.02),
        'w_pool': nrm(ks[14], (DEPTH, N_POOL_GROUPS, POOL_GROUP_DIM, POOL_GROUP_DIM), POOL_GROUP_DIM ** -0.5),
        'b_pool': nrm(ks[15], (DEPTH, N_POOL_GROUPS, POOL_GROUP_DIM), 0.02),
        'pool_scale': 1.0 + nrm(ks[16], (DEPTH, POOL_WIDTH), 0.02),
        'w_a_out': nrm(ks[17], (DEPTH, ATTN_WIDTH, D_MODEL), ATTN_WIDTH ** -0.5),
        'w_b_out': nrm(ks[18], (DEPTH, POOL_WIDTH, D_MODEL), POOL_WIDTH ** -0.5),
        'w_out': nrm(ks[19], (DEPTH, D_MODEL, D_MODEL), D_MODEL ** -0.5),
    }


def reference(x_prompt, x_sample, state_kv_w128, state_kv_w512, state_kv_w2048, state_pool,
              c_prompt, c_sample, norm_g, w_ada, b_ada, w_in, q_gain, k_gain, w_pool, b_pool,
              pool_scale, w_a_out, w_b_out, w_out):
    y_p, y_s = x_prompt, x_sample
    new_p = [[], [], [], []]
    new_s = [[], [], [], []]
    for l in range(DEPTH):
        y_p, *st_p = layer_prompt(y_p, c_prompt, norm_g[l], w_ada[l], b_ada[l], w_in[l], q_gain[l], k_gain[l],
                                  w_pool[l], b_pool[l], pool_scale[l], w_a_out[l], w_b_out[l], w_out[l])
        y_s, *st_s = layer_sample(y_s, c_sample, state_kv_w128[l], state_kv_w512[l], state_kv_w2048[l],
                                  state_pool[l], norm_g[l], w_ada[l], b_ada[l], w_in[l], q_gain[l], k_gain[l],
                                  w_pool[l], b_pool[l], pool_scale[l], w_a_out[l], w_b_out[l], w_out[l])
        for lst, s_ in zip(new_p, st_p):
            lst.append(s_)
        for lst, s_ in zip(new_s, st_s):
            lst.append(s_)
    kv128_p = jnp.stack(new_p[0])
    kv512_p = jnp.stack(new_p[1])
    kv2048_p = jnp.stack(new_p[2])
    pool_p = jnp.stack(new_p[3])
    kv128_s = jnp.stack(new_s[0])
    kv512_s = jnp.stack(new_s[1])
    kv2048_s = jnp.stack(new_s[2])
    pool_s = jnp.stack(new_s[3])
    return (y_p, y_s, kv128_p, kv512_p, kv2048_p, pool_p, kv128_s, kv512_s, kv2048_s, pool_s)
```

```python
import functools

import jax
import jax.numpy as jnp
from jax import lax
from jax.experimental import pallas as pl
from jax.experimental.pallas import tpu as pltpu

F32 = jnp.float32
BF16 = jnp.bfloat16

D_MODEL = 1024
HEAD_DIM = 128
N_HEADS = 4
GROUP_W = N_HEADS * HEAD_DIM
ATTN_PATTERNS = ((128, 1), (512, 4), (2048, 16))
N_GROUPS = len(ATTN_PATTERNS)
N_BACK = 128
QKV_W = 3 * N_GROUPS * GROUP_W
REST_W = 3 * GROUP_W + 2 * D_MODEL
POOL_WINDOWS = (2, 4, 8, 16)
POOL_STATE = 15
PAST_LEN = 8192
EPS = 1e-6
Q_SCALE = HEAD_DIM ** -0.5
NEG = -1e30

ROW_TILE = 512
Q_BLOCK = 128
VMEM_LIMIT = 56 * 1024 * 1024


def _silu(v):
    return v * jax.nn.sigmoid(v)


def _modulated_norm(x, norm_g, shift, scale):
    ms = jnp.mean(x * x, axis=-1, keepdims=True)
    return (x * lax.rsqrt(ms + EPS) * norm_g * (1.0 + scale) + shift).astype(BF16)


def _head_norm(r, gain):
    ms = jnp.mean(r * r, axis=-1, keepdims=True)
    return r * lax.rsqrt(ms + EPS) * gain


def _ada_body(c_ref, w_ref, b_ref, o_ref):
    s = _silu(c_ref[...]).astype(BF16)
    o_ref[...] = jnp.dot(s, w_ref[...].astype(BF16), preferred_element_type=F32) + b_ref[...]


def _ada(c_all, w_ada, b_ada):
    n = c_all.shape[0]
    return pl.pallas_call(
        _ada_body,
        grid=(3,),
        in_specs=[pl.BlockSpec((n, D_MODEL), lambda j: (0, 0)),
                  pl.BlockSpec((D_MODEL, D_MODEL), lambda j: (0, j)),
                  pl.BlockSpec((1, D_MODEL), lambda j: (0, j))],
        out_specs=pl.BlockSpec((n, D_MODEL), lambda j: (0, j)),
        out_shape=jax.ShapeDtypeStruct((n, 3 * D_MODEL), F32),
        name="ada",
    )(c_all, w_ada, b_ada)


def _qkv_body(x_ref, mod_ref, ng_ref, w_ref, qg_ref, kg_ref, *refs, n_tiles):
    qkv_refs = (refs[0:3], refs[3:6], refs[6:9])
    tail_refs = refs[9:12]
    scr = refs[12]
    i = pl.program_id(1)
    tm = x_ref.shape[1]
    h = _modulated_norm(x_ref[0], ng_ref[...], mod_ref[0, :, 0:D_MODEL], mod_ref[0, :, D_MODEL:2 * D_MODEL])
    for t in range(3):
        for g, (window, d) in enumerate(ATTN_PATTERNS):
            c = t * N_GROUPS + g
            res = jnp.dot(h, w_ref[:, c * GROUP_W:(c + 1) * GROUP_W], preferred_element_type=F32)
            out_ref = qkv_refs[t][g]
            tail = tail_refs[g]
            for hh in range(N_HEADS):
                r = res[:, hh * HEAD_DIM:(hh + 1) * HEAD_DIM]
                if t == 0:
                    r = _head_norm(r, qg_ref[g:g + 1, :]) * Q_SCALE
                elif t == 1:
                    r = _head_norm(r, kg_ref[g:g + 1, :])
                if d == 1:
                    out_ref[0, 0, hh] = r.astype(BF16)
                else:
                    slot = (c * N_HEADS + hh) % 2
                    scr[slot] = r
                    for rr in range(d):
                        out_ref[0, rr, hh] = scr[slot, pl.ds(rr, tm // d, stride=d), :].astype(BF16)
                if t > 0:
                    row = (t - 1) * N_HEADS + hh
                    if window >= tm:
                        n_tail = window // tm

                        @pl.when(i >= n_tiles - n_tail)
                        def _():
                            tail[0, :, row, :] = r
                    else:
                        @pl.when(i == n_tiles - 1)
                        def _():
                            tail[0, :, row, :] = r[tm - window:, :]


def _prompt_qkv(x, mod_p, norm_g, w_qkv, q_gain, k_gain):
    B, L, _ = x.shape
    tm = ROW_TILE
    n_tiles = L // tm
    const2 = lambda b, i: (0, 0)
    in_specs = [
        pl.BlockSpec((1, tm, D_MODEL), lambda b, i: (b, i, 0)),
        pl.BlockSpec((1, 1, 3 * D_MODEL), lambda b, i: (b, 0, 0)),
        pl.BlockSpec((1, D_MODEL), const2),
        pl.BlockSpec((D_MODEL, QKV_W), const2, pipeline_mode=pl.Buffered(1)),
        pl.BlockSpec((N_GROUPS, HEAD_DIM), const2),
        pl.BlockSpec((N_GROUPS, HEAD_DIM), const2),
    ]
    out_specs, out_shape = [], []
    for t in range(3):
        for window, d in ATTN_PATTERNS:
            out_specs.append(pl.BlockSpec((1, d, N_HEADS, tm // d, HEAD_DIM), lambda b, i: (b, 0, 0, i, 0)))
            out_shape.append(jax.ShapeDtypeStruct((B, d, N_HEADS, L // d, HEAD_DIM), BF16))
    for window, d in ATTN_PATTERNS:
        w_eff = min(window, L)
        if w_eff >= tm:
            n_tail = w_eff // tm
            out_specs.append(pl.BlockSpec(
                (1, tm, 2 * N_HEADS, HEAD_DIM),
                lambda b, i, n_tail=n_tail: (b, jnp.maximum(i - (n_tiles - n_tail), 0), 0, 0)))
        else:
            out_specs.append(pl.BlockSpec((1, w_eff, 2 * N_HEADS, HEAD_DIM), lambda b, i: (b, 0, 0, 0)))
        out_shape.append(jax.ShapeDtypeStruct((B, w_eff, 2 * N_HEADS, HEAD_DIM), F32))
    return pl.pallas_call(
        functools.partial(_qkv_body, n_tiles=n_tiles),
        grid=(B, n_tiles),
        in_specs=in_specs,
        out_specs=out_specs,
        out_shape=out_shape,
        scratch_shapes=[pltpu.VMEM((2, tm, HEAD_DIM), F32)],
        compiler_params=pltpu.CompilerParams(
            dimension_semantics=("arbitrary", "arbitrary"), vmem_limit_bytes=VMEM_LIMIT),
        name="prompt_qkv",
    )(x, mod_p, norm_g, w_qkv, q_gain, k_gain)


def _attn_body(q_ref, k_ref, v_ref, o_ref, l_ref):
    _, d_blk, h_blk, M, _ = q_ref.shape
    n_q = M // Q_BLOCK
    ii = lax.broadcasted_iota(jnp.int32, (Q_BLOCK, Q_BLOCK), 0)
    jj = lax.broadcasted_iota(jnp.int32, (Q_BLOCK, Q_BLOCK), 1)
    own_ok = jj <= ii
    prev_ok = jj >= ii
    nt = (((1,), (1,)), ((), ()))

    def block(c, hh, n, first):
        rows = pl.ds(pl.multiple_of(n * Q_BLOCK, Q_BLOCK), Q_BLOCK)
        q = q_ref[0, c, hh, rows, :]
        s_own = jnp.where(own_ok, lax.dot_general(q, k_ref[0, c, hh, rows, :], nt, preferred_element_type=F32), NEG)
        m = jnp.max(s_own, axis=-1, keepdims=True)
        if not first:
            prev = pl.ds(pl.multiple_of((n - 1) * Q_BLOCK, Q_BLOCK), Q_BLOCK)
            s_prev = jnp.where(prev_ok, lax.dot_general(q, k_ref[0, c, hh, prev, :], nt, preferred_element_type=F32), NEG)
            m = jnp.maximum(m, jnp.max(s_prev, axis=-1, keepdims=True))
        p_own = jnp.exp(s_own - m)
        l = jnp.sum(p_own, axis=-1, keepdims=True)
        o = jnp.dot(p_own.astype(BF16), v_ref[0, c, hh, rows, :], preferred_element_type=F32)
        if not first:
            p_prev = jnp.exp(s_prev - m)
            l = l + jnp.sum(p_prev, axis=-1, keepdims=True)
            o = o + jnp.dot(p_prev.astype(BF16), v_ref[0, c, hh, prev, :], preferred_element_type=F32)
        o_ref[0, c, hh, rows, :] = o / l
        l_ref[0, c, hh, rows, :] = jnp.broadcast_to(m + jnp.log(l), (Q_BLOCK, HEAD_DIM))

    for c in range(d_blk):
        for hh in range(h_blk):
            block(c, hh, 0, True)

            def step(n, carry, c=c, hh=hh):
                block(c, hh, n, False)
                return carry

            lax.fori_loop(1, n_q, step, 0)


def _prompt_attn(q, k, v, d_blk, h_blk):
    B, d, H, M, E = q.shape
    blk = (1, d_blk, h_blk, M, E)
    spec = pl.BlockSpec(blk, lambda b, c, hh: (b, c, hh, 0, 0))
    shape = jax.ShapeDtypeStruct(q.shape, F32)
    return pl.pallas_call(
        _attn_body,
        grid=(B, d // d_blk, H // h_blk),
        in_specs=[spec, spec, spec],
        out_specs=[spec, spec],
        out_shape=[shape, shape],
        compiler_params=pltpu.CompilerParams(
            dimension_semantics=("arbitrary", "arbitrary", "arbitrary"), vmem_limit_bytes=VMEM_LIMIT),
        name="prompt_attn_d%d" % d,
    )(q, k, v)


def _pool_project(mixed, wp_ref, bp_ref, ps_ref):
    cols = []
    for g in range(len(POOL_WINDOWS)):
        lanes = slice(g * HEAD_DIM, (g + 1) * HEAD_DIM)
        y = jnp.dot(mixed[g].astype(BF16), wp_ref[g], preferred_element_type=F32) + bp_ref[g:g + 1, :]
        cols.append(y * ps_ref[:, lanes])
    return jnp.concatenate(cols, axis=-1)


def _merge_out(x, gate, a, p, z_a, z_b, g_a, g_b, wa_ref, wb_ref, wo_ref):
    ya = jnp.dot((a * _silu(z_a)).astype(BF16), wa_ref[...], preferred_element_type=F32)
    yb = jnp.dot((p * _silu(z_b)).astype(BF16), wb_ref[...], preferred_element_type=F32)
    m = jax.nn.sigmoid(g_a) * ya + jax.nn.sigmoid(g_b) * yb
    return x + gate * jnp.dot(m.astype(BF16), wo_ref[...], preferred_element_type=F32)


def _out_body(x_ref, mod_ref, ng_ref, w_ref, o0, l0, o1, l1, o2, l2, wp_ref, bp_ref, ps_ref,
              wa_ref, wb_ref, wo_ref, y_ref, pool_ref, ext, oscr, lscr, *, n_tiles):
    i = pl.program_id(1)
    tm = x_ref.shape[1]
    x = x_ref[0]
    h = _modulated_norm(x, ng_ref[...], mod_ref[0, :, 0:D_MODEL], mod_ref[0, :, D_MODEL:2 * D_MODEL])
    gate = mod_ref[0, :, 2 * D_MODEL:3 * D_MODEL]

    def proj(lo, width):
        return jnp.dot(h, w_ref[:, lo:lo + width], preferred_element_type=F32)

    o_refs, l_refs = (o0, o1, o2), (l0, l1, l2)
    for g, (window, d) in enumerate(ATTN_PATTERNS):
        if d == 1:
            continue
        for hh in range(N_HEADS):
            for rr in range(d):
                rows = pl.ds(rr, tm // d, stride=d)
                oscr[g - 1, hh, rows, :] = o_refs[g][0, rr, hh]
                lscr[g - 1, hh, rows, :] = l_refs[g][0, rr, hh]
    a_cols = []
    for hh in range(N_HEADS):
        os_ = [o0[0, 0, hh], oscr[0, hh], oscr[1, hh]]
        ls_ = [l0[0, 0, hh], lscr[0, hh], lscr[1, hh]]
        mx = jnp.maximum(jnp.maximum(ls_[0], ls_[1]), ls_[2])
        es = [jnp.exp(l_ - mx) for l_ in ls_]
        den = es[0] + es[1] + es[2]
        a_cols.append((es[0] * os_[0] + es[1] * os_[1] + es[2] * os_[2]) / den)
    a = jnp.concatenate(a_cols, axis=-1)

    u = proj(GROUP_W, GROUP_W)

    @pl.when(i == 0)
    def _():
        ext[0:16, :] = jnp.zeros((16, GROUP_W), F32)

    ext[16:16 + tm, :] = u
    pos = i * tm + lax.broadcasted_iota(jnp.int32, (tm, 1), 0)
    mixed = []
    for g, w in enumerate(POOL_WINDOWS):
        lanes = pl.ds(g * HEAD_DIM, HEAD_DIM)
        acc = ext[pl.ds(16, tm), lanes]
        tok = acc
        for k in range(1, w):
            acc = acc + ext[pl.ds(16 - k, tm), lanes]
        cnt = jnp.minimum(w, pos + 1).astype(F32)
        mixed.append(acc / cnt - tok)

    @pl.when(i == n_tiles - 1)
    def _():
        pool_ref[0, 0] = ext[pl.ds(tm + 1, POOL_STATE), :]

    ext[0:16, :] = ext[tm:tm + 16, :]
    p = _pool_project(mixed, wp_ref, bp_ref, ps_ref)

    z_a = proj(0, GROUP_W)
    z_b = proj(2 * GROUP_W, GROUP_W)
    g_a = proj(3 * GROUP_W, D_MODEL)
    g_b = proj(3 * GROUP_W + D_MODEL, D_MODEL)
    y_ref[0] = _merge_out(x, gate, a, p, z_a, z_b, g_a, g_b, wa_ref, wb_ref, wo_ref)


def _prompt_out(x, mod_p, norm_g, w_rest, ol, w_pool, b_pool, pool_scale, wa, wb, wo):
    B, L, _ = x.shape
    tm = ROW_TILE
    n_tiles = L // tm
    const2 = lambda b, i: (0, 0)
    one = pl.Buffered(1)
    in_specs = [
        pl.BlockSpec((1, tm, D_MODEL), lambda b, i: (b, i, 0)),
        pl.BlockSpec((1, 1, 3 * D_MODEL), lambda b, i: (b, 0, 0)),
        pl.BlockSpec((1, D_MODEL), const2),
        pl.BlockSpec((D_MODEL, REST_W), const2, pipeline_mode=one),
    ]
    for window, d in ATTN_PATTERNS:
        for _ in range(2):
            in_specs.append(pl.BlockSpec((1, d, N_HEADS, tm // d, HEAD_DIM), lambda b, i: (b, 0, 0, i, 0)))
    in_specs += [
        pl.BlockSpec((len(POOL_WINDOWS), HEAD_DIM, HEAD_DIM), lambda b, i: (0, 0, 0)),
        pl.BlockSpec((len(POOL_WINDOWS), HEAD_DIM), const2),
        pl.BlockSpec((1, GROUP_W), const2),
        pl.BlockSpec((GROUP_W, D_MODEL), const2, pipeline_mode=one),
        pl.BlockSpec((GROUP_W, D_MODEL), const2, pipeline_mode=one),
        pl.BlockSpec((D_MODEL, D_MODEL), const2, pipeline_mode=one),
    ]
    return pl.pallas_call(
        functools.partial(_out_body, n_tiles=n_tiles),
        grid=(B, n_tiles),
        in_specs=in_specs,
        out_specs=[pl.BlockSpec((1, tm, D_MODEL), lambda b, i: (b, i, 0)),
                   pl.BlockSpec((1, 1, POOL_STATE, GROUP_W), lambda b, i: (0, b, 0, 0))],
        out_shape=[jax.ShapeDtypeStruct((B, L, D_MODEL), F32),
                   jax.ShapeDtypeStruct((1, B, POOL_STATE, GROUP_W), F32)],
        scratch_shapes=[pltpu.VMEM((tm + 16, GROUP_W), F32),
                        pltpu.VMEM((2, N_HEADS, tm, HEAD_DIM), F32),
                        pltpu.VMEM((2, N_HEADS, tm, HEAD_DIM), F32)],
        compiler_params=pltpu.CompilerParams(
            dimension_semantics=("arbitrary", "arbitrary"), vmem_limit_bytes=VMEM_LIMIT),
        name="prompt_out",
    )(x, mod_p, norm_g, w_rest, *ol, w_pool, b_pool, pool_scale, wa, wb, wo)


def _sample_proj_body(x_ref, mod_ref, ng_ref, w_ref, o_ref):
    h = _modulated_norm(x_ref[...], ng_ref[...], mod_ref[:, 0:D_MODEL], mod_ref[:, D_MODEL:2 * D_MODEL])
    o_ref[...] = jnp.dot(h, w_ref[...], preferred_element_type=F32)


def _sample_proj(xs, mod_s, norm_g, w):
    n, width = xs.shape[0], w.shape[1]
    tn = GROUP_W
    return pl.pallas_call(
        _sample_proj_body,
        grid=(width // tn,),
        in_specs=[pl.BlockSpec((n, D_MODEL), lambda j: (0, 0)),
                  pl.BlockSpec((n, 3 * D_MODEL), lambda j: (0, 0)),
                  pl.BlockSpec((1, D_MODEL), lambda j: (0, 0)),
                  pl.BlockSpec((D_MODEL, tn), lambda j: (0, j))],
        out_specs=pl.BlockSpec((n, tn), lambda j: (0, j)),
        out_shape=jax.ShapeDtypeStruct((n, width), F32),
        name="sample_proj",
    )(xs, mod_s, norm_g, w)


def _sample_attn_body(p_ref, s0, s1, s2, qg_ref, kg_ref, a_ref, n0, n1, n2):
    bt = p_ref.shape[0]
    s_refs, n_refs = (s0, s1, s2), (n0, n1, n2)
    for b in range(bt):
        os_, ls_ = [], []
        for g in range(N_GROUPS):
            row = lambda t: pl.ds((t * N_GROUPS + g) * N_HEADS, N_HEADS)
            q4 = _head_norm(p_ref[b, row(0), :], qg_ref[g:g + 1, :]) * Q_SCALE
            k4 = _head_norm(p_ref[b, row(1), :], kg_ref[g:g + 1, :])
            v4 = p_ref[b, row(2), :]
            n_refs[g][b, 0:N_HEADS, :] = k4
            n_refs[g][b, N_HEADS:2 * N_HEADS, :] = v4
            keys = s_refs[g][b, :, 0:N_HEADS, :]
            vals = s_refs[g][b, :, N_HEADS:2 * N_HEADS, :]
            s_old = jnp.sum(keys * q4[None], axis=-1, keepdims=True)
            s_new = jnp.sum(k4 * q4, axis=-1, keepdims=True)
            m = jnp.maximum(jnp.max(s_old, axis=0), s_new)
            p_old = jnp.exp(s_old - m[None])
            p_new = jnp.exp(s_new - m)
            l = jnp.sum(p_old, axis=0) + p_new
            o = jnp.sum(p_old * vals, axis=0) + p_new * v4
            os_.append(o / l)
            ls_.append(m + jnp.log(l))
        mx = jnp.maximum(jnp.maximum(ls_[0], ls_[1]), ls_[2])
        es = [jnp.exp(l_ - mx) for l_ in ls_]
        a_ref[b] = (es[0] * os_[0] + es[1] * os_[1] + es[2] * os_[2]) / (es[0] + es[1] + es[2])


def _sample_attn(p3, views, q_gain, k_gain):
    n = p3.shape[0]
    bt = 8
    in_specs = [pl.BlockSpec((bt, p3.shape[1], HEAD_DIM), lambda j: (j, 0, 0))]
    for v in views:
        in_specs.append(pl.BlockSpec((bt, N_BACK, 2 * N_HEADS, HEAD_DIM), lambda j: (j, 0, 0, 0)))
    in_specs += [pl.BlockSpec((N_GROUPS, HEAD_DIM), lambda j: (0, 0))] * 2
    new_spec = pl.BlockSpec((bt, 2 * N_HEADS, HEAD_DIM), lambda j: (j, 0, 0))
    new_shape = jax.ShapeDtypeStruct((n, 2 * N_HEADS, HEAD_DIM), F32)
    return pl.pallas_call(
        _sample_attn_body,
        grid=(n // bt,),
        in_specs=in_specs,
        out_specs=[pl.BlockSpec((bt, N_HEADS, HEAD_DIM), lambda j: (j, 0, 0)), new_spec, new_spec, new_spec],
        out_shape=[jax.ShapeDtypeStruct((n, N_HEADS, HEAD_DIM), F32), new_shape, new_shape, new_shape],
        compiler_params=pltpu.CompilerParams(vmem_limit_bytes=VMEM_LIMIT),
        name="sample_attn",
    )(p3, *views, q_gain, k_gain)


def _sample_out_body(x_ref, mod_ref, r_ref, a_ref, sp_ref, wp_ref, bp_ref, ps_ref, wa_ref, wb_ref, wo_ref,
                     y_ref, pool_ref):
    u = r_ref[:, GROUP_W:2 * GROUP_W]
    mixed = []
    for g, w in enumerate(POOL_WINDOWS):
        lanes = pl.ds(g * HEAD_DIM, HEAD_DIM)
        tok = r_ref[:, pl.ds(GROUP_W + g * HEAD_DIM, HEAD_DIM)]
        acc = tok
        for k in range(1, w):
            acc = acc + sp_ref[POOL_STATE - k, :, lanes]
        mixed.append(acc / float(min(w, PAST_LEN + 1)) - tok)
    p = _pool_project(mixed, wp_ref, bp_ref, ps_ref)
    for k in range(POOL_STATE - 1):
        pool_ref[k] = sp_ref[k + 1]
    pool_ref[POOL_STATE - 1] = u
    y_ref[...] = _merge_out(
        x_ref[...], mod_ref[:, 2 * D_MODEL:3 * D_MODEL], a_ref[...], p,
        r_ref[:, 0:GROUP_W], r_ref[:, 2 * GROUP_W:3 * GROUP_W],
        r_ref[:, 3 * GROUP_W:3 * GROUP_W + D_MODEL], r_ref[:, 3 * GROUP_W + D_MODEL:REST_W],
        wa_ref, wb_ref, wo_ref)


def _sample_out(xs, mod_s, rest, a_s, sp_t, w_pool, b_pool, pool_scale, wa, wb, wo):
    n = xs.shape[0]
    return pl.pallas_call(
        _sample_out_body,
        out_shape=[jax.ShapeDtypeStruct((n, D_MODEL), F32),
                   jax.ShapeDtypeStruct((POOL_STATE, n, GROUP_W), F32)],
        compiler_params=pltpu.CompilerParams(vmem_limit_bytes=VMEM_LIMIT),
        name="sample_out",
    )(xs, mod_s, rest, a_s, sp_t, w_pool, b_pool, pool_scale, wa, wb, wo)


SHIFT_CHUNKS = 8
ROWS_PER_POS = 2 * N_HEADS


def _shift_copies(s_refs, n_refs, o_refs, sem, sem_new):
    copies = []
    for g in range(N_GROUPS):
        n_batch, rows, _ = s_refs[g].shape
        keep = rows - ROWS_PER_POS
        per = n_batch // SHIFT_CHUNKS
        for c in range(SHIFT_CHUNKS):
            bs = pl.ds(c * per, per)
            copies.append(pltpu.make_async_copy(
                s_refs[g].at[bs, pl.ds(ROWS_PER_POS, keep), :], o_refs[g].at[bs, pl.ds(0, keep), :], sem.at[g, c]))
        copies.append(pltpu.make_async_copy(n_refs[g], o_refs[g].at[:, pl.ds(keep, ROWS_PER_POS), :], sem_new.at[g]))
    return copies


def _shift_body(s0, s1, s2, n0, n1, n2, o0, o1, o2, sem, sem_new):
    copies = _shift_copies((s0, s1, s2), (n0, n1, n2), (o0, o1, o2), sem, sem_new)
    for cp in copies:
        cp.start()
    for cp in copies:
        cp.wait()


def _shift_states(states, new_rows):
    any_spec = pl.BlockSpec(memory_space=pl.ANY)
    return pl.pallas_call(
        _shift_body,
        in_specs=[any_spec] * 6,
        out_specs=[any_spec] * 3,
        out_shape=[jax.ShapeDtypeStruct(s.shape, F32) for s in states],
        scratch_shapes=[pltpu.SemaphoreType.DMA((N_GROUPS, SHIFT_CHUNKS)), pltpu.SemaphoreType.DMA((N_GROUPS,))],
        name="state_shift",
    )(*states, *new_rows)


def kernel(x_prompt, x_sample, state_kv_w128, state_kv_w512, state_kv_w2048, state_pool, c_prompt, c_sample,
           norm_g, w_ada, b_ada, w_in, q_gain, k_gain, w_pool, b_pool, pool_scale, w_a_out, w_b_out, w_out):
    B, L, _ = x_prompt.shape
    n_s = x_sample.shape[0]
    w_qkv = w_in[0, :, :QKV_W].astype(BF16)
    w_rest = w_in[0, :, QKV_W:].astype(BF16)
    wa, wb, wo, wp = (w[0].astype(BF16) for w in (w_a_out, w_b_out, w_out, w_pool))
    qg, kg, bp = q_gain[0], k_gain[0], b_pool[0]

    mod = _ada(jnp.concatenate([c_prompt, c_sample], axis=0), w_ada[0], b_ada)
    mod_p = mod[:B].reshape(B, 1, 3 * D_MODEL)
    mod_s = mod[B:]

    xs = x_sample[:, 0, :]
    qkv_s = _sample_proj(xs, mod_s, norm_g, w_qkv)
    rest_s = _sample_proj(xs, mod_s, norm_g, w_rest)
    states = (state_kv_w128, state_kv_w512, state_kv_w2048)
    views = [s.reshape(n_s, N_BACK, d * ROWS_PER_POS, HEAD_DIM) for s, (_, d) in zip(states, ATTN_PATTERNS)]
    a_s, *new_rows = _sample_attn(qkv_s.reshape(n_s, QKV_W // HEAD_DIM, HEAD_DIM), views, qg, kg)
    y_s, pool_t = _sample_out(xs, mod_s, rest_s, a_s.reshape(n_s, GROUP_W), state_pool[0].transpose(1, 0, 2),
                              wp, bp, pool_scale, wa, wb, wo)
    flat = [s.reshape(n_s, s.shape[2] * ROWS_PER_POS, HEAD_DIM) for s in states]
    kv_s = [o.reshape(s.shape) for o, s in zip(_shift_states(flat, new_rows), states)]

    *qkv, t0, t1, t2 = _prompt_qkv(x_prompt, mod_p, norm_g, w_qkv, qg, kg)
    ol = []
    for g, (d_blk, h_blk) in enumerate(((1, 1), (1, 4), (4, 4))):
        ol += _prompt_attn(qkv[g], qkv[N_GROUPS + g], qkv[2 * N_GROUPS + g], d_blk, h_blk)
    y_p, pool_p = _prompt_out(x_prompt, mod_p, norm_g, w_rest, ol, wp, bp, pool_scale, wa, wb, wo)
    kv_p = [t.reshape(1, B, t.shape[1], 2, N_HEADS, HEAD_DIM) for t in (t0, t1, t2)]

    return (y_p, y_s.reshape(n_s, 1, D_MODEL), kv_p[0], kv_p[1], kv_p[2], pool_p,
            kv_s[0], kv_s[1], kv_s[2], pool_t.transpose(1, 0, 2)[None])
```

```python
import functools

import jax
import jax.numpy as jnp
from jax import lax
from jax.experimental import pallas as pl
from jax.experimental.pallas import tpu as pltpu

F32 = jnp.float32
BF16 = jnp.bfloat16

D_MODEL = 1024
HEAD_DIM = 128
N_HEADS = 4
GROUP_W = N_HEADS * HEAD_DIM
ATTN_PATTERNS = ((128, 1), (512, 4), (2048, 16))
N_GROUPS = len(ATTN_PATTERNS)
N_BACK = 128
QKV_W = 3 * N_GROUPS * GROUP_W
REST_W = 3 * GROUP_W + 2 * D_MODEL
POOL_WINDOWS = (2, 4, 8, 16)
POOL_STATE = 15
PAST_LEN = 8192
EPS = 1e-6
Q_SCALE = HEAD_DIM ** -0.5
NEG = -1e30

ROW_TILE = 512
Q_BLOCK = 128
VMEM_LIMIT = 56 * 1024 * 1024


def _silu(v):
    return v * jax.nn.sigmoid(v)


def _modulated_norm(x, norm_g, shift, scale):
    ms = jnp.mean(x * x, axis=-1, keepdims=True)
    return (x * lax.rsqrt(ms + EPS) * norm_g * (1.0 + scale) + shift).astype(BF16)


def _head_norm(r, gain):
    ms = jnp.mean(r * r, axis=-1, keepdims=True)
    return r * lax.rsqrt(ms + EPS) * gain


def _ada_body(c_ref, w_ref, b_ref, o_ref):
    s = _silu(c_ref[...]).astype(BF16)
    o_ref[...] = jnp.dot(s, w_ref[...].astype(BF16), preferred_element_type=F32) + b_ref[...]


def _ada(c_all, w_ada, b_ada):
    n = c_all.shape[0]
    return pl.pallas_call(
        _ada_body,
        grid=(3,),
        in_specs=[pl.BlockSpec((n, D_MODEL), lambda j: (0, 0)),
                  pl.BlockSpec((D_MODEL, D_MODEL), lambda j: (0, j)),
                  pl.BlockSpec((1, D_MODEL), lambda j: (0, j))],
        out_specs=pl.BlockSpec((n, D_MODEL), lambda j: (0, j)),
        out_shape=jax.ShapeDtypeStruct((n, 3 * D_MODEL), F32),
        name="ada",
    )(c_all, w_ada, b_ada)


def _class_major_perm(tm, d):
    i = jnp.arange(tm)
    src_row = (i % (tm // d)) * d + i // (tm // d)
    return (src_row[:, None] == jnp.arange(tm)[None, :]).astype(BF16)


def _qkv_body(x_ref, mod_ref, ng_ref, w_ref, qg_ref, kg_ref, perm_ref, *refs, n_tiles):
    qkv_refs = (refs[0:3], refs[3:6], refs[6:9])
    tail_refs = refs[9:12]
    i = pl.program_id(1)
    tm = x_ref.shape[1]
    h = _modulated_norm(x_ref[0], ng_ref[...], mod_ref[0, :, 0:D_MODEL], mod_ref[0, :, D_MODEL:2 * D_MODEL])
    h_by_group = [h] + [jnp.dot(perm_ref[g - 1], h, preferred_element_type=F32).astype(BF16)
                        for g in range(1, N_GROUPS)]

    def heads(t, g, lhs):
        c = t * N_GROUPS + g
        res = jnp.dot(lhs, w_ref[:, c * GROUP_W:(c + 1) * GROUP_W], preferred_element_type=F32)
        out = []
        for hh in range(N_HEADS):
            r = res[:, hh * HEAD_DIM:(hh + 1) * HEAD_DIM]
            if t == 0:
                r = _head_norm(r, qg_ref[g:g + 1, :]) * Q_SCALE
            elif t == 1:
                r = _head_norm(r, kg_ref[g:g + 1, :])
            out.append(r)
        return out

    def write_tail(t, g, rs):
        window = ATTN_PATTERNS[g][0]
        for hh, r in enumerate(rs):
            row = (t - 1) * N_HEADS + hh
            tail_refs[g][0, :, row, :] = r if window >= tm else r[tm - window:, :]

    for t in range(3):
        for g, (window, d) in enumerate(ATTN_PATTERNS):
            rs = heads(t, g, h_by_group[g])
            per = tm // d
            for hh, r in enumerate(rs):
                for rr in range(d):
                    qkv_refs[t][g][0, rr, hh] = r[rr * per:(rr + 1) * per, :].astype(BF16)
            if t > 0:
                n_tail = max(window // tm, 1)

                @pl.when(i >= n_tiles - n_tail)
                def _():
                    write_tail(t, g, rs if d == 1 else heads(t, g, h))


def _prompt_qkv(x, mod_p, norm_g, w_qkv, q_gain, k_gain):
    B, L, _ = x.shape
    tm = ROW_TILE
    n_tiles = L // tm
    const2 = lambda b, i: (0, 0)
    in_specs = [
        pl.BlockSpec((1, tm, D_MODEL), lambda b, i: (b, i, 0)),
        pl.BlockSpec((1, 1, 3 * D_MODEL), lambda b, i: (b, 0, 0)),
        pl.BlockSpec((1, D_MODEL), const2),
        pl.BlockSpec((D_MODEL, QKV_W), const2, pipeline_mode=pl.Buffered(1)),
        pl.BlockSpec((N_GROUPS, HEAD_DIM), const2),
        pl.BlockSpec((N_GROUPS, HEAD_DIM), const2),
        pl.BlockSpec((N_GROUPS - 1, tm, tm), lambda b, i: (0, 0, 0), pipeline_mode=pl.Buffered(1)),
    ]
    perms = jnp.stack([_class_major_perm(tm, d) for _, d in ATTN_PATTERNS[1:]])
    out_specs, out_shape = [], []
    for t in range(3):
        for window, d in ATTN_PATTERNS:
            out_specs.append(pl.BlockSpec((1, d, N_HEADS, tm // d, HEAD_DIM), lambda b, i: (b, 0, 0, i, 0)))
            out_shape.append(jax.ShapeDtypeStruct((B, d, N_HEADS, L // d, HEAD_DIM), BF16))
    for window, d in ATTN_PATTERNS:
        w_eff = min(window, L)
        if w_eff >= tm:
            n_tail = w_eff // tm
            out_specs.append(pl.BlockSpec(
                (1, tm, 2 * N_HEADS, HEAD_DIM),
                lambda b, i, n_tail=n_tail: (b, jnp.maximum(i - (n_tiles - n_tail), 0), 0, 0)))
        else:
            out_specs.append(pl.BlockSpec((1, w_eff, 2 * N_HEADS, HEAD_DIM), lambda b, i: (b, 0, 0, 0)))
        out_shape.append(jax.ShapeDtypeStruct((B, w_eff, 2 * N_HEADS, HEAD_DIM), F32))
    return pl.pallas_call(
        functools.partial(_qkv_body, n_tiles=n_tiles),
        grid=(B, n_tiles),
        in_specs=in_specs,
        out_specs=out_specs,
        out_shape=out_shape,
        compiler_params=pltpu.CompilerParams(
            dimension_semantics=("arbitrary", "arbitrary"), vmem_limit_bytes=VMEM_LIMIT),
        name="prompt_qkv",
    )(x, mod_p, norm_g, w_qkv, q_gain, k_gain, perms)


def _attn_body(q_ref, k_ref, v_ref, o_ref, l_ref, vx):
    _, d_blk, h_blk, M, _ = q_ref.shape
    n_q = M // Q_BLOCK
    for c in range(d_blk):
        for hh in range(h_blk):
            vx[c, hh, :, 0:HEAD_DIM] = v_ref[0, c, hh]
            vx[c, hh, :, HEAD_DIM:2 * HEAD_DIM] = jnp.ones((M, HEAD_DIM), BF16)
    ii = lax.broadcasted_iota(jnp.int32, (Q_BLOCK, 2 * Q_BLOCK), 0)
    jj = lax.broadcasted_iota(jnp.int32, (Q_BLOCK, 2 * Q_BLOCK), 1)
    dist = Q_BLOCK + ii - jj
    band_ok = (dist >= 0) & (dist <= N_BACK)
    causal_ok = (lax.broadcasted_iota(jnp.int32, (Q_BLOCK, Q_BLOCK), 1)
                 <= lax.broadcasted_iota(jnp.int32, (Q_BLOCK, Q_BLOCK), 0))
    nt = (((1,), (1,)), ((), ()))

    def block(c, hh, n, first):
        rows = pl.ds(pl.multiple_of(n * Q_BLOCK, Q_BLOCK), Q_BLOCK)
        if first:
            keys, ok = rows, causal_ok
        else:
            keys, ok = pl.ds(pl.multiple_of((n - 1) * Q_BLOCK, Q_BLOCK), 2 * Q_BLOCK), band_ok
        s = lax.dot_general(q_ref[0, c, hh, rows, :], k_ref[0, c, hh, keys, :], nt, preferred_element_type=F32)
        s = jnp.where(ok, s, NEG)
        m = jnp.max(s, axis=-1, keepdims=True)
        p = jnp.exp(s - m).astype(BF16)
        oe = jnp.dot(p, vx[c, hh, keys, :], preferred_element_type=F32)
        l = oe[:, HEAD_DIM:]
        o_ref[0, c, hh, rows, :] = oe[:, :HEAD_DIM] / l
        l_ref[0, c, hh, rows, :] = m + jnp.log(l)

    for c in range(d_blk):
        for hh in range(h_blk):
            block(c, hh, 0, True)
        if n_q <= 2:
            for n in range(1, n_q):
                for hh in range(h_blk):
                    block(c, hh, n, False)
        else:
            def step(n, carry, c=c):
                for hh in range(h_blk):
                    block(c, hh, n, False)
                return carry

            lax.fori_loop(1, n_q, step, 0, unroll=2)


def _prompt_attn(q, k, v, d_blk, h_blk):
    B, d, H, M, E = q.shape
    blk = (1, d_blk, h_blk, M, E)
    spec = pl.BlockSpec(blk, lambda b, c, hh: (b, c, hh, 0, 0))
    shape = jax.ShapeDtypeStruct(q.shape, F32)
    return pl.pallas_call(
        _attn_body,
        grid=(B, d // d_blk, H // h_blk),
        in_specs=[spec, spec, spec],
        out_specs=[spec, spec],
        out_shape=[shape, shape],
        scratch_shapes=[pltpu.VMEM((d_blk, h_blk, M, 2 * E), BF16)],
        compiler_params=pltpu.CompilerParams(
            dimension_semantics=("arbitrary", "arbitrary", "arbitrary"), vmem_limit_bytes=VMEM_LIMIT),
        name="prompt_attn_d%d" % d,
    )(q, k, v)


def _pool_project(mixed, wp_ref, bp_ref, ps_ref):
    cols = []
    for g in range(len(POOL_WINDOWS)):
        lanes = slice(g * HEAD_DIM, (g + 1) * HEAD_DIM)
        y = jnp.dot(mixed[g].astype(BF16), wp_ref[g], preferred_element_type=F32) + bp_ref[g:g + 1, :]
        cols.append(y * ps_ref[:, lanes])
    return jnp.concatenate(cols, axis=-1)


def _merge_out(x, gate, a, p, z_a, z_b, g_a, g_b, wa_ref, wb_ref, wo_ref):
    ya = jnp.dot((a * _silu(z_a)).astype(BF16), wa_ref[...], preferred_element_type=F32)
    yb = jnp.dot((p * _silu(z_b)).astype(BF16), wb_ref[...], preferred_element_type=F32)
    m = jax.nn.sigmoid(g_a) * ya + jax.nn.sigmoid(g_b) * yb
    return x + gate * jnp.dot(m.astype(BF16), wo_ref[...], preferred_element_type=F32)


def _out_body(x_ref, mod_ref, ng_ref, w_ref, o0, l0, o1, l1, o2, l2, wp_ref, bp_ref, ps_ref,
              wa_ref, wb_ref, wo_ref, y_ref, pool_ref, ext, oscr, lscr, *, n_tiles):
    i = pl.program_id(1)
    tm = x_ref.shape[1]
    x = x_ref[0]
    h = _modulated_norm(x, ng_ref[...], mod_ref[0, :, 0:D_MODEL], mod_ref[0, :, D_MODEL:2 * D_MODEL])
    gate = mod_ref[0, :, 2 * D_MODEL:3 * D_MODEL]

    def proj(lo, width):
        return jnp.dot(h, w_ref[:, lo:lo + width], preferred_element_type=F32)

    o_refs, l_refs = (o0, o1, o2), (l0, l1, l2)
    for g, (window, d) in enumerate(ATTN_PATTERNS):
        if d == 1:
            continue
        for hh in range(N_HEADS):
            for rr in range(d):
                rows = pl.ds(rr, tm // d, stride=d)
                oscr[g - 1, hh, rows, :] = o_refs[g][0, rr, hh]
                lscr[g - 1, hh, rows, :] = l_refs[g][0, rr, hh]
    a_cols = []
    for hh in range(N_HEADS):
        os_ = [o0[0, 0, hh], oscr[0, hh], oscr[1, hh]]
        ls_ = [l0[0, 0, hh], lscr[0, hh], lscr[1, hh]]
        mx = jnp.maximum(jnp.maximum(ls_[0], ls_[1]), ls_[2])
        es = [jnp.exp(l_ - mx) for l_ in ls_]
        den = es[0] + es[1] + es[2]
        a_cols.append((es[0] * os_[0] + es[1] * os_[1] + es[2] * os_[2]) / den)
    a = jnp.concatenate(a_cols, axis=-1)

    u = proj(GROUP_W, GROUP_W)

    @pl.when(i == 0)
    def _():
        ext[0:16, :] = jnp.zeros((16, GROUP_W), F32)

    ext[16:16 + tm, :] = u
    pos = i * tm + lax.broadcasted_iota(jnp.int32, (tm, 1), 0)
    mixed = []
    for g, w in enumerate(POOL_WINDOWS):
        lanes = pl.ds(g * HEAD_DIM, HEAD_DIM)
        acc = ext[pl.ds(16, tm), lanes]
        tok = acc
        for k in range(1, w):
            acc = acc + ext[pl.ds(16 - k, tm), lanes]
        cnt = jnp.minimum(w, pos + 1).astype(F32)
        mixed.append(acc / cnt - tok)

    @pl.when(i == n_tiles - 1)
    def _():
        pool_ref[0, 0] = ext[pl.ds(tm + 1, POOL_STATE), :]

    ext[0:16, :] = ext[tm:tm + 16, :]
    p = _pool_project(mixed, wp_ref, bp_ref, ps_ref)

    z_a = proj(0, GROUP_W)
    z_b = proj(2 * GROUP_W, GROUP_W)
    g_a = proj(3 * GROUP_W, D_MODEL)
    g_b = proj(3 * GROUP_W + D_MODEL, D_MODEL)
    y_ref[0] = _merge_out(x, gate, a, p, z_a, z_b, g_a, g_b, wa_ref, wb_ref, wo_ref)


def _prompt_out(x, mod_p, norm_g, w_rest, ol, w_pool, b_pool, pool_scale, wa, wb, wo):
    B, L, _ = x.shape
    tm = ROW_TILE
    n_tiles = L // tm
    const2 = lambda b, i: (0, 0)
    one = pl.Buffered(1)
    in_specs = [
        pl.BlockSpec((1, tm, D_MODEL), lambda b, i: (b, i, 0)),
        pl.BlockSpec((1, 1, 3 * D_MODEL), lambda b, i: (b, 0, 0)),
        pl.BlockSpec((1, D_MODEL), const2),
        pl.BlockSpec((D_MODEL, REST_W), const2, pipeline_mode=one),
    ]
    for window, d in ATTN_PATTERNS:
        for _ in range(2):
            in_specs.append(pl.BlockSpec((1, d, N_HEADS, tm // d, HEAD_DIM), lambda b, i: (b, 0, 0, i, 0)))
    in_specs += [
        pl.BlockSpec((len(POOL_WINDOWS), HEAD_DIM, HEAD_DIM), lambda b, i: (0, 0, 0)),
        pl.BlockSpec((len(POOL_WINDOWS), HEAD_DIM), const2),
        pl.BlockSpec((1, GROUP_W), const2),
        pl.BlockSpec((GROUP_W, D_MODEL), const2, pipeline_mode=one),
        pl.BlockSpec((GROUP_W, D_MODEL), const2, pipeline_mode=one),
        pl.BlockSpec((D_MODEL, D_MODEL), const2, pipeline_mode=one),
    ]
    return pl.pallas_call(
        functools.partial(_out_body, n_tiles=n_tiles),
        grid=(B, n_tiles),
        in_specs=in_specs,
        out_specs=[pl.BlockSpec((1, tm, D_MODEL), lambda b, i: (b, i, 0)),
                   pl.BlockSpec((1, 1, POOL_STATE, GROUP_W), lambda b, i: (0, b, 0, 0))],
        out_shape=[jax.ShapeDtypeStruct((B, L, D_MODEL), F32),
                   jax.ShapeDtypeStruct((1, B, POOL_STATE, GROUP_W), F32)],
        scratch_shapes=[pltpu.VMEM((tm + 16, GROUP_W), F32),
                        pltpu.VMEM((2, N_HEADS, tm, HEAD_DIM), F32),
                        pltpu.VMEM((2, N_HEADS, tm, HEAD_DIM), F32)],
        compiler_params=pltpu.CompilerParams(
            dimension_semantics=("arbitrary", "arbitrary"), vmem_limit_bytes=VMEM_LIMIT),
        name="prompt_out",
    )(x, mod_p, norm_g, w_rest, *ol, w_pool, b_pool, pool_scale, wa, wb, wo)


def _sample_proj_body(x_ref, mod_ref, ng_ref, w_ref, o_ref):
    h = _modulated_norm(x_ref[...], ng_ref[...], mod_ref[:, 0:D_MODEL], mod_ref[:, D_MODEL:2 * D_MODEL])
    o_ref[...] = jnp.dot(h, w_ref[...], preferred_element_type=F32)


def _sample_proj(xs, mod_s, norm_g, w):
    n, width = xs.shape[0], w.shape[1]
    tn = GROUP_W
    return pl.pallas_call(
        _sample_proj_body,
        grid=(width // tn,),
        in_specs=[pl.BlockSpec((n, D_MODEL), lambda j: (0, 0)),
                  pl.BlockSpec((n, 3 * D_MODEL), lambda j: (0, 0)),
                  pl.BlockSpec((1, D_MODEL), lambda j: (0, 0)),
                  pl.BlockSpec((D_MODEL, tn), lambda j: (0, j))],
        out_specs=pl.BlockSpec((n, tn), lambda j: (0, j)),
        out_shape=jax.ShapeDtypeStruct((n, width), F32),
        name="sample_proj",
    )(xs, mod_s, norm_g, w)


def _sample_attn_body(p_ref, s0, s1, s2, qg_ref, kg_ref, a_ref, n0, n1, n2):
    bt = p_ref.shape[0]
    s_refs, n_refs = (s0, s1, s2), (n0, n1, n2)
    for b in range(bt):
        os_, ls_ = [], []
        for g in range(N_GROUPS):
            row = lambda t: pl.ds((t * N_GROUPS + g) * N_HEADS, N_HEADS)
            q4 = _head_norm(p_ref[b, row(0), :], qg_ref[g:g + 1, :]) * Q_SCALE
            k4 = _head_norm(p_ref[b, row(1), :], kg_ref[g:g + 1, :])
            v4 = p_ref[b, row(2), :]
            n_refs[g][b, 0:N_HEADS, :] = k4
            n_refs[g][b, N_HEADS:2 * N_HEADS, :] = v4
            keys = s_refs[g][b, :, 0:N_HEADS, :]
            vals = s_refs[g][b, :, N_HEADS:2 * N_HEADS, :]
            s_old = jnp.sum(keys * q4[None], axis=-1, keepdims=True)
            s_new = jnp.sum(k4 * q4, axis=-1, keepdims=True)
            m = jnp.maximum(jnp.max(s_old, axis=0), s_new)
            p_old = jnp.exp(s_old - m[None])
            p_new = jnp.exp(s_new - m)
            l = jnp.sum(p_old, axis=0) + p_new
            o = jnp.sum(p_old * vals, axis=0) + p_new * v4
            os_.append(o / l)
            ls_.append(m + jnp.log(l))
        mx = jnp.maximum(jnp.maximum(ls_[0], ls_[1]), ls_[2])
        es = [jnp.exp(l_ - mx) for l_ in ls_]
        a_ref[b] = (es[0] * os_[0] + es[1] * os_[1] + es[2] * os_[2]) / (es[0] + es[1] + es[2])


def _sample_attn(p3, views, q_gain, k_gain):
    n = p3.shape[0]
    bt = 8
    in_specs = [pl.BlockSpec((bt, p3.shape[1], HEAD_DIM), lambda j: (j, 0, 0))]
    for v in views:
        in_specs.append(pl.BlockSpec((bt, N_BACK, 2 * N_HEADS, HEAD_DIM), lambda j: (j, 0, 0, 0)))
    in_specs += [pl.BlockSpec((N_GROUPS, HEAD_DIM), lambda j: (0, 0))] * 2
    new_spec = pl.BlockSpec((bt, 2 * N_HEADS, HEAD_DIM), lambda j: (j, 0, 0))
    new_shape = jax.ShapeDtypeStruct((n, 2 * N_HEADS, HEAD_DIM), F32)
    return pl.pallas_call(
        _sample_attn_body,
        grid=(n // bt,),
        in_specs=in_specs,
        out_specs=[pl.BlockSpec((bt, N_HEADS, HEAD_DIM), lambda j: (j, 0, 0)), new_spec, new_spec, new_spec],
        out_shape=[jax.ShapeDtypeStruct((n, N_HEADS, HEAD_DIM), F32), new_shape, new_shape, new_shape],
        compiler_params=pltpu.CompilerParams(vmem_limit_bytes=VMEM_LIMIT),
        name="sample_attn",
    )(p3, *views, q_gain, k_gain)


def _sample_out_body(x_ref, mod_ref, r_ref, a_ref, sp_ref, wp_ref, bp_ref, ps_ref, wa_ref, wb_ref, wo_ref,
                     y_ref, pool_ref):
    u = r_ref[:, GROUP_W:2 * GROUP_W]
    mixed = []
    for g, w in enumerate(POOL_WINDOWS):
        lanes = pl.ds(g * HEAD_DIM, HEAD_DIM)
        tok = r_ref[:, pl.ds(GROUP_W + g * HEAD_DIM, HEAD_DIM)]
        acc = tok
        for k in range(1, w):
            acc = acc + sp_ref[POOL_STATE - k, :, lanes]
        mixed.append(acc / float(min(w, PAST_LEN + 1)) - tok)
    p = _pool_project(mixed, wp_ref, bp_ref, ps_ref)
    for k in range(POOL_STATE - 1):
        pool_ref[k] = sp_ref[k + 1]
    pool_ref[POOL_STATE - 1] = u
    y_ref[...] = _merge_out(
        x_ref[...], mod_ref[:, 2 * D_MODEL:3 * D_MODEL], a_ref[...], p,
        r_ref[:, 0:GROUP_W], r_ref[:, 2 * GROUP_W:3 * GROUP_W],
        r_ref[:, 3 * GROUP_W:3 * GROUP_W + D_MODEL], r_ref[:, 3 * GROUP_W + D_MODEL:REST_W],
        wa_ref, wb_ref, wo_ref)


def _sample_out(xs, mod_s, rest, a_s, sp_t, w_pool, b_pool, pool_scale, wa, wb, wo):
    n = xs.shape[0]
    return pl.pallas_call(
        _sample_out_body,
        out_shape=[jax.ShapeDtypeStruct((n, D_MODEL), F32),
                   jax.ShapeDtypeStruct((POOL_STATE, n, GROUP_W), F32)],
        compiler_params=pltpu.CompilerParams(vmem_limit_bytes=VMEM_LIMIT),
        name="sample_out",
    )(xs, mod_s, rest, a_s, sp_t, w_pool, b_pool, pool_scale, wa, wb, wo)


ROWS_PER_POS = 2 * N_HEADS
SHIFT_BATCHES = (16, 4, 1)


def _shift_copies(c, slot, base, src, dst, buf, rsem, wsem):
    nb, keep, _ = buf.shape[1:]
    batches = pl.ds(base + c * nb, nb)
    rd = pltpu.make_async_copy(src.at[batches, pl.ds(ROWS_PER_POS, keep), :], buf.at[slot], rsem.at[slot])
    wr = pltpu.make_async_copy(buf.at[slot], dst.at[batches, pl.ds(0, keep), :], wsem.at[slot])
    return rd, wr


def _shift_begin(c, n, base, src, dst, buf, rsem, wsem):
    slot = c % 2

    @pl.when(c == 0)
    def _():
        _shift_copies(c, 0, base, src, dst, buf, rsem, wsem)[0].start()

    @pl.when(c >= 1)
    def _():
        _shift_copies(c - 1, 1 - slot, base, src, dst, buf, rsem, wsem)[1].wait()

    @pl.when(c + 1 < n)
    def _():
        _shift_copies(c + 1, 1 - slot, base, src, dst, buf, rsem, wsem)[0].start()


def _shift_end(c, n, base, src, dst, new_ref, buf, rsem, wsem, nsem):
    slot = c % 2
    nb, keep, _ = buf.shape[1:]
    rd, wr = _shift_copies(c, slot, base, src, dst, buf, rsem, wsem)
    new = pltpu.make_async_copy(new_ref, dst.at[pl.ds(base + c * nb, nb), pl.ds(keep, ROWS_PER_POS), :], nsem)
    new.start()
    rd.wait()
    wr.start()
    new.wait()

    @pl.when(c == n - 1)
    def _():
        wr.wait()


def _shift_body(new_ref, src, dst, buf, rsem, wsem, nsem):
    c, n = pl.program_id(0), pl.num_programs(0)
    _shift_begin(c, n, 0, src, dst, buf, rsem, wsem)
    _shift_end(c, n, 0, src, dst, new_ref, buf, rsem, wsem, nsem)


def _shift_state(state, new_rows, nb):
    n_batch, rows, _ = state.shape
    return pl.pallas_call(
        _shift_body,
        grid=(n_batch // nb,),
        in_specs=[pl.BlockSpec((nb, ROWS_PER_POS, HEAD_DIM), lambda c: (c, 0, 0)),
                  pl.BlockSpec(memory_space=pl.ANY)],
        out_specs=pl.BlockSpec(memory_space=pl.ANY),
        out_shape=jax.ShapeDtypeStruct(state.shape, F32),
        scratch_shapes=[pltpu.VMEM((2, nb, rows - ROWS_PER_POS, HEAD_DIM), F32),
                        pltpu.SemaphoreType.DMA((2,)), pltpu.SemaphoreType.DMA((2,)), pltpu.SemaphoreType.DMA(())],
        compiler_params=pltpu.CompilerParams(
            dimension_semantics=("arbitrary",), vmem_limit_bytes=VMEM_LIMIT),
        name="state_shift_%d" % (rows // ROWS_PER_POS),
    )(new_rows, state)


def kernel(x_prompt, x_sample, state_kv_w128, state_kv_w512, state_kv_w2048, state_pool, c_prompt, c_sample,
           norm_g, w_ada, b_ada, w_in, q_gain, k_gain, w_pool, b_pool, pool_scale, w_a_out, w_b_out, w_out):
    B, L, _ = x_prompt.shape
    n_s = x_sample.shape[0]
    w_qkv = w_in[0, :, :QKV_W].astype(BF16)
    w_rest = w_in[0, :, QKV_W:].astype(BF16)
    wa, wb, wo, wp = (w[0].astype(BF16) for w in (w_a_out, w_b_out, w_out, w_pool))
    qg, kg, bp = q_gain[0], k_gain[0], b_pool[0]

    mod = _ada(jnp.concatenate([c_prompt, c_sample], axis=0), w_ada[0], b_ada)
    mod_p = mod[:B].reshape(B, 1, 3 * D_MODEL)
    mod_s = mod[B:]

    xs = x_sample[:, 0, :]
    qkv_s = _sample_proj(xs, mod_s, norm_g, w_qkv)
    rest_s = _sample_proj(xs, mod_s, norm_g, w_rest)
    states = (state_kv_w128, state_kv_w512, state_kv_w2048)
    views = [s.reshape(n_s, N_BACK, d * ROWS_PER_POS, HEAD_DIM) for s, (_, d) in zip(states, ATTN_PATTERNS)]
    a_s, *new_rows = _sample_attn(qkv_s.reshape(n_s, QKV_W // HEAD_DIM, HEAD_DIM), views, qg, kg)
    y_s, pool_t = _sample_out(xs, mod_s, rest_s, a_s.reshape(n_s, GROUP_W), state_pool[0].transpose(1, 0, 2),
                              wp, bp, pool_scale, wa, wb, wo)
    flat = [s.reshape(n_s, s.shape[2] * ROWS_PER_POS, HEAD_DIM) for s in states]
    kv_s = [_shift_state(f, nr, nb).reshape(s.shape)
            for f, nr, nb, s in zip(flat, new_rows, SHIFT_BATCHES, states)]

    *qkv, t0, t1, t2 = _prompt_qkv(x_prompt, mod_p, norm_g, w_qkv, qg, kg)
    ol = []
    for g, (d_blk, h_blk) in enumerate(((1, 2), (1, 4), (4, 4))):
        ol += _prompt_attn(qkv[g], qkv[N_GROUPS + g], qkv[2 * N_GROUPS + g], d_blk, h_blk)
    y_p, pool_p = _prompt_out(x_prompt, mod_p, norm_g, w_rest, ol, wp, bp, pool_scale, wa, wb, wo)
    kv_p = [t.reshape(1, B, t.shape[1], 2, N_HEADS, HEAD_DIM) for t in (t0, t1, t2)]

    return (y_p, y_s.reshape(n_s, 1, D_MODEL), kv_p[0], kv_p[1], kv_p[2], pool_p,
            kv_s[0], kv_s[1], kv_s[2], pool_t.transpose(1, 0, 2)[None])
```

```python
import functools

import jax
import jax.numpy as jnp
from jax import lax
from jax.experimental import pallas as pl
from jax.experimental.pallas import tpu as pltpu

F32 = jnp.float32
BF16 = jnp.bfloat16

D_MODEL = 1024
HEAD_DIM = 128
N_HEADS = 4
GROUP_W = N_HEADS * HEAD_DIM
ATTN_PATTERNS = ((128, 1), (512, 4), (2048, 16))
N_GROUPS = len(ATTN_PATTERNS)
N_BACK = 128
QKV_W = 3 * N_GROUPS * GROUP_W
REST_W = 3 * GROUP_W + 2 * D_MODEL
POOL_WINDOWS = (2, 4, 8, 16)
POOL_STATE = 15
PAST_LEN = 8192
EPS = 1e-6
Q_SCALE = HEAD_DIM ** -0.5
NEG = -1e30

ROW_TILE = 512
Q_BLOCK = 128
VMEM_LIMIT = 56 * 1024 * 1024
OUT_VMEM_LIMIT = 60 * 1024 * 1024


def _silu(v):
    return v * jax.nn.sigmoid(v)


def _modulated_norm(x, norm_g, shift, scale):
    ms = jnp.mean(x * x, axis=-1, keepdims=True)
    return (x * lax.rsqrt(ms + EPS) * norm_g * (1.0 + scale) + shift).astype(BF16)


def _head_norm(r, gain):
    ms = jnp.mean(r * r, axis=-1, keepdims=True)
    return r * lax.rsqrt(ms + EPS) * gain


def _ada_body(c_ref, w_ref, b_ref, o_ref):
    s = _silu(c_ref[...]).astype(BF16)
    o_ref[...] = jnp.dot(s, w_ref[...].astype(BF16), preferred_element_type=F32) + b_ref[...]


def _ada(c_all, w_ada, b_ada):
    n = c_all.shape[0]
    return pl.pallas_call(
        _ada_body,
        grid=(3,),
        in_specs=[pl.BlockSpec((n, D_MODEL), lambda j: (0, 0)),
                  pl.BlockSpec((D_MODEL, D_MODEL), lambda j: (0, j)),
                  pl.BlockSpec((1, D_MODEL), lambda j: (0, j))],
        out_specs=pl.BlockSpec((n, D_MODEL), lambda j: (0, j)),
        out_shape=jax.ShapeDtypeStruct((n, 3 * D_MODEL), F32),
        name="ada",
    )(c_all, w_ada, b_ada)


ROWS_PER_POS = 2 * N_HEADS
SHIFT_NB = 16
SHIFT_ROWS = 712
SHIFT_SUBSTEPS = 2


def _shift_plan(state_shapes):
    kinds, c0 = [], 0
    for idx, (n_batch, rows_total, _) in enumerate(state_shapes):
        keep = rows_total - ROWS_PER_POS
        pieces = min(p for p in range(1, keep)
                     if keep % (p * ROWS_PER_POS) == 0 and keep // p <= SHIFT_ROWS)
        n = (n_batch // SHIFT_NB) * pieces
        kinds.append((idx, keep // pieces, pieces, c0, c0 + n))
        c0 += n
    return kinds


def _shift_substep(j, lo, hi, last_slot, kinds, srcs, dsts, buf, rsem, wsem):
    def copy(jj, kind, write):
        idx, rows, pieces, c0, _ = kind
        local = jj - c0
        batches = pl.ds((local // pieces) * SHIFT_NB, SHIFT_NB)
        first = (local % pieces) * rows
        slot = jj % 2
        stage = buf.at[slot, :, pl.ds(0, rows), :]
        if write:
            return pltpu.make_async_copy(stage, dsts[idx].at[batches, pl.ds(first, rows), :], wsem.at[slot])
        return pltpu.make_async_copy(srcs[idx].at[batches, pl.ds(first + ROWS_PER_POS, rows), :], stage, rsem.at[slot])

    def for_chunk(jj, also, fn):
        for kind in kinds:
            first, end = max(kind[3], lo), min(kind[4], hi)
            if first < end:
                cond = (jj >= first) & (jj < end)

                @pl.when(cond if also is None else cond & also)
                def _():
                    fn(functools.partial(copy, jj, kind))

    for_chunk(j, j == lo, lambda cp: cp(False).start())

    def landed(cp):
        cp(False).wait()
        cp(True).start()

    for_chunk(j, None, landed)
    for_chunk(j - 1, None, lambda cp: cp(True).wait())
    for_chunk(j + 1, None, lambda cp: cp(False).start())
    for_chunk(j, j == last_slot, lambda cp: cp(True).wait())


def _new_rows_copy(new_ref, dst, nsem):
    keep = dst.shape[1] - ROWS_PER_POS
    return pltpu.make_async_copy(new_ref, dst.at[:, pl.ds(keep, ROWS_PER_POS), :], nsem)


def _shift_scratch():
    return [pltpu.VMEM((2, SHIFT_NB, SHIFT_ROWS, HEAD_DIM), F32),
            pltpu.SemaphoreType.DMA((2,)), pltpu.SemaphoreType.DMA((2,)), pltpu.SemaphoreType.DMA(())]


def _class_major_perm(tm, d):
    i = jnp.arange(tm)
    src_row = (i % (tm // d)) * d + i // (tm // d)
    return (src_row[:, None] == jnp.arange(tm)[None, :]).astype(BF16)


def _qkv_body(x_ref, mod_ref, ng_ref, w_ref, qg_ref, kg_ref, perm_ref, new_ref, src_ref, *refs, n_tiles, shift):
    qkv_refs = (refs[0:3], refs[3:6], refs[6:9])
    tail_refs = refs[9:12]
    dst_ref, buf, rsem, wsem, nsem = refs[12:17]
    i = pl.program_id(1)
    tm = x_ref.shape[1]
    step = pl.program_id(0) * n_tiles + i
    kinds, lo, hi = shift

    last_slot = lo + SHIFT_SUBSTEPS * pl.num_programs(0) * n_tiles - 1

    def shift_substep(k):
        _shift_substep(lo + SHIFT_SUBSTEPS * step + k, lo, hi, last_slot, kinds, [src_ref], [dst_ref],
                       buf, rsem, wsem)

    shift_substep(0)

    @pl.when(step == 0)
    def _():
        _new_rows_copy(new_ref, dst_ref, nsem).start()

    h = _modulated_norm(x_ref[0], ng_ref[...], mod_ref[0, :, 0:D_MODEL], mod_ref[0, :, D_MODEL:2 * D_MODEL])
    h_by_group = [h] + [jnp.dot(perm_ref[g - 1], h, preferred_element_type=F32).astype(BF16)
                        for g in range(1, N_GROUPS)]

    def heads(t, g, lhs):
        c = t * N_GROUPS + g
        res = jnp.dot(lhs, w_ref[:, c * GROUP_W:(c + 1) * GROUP_W], preferred_element_type=F32)
        out = []
        for hh in range(N_HEADS):
            r = res[:, hh * HEAD_DIM:(hh + 1) * HEAD_DIM]
            if t == 0:
                r = _head_norm(r, qg_ref[g:g + 1, :]) * Q_SCALE
            elif t == 1:
                r = _head_norm(r, kg_ref[g:g + 1, :])
            out.append(r)
        return out

    def write_tail(t, g, rs):
        window = ATTN_PATTERNS[g][0]
        for hh, r in enumerate(rs):
            row = (t - 1) * N_HEADS + hh
            tail_refs[g][0, :, row, :] = r if window >= tm else r[tm - window:, :]

    for t in range(3):
        for g, (window, d) in enumerate(ATTN_PATTERNS):
            if (t, g) == (1, 1):
                shift_substep(1)

                @pl.when(step == 0)
                def _():
                    _new_rows_copy(new_ref, dst_ref, nsem).wait()

            rs = heads(t, g, h_by_group[g])
            per = tm // d
            for hh, r in enumerate(rs):
                for rr in range(d):
                    qkv_refs[t][g][0, rr, hh] = r[rr * per:(rr + 1) * per, :].astype(BF16)
            if t > 0:
                n_tail = max(window // tm, 1)

                @pl.when(i >= n_tiles - n_tail)
                def _():
                    write_tail(t, g, rs if d == 1 else heads(t, g, h))


def _prompt_qkv(x, mod_p, norm_g, w_qkv, q_gain, k_gain, new_rows, state, shift):
    B, L, _ = x.shape
    tm = ROW_TILE
    n_tiles = L // tm
    const2 = lambda b, i: (0, 0)
    in_specs = [
        pl.BlockSpec((1, tm, D_MODEL), lambda b, i: (b, i, 0)),
        pl.BlockSpec((1, 1, 3 * D_MODEL), lambda b, i: (b, 0, 0)),
        pl.BlockSpec((1, D_MODEL), const2),
        pl.BlockSpec((D_MODEL, QKV_W), const2, pipeline_mode=pl.Buffered(1)),
        pl.BlockSpec((N_GROUPS, HEAD_DIM), const2),
        pl.BlockSpec((N_GROUPS, HEAD_DIM), const2),
        pl.BlockSpec((N_GROUPS - 1, tm, tm), lambda b, i: (0, 0, 0), pipeline_mode=pl.Buffered(1)),
        pl.BlockSpec(new_rows.shape, lambda b, i: (0, 0, 0), pipeline_mode=pl.Buffered(1)),
        pl.BlockSpec(memory_space=pl.ANY),
    ]
    perms = jnp.stack([_class_major_perm(tm, d) for _, d in ATTN_PATTERNS[1:]])
    out_specs, out_shape = [], []
    for t in range(3):
        for window, d in ATTN_PATTERNS:
            out_specs.append(pl.BlockSpec((1, d, N_HEADS, tm // d, HEAD_DIM), lambda b, i: (b, 0, 0, i, 0)))
            out_shape.append(jax.ShapeDtypeStruct((B, d, N_HEADS, L // d, HEAD_DIM), BF16))
    for window, d in ATTN_PATTERNS:
        w_eff = min(window, L)
        if w_eff >= tm:
            n_tail = w_eff // tm
            out_specs.append(pl.BlockSpec(
                (1, tm, 2 * N_HEADS, HEAD_DIM),
                lambda b, i, n_tail=n_tail: (b, jnp.maximum(i - (n_tiles - n_tail), 0), 0, 0)))
        else:
            out_specs.append(pl.BlockSpec((1, w_eff, 2 * N_HEADS, HEAD_DIM), lambda b, i: (b, 0, 0, 0)))
        out_shape.append(jax.ShapeDtypeStruct((B, w_eff, 2 * N_HEADS, HEAD_DIM), F32))
    out_specs.append(pl.BlockSpec(memory_space=pl.ANY))
    out_shape.append(jax.ShapeDtypeStruct(state.shape, F32))
    return pl.pallas_call(
        functools.partial(_qkv_body, n_tiles=n_tiles, shift=shift),
        grid=(B, n_tiles),
        in_specs=in_specs,
        out_specs=out_specs,
        out_shape=out_shape,
        scratch_shapes=_shift_scratch(),
        compiler_params=pltpu.CompilerParams(
            dimension_semantics=("arbitrary", "arbitrary"), vmem_limit_bytes=VMEM_LIMIT),
        name="prompt_qkv",
    )(x, mod_p, norm_g, w_qkv, q_gain, k_gain, perms, new_rows, state)


def _attn_body(q_ref, k_ref, v_ref, o_ref, l_ref, vx):
    _, d_blk, h_blk, M, _ = q_ref.shape
    n_q = M // Q_BLOCK
    for c in range(d_blk):
        for hh in range(h_blk):
            vx[c, hh, :, 0:HEAD_DIM] = v_ref[0, c, hh]
            vx[c, hh, :, HEAD_DIM:2 * HEAD_DIM] = jnp.ones((M, HEAD_DIM), BF16)
    ii = lax.broadcasted_iota(jnp.int32, (Q_BLOCK, 2 * Q_BLOCK), 0)
    jj = lax.broadcasted_iota(jnp.int32, (Q_BLOCK, 2 * Q_BLOCK), 1)
    dist = Q_BLOCK + ii - jj
    band_ok = (dist >= 0) & (dist <= N_BACK)
    causal_ok = (lax.broadcasted_iota(jnp.int32, (Q_BLOCK, Q_BLOCK), 1)
                 <= lax.broadcasted_iota(jnp.int32, (Q_BLOCK, Q_BLOCK), 0))
    nt = (((1,), (1,)), ((), ()))

    def block(c, hh, n, first):
        rows = pl.ds(pl.multiple_of(n * Q_BLOCK, Q_BLOCK), Q_BLOCK)
        if first:
            keys, ok = rows, causal_ok
        else:
            keys, ok = pl.ds(pl.multiple_of((n - 1) * Q_BLOCK, Q_BLOCK), 2 * Q_BLOCK), band_ok
        s = lax.dot_general(q_ref[0, c, hh, rows, :], k_ref[0, c, hh, keys, :], nt, preferred_element_type=F32)
        s = jnp.where(ok, s, NEG)
        m = jnp.max(s, axis=-1, keepdims=True)
        p = jnp.exp(s - m).astype(BF16)
        oe = jnp.dot(p, vx[c, hh, keys, :], preferred_element_type=F32)
        l = oe[:, HEAD_DIM:]
        o_ref[0, c, hh, rows, :] = oe[:, :HEAD_DIM] / l
        l_ref[0, c, hh, rows, :] = m + jnp.log(l)

    for c in range(d_blk):
        for hh in range(h_blk):
            block(c, hh, 0, True)
        if n_q <= 2:
            for n in range(1, n_q):
                for hh in range(h_blk):
                    block(c, hh, n, False)
        else:
            def step(n, carry, c=c):
                for hh in range(h_blk):
                    block(c, hh, n, False)
                return carry

            lax.fori_loop(1, n_q, step, 0, unroll=2)


def _prompt_attn(q, k, v, d_blk, h_blk):
    B, d, H, M, E = q.shape
    blk = (1, d_blk, h_blk, M, E)
    spec = pl.BlockSpec(blk, lambda b, c, hh: (b, c, hh, 0, 0))
    shape = jax.ShapeDtypeStruct(q.shape, F32)
    return pl.pallas_call(
        _attn_body,
        grid=(B, d // d_blk, H // h_blk),
        in_specs=[spec, spec, spec],
        out_specs=[spec, spec],
        out_shape=[shape, shape],
        scratch_shapes=[pltpu.VMEM((d_blk, h_blk, M, 2 * E), BF16)],
        compiler_params=pltpu.CompilerParams(
            dimension_semantics=("arbitrary", "arbitrary", "arbitrary"), vmem_limit_bytes=VMEM_LIMIT),
        name="prompt_attn_d%d" % d,
    )(q, k, v)


def _pool_project(mixed, wp_ref, bp_ref, ps_ref):
    cols = []
    for g in range(len(POOL_WINDOWS)):
        lanes = slice(g * HEAD_DIM, (g + 1) * HEAD_DIM)
        y = jnp.dot(mixed[g].astype(BF16), wp_ref[g], preferred_element_type=F32) + bp_ref[g:g + 1, :]
        cols.append(y * ps_ref[:, lanes])
    return jnp.concatenate(cols, axis=-1)


def _merge_out(x, gate, a, p, z_a, z_b, g_a, g_b, wa_ref, wb_ref, wo_ref):
    ya = jnp.dot((a * _silu(z_a)).astype(BF16), wa_ref[...], preferred_element_type=F32)
    yb = jnp.dot((p * _silu(z_b)).astype(BF16), wb_ref[...], preferred_element_type=F32)
    m = jax.nn.sigmoid(g_a) * ya + jax.nn.sigmoid(g_b) * yb
    return x + gate * jnp.dot(m.astype(BF16), wo_ref[...], preferred_element_type=F32)


def _out_body(x_ref, mod_ref, ng_ref, w_ref, o0, l0, o1, l1, o2, l2, wp_ref, bp_ref, ps_ref,
              wa_ref, wb_ref, wo_ref, new_ref, src_a, src_b, dst_a_in, y_ref, pool_ref, dst_a, dst_b,
              ext, oscr, lscr, buf, rsem, wsem, nsem, *, n_tiles, shift):
    del dst_a_in
    i = pl.program_id(1)
    tm = x_ref.shape[1]
    step = pl.program_id(0) * n_tiles + i
    kinds, lo, hi = shift

    last_slot = lo + SHIFT_SUBSTEPS * pl.num_programs(0) * n_tiles - 1

    def shift_substep(k):
        _shift_substep(lo + SHIFT_SUBSTEPS * step + k, lo, hi, last_slot, kinds, [src_a, src_b], [dst_a, dst_b],
                       buf, rsem, wsem)

    shift_substep(0)

    @pl.when(step == 0)
    def _():
        _new_rows_copy(new_ref, dst_b, nsem).start()

    x = x_ref[0]
    h = _modulated_norm(x, ng_ref[...], mod_ref[0, :, 0:D_MODEL], mod_ref[0, :, D_MODEL:2 * D_MODEL])
    gate = mod_ref[0, :, 2 * D_MODEL:3 * D_MODEL]

    def proj(lo, width):
        return jnp.dot(h, w_ref[:, lo:lo + width], preferred_element_type=F32)

    o_refs, l_refs = (o0, o1, o2), (l0, l1, l2)
    for g, (window, d) in enumerate(ATTN_PATTERNS):
        if d == 1:
            continue
        for hh in range(N_HEADS):
            for rr in range(d):
                rows = pl.ds(rr, tm // d, stride=d)
                oscr[g - 1, hh, rows, :] = o_refs[g][0, rr, hh]
                lscr[g - 1, hh, rows, :] = l_refs[g][0, rr, hh]
    a_cols = []
    for hh in range(N_HEADS):
        os_ = [o0[0, 0, hh], oscr[0, hh], oscr[1, hh]]
        ls_ = [l0[0, 0, hh], lscr[0, hh], lscr[1, hh]]
        mx = jnp.maximum(jnp.maximum(ls_[0], ls_[1]), ls_[2])
        es = [jnp.exp(l_ - mx) for l_ in ls_]
        den = es[0] + es[1] + es[2]
        a_cols.append((es[0] * os_[0] + es[1] * os_[1] + es[2] * os_[2]) / den)
    a = jnp.concatenate(a_cols, axis=-1)

    u = proj(GROUP_W, GROUP_W)

    @pl.when(i == 0)
    def _():
        ext[0:16, :] = jnp.zeros((16, GROUP_W), F32)

    ext[16:16 + tm, :] = u
    pos = i * tm + lax.broadcasted_iota(jnp.int32, (tm, 1), 0)
    mixed = []
    for g, w in enumerate(POOL_WINDOWS):
        lanes = pl.ds(g * HEAD_DIM, HEAD_DIM)
        acc = ext[pl.ds(16, tm), lanes]
        tok = acc
        for k in range(1, w):
            acc = acc + ext[pl.ds(16 - k, tm), lanes]
        cnt = jnp.minimum(w, pos + 1).astype(F32)
        mixed.append(acc / cnt - tok)

    @pl.when(i == n_tiles - 1)
    def _():
        pool_ref[0, 0] = ext[pl.ds(tm + 1, POOL_STATE), :]

    ext[0:16, :] = ext[tm:tm + 16, :]
    p = _pool_project(mixed, wp_ref, bp_ref, ps_ref)
    shift_substep(1)

    @pl.when(step == 0)
    def _():
        _new_rows_copy(new_ref, dst_b, nsem).wait()

    z_a = proj(0, GROUP_W)
    z_b = proj(2 * GROUP_W, GROUP_W)
    g_a = proj(3 * GROUP_W, D_MODEL)
    g_b = proj(3 * GROUP_W + D_MODEL, D_MODEL)
    y_ref[0] = _merge_out(x, gate, a, p, z_a, z_b, g_a, g_b, wa_ref, wb_ref, wo_ref)


def _prompt_out(x, mod_p, norm_g, w_rest, ol, w_pool, b_pool, pool_scale, wa, wb, wo,
                new_rows, state_a, state_b, shifted_a, shift):
    B, L, _ = x.shape
    tm = ROW_TILE
    n_tiles = L // tm
    const2 = lambda b, i: (0, 0)
    one = pl.Buffered(1)
    in_specs = [
        pl.BlockSpec((1, tm, D_MODEL), lambda b, i: (b, i, 0)),
        pl.BlockSpec((1, 1, 3 * D_MODEL), lambda b, i: (b, 0, 0)),
        pl.BlockSpec((1, D_MODEL), const2),
        pl.BlockSpec((D_MODEL, REST_W), const2, pipeline_mode=one),
    ]
    for window, d in ATTN_PATTERNS:
        for _ in range(2):
            in_specs.append(pl.BlockSpec((1, d, N_HEADS, tm // d, HEAD_DIM), lambda b, i: (b, 0, 0, i, 0)))
    in_specs += [
        pl.BlockSpec((len(POOL_WINDOWS), HEAD_DIM, HEAD_DIM), lambda b, i: (0, 0, 0)),
        pl.BlockSpec((len(POOL_WINDOWS), HEAD_DIM), const2),
        pl.BlockSpec((1, GROUP_W), const2),
        pl.BlockSpec((GROUP_W, D_MODEL), const2, pipeline_mode=one),
        pl.BlockSpec((GROUP_W, D_MODEL), const2, pipeline_mode=one),
        pl.BlockSpec((D_MODEL, D_MODEL), const2, pipeline_mode=one),
        pl.BlockSpec(new_rows.shape, lambda b, i: (0, 0, 0), pipeline_mode=one),
    ]
    any_spec = pl.BlockSpec(memory_space=pl.ANY)
    in_specs += [any_spec] * 3
    return pl.pallas_call(
        functools.partial(_out_body, n_tiles=n_tiles, shift=shift),
        grid=(B, n_tiles),
        in_specs=in_specs,
        out_specs=[pl.BlockSpec((1, tm, D_MODEL), lambda b, i: (b, i, 0)),
                   pl.BlockSpec((1, 1, POOL_STATE, GROUP_W), lambda b, i: (0, b, 0, 0)),
                   any_spec, any_spec],
        out_shape=[jax.ShapeDtypeStruct((B, L, D_MODEL), F32),
                   jax.ShapeDtypeStruct((1, B, POOL_STATE, GROUP_W), F32),
                   jax.ShapeDtypeStruct(state_a.shape, F32),
                   jax.ShapeDtypeStruct(state_b.shape, F32)],
        scratch_shapes=[pltpu.VMEM((tm + 16, GROUP_W), F32),
                        pltpu.VMEM((2, N_HEADS, tm, HEAD_DIM), F32),
                        pltpu.VMEM((2, N_HEADS, tm, HEAD_DIM), F32)] + _shift_scratch(),
        input_output_aliases={len(in_specs) - 1: 2},
        compiler_params=pltpu.CompilerParams(
            dimension_semantics=("arbitrary", "arbitrary"), vmem_limit_bytes=OUT_VMEM_LIMIT),
        name="prompt_out",
    )(x, mod_p, norm_g, w_rest, *ol, w_pool, b_pool, pool_scale, wa, wb, wo,
      new_rows, state_a, state_b, shifted_a)


def _sample_proj_body(x_ref, mod_ref, ng_ref, w_ref, o_ref):
    h = _modulated_norm(x_ref[...], ng_ref[...], mod_ref[:, 0:D_MODEL], mod_ref[:, D_MODEL:2 * D_MODEL])
    o_ref[...] = jnp.dot(h, w_ref[...], preferred_element_type=F32)


def _sample_proj(xs, mod_s, norm_g, w):
    n, width = xs.shape[0], w.shape[1]
    tn = GROUP_W
    return pl.pallas_call(
        _sample_proj_body,
        grid=(width // tn,),
        in_specs=[pl.BlockSpec((n, D_MODEL), lambda j: (0, 0)),
                  pl.BlockSpec((n, 3 * D_MODEL), lambda j: (0, 0)),
                  pl.BlockSpec((1, D_MODEL), lambda j: (0, 0)),
                  pl.BlockSpec((D_MODEL, tn), lambda j: (0, j))],
        out_specs=pl.BlockSpec((n, tn), lambda j: (0, j)),
        out_shape=jax.ShapeDtypeStruct((n, width), F32),
        name="sample_proj",
    )(xs, mod_s, norm_g, w)


def _sample_attn_body(p_ref, s0, s1, s2, qg_ref, kg_ref, a_ref, shifted0, n1, n2):
    bt = p_ref.shape[0]
    s_refs = (s0, s1, s2)
    last = N_BACK - 1
    for b in range(bt):
        os_, ls_ = [], []
        shifted0[b, 0:last] = s0[b, 1:N_BACK]
        for g in range(N_GROUPS):
            row = lambda t: pl.ds((t * N_GROUPS + g) * N_HEADS, N_HEADS)
            q4 = _head_norm(p_ref[b, row(0), :], qg_ref[g:g + 1, :]) * Q_SCALE
            k4 = _head_norm(p_ref[b, row(1), :], kg_ref[g:g + 1, :])
            v4 = p_ref[b, row(2), :]
            if g == 0:
                shifted0[b, last, 0:N_HEADS, :] = k4
                shifted0[b, last, N_HEADS:2 * N_HEADS, :] = v4
            else:
                (n1, n2)[g - 1][b, 0:N_HEADS, :] = k4
                (n1, n2)[g - 1][b, N_HEADS:2 * N_HEADS, :] = v4
            keys = s_refs[g][b, :, 0:N_HEADS, :]
            vals = s_refs[g][b, :, N_HEADS:2 * N_HEADS, :]
            s_old = jnp.sum(keys * q4[None], axis=-1, keepdims=True)
            s_new = jnp.sum(k4 * q4, axis=-1, keepdims=True)
            m = jnp.maximum(jnp.max(s_old, axis=0), s_new)
            p_old = jnp.exp(s_old - m[None])
            p_new = jnp.exp(s_new - m)
            l = jnp.sum(p_old, axis=0) + p_new
            o = jnp.sum(p_old * vals, axis=0) + p_new * v4
            os_.append(o / l)
            ls_.append(m + jnp.log(l))
        mx = jnp.maximum(jnp.maximum(ls_[0], ls_[1]), ls_[2])
        es = [jnp.exp(l_ - mx) for l_ in ls_]
        a_ref[b] = (es[0] * os_[0] + es[1] * os_[1] + es[2] * os_[2]) / (es[0] + es[1] + es[2])


def _sample_attn(p3, views, q_gain, k_gain):
    n = p3.shape[0]
    bt = 8
    in_specs = [pl.BlockSpec((bt, p3.shape[1], HEAD_DIM), lambda j: (j, 0, 0))]
    for v in views:
        in_specs.append(pl.BlockSpec((bt, N_BACK, 2 * N_HEADS, HEAD_DIM), lambda j: (j, 0, 0, 0)))
    in_specs += [pl.BlockSpec((N_GROUPS, HEAD_DIM), lambda j: (0, 0))] * 2
    new_spec = pl.BlockSpec((bt, 2 * N_HEADS, HEAD_DIM), lambda j: (j, 0, 0))
    new_shape = jax.ShapeDtypeStruct((n, 2 * N_HEADS, HEAD_DIM), F32)
    state_spec = pl.BlockSpec((bt, N_BACK, 2 * N_HEADS, HEAD_DIM), lambda j: (j, 0, 0, 0))
    return pl.pallas_call(
        _sample_attn_body,
        grid=(n // bt,),
        in_specs=in_specs,
        out_specs=[pl.BlockSpec((bt, N_HEADS, HEAD_DIM), lambda j: (j, 0, 0)), state_spec, new_spec, new_spec],
        out_shape=[jax.ShapeDtypeStruct((n, N_HEADS, HEAD_DIM), F32),
                   jax.ShapeDtypeStruct(views[0].shape, F32), new_shape, new_shape],
        compiler_params=pltpu.CompilerParams(vmem_limit_bytes=VMEM_LIMIT),
        name="sample_attn",
    )(p3, *views, q_gain, k_gain)


def _sample_out_body(x_ref, mod_ref, r_ref, a_ref, sp_ref, wp_ref, bp_ref, ps_ref, wa_ref, wb_ref, wo_ref,
                     y_ref, pool_ref):
    u = r_ref[:, GROUP_W:2 * GROUP_W]
    mixed = []
    for g, w in enumerate(POOL_WINDOWS):
        lanes = pl.ds(g * HEAD_DIM, HEAD_DIM)
        tok = r_ref[:, pl.ds(GROUP_W + g * HEAD_DIM, HEAD_DIM)]
        acc = tok
        for k in range(1, w):
            acc = acc + sp_ref[POOL_STATE - k, :, lanes]
        mixed.append(acc / float(min(w, PAST_LEN + 1)) - tok)
    p = _pool_project(mixed, wp_ref, bp_ref, ps_ref)
    for k in range(POOL_STATE - 1):
        pool_ref[k] = sp_ref[k + 1]
    pool_ref[POOL_STATE - 1] = u
    y_ref[...] = _merge_out(
        x_ref[...], mod_ref[:, 2 * D_MODEL:3 * D_MODEL], a_ref[...], p,
        r_ref[:, 0:GROUP_W], r_ref[:, 2 * GROUP_W:3 * GROUP_W],
        r_ref[:, 3 * GROUP_W:3 * GROUP_W + D_MODEL], r_ref[:, 3 * GROUP_W + D_MODEL:REST_W],
        wa_ref, wb_ref, wo_ref)


def _sample_out(xs, mod_s, rest, a_s, sp_t, w_pool, b_pool, pool_scale, wa, wb, wo):
    n = xs.shape[0]
    return pl.pallas_call(
        _sample_out_body,
        out_shape=[jax.ShapeDtypeStruct((n, D_MODEL), F32),
                   jax.ShapeDtypeStruct((POOL_STATE, n, GROUP_W), F32)],
        compiler_params=pltpu.CompilerParams(vmem_limit_bytes=VMEM_LIMIT),
        name="sample_out",
    )(xs, mod_s, rest, a_s, sp_t, w_pool, b_pool, pool_scale, wa, wb, wo)


def kernel(x_prompt, x_sample, state_kv_w128, state_kv_w512, state_kv_w2048, state_pool, c_prompt, c_sample,
           norm_g, w_ada, b_ada, w_in, q_gain, k_gain, w_pool, b_pool, pool_scale, w_a_out, w_b_out, w_out):
    B, L, _ = x_prompt.shape
    n_s = x_sample.shape[0]
    w_qkv = w_in[0, :, :QKV_W].astype(BF16)
    w_rest = w_in[0, :, QKV_W:].astype(BF16)
    wa, wb, wo, wp = (w[0].astype(BF16) for w in (w_a_out, w_b_out, w_out, w_pool))
    qg, kg, bp = q_gain[0], k_gain[0], b_pool[0]

    mod = _ada(jnp.concatenate([c_prompt, c_sample], axis=0), w_ada[0], b_ada)
    mod_p = mod[:B].reshape(B, 1, 3 * D_MODEL)
    mod_s = mod[B:]

    xs = x_sample[:, 0, :]
    qkv_s = _sample_proj(xs, mod_s, norm_g, w_qkv)
    rest_s = _sample_proj(xs, mod_s, norm_g, w_rest)
    states = (state_kv_w128, state_kv_w512, state_kv_w2048)
    views = [s.reshape(n_s, N_BACK, d * ROWS_PER_POS, HEAD_DIM) for s, (_, d) in zip(states, ATTN_PATTERNS)]
    a_s, shifted_128, new_512, new_2048 = _sample_attn(
        qkv_s.reshape(n_s, QKV_W // HEAD_DIM, HEAD_DIM), views, qg, kg)
    y_s, pool_t = _sample_out(xs, mod_s, rest_s, a_s.reshape(n_s, GROUP_W), state_pool[0].transpose(1, 0, 2),
                              wp, bp, pool_scale, wa, wb, wo)

    flat_2048, flat_512 = (s.reshape(n_s, s.shape[2] * ROWS_PER_POS, HEAD_DIM) for s in states[:0:-1])
    kinds = _shift_plan([flat_2048.shape, flat_512.shape])
    n_sub = B * (L // ROW_TILE) * SHIFT_SUBSTEPS
    n_chunks = kinds[-1][-1]
    assert n_sub <= kinds[0][4] and n_chunks <= 2 * n_sub
    *qkv, t0, t1, t2, part_2048 = _prompt_qkv(x_prompt, mod_p, norm_g, w_qkv, qg, kg, new_2048, flat_2048,
                                              (kinds, 0, n_sub))
    ol = []
    for g, (d_blk, h_blk) in enumerate(((1, 2), (1, 4), (4, 4))):
        ol += _prompt_attn(qkv[g], qkv[N_GROUPS + g], qkv[2 * N_GROUPS + g], d_blk, h_blk)
    y_p, pool_p, shifted_2048, shifted_512 = _prompt_out(
        x_prompt, mod_p, norm_g, w_rest, ol, wp, bp, pool_scale, wa, wb, wo,
        new_512, flat_2048, flat_512, part_2048, (kinds, n_sub, n_chunks))
    kv_s = [o.reshape(s.shape) for o, s in zip((shifted_128, shifted_512, shifted_2048), states)]
    kv_p = [t.reshape(1, B, t.shape[1], 2, N_HEADS, HEAD_DIM) for t in (t0, t1, t2)]

    return (y_p, y_s.reshape(n_s, 1, D_MODEL), kv_p[0], kv_p[1], kv_p[2], pool_p,
            kv_s[0], kv_s[1], kv_s[2], pool_t.transpose(1, 0, 2)[None])
```

```python
import functools

import jax
import jax.numpy as jnp
from jax import lax
from jax.experimental import pallas as pl
from jax.experimental.pallas import tpu as pltpu

F32 = jnp.float32
BF16 = jnp.bfloat16

D_MODEL = 1024
HEAD_DIM = 128
N_HEADS = 4
GROUP_W = N_HEADS * HEAD_DIM
ATTN_PATTERNS = ((128, 1), (512, 4), (2048, 16))
N_GROUPS = len(ATTN_PATTERNS)
N_BACK = 128
QKV_W = 3 * N_GROUPS * GROUP_W
REST_W = 3 * GROUP_W + 2 * D_MODEL
POOL_WINDOWS = (2, 4, 8, 16)
POOL_STATE = 15
PAST_LEN = 8192
EPS = 1e-6
Q_SCALE = HEAD_DIM ** -0.5
NEG = -1e30

ROW_TILE = 512
Q_BLOCK = 128
VMEM_LIMIT = 56 * 1024 * 1024
OUT_VMEM_LIMIT = 60 * 1024 * 1024


def _silu(v):
    return v * jax.nn.sigmoid(v)


def _modulated_norm(x, norm_g, shift, scale):
    ms = jnp.mean(x * x, axis=-1, keepdims=True)
    return (x * lax.rsqrt(ms + EPS) * norm_g * (1.0 + scale) + shift).astype(BF16)


def _head_norm(r, gain):
    ms = jnp.mean(r * r, axis=-1, keepdims=True)
    return r * lax.rsqrt(ms + EPS) * gain


def _ada_body(c_ref, w_ref, b_ref, o_ref):
    s = _silu(c_ref[...]).astype(BF16)
    o_ref[...] = jnp.dot(s, w_ref[...].astype(BF16), preferred_element_type=F32) + b_ref[...]


def _ada(c_all, w_ada, b_ada):
    n = c_all.shape[0]
    return pl.pallas_call(
        _ada_body,
        grid=(3,),
        in_specs=[pl.BlockSpec((n, D_MODEL), lambda j: (0, 0)),
                  pl.BlockSpec((D_MODEL, D_MODEL), lambda j: (0, j)),
                  pl.BlockSpec((1, D_MODEL), lambda j: (0, j))],
        out_specs=pl.BlockSpec((n, D_MODEL), lambda j: (0, j)),
        out_shape=jax.ShapeDtypeStruct((n, 3 * D_MODEL), F32),
        name="ada",
    )(c_all, w_ada, b_ada)


ROWS_PER_POS = 2 * N_HEADS
SHIFT_NB = 8
SHIFT_ROWS = 712
SHIFT_SUBSTEPS = 4
SHIFT_AHEAD = 2
SHIFT_SLOTS = 2 * SHIFT_AHEAD


def _shift_plan(state_shapes):
    kinds, c0 = [], 0
    for idx, (n_batch, rows_total, _) in enumerate(state_shapes):
        keep = rows_total - ROWS_PER_POS
        pieces = min(p for p in range(1, keep)
                     if keep % (p * ROWS_PER_POS) == 0 and keep // p <= SHIFT_ROWS)
        n = (n_batch // SHIFT_NB) * pieces
        kinds.append((idx, keep // pieces, pieces, c0, c0 + n))
        c0 += n
    return kinds


def _shift_substep(j, lo, hi, last_slot, kinds, srcs, dsts, buf, rsem, wsem):
    def copy(jj, kind, write):
        idx, rows, pieces, c0, _ = kind
        local = jj - c0
        batches = pl.ds((local // pieces) * SHIFT_NB, SHIFT_NB)
        first = (local % pieces) * rows
        slot = jj % SHIFT_SLOTS
        stage = buf.at[slot, :, pl.ds(0, rows), :]
        if write:
            return pltpu.make_async_copy(stage, dsts[idx].at[batches, pl.ds(first, rows), :], wsem.at[slot])
        return pltpu.make_async_copy(srcs[idx].at[batches, pl.ds(first + ROWS_PER_POS, rows), :], stage, rsem.at[slot])

    def for_chunk(jj, also, fn):
        for kind in kinds:
            first, end = max(kind[3], lo), min(kind[4], hi)
            if first < end:
                cond = (jj >= first) & (jj < end)

                @pl.when(cond if also is None else cond & also)
                def _():
                    fn(functools.partial(copy, jj, kind))

    for a in range(SHIFT_AHEAD):
        for_chunk(j + a, j == lo, lambda cp: cp(False).start())

    def landed(cp):
        cp(False).wait()
        cp(True).start()

    for_chunk(j, None, landed)
    for_chunk(j - SHIFT_AHEAD, None, lambda cp: cp(True).wait())
    for_chunk(j + SHIFT_AHEAD, None, lambda cp: cp(False).start())
    for a in range(SHIFT_AHEAD):
        for_chunk(j - a, j == last_slot, lambda cp: cp(True).wait())


def _new_rows_copy(new_ref, dst, nsem):
    keep = dst.shape[1] - ROWS_PER_POS
    return pltpu.make_async_copy(new_ref, dst.at[:, pl.ds(keep, ROWS_PER_POS), :], nsem)


def _shift_scratch():
    return [pltpu.VMEM((SHIFT_SLOTS, SHIFT_NB, SHIFT_ROWS, HEAD_DIM), F32),
            pltpu.SemaphoreType.DMA((SHIFT_SLOTS,)), pltpu.SemaphoreType.DMA((SHIFT_SLOTS,)),
            pltpu.SemaphoreType.DMA(())]


def _class_major_perm(tm, d):
    i = jnp.arange(tm)
    src_row = (i % (tm // d)) * d + i // (tm // d)
    return (src_row[:, None] == jnp.arange(tm)[None, :]).astype(BF16)


SHIFT_SITES = ((0, 1), (1, 1), (2, 0))


def _qkv_body(x_ref, mod_ref, ng_ref, w_ref, qg_ref, kg_ref, perm_ref, new_ref, src_ref, *refs, n_tiles, shift):
    qkv_refs = (refs[0:3], refs[3:6], refs[6:9])
    tail_refs = refs[9:12]
    dst_ref, buf, rsem, wsem, nsem = refs[12:17]
    i = pl.program_id(1)
    tm = x_ref.shape[1]
    step = pl.program_id(0) * n_tiles + i
    kinds, lo, hi = shift

    last_slot = lo + SHIFT_SUBSTEPS * pl.num_programs(0) * n_tiles - 1

    def shift_substep(k):
        _shift_substep(lo + SHIFT_SUBSTEPS * step + k, lo, hi, last_slot, kinds, [src_ref], [dst_ref],
                       buf, rsem, wsem)

    shift_substep(0)

    @pl.when(step == 0)
    def _():
        _new_rows_copy(new_ref, dst_ref, nsem).start()

    h = _modulated_norm(x_ref[0], ng_ref[...], mod_ref[0, :, 0:D_MODEL], mod_ref[0, :, D_MODEL:2 * D_MODEL])
    h_by_group = [h] + [jnp.dot(perm_ref[g - 1], h, preferred_element_type=F32).astype(BF16)
                        for g in range(1, N_GROUPS)]

    def heads(t, g, lhs):
        c = t * N_GROUPS + g
        res = jnp.dot(lhs, w_ref[:, c * GROUP_W:(c + 1) * GROUP_W], preferred_element_type=F32)
        out = []
        for hh in range(N_HEADS):
            r = res[:, hh * HEAD_DIM:(hh + 1) * HEAD_DIM]
            if t == 0:
                r = _head_norm(r, qg_ref[g:g + 1, :]) * Q_SCALE
            elif t == 1:
                r = _head_norm(r, kg_ref[g:g + 1, :])
            out.append(r)
        return out

    def write_tail(t, g, rs):
        window = ATTN_PATTERNS[g][0]
        for hh, r in enumerate(rs):
            row = (t - 1) * N_HEADS + hh
            tail_refs[g][0, :, row, :] = r if window >= tm else r[tm - window:, :]

    for t in range(3):
        for g, (window, d) in enumerate(ATTN_PATTERNS):
            if (t, g) in SHIFT_SITES:
                shift_substep(SHIFT_SITES.index((t, g)) + 1)
            if (t, g) == SHIFT_SITES[0]:
                @pl.when(step == 0)
                def _():
                    _new_rows_copy(new_ref, dst_ref, nsem).wait()

            rs = heads(t, g, h_by_group[g])
            per = tm // d
            for hh, r in enumerate(rs):
                for rr in range(d):
                    qkv_refs[t][g][0, rr, hh] = r[rr * per:(rr + 1) * per, :].astype(BF16)
            if t > 0:
                n_tail = max(window // tm, 1)

                @pl.when(i >= n_tiles - n_tail)
                def _():
                    write_tail(t, g, rs if d == 1 else heads(t, g, h))


def _prompt_qkv(x, mod_p, norm_g, w_qkv, q_gain, k_gain, new_rows, state, shift):
    B, L, _ = x.shape
    tm = ROW_TILE
    n_tiles = L // tm
    const2 = lambda b, i: (0, 0)
    in_specs = [
        pl.BlockSpec((1, tm, D_MODEL), lambda b, i: (b, i, 0)),
        pl.BlockSpec((1, 1, 3 * D_MODEL), lambda b, i: (b, 0, 0)),
        pl.BlockSpec((1, D_MODEL), const2),
        pl.BlockSpec((D_MODEL, QKV_W), const2, pipeline_mode=pl.Buffered(1)),
        pl.BlockSpec((N_GROUPS, HEAD_DIM), const2),
        pl.BlockSpec((N_GROUPS, HEAD_DIM), const2),
        pl.BlockSpec((N_GROUPS - 1, tm, tm), lambda b, i: (0, 0, 0), pipeline_mode=pl.Buffered(1)),
        pl.BlockSpec(new_rows.shape, lambda b, i: (0, 0, 0), pipeline_mode=pl.Buffered(1)),
        pl.BlockSpec(memory_space=pl.ANY),
    ]
    perms = jnp.stack([_class_major_perm(tm, d) for _, d in ATTN_PATTERNS[1:]])
    out_specs, out_shape = [], []
    for t in range(3):
        for window, d in ATTN_PATTERNS:
            out_specs.append(pl.BlockSpec((1, d, N_HEADS, tm // d, HEAD_DIM), lambda b, i: (b, 0, 0, i, 0)))
            out_shape.append(jax.ShapeDtypeStruct((B, d, N_HEADS, L // d, HEAD_DIM), BF16))
    for window, d in ATTN_PATTERNS:
        w_eff = min(window, L)
        if w_eff >= tm:
            n_tail = w_eff // tm
            out_specs.append(pl.BlockSpec(
                (1, tm, 2 * N_HEADS, HEAD_DIM),
                lambda b, i, n_tail=n_tail: (b, jnp.maximum(i - (n_tiles - n_tail), 0), 0, 0)))
        else:
            out_specs.append(pl.BlockSpec((1, w_eff, 2 * N_HEADS, HEAD_DIM), lambda b, i: (b, 0, 0, 0)))
        out_shape.append(jax.ShapeDtypeStruct((B, w_eff, 2 * N_HEADS, HEAD_DIM), F32))
    out_specs.append(pl.BlockSpec(memory_space=pl.ANY))
    out_shape.append(jax.ShapeDtypeStruct(state.shape, F32))
    return pl.pallas_call(
        functools.partial(_qkv_body, n_tiles=n_tiles, shift=shift),
        grid=(B, n_tiles),
        in_specs=in_specs,
        out_specs=out_specs,
        out_shape=out_shape,
        scratch_shapes=_shift_scratch(),
        compiler_params=pltpu.CompilerParams(
            dimension_semantics=("arbitrary", "arbitrary"), vmem_limit_bytes=VMEM_LIMIT),
        name="prompt_qkv",
    )(x, mod_p, norm_g, w_qkv, q_gain, k_gain, perms, new_rows, state)


def _attn_body(q_ref, k_ref, v_ref, o_ref, l_ref, vx):
    _, d_blk, h_blk, M, _ = q_ref.shape
    n_q = M // Q_BLOCK
    for c in range(d_blk):
        for hh in range(h_blk):
            vx[c, hh, :, 0:HEAD_DIM] = v_ref[0, c, hh]
            vx[c, hh, :, HEAD_DIM:2 * HEAD_DIM] = jnp.ones((M, HEAD_DIM), BF16)
    ii = lax.broadcasted_iota(jnp.int32, (Q_BLOCK, 2 * Q_BLOCK), 0)
    jj = lax.broadcasted_iota(jnp.int32, (Q_BLOCK, 2 * Q_BLOCK), 1)
    dist = Q_BLOCK + ii - jj
    band_ok = (dist >= 0) & (dist <= N_BACK)
    causal_ok = (lax.broadcasted_iota(jnp.int32, (Q_BLOCK, Q_BLOCK), 1)
                 <= lax.broadcasted_iota(jnp.int32, (Q_BLOCK, Q_BLOCK), 0))
    nt = (((1,), (1,)), ((), ()))

    def block(c, hh, n, first):
        rows = pl.ds(pl.multiple_of(n * Q_BLOCK, Q_BLOCK), Q_BLOCK)
        if first:
            keys, ok = rows, causal_ok
        else:
            keys, ok = pl.ds(pl.multiple_of((n - 1) * Q_BLOCK, Q_BLOCK), 2 * Q_BLOCK), band_ok
        s = lax.dot_general(q_ref[0, c, hh, rows, :], k_ref[0, c, hh, keys, :], nt, preferred_element_type=F32)
        s = jnp.where(ok, s, NEG)
        m = jnp.max(s, axis=-1, keepdims=True)
        p = jnp.exp(s - m).astype(BF16)
        oe = jnp.dot(p, vx[c, hh, keys, :], preferred_element_type=F32)
        l = oe[:, HEAD_DIM:]
        o_ref[0, c, hh, rows, :] = oe[:, :HEAD_DIM] / l
        l_ref[0, c, hh, rows, :] = m + jnp.log(l)

    for c in range(d_blk):
        for hh in range(h_blk):
            block(c, hh, 0, True)
        if n_q <= 2:
            for n in range(1, n_q):
                for hh in range(h_blk):
                    block(c, hh, n, False)
        else:
            def step(n, carry, c=c):
                for hh in range(h_blk):
                    block(c, hh, n, False)
                return carry

            lax.fori_loop(1, n_q, step, 0, unroll=2)


def _prompt_attn(q, k, v, d_blk, h_blk):
    B, d, H, M, E = q.shape
    blk = (1, d_blk, h_blk, M, E)
    spec = pl.BlockSpec(blk, lambda b, c, hh: (b, c, hh, 0, 0))
    shape = jax.ShapeDtypeStruct(q.shape, F32)
    return pl.pallas_call(
        _attn_body,
        grid=(B, d // d_blk, H // h_blk),
        in_specs=[spec, spec, spec],
        out_specs=[spec, spec],
        out_shape=[shape, shape],
        scratch_shapes=[pltpu.VMEM((d_blk, h_blk, M, 2 * E), BF16)],
        compiler_params=pltpu.CompilerParams(
            dimension_semantics=("arbitrary", "arbitrary", "arbitrary"), vmem_limit_bytes=VMEM_LIMIT),
        name="prompt_attn_d%d" % d,
    )(q, k, v)


def _pool_project(mixed, wp_ref, bp_ref, ps_ref):
    cols = []
    for g in range(len(POOL_WINDOWS)):
        lanes = slice(g * HEAD_DIM, (g + 1) * HEAD_DIM)
        y = jnp.dot(mixed[g].astype(BF16), wp_ref[g], preferred_element_type=F32) + bp_ref[g:g + 1, :]
        cols.append(y * ps_ref[:, lanes])
    return jnp.concatenate(cols, axis=-1)


def _merge_out(x, gate, a, p, z_a, z_b, g_a, g_b, wa_ref, wb_ref, wo_ref):
    ya = jnp.dot((a * _silu(z_a)).astype(BF16), wa_ref[...], preferred_element_type=F32)
    yb = jnp.dot((p * _silu(z_b)).astype(BF16), wb_ref[...], preferred_element_type=F32)
    m = jax.nn.sigmoid(g_a) * ya + jax.nn.sigmoid(g_b) * yb
    return x + gate * jnp.dot(m.astype(BF16), wo_ref[...], preferred_element_type=F32)


def _out_body(x_ref, mod_ref, ng_ref, w_ref, o0, l0, o1, l1, o2, l2, wp_ref, bp_ref, ps_ref,
              wa_ref, wb_ref, wo_ref, new_ref, src_a, src_b, dst_a_in, y_ref, pool_ref, dst_a, dst_b,
              ext, oscr, lscr, buf, rsem, wsem, nsem, *, n_tiles, shift):
    del dst_a_in
    i = pl.program_id(1)
    tm = x_ref.shape[1]
    step = pl.program_id(0) * n_tiles + i
    kinds, lo, hi = shift

    last_slot = lo + SHIFT_SUBSTEPS * pl.num_programs(0) * n_tiles - 1

    def shift_substep(k):
        _shift_substep(lo + SHIFT_SUBSTEPS * step + k, lo, hi, last_slot, kinds, [src_a, src_b], [dst_a, dst_b],
                       buf, rsem, wsem)

    shift_substep(0)

    @pl.when(step == 0)
    def _():
        _new_rows_copy(new_ref, dst_b, nsem).start()

    x = x_ref[0]
    h = _modulated_norm(x, ng_ref[...], mod_ref[0, :, 0:D_MODEL], mod_ref[0, :, D_MODEL:2 * D_MODEL])
    gate = mod_ref[0, :, 2 * D_MODEL:3 * D_MODEL]

    def proj(lo, width):
        return jnp.dot(h, w_ref[:, lo:lo + width], preferred_element_type=F32)

    o_refs, l_refs = (o0, o1, o2), (l0, l1, l2)
    for g, (window, d) in enumerate(ATTN_PATTERNS):
        if d == 1:
            continue
        for hh in range(N_HEADS):
            for rr in range(d):
                rows = pl.ds(rr, tm // d, stride=d)
                oscr[g - 1, hh, rows, :] = o_refs[g][0, rr, hh]
                lscr[g - 1, hh, rows, :] = l_refs[g][0, rr, hh]
    a_cols = []
    for hh in range(N_HEADS):
        os_ = [o0[0, 0, hh], oscr[0, hh], oscr[1, hh]]
        ls_ = [l0[0, 0, hh], lscr[0, hh], lscr[1, hh]]
        mx = jnp.maximum(jnp.maximum(ls_[0], ls_[1]), ls_[2])
        es = [jnp.exp(l_ - mx) for l_ in ls_]
        den = es[0] + es[1] + es[2]
        a_cols.append((es[0] * os_[0] + es[1] * os_[1] + es[2] * os_[2]) / den)
    a = jnp.concatenate(a_cols, axis=-1)
    shift_substep(1)

    @pl.when(step == 0)
    def _():
        _new_rows_copy(new_ref, dst_b, nsem).wait()

    u = proj(GROUP_W, GROUP_W)

    @pl.when(i == 0)
    def _():
        ext[0:16, :] = jnp.zeros((16, GROUP_W), F32)

    ext[16:16 + tm, :] = u
    pos = i * tm + lax.broadcasted_iota(jnp.int32, (tm, 1), 0)
    mixed = []
    for g, w in enumerate(POOL_WINDOWS):
        lanes = pl.ds(g * HEAD_DIM, HEAD_DIM)
        acc = ext[pl.ds(16, tm), lanes]
        tok = acc
        for k in range(1, w):
            acc = acc + ext[pl.ds(16 - k, tm), lanes]
        cnt = jnp.minimum(w, pos + 1).astype(F32)
        mixed.append(acc / cnt - tok)

    @pl.when(i == n_tiles - 1)
    def _():
        pool_ref[0, 0] = ext[pl.ds(tm + 1, POOL_STATE), :]

    ext[0:16, :] = ext[tm:tm + 16, :]
    p = _pool_project(mixed, wp_ref, bp_ref, ps_ref)
    shift_substep(2)
    z_a = proj(0, GROUP_W)
    z_b = proj(2 * GROUP_W, GROUP_W)
    g_a = proj(3 * GROUP_W, D_MODEL)
    shift_substep(3)
    g_b = proj(3 * GROUP_W + D_MODEL, D_MODEL)
    y_ref[0] = _merge_out(x, gate, a, p, z_a, z_b, g_a, g_b, wa_ref, wb_ref, wo_ref)


def _prompt_out(x, mod_p, norm_g, w_rest, ol, w_pool, b_pool, pool_scale, wa, wb, wo,
                new_rows, state_a, state_b, shifted_a, shift):
    B, L, _ = x.shape
    tm = ROW_TILE
    n_tiles = L // tm
    const2 = lambda b, i: (0, 0)
    one = pl.Buffered(1)
    in_specs = [
        pl.BlockSpec((1, tm, D_MODEL), lambda b, i: (b, i, 0)),
        pl.BlockSpec((1, 1, 3 * D_MODEL), lambda b, i: (b, 0, 0)),
        pl.BlockSpec((1, D_MODEL), const2),
        pl.BlockSpec((D_MODEL, REST_W), const2, pipeline_mode=one),
    ]
    for window, d in ATTN_PATTERNS:
        for _ in range(2):
            in_specs.append(pl.BlockSpec((1, d, N_HEADS, tm // d, HEAD_DIM), lambda b, i: (b, 0, 0, i, 0)))
    in_specs += [
        pl.BlockSpec((len(POOL_WINDOWS), HEAD_DIM, HEAD_DIM), lambda b, i: (0, 0, 0)),
        pl.BlockSpec((len(POOL_WINDOWS), HEAD_DIM), const2),
        pl.BlockSpec((1, GROUP_W), const2),
        pl.BlockSpec((GROUP_W, D_MODEL), const2, pipeline_mode=one),
        pl.BlockSpec((GROUP_W, D_MODEL), const2, pipeline_mode=one),
        pl.BlockSpec((D_MODEL, D_MODEL), const2, pipeline_mode=one),
        pl.BlockSpec(new_rows.shape, lambda b, i: (0, 0, 0), pipeline_mode=one),
    ]
    any_spec = pl.BlockSpec(memory_space=pl.ANY)
    in_specs += [any_spec] * 3
    return pl.pallas_call(
        functools.partial(_out_body, n_tiles=n_tiles, shift=shift),
        grid=(B, n_tiles),
        in_specs=in_specs,
        out_specs=[pl.BlockSpec((1, tm, D_MODEL), lambda b, i: (b, i, 0)),
                   pl.BlockSpec((1, 1, POOL_STATE, GROUP_W), lambda b, i: (0, b, 0, 0)),
                   any_spec, any_spec],
        out_shape=[jax.ShapeDtypeStruct((B, L, D_MODEL), F32),
                   jax.ShapeDtypeStruct((1, B, POOL_STATE, GROUP_W), F32),
                   jax.ShapeDtypeStruct(state_a.shape, F32),
                   jax.ShapeDtypeStruct(state_b.shape, F32)],
        scratch_shapes=[pltpu.VMEM((tm + 16, GROUP_W), F32),
                        pltpu.VMEM((2, N_HEADS, tm, HEAD_DIM), F32),
                        pltpu.VMEM((2, N_HEADS, tm, HEAD_DIM), F32)] + _shift_scratch(),
        input_output_aliases={len(in_specs) - 1: 2},
        compiler_params=pltpu.CompilerParams(
            dimension_semantics=("arbitrary", "arbitrary"), vmem_limit_bytes=OUT_VMEM_LIMIT),
        name="prompt_out",
    )(x, mod_p, norm_g, w_rest, *ol, w_pool, b_pool, pool_scale, wa, wb, wo,
      new_rows, state_a, state_b, shifted_a)


def _sample_proj_body(x_ref, mod_ref, ng_ref, w_ref, o_ref):
    h = _modulated_norm(x_ref[...], ng_ref[...], mod_ref[:, 0:D_MODEL], mod_ref[:, D_MODEL:2 * D_MODEL])
    o_ref[...] = jnp.dot(h, w_ref[...], preferred_element_type=F32)


def _sample_proj(xs, mod_s, norm_g, w):
    n, width = xs.shape[0], w.shape[1]
    tn = GROUP_W
    return pl.pallas_call(
        _sample_proj_body,
        grid=(width // tn,),
        in_specs=[pl.BlockSpec((n, D_MODEL), lambda j: (0, 0)),
                  pl.BlockSpec((n, 3 * D_MODEL), lambda j: (0, 0)),
                  pl.BlockSpec((1, D_MODEL), lambda j: (0, 0)),
                  pl.BlockSpec((D_MODEL, tn), lambda j: (0, j))],
        out_specs=pl.BlockSpec((n, tn), lambda j: (0, j)),
        out_shape=jax.ShapeDtypeStruct((n, width), F32),
        name="sample_proj",
    )(xs, mod_s, norm_g, w)


def _sample_attn_body(p_ref, s0, s1, s2, qg_ref, kg_ref, a_ref, shifted0, n1, n2):
    bt = p_ref.shape[0]
    s_refs = (s0, s1, s2)
    last = N_BACK - 1
    for b in range(bt):
        os_, ls_ = [], []
        shifted0[b, 0:last] = s0[b, 1:N_BACK]
        for g in range(N_GROUPS):
            row = lambda t: pl.ds((t * N_GROUPS + g) * N_HEADS, N_HEADS)
            q4 = _head_norm(p_ref[b, row(0), :], qg_ref[g:g + 1, :]) * Q_SCALE
            k4 = _head_norm(p_ref[b, row(1), :], kg_ref[g:g + 1, :])
            v4 = p_ref[b, row(2), :]
            if g == 0:
                shifted0[b, last, 0:N_HEADS, :] = k4
                shifted0[b, last, N_HEADS:2 * N_HEADS, :] = v4
            else:
                (n1, n2)[g - 1][b, 0:N_HEADS, :] = k4
                (n1, n2)[g - 1][b, N_HEADS:2 * N_HEADS, :] = v4
            keys = s_refs[g][b, :, 0:N_HEADS, :]
            vals = s_refs[g][b, :, N_HEADS:2 * N_HEADS, :]
            s_old = jnp.sum(keys * q4[None], axis=-1, keepdims=True)
            s_new = jnp.sum(k4 * q4, axis=-1, keepdims=True)
            m = jnp.maximum(jnp.max(s_old, axis=0), s_new)
            p_old = jnp.exp(s_old - m[None])
            p_new = jnp.exp(s_new - m)
            l = jnp.sum(p_old, axis=0) + p_new
            o = jnp.sum(p_old * vals, axis=0) + p_new * v4
            os_.append(o / l)
            ls_.append(m + jnp.log(l))
        mx = jnp.maximum(jnp.maximum(ls_[0], ls_[1]), ls_[2])
        es = [jnp.exp(l_ - mx) for l_ in ls_]
        a_ref[b] = (es[0] * os_[0] + es[1] * os_[1] + es[2] * os_[2]) / (es[0] + es[1] + es[2])


def _sample_attn(p3, views, q_gain, k_gain):
    n = p3.shape[0]
    bt = 8
    in_specs = [pl.BlockSpec((bt, p3.shape[1], HEAD_DIM), lambda j: (j, 0, 0))]
    for v in views:
        in_specs.append(pl.BlockSpec((bt, N_BACK, 2 * N_HEADS, HEAD_DIM), lambda j: (j, 0, 0, 0)))
    in_specs += [pl.BlockSpec((N_GROUPS, HEAD_DIM), lambda j: (0, 0))] * 2
    new_spec = pl.BlockSpec((bt, 2 * N_HEADS, HEAD_DIM), lambda j: (j, 0, 0))
    new_shape = jax.ShapeDtypeStruct((n, 2 * N_HEADS, HEAD_DIM), F32)
    state_spec = pl.BlockSpec((bt, N_BACK, 2 * N_HEADS, HEAD_DIM), lambda j: (j, 0, 0, 0))
    return pl.pallas_call(
        _sample_attn_body,
        grid=(n // bt,),
        in_specs=in_specs,
        out_specs=[pl.BlockSpec((bt, N_HEADS, HEAD_DIM), lambda j: (j, 0, 0)), state_spec, new_spec, new_spec],
        out_shape=[jax.ShapeDtypeStruct((n, N_HEADS, HEAD_DIM), F32),
                   jax.ShapeDtypeStruct(views[0].shape, F32), new_shape, new_shape],
        compiler_params=pltpu.CompilerParams(vmem_limit_bytes=VMEM_LIMIT),
        name="sample_attn",
    )(p3, *views, q_gain, k_gain)


def _sample_out_body(x_ref, mod_ref, r_ref, a_ref, sp_ref, wp_ref, bp_ref, ps_ref, wa_ref, wb_ref, wo_ref,
                     y_ref, pool_ref):
    u = r_ref[:, GROUP_W:2 * GROUP_W]
    mixed = []
    for g, w in enumerate(POOL_WINDOWS):
        lanes = pl.ds(g * HEAD_DIM, HEAD_DIM)
        tok = r_ref[:, pl.ds(GROUP_W + g * HEAD_DIM, HEAD_DIM)]
        acc = tok
        for k in range(1, w):
            acc = acc + sp_ref[POOL_STATE - k, :, lanes]
        mixed.append(acc / float(min(w, PAST_LEN + 1)) - tok)
    p = _pool_project(mixed, wp_ref, bp_ref, ps_ref)
    for k in range(POOL_STATE - 1):
        pool_ref[k] = sp_ref[k + 1]
    pool_ref[POOL_STATE - 1] = u
    y_ref[...] = _merge_out(
        x_ref[...], mod_ref[:, 2 * D_MODEL:3 * D_MODEL], a_ref[...], p,
        r_ref[:, 0:GROUP_W], r_ref[:, 2 * GROUP_W:3 * GROUP_W],
        r_ref[:, 3 * GROUP_W:3 * GROUP_W + D_MODEL], r_ref[:, 3 * GROUP_W + D_MODEL:REST_W],
        wa_ref, wb_ref, wo_ref)


def _sample_out(xs, mod_s, rest, a_s, sp_t, w_pool, b_pool, pool_scale, wa, wb, wo):
    n = xs.shape[0]
    return pl.pallas_call(
        _sample_out_body,
        out_shape=[jax.ShapeDtypeStruct((n, D_MODEL), F32),
                   jax.ShapeDtypeStruct((POOL_STATE, n, GROUP_W), F32)],
        compiler_params=pltpu.CompilerParams(vmem_limit_bytes=VMEM_LIMIT),
        name="sample_out",
    )(xs, mod_s, rest, a_s, sp_t, w_pool, b_pool, pool_scale, wa, wb, wo)


def kernel(x_prompt, x_sample, state_kv_w128, state_kv_w512, state_kv_w2048, state_pool, c_prompt, c_sample,
           norm_g, w_ada, b_ada, w_in, q_gain, k_gain, w_pool, b_pool, pool_scale, w_a_out, w_b_out, w_out):
    B, L, _ = x_prompt.shape
    n_s = x_sample.shape[0]
    w_qkv = w_in[0, :, :QKV_W].astype(BF16)
    w_rest = w_in[0, :, QKV_W:].astype(BF16)
    wa, wb, wo, wp = (w[0].astype(BF16) for w in (w_a_out, w_b_out, w_out, w_pool))
    qg, kg, bp = q_gain[0], k_gain[0], b_pool[0]

    mod = _ada(jnp.concatenate([c_prompt, c_sample], axis=0), w_ada[0], b_ada)
    mod_p = mod[:B].reshape(B, 1, 3 * D_MODEL)
    mod_s = mod[B:]

    xs = x_sample[:, 0, :]
    qkv_s = _sample_proj(xs, mod_s, norm_g, w_qkv)
    rest_s = _sample_proj(xs, mod_s, norm_g, w_rest)
    states = (state_kv_w128, state_kv_w512, state_kv_w2048)
    views = [s.reshape(n_s, N_BACK, d * ROWS_PER_POS, HEAD_DIM) for s, (_, d) in zip(states, ATTN_PATTERNS)]
    a_s, shifted_128, new_512, new_2048 = _sample_attn(
        qkv_s.reshape(n_s, QKV_W // HEAD_DIM, HEAD_DIM), views, qg, kg)
    y_s, pool_t = _sample_out(xs, mod_s, rest_s, a_s.reshape(n_s, GROUP_W), state_pool[0].transpose(1, 0, 2),
                              wp, bp, pool_scale, wa, wb, wo)

    flat_2048, flat_512 = (s.reshape(n_s, s.shape[2] * ROWS_PER_POS, HEAD_DIM) for s in states[:0:-1])
    kinds = _shift_plan([flat_2048.shape, flat_512.shape])
    n_sub = B * (L // ROW_TILE) * SHIFT_SUBSTEPS
    n_chunks = kinds[-1][-1]
    assert n_sub <= kinds[0][4] and n_chunks <= 2 * n_sub
    *qkv, t0, t1, t2, part_2048 = _prompt_qkv(x_prompt, mod_p, norm_g, w_qkv, qg, kg, new_2048, flat_2048,
                                              (kinds, 0, n_sub))
    ol = []
    for g, (d_blk, h_blk) in enumerate(((1, 2), (1, 4), (4, 4))):
        ol += _prompt_attn(qkv[g], qkv[N_GROUPS + g], qkv[2 * N_GROUPS + g], d_blk, h_blk)
    y_p, pool_p, shifted_2048, shifted_512 = _prompt_out(
        x_prompt, mod_p, norm_g, w_rest, ol, wp, bp, pool_scale, wa, wb, wo,
        new_512, flat_2048, flat_512, part_2048, (kinds, n_sub, n_chunks))
    kv_s = [o.reshape(s.shape) for o, s in zip((shifted_128, shifted_512, shifted_2048), states)]
    kv_p = [t.reshape(1, B, t.shape[1], 2, N_HEADS, HEAD_DIM) for t in (t0, t1, t2)]

    return (y_p, y_s.reshape(n_s, 1, D_MODEL), kv_p[0], kv_p[1], kv_p[2], pool_p,
            kv_s[0], kv_s[1], kv_s[2], pool_t.transpose(1, 0, 2)[None])
```

```python
import functools

import jax
import jax.numpy as jnp
from jax import lax
from jax.experimental import pallas as pl
from jax.experimental.pallas import tpu as pltpu

F32 = jnp.float32
BF16 = jnp.bfloat16

D_MODEL = 1024
HEAD_DIM = 128
N_HEADS = 4
GROUP_W = N_HEADS * HEAD_DIM
ATTN_PATTERNS = ((128, 1), (512, 4), (2048, 16))
N_GROUPS = len(ATTN_PATTERNS)
N_BACK = 128
QKV_W = 3 * N_GROUPS * GROUP_W
REST_W = 3 * GROUP_W + 2 * D_MODEL
POOL_WINDOWS = (2, 4, 8, 16)
POOL_STATE = 15
PAST_LEN = 8192
EPS = 1e-6
Q_SCALE = HEAD_DIM ** -0.5
NEG = -1e30

ROW_TILE = 512
Q_BLOCK = 128
VMEM_LIMIT = 56 * 1024 * 1024
OUT_VMEM_LIMIT = 60 * 1024 * 1024


def _sigmoid(v):
    return 0.5 * jnp.tanh(0.5 * v) + 0.5


def _silu(v):
    return v * _sigmoid(v)


def _modulated_norm(x, norm_g, shift, scale):
    ms = jnp.mean(x * x, axis=-1, keepdims=True)
    return (x * lax.rsqrt(ms + EPS) * norm_g * (1.0 + scale) + shift).astype(BF16)


def _head_norm(r, gain):
    ms = jnp.mean(r * r, axis=-1, keepdims=True)
    return r * lax.rsqrt(ms + EPS) * gain


def _ada_body(c_ref, w_ref, b_ref, o_ref):
    s = _silu(c_ref[...]).astype(BF16)
    o_ref[...] = jnp.dot(s, w_ref[...].astype(BF16), preferred_element_type=F32) + b_ref[...]


def _ada(c_all, w_ada, b_ada):
    n = c_all.shape[0]
    return pl.pallas_call(
        _ada_body,
        grid=(3,),
        in_specs=[pl.BlockSpec((n, D_MODEL), lambda j: (0, 0)),
                  pl.BlockSpec((D_MODEL, D_MODEL), lambda j: (0, j)),
                  pl.BlockSpec((1, D_MODEL), lambda j: (0, j))],
        out_specs=pl.BlockSpec((n, D_MODEL), lambda j: (0, j)),
        out_shape=jax.ShapeDtypeStruct((n, 3 * D_MODEL), F32),
        name="ada",
    )(c_all, w_ada, b_ada)


ROWS_PER_POS = 2 * N_HEADS
SHIFT_NB = 8
SHIFT_ROWS = 712
SHIFT_SUBSTEPS = 4
SHIFT_AHEAD = 2
SHIFT_SLOTS = 2 * SHIFT_AHEAD
SHIFT_PRIORITY = 1


def _shift_plan(state_shapes):
    kinds, c0 = [], 0
    for idx, (n_batch, rows_total, _) in enumerate(state_shapes):
        keep = rows_total - ROWS_PER_POS
        pieces = min(p for p in range(1, keep)
                     if keep % (p * ROWS_PER_POS) == 0 and keep // p <= SHIFT_ROWS)
        n = (n_batch // SHIFT_NB) * pieces
        kinds.append((idx, keep // pieces, pieces, c0, c0 + n))
        c0 += n
    return kinds


def _shift_substep(j, lo, hi, last_slot, kinds, srcs, dsts, buf, rsem, wsem):
    def copy(jj, kind, write):
        idx, rows, pieces, c0, _ = kind
        local = jj - c0
        batches = pl.ds((local // pieces) * SHIFT_NB, SHIFT_NB)
        first = (local % pieces) * rows
        slot = jj % SHIFT_SLOTS
        stage = buf.at[slot, :, pl.ds(0, rows), :]
        if write:
            return pltpu.make_async_copy(stage, dsts[idx].at[batches, pl.ds(first, rows), :], wsem.at[slot])
        return pltpu.make_async_copy(srcs[idx].at[batches, pl.ds(first + ROWS_PER_POS, rows), :], stage, rsem.at[slot])

    def for_chunk(jj, also, fn):
        for kind in kinds:
            first, end = max(kind[3], lo), min(kind[4], hi)
            if first < end:
                cond = (jj >= first) & (jj < end)

                @pl.when(cond if also is None else cond & also)
                def _():
                    fn(functools.partial(copy, jj, kind))

    for a in range(SHIFT_AHEAD):
        for_chunk(j + a, j == lo, lambda cp: cp(False).start(priority=SHIFT_PRIORITY))

    def landed(cp):
        cp(False).wait()
        cp(True).start(priority=SHIFT_PRIORITY)

    for_chunk(j, None, landed)
    for_chunk(j - SHIFT_AHEAD, None, lambda cp: cp(True).wait())
    for_chunk(j + SHIFT_AHEAD, None, lambda cp: cp(False).start(priority=SHIFT_PRIORITY))
    for a in range(SHIFT_AHEAD):
        for_chunk(j - a, j == last_slot, lambda cp: cp(True).wait())


def _new_rows_copy(new_ref, dst, nsem):
    keep = dst.shape[1] - ROWS_PER_POS
    return pltpu.make_async_copy(new_ref, dst.at[:, pl.ds(keep, ROWS_PER_POS), :], nsem)


def _shift_scratch():
    return [pltpu.VMEM((SHIFT_SLOTS, SHIFT_NB, SHIFT_ROWS, HEAD_DIM), F32),
            pltpu.SemaphoreType.DMA((SHIFT_SLOTS,)), pltpu.SemaphoreType.DMA((SHIFT_SLOTS,)),
            pltpu.SemaphoreType.DMA(())]


def _class_major_perm(tm, d):
    i = jnp.arange(tm)
    src_row = (i % (tm // d)) * d + i // (tm // d)
    return (src_row[:, None] == jnp.arange(tm)[None, :]).astype(BF16)


SHIFT_SITES = ((0, 1), (1, 1), (2, 0))


def _qkv_body(x_ref, mod_ref, ng_ref, w_ref, qg_ref, kg_ref, perm_ref, new_ref, src_ref, *refs, n_tiles, shift):
    qkv_refs = (refs[0:3], refs[3:6], refs[6:9])
    tail_refs = refs[9:12]
    dst_ref, buf, rsem, wsem, nsem = refs[12:17]
    i = pl.program_id(1)
    tm = x_ref.shape[1]
    step = pl.program_id(0) * n_tiles + i
    kinds, lo, hi = shift

    last_slot = lo + SHIFT_SUBSTEPS * pl.num_programs(0) * n_tiles - 1

    def shift_substep(k):
        _shift_substep(lo + SHIFT_SUBSTEPS * step + k, lo, hi, last_slot, kinds, [src_ref], [dst_ref],
                       buf, rsem, wsem)

    shift_substep(0)

    @pl.when(step == 0)
    def _():
        _new_rows_copy(new_ref, dst_ref, nsem).start()

    h = _modulated_norm(x_ref[0], ng_ref[...], mod_ref[0, :, 0:D_MODEL], mod_ref[0, :, D_MODEL:2 * D_MODEL])
    h_by_group = [h] + [jnp.dot(perm_ref[g - 1], h, preferred_element_type=F32).astype(BF16)
                        for g in range(1, N_GROUPS)]

    def heads(t, g, lhs):
        c = t * N_GROUPS + g
        res = jnp.dot(lhs, w_ref[:, c * GROUP_W:(c + 1) * GROUP_W], preferred_element_type=F32)
        out = []
        for hh in range(N_HEADS):
            r = res[:, hh * HEAD_DIM:(hh + 1) * HEAD_DIM]
            if t == 0:
                r = _head_norm(r, qg_ref[g:g + 1, :]) * Q_SCALE
            elif t == 1:
                r = _head_norm(r, kg_ref[g:g + 1, :])
            out.append(r)
        return out

    def write_tail(t, g, rs):
        window = ATTN_PATTERNS[g][0]
        for hh, r in enumerate(rs):
            row = (t - 1) * N_HEADS + hh
            tail_refs[g][0, :, row, :] = r if window >= tm else r[tm - window:, :]

    for t in range(3):
        for g, (window, d) in enumerate(ATTN_PATTERNS):
            if (t, g) in SHIFT_SITES:
                shift_substep(SHIFT_SITES.index((t, g)) + 1)
            if (t, g) == SHIFT_SITES[0]:
                @pl.when(step == 0)
                def _():
                    _new_rows_copy(new_ref, dst_ref, nsem).wait()

            rs = heads(t, g, h_by_group[g])
            per = tm // d
            for hh, r in enumerate(rs):
                for rr in range(d):
                    qkv_refs[t][g][0, rr, hh] = r[rr * per:(rr + 1) * per, :].astype(BF16)
            if t > 0:
                n_tail = max(window // tm, 1)

                @pl.when(i >= n_tiles - n_tail)
                def _():
                    write_tail(t, g, rs if d == 1 else heads(t, g, h))


def _prompt_qkv(x, mod_p, norm_g, w_qkv, q_gain, k_gain, new_rows, state, shift):
    B, L, _ = x.shape
    tm = ROW_TILE
    n_tiles = L // tm
    const2 = lambda b, i: (0, 0)
    in_specs = [
        pl.BlockSpec((1, tm, D_MODEL), lambda b, i: (b, i, 0)),
        pl.BlockSpec((1, 1, 3 * D_MODEL), lambda b, i: (b, 0, 0)),
        pl.BlockSpec((1, D_MODEL), const2),
        pl.BlockSpec((D_MODEL, QKV_W), const2, pipeline_mode=pl.Buffered(1)),
        pl.BlockSpec((N_GROUPS, HEAD_DIM), const2),
        pl.BlockSpec((N_GROUPS, HEAD_DIM), const2),
        pl.BlockSpec((N_GROUPS - 1, tm, tm), lambda b, i: (0, 0, 0), pipeline_mode=pl.Buffered(1)),
        pl.BlockSpec(new_rows.shape, lambda b, i: (0, 0, 0), pipeline_mode=pl.Buffered(1)),
        pl.BlockSpec(memory_space=pl.ANY),
    ]
    perms = jnp.stack([_class_major_perm(tm, d) for _, d in ATTN_PATTERNS[1:]])
    out_specs, out_shape = [], []
    for t in range(3):
        for window, d in ATTN_PATTERNS:
            out_specs.append(pl.BlockSpec((1, d, N_HEADS, tm // d, HEAD_DIM), lambda b, i: (b, 0, 0, i, 0)))
            out_shape.append(jax.ShapeDtypeStruct((B, d, N_HEADS, L // d, HEAD_DIM), BF16))
    for window, d in ATTN_PATTERNS:
        w_eff = min(window, L)
        if w_eff >= tm:
            n_tail = w_eff // tm
            out_specs.append(pl.BlockSpec(
                (1, tm, 2 * N_HEADS, HEAD_DIM),
                lambda b, i, n_tail=n_tail: (b, jnp.maximum(i - (n_tiles - n_tail), 0), 0, 0)))
        else:
            out_specs.append(pl.BlockSpec((1, w_eff, 2 * N_HEADS, HEAD_DIM), lambda b, i: (b, 0, 0, 0)))
        out_shape.append(jax.ShapeDtypeStruct((B, w_eff, 2 * N_HEADS, HEAD_DIM), F32))
    out_specs.append(pl.BlockSpec(memory_space=pl.ANY))
    out_shape.append(jax.ShapeDtypeStruct(state.shape, F32))
    return pl.pallas_call(
        functools.partial(_qkv_body, n_tiles=n_tiles, shift=shift),
        grid=(B, n_tiles),
        in_specs=in_specs,
        out_specs=out_specs,
        out_shape=out_shape,
        scratch_shapes=_shift_scratch(),
        compiler_params=pltpu.CompilerParams(
            dimension_semantics=("arbitrary", "arbitrary"), vmem_limit_bytes=VMEM_LIMIT),
        name="prompt_qkv",
    )(x, mod_p, norm_g, w_qkv, q_gain, k_gain, perms, new_rows, state)


def _attn_body(q_ref, k_ref, v_ref, o_ref, l_ref, vx):
    _, d_blk, h_blk, M, _ = q_ref.shape
    n_q = M // Q_BLOCK
    for c in range(d_blk):
        for hh in range(h_blk):
            vx[c, hh, :, 0:HEAD_DIM] = v_ref[0, c, hh]
            vx[c, hh, :, HEAD_DIM:2 * HEAD_DIM] = jnp.ones((M, HEAD_DIM), BF16)
    ii = lax.broadcasted_iota(jnp.int32, (Q_BLOCK, 2 * Q_BLOCK), 0)
    jj = lax.broadcasted_iota(jnp.int32, (Q_BLOCK, 2 * Q_BLOCK), 1)
    dist = Q_BLOCK + ii - jj
    band_ok = (dist >= 0) & (dist <= N_BACK)
    causal_ok = (lax.broadcasted_iota(jnp.int32, (Q_BLOCK, Q_BLOCK), 1)
                 <= lax.broadcasted_iota(jnp.int32, (Q_BLOCK, Q_BLOCK), 0))
    nt = (((1,), (1,)), ((), ()))

    def block(c, hh, n, first):
        rows = pl.ds(pl.multiple_of(n * Q_BLOCK, Q_BLOCK), Q_BLOCK)
        if first:
            keys, ok = rows, causal_ok
        else:
            keys, ok = pl.ds(pl.multiple_of((n - 1) * Q_BLOCK, Q_BLOCK), 2 * Q_BLOCK), band_ok
        s = lax.dot_general(q_ref[0, c, hh, rows, :], k_ref[0, c, hh, keys, :], nt, preferred_element_type=F32)
        s = jnp.where(ok, s, NEG)
        m = jnp.max(s, axis=-1, keepdims=True)
        p = jnp.exp(s - m).astype(BF16)
        oe = jnp.dot(p, vx[c, hh, keys, :], preferred_element_type=F32)
        l = oe[:, HEAD_DIM:]
        o_ref[0, c, hh, rows, :] = oe[:, :HEAD_DIM] / l
        l_ref[0, c, hh, rows, :] = m + jnp.log(l)

    for c in range(d_blk):
        for hh in range(h_blk):
            block(c, hh, 0, True)
        if n_q <= 2:
            for n in range(1, n_q):
                for hh in range(h_blk):
                    block(c, hh, n, False)
        else:
            def step(n, carry, c=c):
                for hh in range(h_blk):
                    block(c, hh, n, False)
                return carry

            lax.fori_loop(1, n_q, step, 0, unroll=2)


def _prompt_attn(q, k, v, d_blk, h_blk):
    B, d, H, M, E = q.shape
    blk = (1, d_blk, h_blk, M, E)
    spec = pl.BlockSpec(blk, lambda b, c, hh: (b, c, hh, 0, 0))
    shape = jax.ShapeDtypeStruct(q.shape, F32)
    return pl.pallas_call(
        _attn_body,
        grid=(B, d // d_blk, H // h_blk),
        in_specs=[spec, spec, spec],
        out_specs=[spec, spec],
        out_shape=[shape, shape],
        scratch_shapes=[pltpu.VMEM((d_blk, h_blk, M, 2 * E), BF16)],
        compiler_params=pltpu.CompilerParams(
            dimension_semantics=("arbitrary", "arbitrary", "arbitrary"), vmem_limit_bytes=VMEM_LIMIT),
        name="prompt_attn_d%d" % d,
    )(q, k, v)


def _pool_project(mixed, wp_ref, bp_ref, ps_ref):
    cols = []
    for g in range(len(POOL_WINDOWS)):
        lanes = slice(g * HEAD_DIM, (g + 1) * HEAD_DIM)
        y = jnp.dot(mixed[g].astype(BF16), wp_ref[g], preferred_element_type=F32) + bp_ref[g:g + 1, :]
        cols.append(y * ps_ref[:, lanes])
    return jnp.concatenate(cols, axis=-1)


def _merge_out(x, gate, a, p, z_a, z_b, g_a, g_b, wa_ref, wb_ref, wo_ref):
    ya = jnp.dot((a * _silu(z_a)).astype(BF16), wa_ref[...], preferred_element_type=F32)
    yb = jnp.dot((p * _silu(z_b)).astype(BF16), wb_ref[...], preferred_element_type=F32)
    m = _sigmoid(g_a) * ya + _sigmoid(g_b) * yb
    return x + gate * jnp.dot(m.astype(BF16), wo_ref[...], preferred_element_type=F32)


def _out_body(x_ref, mod_ref, ng_ref, w_ref, o0, l0, o1, l1, o2, l2, wp_ref, bp_ref, ps_ref,
              wa_ref, wb_ref, wo_ref, new_ref, src_a, src_b, dst_a_in, y_ref, pool_ref, dst_a, dst_b,
              ext, oscr, lscr, buf, rsem, wsem, nsem, *, n_tiles, shift):
    del dst_a_in
    i = pl.program_id(1)
    tm = x_ref.shape[1]
    step = pl.program_id(0) * n_tiles + i
    kinds, lo, hi = shift

    last_slot = lo + SHIFT_SUBSTEPS * pl.num_programs(0) * n_tiles - 1

    def shift_substep(k):
        _shift_substep(lo + SHIFT_SUBSTEPS * step + k, lo, hi, last_slot, kinds, [src_a, src_b], [dst_a, dst_b],
                       buf, rsem, wsem)

    shift_substep(0)

    @pl.when(step == 0)
    def _():
        _new_rows_copy(new_ref, dst_b, nsem).start()

    x = x_ref[0]
    h = _modulated_norm(x, ng_ref[...], mod_ref[0, :, 0:D_MODEL], mod_ref[0, :, D_MODEL:2 * D_MODEL])
    gate = mod_ref[0, :, 2 * D_MODEL:3 * D_MODEL]

    def proj(lo, width):
        return jnp.dot(h, w_ref[:, lo:lo + width], preferred_element_type=F32)

    o_refs, l_refs = (o0, o1, o2), (l0, l1, l2)
    for g, (window, d) in enumerate(ATTN_PATTERNS):
        if d == 1:
            continue
        for hh in range(N_HEADS):
            for rr in range(d):
                rows = pl.ds(rr, tm // d, stride=d)
                oscr[g - 1, hh, rows, :] = o_refs[g][0, rr, hh]
                lscr[g - 1, hh, rows, :] = l_refs[g][0, rr, hh]
    a_cols = []
    for hh in range(N_HEADS):
        os_ = [o0[0, 0, hh], oscr[0, hh], oscr[1, hh]]
        ls_ = [l0[0, 0, hh], lscr[0, hh], lscr[1, hh]]
        mx = jnp.maximum(jnp.maximum(ls_[0], ls_[1]), ls_[2])
        es = [jnp.exp(l_ - mx) for l_ in ls_]
        den = es[0] + es[1] + es[2]
        a_cols.append((es[0] * os_[0] + es[1] * os_[1] + es[2] * os_[2]) / den)
    a = jnp.concatenate(a_cols, axis=-1)
    shift_substep(1)

    @pl.when(step == 0)
    def _():
        _new_rows_copy(new_ref, dst_b, nsem).wait()

    u = proj(GROUP_W, GROUP_W)

    @pl.when(i == 0)
    def _():
        ext[0:16, :] = jnp.zeros((16, GROUP_W), F32)

    ext[16:16 + tm, :] = u
    pos = i * tm + lax.broadcasted_iota(jnp.int32, (tm, 1), 0)
    mixed = []
    for g, w in enumerate(POOL_WINDOWS):
        lanes = pl.ds(g * HEAD_DIM, HEAD_DIM)
        acc = ext[pl.ds(16, tm), lanes]
        tok = acc
        for k in range(1, w):
            acc = acc + ext[pl.ds(16 - k, tm), lanes]
        cnt = jnp.minimum(w, pos + 1).astype(F32)
        mixed.append(acc / cnt - tok)

    @pl.when(i == n_tiles - 1)
    def _():
        pool_ref[0, 0] = ext[pl.ds(tm + 1, POOL_STATE), :]

    ext[0:16, :] = ext[tm:tm + 16, :]
    p = _pool_project(mixed, wp_ref, bp_ref, ps_ref)
    shift_substep(2)
    z_a = proj(0, GROUP_W)
    z_b = proj(2 * GROUP_W, GROUP_W)
    g_a = proj(3 * GROUP_W, D_MODEL)
    shift_substep(3)
    g_b = proj(3 * GROUP_W + D_MODEL, D_MODEL)
    y_ref[0] = _merge_out(x, gate, a, p, z_a, z_b, g_a, g_b, wa_ref, wb_ref, wo_ref)


def _prompt_out(x, mod_p, norm_g, w_rest, ol, w_pool, b_pool, pool_scale, wa, wb, wo,
                new_rows, state_a, state_b, shifted_a, shift):
    B, L, _ = x.shape
    tm = ROW_TILE
    n_tiles = L // tm
    const2 = lambda b, i: (0, 0)
    one = pl.Buffered(1)
    in_specs = [
        pl.BlockSpec((1, tm, D_MODEL), lambda b, i: (b, i, 0)),
        pl.BlockSpec((1, 1, 3 * D_MODEL), lambda b, i: (b, 0, 0)),
        pl.BlockSpec((1, D_MODEL), const2),
        pl.BlockSpec((D_MODEL, REST_W), const2, pipeline_mode=one),
    ]
    for window, d in ATTN_PATTERNS:
        for _ in range(2):
            in_specs.append(pl.BlockSpec((1, d, N_HEADS, tm // d, HEAD_DIM), lambda b, i: (b, 0, 0, i, 0)))
    in_specs += [
        pl.BlockSpec((len(POOL_WINDOWS), HEAD_DIM, HEAD_DIM), lambda b, i: (0, 0, 0)),
        pl.BlockSpec((len(POOL_WINDOWS), HEAD_DIM), const2),
        pl.BlockSpec((1, GROUP_W), const2),
        pl.BlockSpec((GROUP_W, D_MODEL), const2, pipeline_mode=one),
        pl.BlockSpec((GROUP_W, D_MODEL), const2, pipeline_mode=one),
        pl.BlockSpec((D_MODEL, D_MODEL), const2, pipeline_mode=one),
        pl.BlockSpec(new_rows.shape, lambda b, i: (0, 0, 0), pipeline_mode=one),
    ]
    any_spec = pl.BlockSpec(memory_space=pl.ANY)
    in_specs += [any_spec] * 3
    return pl.pallas_call(
        functools.partial(_out_body, n_tiles=n_tiles, shift=shift),
        grid=(B, n_tiles),
        in_specs=in_specs,
        out_specs=[pl.BlockSpec((1, tm, D_MODEL), lambda b, i: (b, i, 0)),
                   pl.BlockSpec((1, 1, POOL_STATE, GROUP_W), lambda b, i: (0, b, 0, 0)),
                   any_spec, any_spec],
        out_shape=[jax.ShapeDtypeStruct((B, L, D_MODEL), F32),
                   jax.ShapeDtypeStruct((1, B, POOL_STATE, GROUP_W), F32),
                   jax.ShapeDtypeStruct(state_a.shape, F32),
                   jax.ShapeDtypeStruct(state_b.shape, F32)],
        scratch_shapes=[pltpu.VMEM((tm + 16, GROUP_W), F32),
                        pltpu.VMEM((2, N_HEADS, tm, HEAD_DIM), F32),
                        pltpu.VMEM((2, N_HEADS, tm, HEAD_DIM), F32)] + _shift_scratch(),
        input_output_aliases={len(in_specs) - 1: 2},
        compiler_params=pltpu.CompilerParams(
            dimension_semantics=("arbitrary", "arbitrary"), vmem_limit_bytes=OUT_VMEM_LIMIT),
        name="prompt_out",
    )(x, mod_p, norm_g, w_rest, *ol, w_pool, b_pool, pool_scale, wa, wb, wo,
      new_rows, state_a, state_b, shifted_a)


def _sample_proj_body(x_ref, mod_ref, ng_ref, w_ref, o_ref):
    h = _modulated_norm(x_ref[...], ng_ref[...], mod_ref[:, 0:D_MODEL], mod_ref[:, D_MODEL:2 * D_MODEL])
    o_ref[...] = jnp.dot(h, w_ref[...], preferred_element_type=F32)


def _sample_proj(xs, mod_s, norm_g, w):
    n, width = xs.shape[0], w.shape[1]
    tn = GROUP_W
    return pl.pallas_call(
        _sample_proj_body,
        grid=(width // tn,),
        in_specs=[pl.BlockSpec((n, D_MODEL), lambda j: (0, 0)),
                  pl.BlockSpec((n, 3 * D_MODEL), lambda j: (0, 0)),
                  pl.BlockSpec((1, D_MODEL), lambda j: (0, 0)),
                  pl.BlockSpec((D_MODEL, tn), lambda j: (0, j))],
        out_specs=pl.BlockSpec((n, tn), lambda j: (0, j)),
        out_shape=jax.ShapeDtypeStruct((n, width), F32),
        name="sample_proj",
    )(xs, mod_s, norm_g, w)


def _sample_attn_body(p_ref, s0, s1, s2, qg_ref, kg_ref, a_ref, shifted0, n1, n2):
    bt = p_ref.shape[0]
    s_refs = (s0, s1, s2)
    last = N_BACK - 1
    for b in range(bt):
        os_, ls_ = [], []
        shifted0[b, 0:last] = s0[b, 1:N_BACK]
        for g in range(N_GROUPS):
            row = lambda t: pl.ds((t * N_GROUPS + g) * N_HEADS, N_HEADS)
            q4 = _head_norm(p_ref[b, row(0), :], qg_ref[g:g + 1, :]) * Q_SCALE
            k4 = _head_norm(p_ref[b, row(1), :], kg_ref[g:g + 1, :])
            v4 = p_ref[b, row(2), :]
            if g == 0:
                shifted0[b, last, 0:N_HEADS, :] = k4
                shifted0[b, last, N_HEADS:2 * N_HEADS, :] = v4
            else:
                (n1, n2)[g - 1][b, 0:N_HEADS, :] = k4
                (n1, n2)[g - 1][b, N_HEADS:2 * N_HEADS, :] = v4
            keys = s_refs[g][b, :, 0:N_HEADS, :]
            vals = s_refs[g][b, :, N_HEADS:2 * N_HEADS, :]
            s_old = jnp.sum(keys * q4[None], axis=-1, keepdims=True)
            s_new = jnp.sum(k4 * q4, axis=-1, keepdims=True)
            m = jnp.maximum(jnp.max(s_old, axis=0), s_new)
            p_old = jnp.exp(s_old - m[None])
            p_new = jnp.exp(s_new - m)
            l = jnp.sum(p_old, axis=0) + p_new
            o = jnp.sum(p_old * vals, axis=0) + p_new * v4
            os_.append(o / l)
            ls_.append(m + jnp.log(l))
        mx = jnp.maximum(jnp.maximum(ls_[0], ls_[1]), ls_[2])
        es = [jnp.exp(l_ - mx) for l_ in ls_]
        a_ref[b] = (es[0] * os_[0] + es[1] * os_[1] + es[2] * os_[2]) / (es[0] + es[1] + es[2])


def _sample_attn(p3, views, q_gain, k_gain):
    n = p3.shape[0]
    bt = 8
    in_specs = [pl.BlockSpec((bt, p3.shape[1], HEAD_DIM), lambda j: (j, 0, 0))]
    for v in views:
        in_specs.append(pl.BlockSpec((bt, N_BACK, 2 * N_HEADS, HEAD_DIM), lambda j: (j, 0, 0, 0)))
    in_specs += [pl.BlockSpec((N_GROUPS, HEAD_DIM), lambda j: (0, 0))] * 2
    new_spec = pl.BlockSpec((bt, 2 * N_HEADS, HEAD_DIM), lambda j: (j, 0, 0))
    new_shape = jax.ShapeDtypeStruct((n, 2 * N_HEADS, HEAD_DIM), F32)
    state_spec = pl.BlockSpec((bt, N_BACK, 2 * N_HEADS, HEAD_DIM), lambda j: (j, 0, 0, 0))
    return pl.pallas_call(
        _sample_attn_body,
        grid=(n // bt,),
        in_specs=in_specs,
        out_specs=[pl.BlockSpec((bt, N_HEADS, HEAD_DIM), lambda j: (j, 0, 0)), state_spec, new_spec, new_spec],
        out_shape=[jax.ShapeDtypeStruct((n, N_HEADS, HEAD_DIM), F32),
                   jax.ShapeDtypeStruct(views[0].shape, F32), new_shape, new_shape],
        compiler_params=pltpu.CompilerParams(vmem_limit_bytes=VMEM_LIMIT),
        name="sample_attn",
    )(p3, *views, q_gain, k_gain)


def _sample_out_body(x_ref, mod_ref, r_ref, a_ref, sp_ref, wp_ref, bp_ref, ps_ref, wa_ref, wb_ref, wo_ref,
                     y_ref, pool_ref):
    u = r_ref[:, GROUP_W:2 * GROUP_W]
    mixed = []
    for g, w in enumerate(POOL_WINDOWS):
        lanes = pl.ds(g * HEAD_DIM, HEAD_DIM)
        tok = r_ref[:, pl.ds(GROUP_W + g * HEAD_DIM, HEAD_DIM)]
        acc = tok
        for k in range(1, w):
            acc = acc + sp_ref[POOL_STATE - k, :, lanes]
        mixed.append(acc / float(min(w, PAST_LEN + 1)) - tok)
    p = _pool_project(mixed, wp_ref, bp_ref, ps_ref)
    for k in range(POOL_STATE - 1):
        pool_ref[k] = sp_ref[k + 1]
    pool_ref[POOL_STATE - 1] = u
    y_ref[...] = _merge_out(
        x_ref[...], mod_ref[:, 2 * D_MODEL:3 * D_MODEL], a_ref[...], p,
        r_ref[:, 0:GROUP_W], r_ref[:, 2 * GROUP_W:3 * GROUP_W],
        r_ref[:, 3 * GROUP_W:3 * GROUP_W + D_MODEL], r_ref[:, 3 * GROUP_W + D_MODEL:REST_W],
        wa_ref, wb_ref, wo_ref)


def _sample_out(xs, mod_s, rest, a_s, sp_t, w_pool, b_pool, pool_scale, wa, wb, wo):
    n = xs.shape[0]
    return pl.pallas_call(
        _sample_out_body,
        out_shape=[jax.ShapeDtypeStruct((n, D_MODEL), F32),
                   jax.ShapeDtypeStruct((POOL_STATE, n, GROUP_W), F32)],
        compiler_params=pltpu.CompilerParams(vmem_limit_bytes=VMEM_LIMIT),
        name="sample_out",
    )(xs, mod_s, rest, a_s, sp_t, w_pool, b_pool, pool_scale, wa, wb, wo)


def kernel(x_prompt, x_sample, state_kv_w128, state_kv_w512, state_kv_w2048, state_pool, c_prompt, c_sample,
           norm_g, w_ada, b_ada, w_in, q_gain, k_gain, w_pool, b_pool, pool_scale, w_a_out, w_b_out, w_out):
    B, L, _ = x_prompt.shape
    n_s = x_sample.shape[0]
    w_qkv = w_in[0, :, :QKV_W].astype(BF16)
    w_rest = w_in[0, :, QKV_W:].astype(BF16)
    wa, wb, wo, wp = (w[0].astype(BF16) for w in (w_a_out, w_b_out, w_out, w_pool))
    qg, kg, bp = q_gain[0], k_gain[0], b_pool[0]

    mod = _ada(jnp.concatenate([c_prompt, c_sample], axis=0), w_ada[0], b_ada)
    mod_p = mod[:B].reshape(B, 1, 3 * D_MODEL)
    mod_s = mod[B:]

    xs = x_sample[:, 0, :]
    qkv_s = _sample_proj(xs, mod_s, norm_g, w_qkv)
    rest_s = _sample_proj(xs, mod_s, norm_g, w_rest)
    states = (state_kv_w128, state_kv_w512, state_kv_w2048)
    views = [s.reshape(n_s, N_BACK, d * ROWS_PER_POS, HEAD_DIM) for s, (_, d) in zip(states, ATTN_PATTERNS)]
    a_s, shifted_128, new_512, new_2048 = _sample_attn(
        qkv_s.reshape(n_s, QKV_W // HEAD_DIM, HEAD_DIM), views, qg, kg)
    y_s, pool_t = _sample_out(xs, mod_s, rest_s, a_s.reshape(n_s, GROUP_W), state_pool[0].transpose(1, 0, 2),
                              wp, bp, pool_scale, wa, wb, wo)

    flat_2048, flat_512 = (s.reshape(n_s, s.shape[2] * ROWS_PER_POS, HEAD_DIM) for s in states[:0:-1])
    kinds = _shift_plan([flat_2048.shape, flat_512.shape])
    n_sub = B * (L // ROW_TILE) * SHIFT_SUBSTEPS
    n_chunks = kinds[-1][-1]
    assert n_sub <= kinds[0][4] and n_chunks <= 2 * n_sub
    *qkv, t0, t1, t2, part_2048 = _prompt_qkv(x_prompt, mod_p, norm_g, w_qkv, qg, kg, new_2048, flat_2048,
                                              (kinds, 0, n_sub))
    ol = []
    for g, (d_blk, h_blk) in enumerate(((1, 2), (1, 4), (4, 4))):
        ol += _prompt_attn(qkv[g], qkv[N_GROUPS + g], qkv[2 * N_GROUPS + g], d_blk, h_blk)
    y_p, pool_p, shifted_2048, shifted_512 = _prompt_out(
        x_prompt, mod_p, norm_g, w_rest, ol, wp, bp, pool_scale, wa, wb, wo,
        new_512, flat_2048, flat_512, part_2048, (kinds, n_sub, n_chunks))
    kv_s = [o.reshape(s.shape) for o, s in zip((shifted_128, shifted_512, shifted_2048), states)]
    kv_p = [t.reshape(1, B, t.shape[1], 2, N_HEADS, HEAD_DIM) for t in (t0, t1, t2)]

    return (y_p, y_s.reshape(n_s, 1, D_MODEL), kv_p[0], kv_p[1], kv_p[2], pool_p,
            kv_s[0], kv_s[1], kv_s[2], pool_t.transpose(1, 0, 2)[None])
```

```python
import functools

import jax
import jax.numpy as jnp
from jax import lax
from jax.experimental import pallas as pl
from jax.experimental.pallas import tpu as pltpu

F32 = jnp.float32
BF16 = jnp.bfloat16

D_MODEL = 1024
HEAD_DIM = 128
N_HEADS = 4
GROUP_W = N_HEADS * HEAD_DIM
ATTN_PATTERNS = ((128, 1), (512, 4), (2048, 16))
N_GROUPS = len(ATTN_PATTERNS)
N_BACK = 128
QKV_W = 3 * N_GROUPS * GROUP_W
REST_W = 3 * GROUP_W + 2 * D_MODEL
POOL_WINDOWS = (2, 4, 8, 16)
POOL_STATE = 15
PAST_LEN = 8192
EPS = 1e-6
Q_SCALE = HEAD_DIM ** -0.5
NEG = -1e30

ROW_TILE = 512
Q_BLOCK = 128
Q_UNROLL = 8
VMEM_LIMIT = 56 * 1024 * 1024
OUT_VMEM_LIMIT = 60 * 1024 * 1024


def _sigmoid(v):
    return 0.5 * jnp.tanh(0.5 * v) + 0.5


def _silu(v):
    return v * _sigmoid(v)


def _modulated_norm(x, norm_g, shift, scale):
    ms = jnp.mean(x * x, axis=-1, keepdims=True)
    return (x * lax.rsqrt(ms + EPS) * norm_g * (1.0 + scale) + shift).astype(BF16)


def _head_norm(r, gain):
    ms = jnp.mean(r * r, axis=-1, keepdims=True)
    return r * lax.rsqrt(ms + EPS) * gain


def _ada_body(c_ref, w_ref, b_ref, o_ref):
    s = _silu(c_ref[...]).astype(BF16)
    o_ref[...] = jnp.dot(s, w_ref[...].astype(BF16), preferred_element_type=F32) + b_ref[...]


def _ada(c_all, w_ada, b_ada):
    n = c_all.shape[0]
    return pl.pallas_call(
        _ada_body,
        grid=(3,),
        in_specs=[pl.BlockSpec((n, D_MODEL), lambda j: (0, 0)),
                  pl.BlockSpec((D_MODEL, D_MODEL), lambda j: (0, j)),
                  pl.BlockSpec((1, D_MODEL), lambda j: (0, j))],
        out_specs=pl.BlockSpec((n, D_MODEL), lambda j: (0, j)),
        out_shape=jax.ShapeDtypeStruct((n, 3 * D_MODEL), F32),
        name="ada",
    )(c_all, w_ada, b_ada)


ROWS_PER_POS = 2 * N_HEADS
SHIFT_NB = 8
SHIFT_ROWS = 712
SHIFT_SUBSTEPS = 4
SHIFT_AHEAD = 2
SHIFT_SLOTS = 2 * SHIFT_AHEAD
SHIFT_PRIORITY = 1


def _shift_plan(state_shapes):
    kinds, c0 = [], 0
    for idx, (n_batch, rows_total, _) in enumerate(state_shapes):
        keep = rows_total - ROWS_PER_POS
        pieces = min(p for p in range(1, keep)
                     if keep % (p * ROWS_PER_POS) == 0 and keep // p <= SHIFT_ROWS)
        n = (n_batch // SHIFT_NB) * pieces
        kinds.append((idx, keep // pieces, pieces, c0, c0 + n))
        c0 += n
    return kinds


def _shift_substep(j, lo, hi, last_slot, kinds, srcs, dsts, buf, rsem, wsem):
    def copy(jj, kind, write):
        idx, rows, pieces, c0, _ = kind
        local = jj - c0
        batches = pl.ds((local // pieces) * SHIFT_NB, SHIFT_NB)
        first = (local % pieces) * rows
        slot = jj % SHIFT_SLOTS
        stage = buf.at[slot, :, pl.ds(0, rows), :]
        if write:
            return pltpu.make_async_copy(stage, dsts[idx].at[batches, pl.ds(first, rows), :], wsem.at[slot])
        return pltpu.make_async_copy(srcs[idx].at[batches, pl.ds(first + ROWS_PER_POS, rows), :], stage, rsem.at[slot])

    def for_chunk(jj, also, fn):
        for kind in kinds:
            first, end = max(kind[3], lo), min(kind[4], hi)
            if first < end:
                cond = (jj >= first) & (jj < end)

                @pl.when(cond if also is None else cond & also)
                def _():
                    fn(functools.partial(copy, jj, kind))

    for a in range(SHIFT_AHEAD):
        for_chunk(j + a, j == lo, lambda cp: cp(False).start(priority=SHIFT_PRIORITY))

    def landed(cp):
        cp(False).wait()
        cp(True).start(priority=SHIFT_PRIORITY)

    for_chunk(j, None, landed)
    for_chunk(j - SHIFT_AHEAD, None, lambda cp: cp(True).wait())
    for_chunk(j + SHIFT_AHEAD, None, lambda cp: cp(False).start(priority=SHIFT_PRIORITY))
    for a in range(SHIFT_AHEAD):
        for_chunk(j - a, j == last_slot, lambda cp: cp(True).wait())


def _new_rows_copy(new_ref, dst, nsem):
    keep = dst.shape[1] - ROWS_PER_POS
    return pltpu.make_async_copy(new_ref, dst.at[:, pl.ds(keep, ROWS_PER_POS), :], nsem)


def _shift_scratch():
    return [pltpu.VMEM((SHIFT_SLOTS, SHIFT_NB, SHIFT_ROWS, HEAD_DIM), F32),
            pltpu.SemaphoreType.DMA((SHIFT_SLOTS,)), pltpu.SemaphoreType.DMA((SHIFT_SLOTS,)),
            pltpu.SemaphoreType.DMA(())]


def _class_major_perm(tm, d):
    i = jnp.arange(tm)
    src_row = (i % (tm // d)) * d + i // (tm // d)
    return (src_row[:, None] == jnp.arange(tm)[None, :]).astype(BF16)


SHIFT_SITES = ((0, 1), (1, 1), (2, 0))


def _qkv_body(x_ref, mod_ref, ng_ref, w_ref, qg_ref, kg_ref, perm_ref, new_ref, src_ref, *refs, n_tiles, shift):
    qkv_refs = (refs[0:3], refs[3:6], refs[6:9])
    tail_refs = refs[9:12]
    dst_ref, buf, rsem, wsem, nsem = refs[12:17]
    i = pl.program_id(1)
    tm = x_ref.shape[1]
    step = pl.program_id(0) * n_tiles + i
    kinds, lo, hi = shift

    last_slot = lo + SHIFT_SUBSTEPS * pl.num_programs(0) * n_tiles - 1

    def shift_substep(k):
        _shift_substep(lo + SHIFT_SUBSTEPS * step + k, lo, hi, last_slot, kinds, [src_ref], [dst_ref],
                       buf, rsem, wsem)

    shift_substep(0)

    @pl.when(step == 0)
    def _():
        _new_rows_copy(new_ref, dst_ref, nsem).start()

    h = _modulated_norm(x_ref[0], ng_ref[...], mod_ref[0, :, 0:D_MODEL], mod_ref[0, :, D_MODEL:2 * D_MODEL])
    h_by_group = [h] + [jnp.dot(perm_ref[g - 1], h, preferred_element_type=F32).astype(BF16)
                        for g in range(1, N_GROUPS)]

    def heads(t, g, lhs):
        c = t * N_GROUPS + g
        res = jnp.dot(lhs, w_ref[:, c * GROUP_W:(c + 1) * GROUP_W], preferred_element_type=F32)
        out = []
        for hh in range(N_HEADS):
            r = res[:, hh * HEAD_DIM:(hh + 1) * HEAD_DIM]
            if t == 0:
                r = _head_norm(r, qg_ref[g:g + 1, :]) * Q_SCALE
            elif t == 1:
                r = _head_norm(r, kg_ref[g:g + 1, :])
            out.append(r)
        return out

    def write_tail(t, g, rs):
        window, d = ATTN_PATTERNS[g]
        per = tm // d
        for hh, r in enumerate(rs):
            row = (t - 1) * N_HEADS + hh
            if d == 1:
                tail_refs[g][0, :, row, :] = r if window >= tm else r[tm - window:, :]
            else:
                assert window >= tm
                for rr in range(d):
                    tail_refs[g][0, pl.ds(rr, per, stride=d), row, :] = r[rr * per:(rr + 1) * per, :]

    for t in range(3):
        for g, (window, d) in enumerate(ATTN_PATTERNS):
            if (t, g) in SHIFT_SITES:
                shift_substep(SHIFT_SITES.index((t, g)) + 1)
            if (t, g) == SHIFT_SITES[0]:
                @pl.when(step == 0)
                def _():
                    _new_rows_copy(new_ref, dst_ref, nsem).wait()

            rs = heads(t, g, h_by_group[g])
            per = tm // d
            for hh, r in enumerate(rs):
                for rr in range(d):
                    qkv_refs[t][g][0, rr, hh] = r[rr * per:(rr + 1) * per, :].astype(BF16)
            if t > 0:
                n_tail = max(window // tm, 1)

                @pl.when(i >= n_tiles - n_tail)
                def _():
                    write_tail(t, g, rs)


def _prompt_qkv(x, mod_p, norm_g, w_qkv, q_gain, k_gain, new_rows, state, shift):
    B, L, _ = x.shape
    tm = ROW_TILE
    n_tiles = L // tm
    const2 = lambda b, i: (0, 0)
    in_specs = [
        pl.BlockSpec((1, tm, D_MODEL), lambda b, i: (b, i, 0)),
        pl.BlockSpec((1, 1, 3 * D_MODEL), lambda b, i: (b, 0, 0)),
        pl.BlockSpec((1, D_MODEL), const2),
        pl.BlockSpec((D_MODEL, QKV_W), const2, pipeline_mode=pl.Buffered(1)),
        pl.BlockSpec((N_GROUPS, HEAD_DIM), const2),
        pl.BlockSpec((N_GROUPS, HEAD_DIM), const2),
        pl.BlockSpec((N_GROUPS - 1, tm, tm), lambda b, i: (0, 0, 0), pipeline_mode=pl.Buffered(1)),
        pl.BlockSpec(new_rows.shape, lambda b, i: (0, 0, 0), pipeline_mode=pl.Buffered(1)),
        pl.BlockSpec(memory_space=pl.ANY),
    ]
    perms = jnp.stack([_class_major_perm(tm, d) for _, d in ATTN_PATTERNS[1:]])
    out_specs, out_shape = [], []
    for t in range(3):
        for window, d in ATTN_PATTERNS:
            out_specs.append(pl.BlockSpec((1, d, N_HEADS, tm // d, HEAD_DIM), lambda b, i: (b, 0, 0, i, 0)))
            out_shape.append(jax.ShapeDtypeStruct((B, d, N_HEADS, L // d, HEAD_DIM), BF16))
    for window, d in ATTN_PATTERNS:
        w_eff = min(window, L)
        if w_eff >= tm:
            n_tail = w_eff // tm
            out_specs.append(pl.BlockSpec(
                (1, tm, 2 * N_HEADS, HEAD_DIM),
                lambda b, i, n_tail=n_tail: (b, jnp.maximum(i - (n_tiles - n_tail), 0), 0, 0)))
        else:
            out_specs.append(pl.BlockSpec((1, w_eff, 2 * N_HEADS, HEAD_DIM), lambda b, i: (b, 0, 0, 0)))
        out_shape.append(jax.ShapeDtypeStruct((B, w_eff, 2 * N_HEADS, HEAD_DIM), F32))
    out_specs.append(pl.BlockSpec(memory_space=pl.ANY))
    out_shape.append(jax.ShapeDtypeStruct(state.shape, F32))
    return pl.pallas_call(
        functools.partial(_qkv_body, n_tiles=n_tiles, shift=shift),
        grid=(B, n_tiles),
        in_specs=in_specs,
        out_specs=out_specs,
        out_shape=out_shape,
        scratch_shapes=_shift_scratch(),
        compiler_params=pltpu.CompilerParams(
            dimension_semantics=("arbitrary", "arbitrary"), vmem_limit_bytes=VMEM_LIMIT),
        name="prompt_qkv",
    )(x, mod_p, norm_g, w_qkv, q_gain, k_gain, perms, new_rows, state)


def _attn_body(q_ref, k_ref, v_ref, o_ref, l_ref, vx):
    _, d_blk, h_blk, M, _ = q_ref.shape
    n_q = M // Q_BLOCK
    for c in range(d_blk):
        for hh in range(h_blk):
            vx[c, hh, :, 0:HEAD_DIM] = v_ref[0, c, hh]
            vx[c, hh, :, HEAD_DIM:2 * HEAD_DIM] = jnp.ones((M, HEAD_DIM), BF16)
    ii = lax.broadcasted_iota(jnp.int32, (Q_BLOCK, 2 * Q_BLOCK), 0)
    jj = lax.broadcasted_iota(jnp.int32, (Q_BLOCK, 2 * Q_BLOCK), 1)
    dist = Q_BLOCK + ii - jj
    band_ok = (dist >= 0) & (dist <= N_BACK)
    causal_ok = (lax.broadcasted_iota(jnp.int32, (Q_BLOCK, Q_BLOCK), 1)
                 <= lax.broadcasted_iota(jnp.int32, (Q_BLOCK, Q_BLOCK), 0))
    nt = (((1,), (1,)), ((), ()))

    def block(c, hh, n, first):
        rows = pl.ds(pl.multiple_of(n * Q_BLOCK, Q_BLOCK), Q_BLOCK)
        if first:
            keys, ok = rows, causal_ok
        else:
            keys, ok = pl.ds(pl.multiple_of((n - 1) * Q_BLOCK, Q_BLOCK), 2 * Q_BLOCK), band_ok
        s = lax.dot_general(q_ref[0, c, hh, rows, :], k_ref[0, c, hh, keys, :], nt, preferred_element_type=F32)
        s = jnp.where(ok, s, NEG)
        m = jnp.max(s, axis=-1, keepdims=True)
        p = jnp.exp(s - m).astype(BF16)
        oe = jnp.dot(p, vx[c, hh, keys, :], preferred_element_type=F32)
        l = oe[:, HEAD_DIM:]
        o_ref[0, c, hh, rows, :] = oe[:, :HEAD_DIM] / l
        l_ref[0, c, hh, rows, :] = m + jnp.log(l)

    for c in range(d_blk):
        for hh in range(h_blk):
            block(c, hh, 0, True)
        if n_q <= 2:
            for n in range(1, n_q):
                for hh in range(h_blk):
                    block(c, hh, n, False)
        else:
            def step(n, carry, c=c):
                for hh in range(h_blk):
                    block(c, hh, n, False)
                return carry

            lax.fori_loop(1, n_q, step, 0, unroll=Q_UNROLL)


def _prompt_attn(q, k, v, d_blk, h_blk):
    B, d, H, M, E = q.shape
    blk = (1, d_blk, h_blk, M, E)
    spec = pl.BlockSpec(blk, lambda b, c, hh: (b, c, hh, 0, 0))
    shape = jax.ShapeDtypeStruct(q.shape, F32)
    return pl.pallas_call(
        _attn_body,
        grid=(B, d // d_blk, H // h_blk),
        in_specs=[spec, spec, spec],
        out_specs=[spec, spec],
        out_shape=[shape, shape],
        scratch_shapes=[pltpu.VMEM((d_blk, h_blk, M, 2 * E), BF16)],
        compiler_params=pltpu.CompilerParams(
            dimension_semantics=("arbitrary", "arbitrary", "arbitrary"), vmem_limit_bytes=VMEM_LIMIT),
        name="prompt_attn_d%d" % d,
    )(q, k, v)


def _pool_project(mixed, wp_ref, bp_ref, ps_ref):
    cols = []
    for g in range(len(POOL_WINDOWS)):
        lanes = slice(g * HEAD_DIM, (g + 1) * HEAD_DIM)
        y = jnp.dot(mixed[g].astype(BF16), wp_ref[g], preferred_element_type=F32) + bp_ref[g:g + 1, :]
        cols.append(y * ps_ref[:, lanes])
    return jnp.concatenate(cols, axis=-1)


def _merge_out(x, gate, a, p, z_a, z_b, g_a, g_b, wa_ref, wb_ref, wo_ref):
    ya = jnp.dot((a * _silu(z_a)).astype(BF16), wa_ref[...], preferred_element_type=F32)
    yb = jnp.dot((p * _silu(z_b)).astype(BF16), wb_ref[...], preferred_element_type=F32)
    m = _sigmoid(g_a) * ya + _sigmoid(g_b) * yb
    return x + gate * jnp.dot(m.astype(BF16), wo_ref[...], preferred_element_type=F32)


def _out_body(x_ref, mod_ref, ng_ref, w_ref, o0, l0, o1, l1, o2, l2, wp_ref, bp_ref, ps_ref,
              wa_ref, wb_ref, wo_ref, new_ref, src_a, src_b, dst_a_in, y_ref, pool_ref, dst_a, dst_b,
              ext, oscr, lscr, buf, rsem, wsem, nsem, *, n_tiles, shift):
    del dst_a_in
    i = pl.program_id(1)
    tm = x_ref.shape[1]
    step = pl.program_id(0) * n_tiles + i
    kinds, lo, hi = shift

    last_slot = lo + SHIFT_SUBSTEPS * pl.num_programs(0) * n_tiles - 1

    def shift_substep(k):
        _shift_substep(lo + SHIFT_SUBSTEPS * step + k, lo, hi, last_slot, kinds, [src_a, src_b], [dst_a, dst_b],
                       buf, rsem, wsem)

    shift_substep(0)

    @pl.when(step == 0)
    def _():
        _new_rows_copy(new_ref, dst_b, nsem).start()
        ext[0:16, :] = jnp.zeros((16, GROUP_W), F32)

    x = x_ref[0]
    h = _modulated_norm(x, ng_ref[...], mod_ref[0, :, 0:D_MODEL], mod_ref[0, :, D_MODEL:2 * D_MODEL])
    gate = mod_ref[0, :, 2 * D_MODEL:3 * D_MODEL]

    def proj(lo, width):
        return jnp.dot(h, w_ref[:, lo:lo + width], preferred_element_type=F32)

    z_a = proj(0, GROUP_W)
    u = proj(GROUP_W, GROUP_W)
    z_b = proj(2 * GROUP_W, GROUP_W)
    o_refs, l_refs = (o0, o1, o2), (l0, l1, l2)
    for g, (window, d) in enumerate(ATTN_PATTERNS):
        if d == 1:
            continue
        for hh in range(N_HEADS):
            for rr in range(d):
                rows = pl.ds(rr, tm // d, stride=d)
                oscr[g - 1, hh, rows, :] = o_refs[g][0, rr, hh]
                lscr[g - 1, hh, rows, :] = l_refs[g][0, rr, hh]
    a_cols = []
    for hh in range(N_HEADS):
        os_ = [o0[0, 0, hh], oscr[0, hh], oscr[1, hh]]
        ls_ = [l0[0, 0, hh], lscr[0, hh], lscr[1, hh]]
        mx = jnp.maximum(jnp.maximum(ls_[0], ls_[1]), ls_[2])
        es = [jnp.exp(l_ - mx) for l_ in ls_]
        den = es[0] + es[1] + es[2]
        a_cols.append((es[0] * os_[0] + es[1] * os_[1] + es[2] * os_[2]) / den)
    ya_in = (jnp.concatenate(a_cols, axis=-1) * _silu(z_a)).astype(BF16)
    shift_substep(1)

    @pl.when(step == 0)
    def _():
        _new_rows_copy(new_ref, dst_b, nsem).wait()

    g_a = proj(3 * GROUP_W, D_MODEL)
    ext[16:16 + tm, :] = u
    pos = i * tm + lax.broadcasted_iota(jnp.int32, (tm, 1), 0)
    mixed = []
    for g, w in enumerate(POOL_WINDOWS):
        lanes = pl.ds(g * HEAD_DIM, HEAD_DIM)
        acc = ext[pl.ds(16, tm), lanes]
        tok = acc
        for k in range(1, w):
            acc = acc + ext[pl.ds(16 - k, tm), lanes]
        cnt = jnp.minimum(w, pos + 1).astype(F32)
        mixed.append(acc / cnt - tok)
    pool_ref[0, 0] = ext[pl.ds(tm + 1, POOL_STATE), :]
    ext[0:16, :] = jnp.where(i == n_tiles - 1, 0.0, ext[tm:tm + 16, :])
    yb_in = (_pool_project(mixed, wp_ref, bp_ref, ps_ref) * _silu(z_b)).astype(BF16)
    shift_substep(2)

    g_b = proj(3 * GROUP_W + D_MODEL, D_MODEL)
    ya = jnp.dot(ya_in, wa_ref[...], preferred_element_type=F32)
    yb = jnp.dot(yb_in, wb_ref[...], preferred_element_type=F32)
    m = (_sigmoid(g_a) * ya + _sigmoid(g_b) * yb).astype(BF16)
    shift_substep(3)
    y_ref[0] = x + gate * jnp.dot(m, wo_ref[...], preferred_element_type=F32)


def _prompt_out(x, mod_p, norm_g, w_rest, ol, w_pool, b_pool, pool_scale, wa, wb, wo,
                new_rows, state_a, state_b, shifted_a, shift):
    B, L, _ = x.shape
    tm = ROW_TILE
    n_tiles = L // tm
    const2 = lambda b, i: (0, 0)
    one = pl.Buffered(1)
    in_specs = [
        pl.BlockSpec((1, tm, D_MODEL), lambda b, i: (b, i, 0)),
        pl.BlockSpec((1, 1, 3 * D_MODEL), lambda b, i: (b, 0, 0)),
        pl.BlockSpec((1, D_MODEL), const2),
        pl.BlockSpec((D_MODEL, REST_W), const2, pipeline_mode=one),
    ]
    for window, d in ATTN_PATTERNS:
        for _ in range(2):
            in_specs.append(pl.BlockSpec((1, d, N_HEADS, tm // d, HEAD_DIM), lambda b, i: (b, 0, 0, i, 0)))
    in_specs += [
        pl.BlockSpec((len(POOL_WINDOWS), HEAD_DIM, HEAD_DIM), lambda b, i: (0, 0, 0)),
        pl.BlockSpec((len(POOL_WINDOWS), HEAD_DIM), const2),
        pl.BlockSpec((1, GROUP_W), const2),
        pl.BlockSpec((GROUP_W, D_MODEL), const2, pipeline_mode=one),
        pl.BlockSpec((GROUP_W, D_MODEL), const2, pipeline_mode=one),
        pl.BlockSpec((D_MODEL, D_MODEL), const2, pipeline_mode=one),
        pl.BlockSpec(new_rows.shape, lambda b, i: (0, 0, 0), pipeline_mode=one),
    ]
    any_spec = pl.BlockSpec(memory_space=pl.ANY)
    in_specs += [any_spec] * 3
    return pl.pallas_call(
        functools.partial(_out_body, n_tiles=n_tiles, shift=shift),
        grid=(B, n_tiles),
        in_specs=in_specs,
        out_specs=[pl.BlockSpec((1, tm, D_MODEL), lambda b, i: (b, i, 0)),
                   pl.BlockSpec((1, 1, POOL_STATE, GROUP_W), lambda b, i: (0, b, 0, 0)),
                   any_spec, any_spec],
        out_shape=[jax.ShapeDtypeStruct((B, L, D_MODEL), F32),
                   jax.ShapeDtypeStruct((1, B, POOL_STATE, GROUP_W), F32),
                   jax.ShapeDtypeStruct(state_a.shape, F32),
                   jax.ShapeDtypeStruct(state_b.shape, F32)],
        scratch_shapes=[pltpu.VMEM((tm + 16, GROUP_W), F32),
                        pltpu.VMEM((2, N_HEADS, tm, HEAD_DIM), F32),
                        pltpu.VMEM((2, N_HEADS, tm, HEAD_DIM), F32)] + _shift_scratch(),
        input_output_aliases={len(in_specs) - 1: 2},
        compiler_params=pltpu.CompilerParams(
            dimension_semantics=("arbitrary", "arbitrary"), vmem_limit_bytes=OUT_VMEM_LIMIT),
        name="prompt_out",
    )(x, mod_p, norm_g, w_rest, *ol, w_pool, b_pool, pool_scale, wa, wb, wo,
      new_rows, state_a, state_b, shifted_a)


def _sample_proj_body(x_ref, mod_ref, ng_ref, w_ref, o_ref):
    h = _modulated_norm(x_ref[...], ng_ref[...], mod_ref[:, 0:D_MODEL], mod_ref[:, D_MODEL:2 * D_MODEL])
    o_ref[...] = jnp.dot(h, w_ref[...], preferred_element_type=F32)


def _sample_proj(xs, mod_s, norm_g, w):
    n, width = xs.shape[0], w.shape[1]
    tn = GROUP_W
    return pl.pallas_call(
        _sample_proj_body,
        grid=(width // tn,),
        in_specs=[pl.BlockSpec((n, D_MODEL), lambda j: (0, 0)),
                  pl.BlockSpec((n, 3 * D_MODEL), lambda j: (0, 0)),
                  pl.BlockSpec((1, D_MODEL), lambda j: (0, 0)),
                  pl.BlockSpec((D_MODEL, tn), lambda j: (0, j))],
        out_specs=pl.BlockSpec((n, tn), lambda j: (0, j)),
        out_shape=jax.ShapeDtypeStruct((n, width), F32),
        name="sample_proj",
    )(xs, mod_s, norm_g, w)


def _sample_attn_body(p_ref, s0, s1, s2, qg_ref, kg_ref, a_ref, shifted0, n1, n2):
    bt = p_ref.shape[0]
    s_refs = (s0, s1, s2)
    last = N_BACK - 1
    for b in range(bt):
        os_, ls_ = [], []
        shifted0[b, 0:last] = s0[b, 1:N_BACK]
        for g in range(N_GROUPS):
            row = lambda t: pl.ds((t * N_GROUPS + g) * N_HEADS, N_HEADS)
            q4 = _head_norm(p_ref[b, row(0), :], qg_ref[g:g + 1, :]) * Q_SCALE
            k4 = _head_norm(p_ref[b, row(1), :], kg_ref[g:g + 1, :])
            v4 = p_ref[b, row(2), :]
            if g == 0:
                shifted0[b, last, 0:N_HEADS, :] = k4
                shifted0[b, last, N_HEADS:2 * N_HEADS, :] = v4
            else:
                (n1, n2)[g - 1][b, 0:N_HEADS, :] = k4
                (n1, n2)[g - 1][b, N_HEADS:2 * N_HEADS, :] = v4
            keys = s_refs[g][b, :, 0:N_HEADS, :]
            vals = s_refs[g][b, :, N_HEADS:2 * N_HEADS, :]
            s_old = jnp.sum(keys * q4[None], axis=-1, keepdims=True)
            s_new = jnp.sum(k4 * q4, axis=-1, keepdims=True)
            m = jnp.maximum(jnp.max(s_old, axis=0), s_new)
            p_old = jnp.exp(s_old - m[None])
            p_new = jnp.exp(s_new - m)
            l = jnp.sum(p_old, axis=0) + p_new
            o = jnp.sum(p_old * vals, axis=0) + p_new * v4
            os_.append(o / l)
            ls_.append(m + jnp.log(l))
        mx = jnp.maximum(jnp.maximum(ls_[0], ls_[1]), ls_[2])
        es = [jnp.exp(l_ - mx) for l_ in ls_]
        a_ref[b] = (es[0] * os_[0] + es[1] * os_[1] + es[2] * os_[2]) / (es[0] + es[1] + es[2])


def _sample_attn(p3, views, q_gain, k_gain):
    n = p3.shape[0]
    bt = 8
    in_specs = [pl.BlockSpec((bt, p3.shape[1], HEAD_DIM), lambda j: (j, 0, 0))]
    for v in views:
        in_specs.append(pl.BlockSpec((bt, N_BACK, 2 * N_HEADS, HEAD_DIM), lambda j: (j, 0, 0, 0)))
    in_specs += [pl.BlockSpec((N_GROUPS, HEAD_DIM), lambda j: (0, 0))] * 2
    new_spec = pl.BlockSpec((bt, 2 * N_HEADS, HEAD_DIM), lambda j: (j, 0, 0))
    new_shape = jax.ShapeDtypeStruct((n, 2 * N_HEADS, HEAD_DIM), F32)
    state_spec = pl.BlockSpec((bt, N_BACK, 2 * N_HEADS, HEAD_DIM), lambda j: (j, 0, 0, 0))
    return pl.pallas_call(
        _sample_attn_body,
        grid=(n // bt,),
        in_specs=in_specs,
        out_specs=[pl.BlockSpec((bt, N_HEADS, HEAD_DIM), lambda j: (j, 0, 0)), state_spec, new_spec, new_spec],
        out_shape=[jax.ShapeDtypeStruct((n, N_HEADS, HEAD_DIM), F32),
                   jax.ShapeDtypeStruct(views[0].shape, F32), new_shape, new_shape],
        compiler_params=pltpu.CompilerParams(vmem_limit_bytes=VMEM_LIMIT),
        name="sample_attn",
    )(p3, *views, q_gain, k_gain)


def _sample_out_body(x_ref, mod_ref, r_ref, a_ref, sp_ref, wp_ref, bp_ref, ps_ref, wa_ref, wb_ref, wo_ref,
                     y_ref, pool_ref):
    u = r_ref[:, GROUP_W:2 * GROUP_W]
    mixed = []
    for g, w in enumerate(POOL_WINDOWS):
        lanes = pl.ds(g * HEAD_DIM, HEAD_DIM)
        tok = r_ref[:, pl.ds(GROUP_W + g * HEAD_DIM, HEAD_DIM)]
        acc = tok
        for k in range(1, w):
            acc = acc + sp_ref[POOL_STATE - k, :, lanes]
        mixed.append(acc / float(min(w, PAST_LEN + 1)) - tok)
    p = _pool_project(mixed, wp_ref, bp_ref, ps_ref)
    for k in range(POOL_STATE - 1):
        pool_ref[k] = sp_ref[k + 1]
    pool_ref[POOL_STATE - 1] = u
    y_ref[...] = _merge_out(
        x_ref[...], mod_ref[:, 2 * D_MODEL:3 * D_MODEL], a_ref[...], p,
        r_ref[:, 0:GROUP_W], r_ref[:, 2 * GROUP_W:3 * GROUP_W],
        r_ref[:, 3 * GROUP_W:3 * GROUP_W + D_MODEL], r_ref[:, 3 * GROUP_W + D_MODEL:REST_W],
        wa_ref, wb_ref, wo_ref)


def _sample_out(xs, mod_s, rest, a_s, sp_t, w_pool, b_pool, pool_scale, wa, wb, wo):
    n = xs.shape[0]
    return pl.pallas_call(
        _sample_out_body,
        out_shape=[jax.ShapeDtypeStruct((n, D_MODEL), F32),
                   jax.ShapeDtypeStruct((POOL_STATE, n, GROUP_W), F32)],
        compiler_params=pltpu.CompilerParams(vmem_limit_bytes=VMEM_LIMIT),
        name="sample_out",
    )(xs, mod_s, rest, a_s, sp_t, w_pool, b_pool, pool_scale, wa, wb, wo)


def kernel(x_prompt, x_sample, state_kv_w128, state_kv_w512, state_kv_w2048, state_pool, c_prompt, c_sample,
           norm_g, w_ada, b_ada, w_in, q_gain, k_gain, w_pool, b_pool, pool_scale, w_a_out, w_b_out, w_out):
    B, L, _ = x_prompt.shape
    n_s = x_sample.shape[0]
    w_qkv = w_in[0, :, :QKV_W].astype(BF16)
    w_rest = w_in[0, :, QKV_W:].astype(BF16)
    wa, wb, wo, wp = (w[0].astype(BF16) for w in (w_a_out, w_b_out, w_out, w_pool))
    qg, kg, bp = q_gain[0], k_gain[0], b_pool[0]

    mod = _ada(jnp.concatenate([c_prompt, c_sample], axis=0), w_ada[0], b_ada)
    mod_p = mod[:B].reshape(B, 1, 3 * D_MODEL)
    mod_s = mod[B:]

    xs = x_sample[:, 0, :]
    qkv_s = _sample_proj(xs, mod_s, norm_g, w_qkv)
    rest_s = _sample_proj(xs, mod_s, norm_g, w_rest)
    states = (state_kv_w128, state_kv_w512, state_kv_w2048)
    views = [s.reshape(n_s, N_BACK, d * ROWS_PER_POS, HEAD_DIM) for s, (_, d) in zip(states, ATTN_PATTERNS)]
    a_s, shifted_128, new_512, new_2048 = _sample_attn(
        qkv_s.reshape(n_s, QKV_W // HEAD_DIM, HEAD_DIM), views, qg, kg)
    y_s, pool_t = _sample_out(xs, mod_s, rest_s, a_s.reshape(n_s, GROUP_W), state_pool[0].transpose(1, 0, 2),
                              wp, bp, pool_scale, wa, wb, wo)

    flat_2048, flat_512 = (s.reshape(n_s, s.shape[2] * ROWS_PER_POS, HEAD_DIM) for s in states[:0:-1])
    kinds = _shift_plan([flat_2048.shape, flat_512.shape])
    n_sub = B * (L // ROW_TILE) * SHIFT_SUBSTEPS
    n_chunks = kinds[-1][-1]
    assert n_sub <= kinds[0][4] and n_chunks <= 2 * n_sub
    *qkv, t0, t1, t2, part_2048 = _prompt_qkv(x_prompt, mod_p, norm_g, w_qkv, qg, kg, new_2048, flat_2048,
                                              (kinds, 0, n_sub))
    ol = []
    for g, (d_blk, h_blk) in enumerate(((1, 2), (1, 4), (4, 4))):
        ol += _prompt_attn(qkv[g], qkv[N_GROUPS + g], qkv[2 * N_GROUPS + g], d_blk, h_blk)
    y_p, pool_p, shifted_2048, shifted_512 = _prompt_out(
        x_prompt, mod_p, norm_g, w_rest, ol, wp, bp, pool_scale, wa, wb, wo,
        new_512, flat_2048, flat_512, part_2048, (kinds, n_sub, n_chunks))
    kv_s = [o.reshape(s.shape) for o, s in zip((shifted_128, shifted_512, shifted_2048), states)]
    kv_p = [t.reshape(1, B, t.shape[1], 2, N_HEADS, HEAD_DIM) for t in (t0, t1, t2)]

    return (y_p, y_s.reshape(n_s, 1, D_MODEL), kv_p[0], kv_p[1], kv_p[2], pool_p,
            kv_s[0], kv_s[1], kv_s[2], pool_t.transpose(1, 0, 2)[None])
```

```python
import functools

import jax
import jax.numpy as jnp
from jax import lax
from jax.experimental import pallas as pl
from jax.experimental.pallas import tpu as pltpu

F32 = jnp.float32
BF16 = jnp.bfloat16

D_MODEL = 1024
HEAD_DIM = 128
N_HEADS = 4
GROUP_W = N_HEADS * HEAD_DIM
ATTN_PATTERNS = ((128, 1), (512, 4), (2048, 16))
N_GROUPS = len(ATTN_PATTERNS)
N_BACK = 128
QKV_W = 3 * N_GROUPS * GROUP_W
REST_W = 3 * GROUP_W + 2 * D_MODEL
POOL_WINDOWS = (2, 4, 8, 16)
POOL_STATE = 15
PAST_LEN = 8192
EPS = 1e-6
Q_SCALE = HEAD_DIM ** -0.5
NEG = -1e30

ROW_TILE = 512
Q_BLOCK = 128
Q_UNROLL = 8
VMEM_LIMIT = 56 * 1024 * 1024
OUT_VMEM_LIMIT = 60 * 1024 * 1024


def _sigmoid(v):
    return 0.5 * jnp.tanh(0.5 * v) + 0.5


def _silu(v):
    return v * _sigmoid(v)


def _modulated_norm(x, norm_g, shift, scale):
    ms = jnp.mean(x * x, axis=-1, keepdims=True)
    return (x * lax.rsqrt(ms + EPS) * norm_g * (1.0 + scale) + shift).astype(BF16)


def _head_norm(r, gain):
    ms = jnp.mean(r * r, axis=-1, keepdims=True)
    return r * lax.rsqrt(ms + EPS) * gain


def _ada_body(c_ref, w_ref, b_ref, o_ref):
    s = _silu(c_ref[...]).astype(BF16)
    o_ref[...] = jnp.dot(s, w_ref[...].astype(BF16), preferred_element_type=F32) + b_ref[...]


def _ada(c_all, w_ada, b_ada):
    n = c_all.shape[0]
    return pl.pallas_call(
        _ada_body,
        grid=(3,),
        in_specs=[pl.BlockSpec((n, D_MODEL), lambda j: (0, 0)),
                  pl.BlockSpec((D_MODEL, D_MODEL), lambda j: (0, j)),
                  pl.BlockSpec((1, D_MODEL), lambda j: (0, j))],
        out_specs=pl.BlockSpec((n, D_MODEL), lambda j: (0, j)),
        out_shape=jax.ShapeDtypeStruct((n, 3 * D_MODEL), F32),
        name="ada",
    )(c_all, w_ada, b_ada)


ROWS_PER_POS = 2 * N_HEADS
SHIFT_NB = 8
SHIFT_ROWS = 712
SHIFT_SUBSTEPS = 4
SHIFT_AHEAD = 2
SHIFT_SLOTS = 2 * SHIFT_AHEAD
SHIFT_PRIORITY = 1


def _shift_plan(state_shapes):
    kinds, c0 = [], 0
    for idx, (n_batch, rows_total, _) in enumerate(state_shapes):
        keep = rows_total - ROWS_PER_POS
        pieces = min(p for p in range(1, keep)
                     if keep % (p * ROWS_PER_POS) == 0 and keep // p <= SHIFT_ROWS)
        n = (n_batch // SHIFT_NB) * pieces
        kinds.append((idx, keep // pieces, pieces, c0, c0 + n))
        c0 += n
    return kinds


def _shift_substep(j, lo, hi, last_slot, kinds, srcs, dsts, buf, rsem, wsem):
    def copy(jj, kind, write):
        idx, rows, pieces, c0, _ = kind
        local = jj - c0
        batches = pl.ds((local // pieces) * SHIFT_NB, SHIFT_NB)
        first = (local % pieces) * rows
        slot = jj % SHIFT_SLOTS
        stage = buf.at[slot, :, pl.ds(0, rows), :]
        if write:
            return pltpu.make_async_copy(stage, dsts[idx].at[batches, pl.ds(first, rows), :], wsem.at[slot])
        return pltpu.make_async_copy(srcs[idx].at[batches, pl.ds(first + ROWS_PER_POS, rows), :], stage, rsem.at[slot])

    def for_chunk(jj, also, fn):
        for kind in kinds:
            first, end = max(kind[3], lo), min(kind[4], hi)
            if first < end:
                cond = (jj >= first) & (jj < end)

                @pl.when(cond if also is None else cond & also)
                def _():
                    fn(functools.partial(copy, jj, kind))

    for a in range(SHIFT_AHEAD):
        for_chunk(j + a, j == lo, lambda cp: cp(False).start(priority=SHIFT_PRIORITY))

    def landed(cp):
        cp(False).wait()
        cp(True).start(priority=SHIFT_PRIORITY)

    for_chunk(j, None, landed)
    for_chunk(j - SHIFT_AHEAD, None, lambda cp: cp(True).wait())
    for_chunk(j + SHIFT_AHEAD, None, lambda cp: cp(False).start(priority=SHIFT_PRIORITY))
    for a in range(SHIFT_AHEAD):
        for_chunk(j - a, j == last_slot, lambda cp: cp(True).wait())


def _new_rows_copy(new_ref, dst, nsem):
    keep = dst.shape[1] - ROWS_PER_POS
    return pltpu.make_async_copy(new_ref, dst.at[:, pl.ds(keep, ROWS_PER_POS), :], nsem)


def _shift_scratch():
    return [pltpu.VMEM((SHIFT_SLOTS, SHIFT_NB, SHIFT_ROWS, HEAD_DIM), F32),
            pltpu.SemaphoreType.DMA((SHIFT_SLOTS,)), pltpu.SemaphoreType.DMA((SHIFT_SLOTS,)),
            pltpu.SemaphoreType.DMA(())]


def _class_major_perm(tm, d):
    i = jnp.arange(tm)
    src_row = (i % (tm // d)) * d + i // (tm // d)
    return (src_row[:, None] == jnp.arange(tm)[None, :]).astype(BF16)


SHIFT_SITES = ((0, 1), (1, 1), (2, 0))


def _qkv_body(x_ref, mod_ref, ng_ref, w_ref, qg_ref, kg_ref, perm_ref, new_ref, src_ref, *refs, n_tiles, shift):
    qkv_refs = (refs[0:3], refs[3:6], refs[6:9])
    tail_refs = refs[9:12]
    dst_ref, buf, rsem, wsem, nsem = refs[12:17]
    i = pl.program_id(1)
    tm = x_ref.shape[1]
    step = pl.program_id(0) * n_tiles + i
    kinds, lo, hi = shift

    last_slot = lo + SHIFT_SUBSTEPS * pl.num_programs(0) * n_tiles - 1

    def shift_substep(k):
        _shift_substep(lo + SHIFT_SUBSTEPS * step + k, lo, hi, last_slot, kinds, [src_ref], [dst_ref],
                       buf, rsem, wsem)

    shift_substep(0)

    @pl.when(step == 0)
    def _():
        _new_rows_copy(new_ref, dst_ref, nsem).start()

    h = _modulated_norm(x_ref[0], ng_ref[...], mod_ref[0, :, 0:D_MODEL], mod_ref[0, :, D_MODEL:2 * D_MODEL])
    h_by_group = [h] + [jnp.dot(perm_ref[g - 1], h, preferred_element_type=F32).astype(BF16)
                        for g in range(1, N_GROUPS)]

    def heads(t, g, lhs):
        c = t * N_GROUPS + g
        res = jnp.dot(lhs, w_ref[:, c * GROUP_W:(c + 1) * GROUP_W], preferred_element_type=F32)
        out = []
        for hh in range(N_HEADS):
            r = res[:, hh * HEAD_DIM:(hh + 1) * HEAD_DIM]
            if t == 0:
                r = _head_norm(r, qg_ref[g:g + 1, :]) * Q_SCALE
            elif t == 1:
                r = _head_norm(r, kg_ref[g:g + 1, :])
            out.append(r)
        return out

    def write_tail(t, g, rs):
        window, d = ATTN_PATTERNS[g]
        per = tm // d
        for hh, r in enumerate(rs):
            row = (t - 1) * N_HEADS + hh
            if d == 1:
                tail_refs[g][0, :, row, :] = r if window >= tm else r[tm - window:, :]
            else:
                assert window >= tm
                for rr in range(d):
                    tail_refs[g][0, pl.ds(rr, per, stride=d), row, :] = r[rr * per:(rr + 1) * per, :]

    for t in range(3):
        for g, (window, d) in enumerate(ATTN_PATTERNS):
            if (t, g) in SHIFT_SITES:
                shift_substep(SHIFT_SITES.index((t, g)) + 1)
            if (t, g) == SHIFT_SITES[0]:
                @pl.when(step == 0)
                def _():
                    _new_rows_copy(new_ref, dst_ref, nsem).wait()

            rs = heads(t, g, h_by_group[g])
            per = tm // d
            for hh, r in enumerate(rs):
                for rr in range(d):
                    qkv_refs[t][g][0, rr, hh] = r[rr * per:(rr + 1) * per, :].astype(BF16)
            if t > 0:
                n_tail = max(window // tm, 1)

                @pl.when(i >= n_tiles - n_tail)
                def _():
                    write_tail(t, g, rs)


def _prompt_qkv(x, mod_p, norm_g, w_qkv, q_gain, k_gain, new_rows, state, shift):
    B, L, _ = x.shape
    tm = ROW_TILE
    n_tiles = L // tm
    const2 = lambda b, i: (0, 0)
    in_specs = [
        pl.BlockSpec((1, tm, D_MODEL), lambda b, i: (b, i, 0)),
        pl.BlockSpec((1, 1, 3 * D_MODEL), lambda b, i: (b, 0, 0)),
        pl.BlockSpec((1, D_MODEL), const2),
        pl.BlockSpec((D_MODEL, QKV_W), const2, pipeline_mode=pl.Buffered(1)),
        pl.BlockSpec((N_GROUPS, HEAD_DIM), const2),
        pl.BlockSpec((N_GROUPS, HEAD_DIM), const2),
        pl.BlockSpec((N_GROUPS - 1, tm, tm), lambda b, i: (0, 0, 0), pipeline_mode=pl.Buffered(1)),
        pl.BlockSpec(new_rows.shape, lambda b, i: (0, 0, 0), pipeline_mode=pl.Buffered(1)),
        pl.BlockSpec(memory_space=pl.ANY),
    ]
    perms = jnp.stack([_class_major_perm(tm, d) for _, d in ATTN_PATTERNS[1:]])
    out_specs, out_shape = [], []
    for t in range(3):
        for window, d in ATTN_PATTERNS:
            out_specs.append(pl.BlockSpec((1, d, N_HEADS, tm // d, HEAD_DIM), lambda b, i: (b, 0, 0, i, 0)))
            out_shape.append(jax.ShapeDtypeStruct((B, d, N_HEADS, L // d, HEAD_DIM), BF16))
    for window, d in ATTN_PATTERNS:
        w_eff = min(window, L)
        if w_eff >= tm:
            n_tail = w_eff // tm
            out_specs.append(pl.BlockSpec(
                (1, tm, 2 * N_HEADS, HEAD_DIM),
                lambda b, i, n_tail=n_tail: (b, jnp.maximum(i - (n_tiles - n_tail), 0), 0, 0)))
        else:
            out_specs.append(pl.BlockSpec((1, w_eff, 2 * N_HEADS, HEAD_DIM), lambda b, i: (b, 0, 0, 0)))
        out_shape.append(jax.ShapeDtypeStruct((B, w_eff, 2 * N_HEADS, HEAD_DIM), F32))
    out_specs.append(pl.BlockSpec(memory_space=pl.ANY))
    out_shape.append(jax.ShapeDtypeStruct(state.shape, F32))
    return pl.pallas_call(
        functools.partial(_qkv_body, n_tiles=n_tiles, shift=shift),
        grid=(B, n_tiles),
        in_specs=in_specs,
        out_specs=out_specs,
        out_shape=out_shape,
        scratch_shapes=_shift_scratch(),
        compiler_params=pltpu.CompilerParams(
            dimension_semantics=("arbitrary", "arbitrary"), vmem_limit_bytes=VMEM_LIMIT),
        name="prompt_qkv",
    )(x, mod_p, norm_g, w_qkv, q_gain, k_gain, perms, new_rows, state)


def _attn_body(q_ref, k_ref, v_ref, o_ref, l_ref, vx):
    _, d_blk, h_blk, M, _ = q_ref.shape
    n_q = M // Q_BLOCK
    for c in range(d_blk):
        for hh in range(h_blk):
            vx[c, hh, :, 0:HEAD_DIM] = v_ref[0, c, hh]
            vx[c, hh, :, HEAD_DIM:2 * HEAD_DIM] = jnp.ones((M, HEAD_DIM), BF16)
    ii = lax.broadcasted_iota(jnp.int32, (Q_BLOCK, 2 * Q_BLOCK), 0)
    jj = lax.broadcasted_iota(jnp.int32, (Q_BLOCK, 2 * Q_BLOCK), 1)
    dist = Q_BLOCK + ii - jj
    band_ok = (dist >= 0) & (dist <= N_BACK)
    causal_ok = (lax.broadcasted_iota(jnp.int32, (Q_BLOCK, Q_BLOCK), 1)
                 <= lax.broadcasted_iota(jnp.int32, (Q_BLOCK, Q_BLOCK), 0))
    nt = (((1,), (1,)), ((), ()))

    def block(c, hh, n, first):
        rows = pl.ds(pl.multiple_of(n * Q_BLOCK, Q_BLOCK), Q_BLOCK)
        if first:
            keys, ok = rows, causal_ok
        else:
            keys, ok = pl.ds(pl.multiple_of((n - 1) * Q_BLOCK, Q_BLOCK), 2 * Q_BLOCK), band_ok
        s = lax.dot_general(q_ref[0, c, hh, rows, :], k_ref[0, c, hh, keys, :], nt, preferred_element_type=F32)
        s = jnp.where(ok, s, NEG)
        m = jnp.max(s, axis=-1, keepdims=True)
        p = jnp.exp(s - m).astype(BF16)
        oe = jnp.dot(p, vx[c, hh, keys, :], preferred_element_type=F32)
        l = oe[:, HEAD_DIM:]
        o_ref[0, c, hh, rows, :] = (oe[:, :HEAD_DIM] / l).astype(BF16)
        l_ref[0, c, hh, pl.ds(n, 1), :] = jnp.transpose(m + jnp.log(l))[0:1, :]

    for c in range(d_blk):
        for hh in range(h_blk):
            block(c, hh, 0, True)
        if n_q <= 2:
            for n in range(1, n_q):
                for hh in range(h_blk):
                    block(c, hh, n, False)
        else:
            def step(n, carry, c=c):
                for hh in range(h_blk):
                    block(c, hh, n, False)
                return carry

            lax.fori_loop(1, n_q, step, 0, unroll=Q_UNROLL)


def _prompt_attn(q, k, v, d_blk, h_blk):
    B, d, H, M, E = q.shape
    blk = (1, d_blk, h_blk, M, E)
    spec = pl.BlockSpec(blk, lambda b, c, hh: (b, c, hh, 0, 0))
    lse_spec = pl.BlockSpec((1, d_blk, h_blk, M // Q_BLOCK, Q_BLOCK), lambda b, c, hh: (b, c, hh, 0, 0))
    return pl.pallas_call(
        _attn_body,
        grid=(B, d // d_blk, H // h_blk),
        in_specs=[spec, spec, spec],
        out_specs=[spec, lse_spec],
        out_shape=[jax.ShapeDtypeStruct(q.shape, BF16),
                   jax.ShapeDtypeStruct((B, d, H, M // Q_BLOCK, Q_BLOCK), F32)],
        scratch_shapes=[pltpu.VMEM((d_blk, h_blk, M, 2 * E), BF16)],
        compiler_params=pltpu.CompilerParams(
            dimension_semantics=("arbitrary", "arbitrary", "arbitrary"), vmem_limit_bytes=VMEM_LIMIT),
        name="prompt_attn_d%d" % d,
    )(q, k, v)


def _pool_project(mixed, wp_ref, bp_ref, ps_ref):
    cols = []
    for g in range(len(POOL_WINDOWS)):
        lanes = slice(g * HEAD_DIM, (g + 1) * HEAD_DIM)
        y = jnp.dot(mixed[g].astype(BF16), wp_ref[g], preferred_element_type=F32) + bp_ref[g:g + 1, :]
        cols.append(y * ps_ref[:, lanes])
    return jnp.concatenate(cols, axis=-1)


def _merge_out(x, gate, a, p, z_a, z_b, g_a, g_b, wa_ref, wb_ref, wo_ref):
    ya = jnp.dot((a * _silu(z_a)).astype(BF16), wa_ref[...], preferred_element_type=F32)
    yb = jnp.dot((p * _silu(z_b)).astype(BF16), wb_ref[...], preferred_element_type=F32)
    m = _sigmoid(g_a) * ya + _sigmoid(g_b) * yb
    return x + gate * jnp.dot(m.astype(BF16), wo_ref[...], preferred_element_type=F32)


def _out_body(x_ref, mod_ref, ng_ref, w_ref, o0, l0, o1, l1, o2, l2, wp_ref, bp_ref, ps_ref,
              wa_ref, wb_ref, wo_ref, new_ref, src_a, src_b, dst_a_in, y_ref, pool_ref, dst_a, dst_b,
              ext, oscr, lscr, ltmp, buf, rsem, wsem, nsem, *, n_tiles, shift):
    del dst_a_in
    i = pl.program_id(1)
    tm = x_ref.shape[1]
    step = pl.program_id(0) * n_tiles + i
    kinds, lo, hi = shift

    last_slot = lo + SHIFT_SUBSTEPS * pl.num_programs(0) * n_tiles - 1

    def shift_substep(k):
        _shift_substep(lo + SHIFT_SUBSTEPS * step + k, lo, hi, last_slot, kinds, [src_a, src_b], [dst_a, dst_b],
                       buf, rsem, wsem)

    shift_substep(0)

    @pl.when(step == 0)
    def _():
        _new_rows_copy(new_ref, dst_b, nsem).start()
        ext[0:16, :] = jnp.zeros((16, GROUP_W), F32)

    x = x_ref[0]
    h = _modulated_norm(x, ng_ref[...], mod_ref[0, :, 0:D_MODEL], mod_ref[0, :, D_MODEL:2 * D_MODEL])
    gate = mod_ref[0, :, 2 * D_MODEL:3 * D_MODEL]

    def proj(lo, width):
        return jnp.dot(h, w_ref[:, lo:lo + width], preferred_element_type=F32)

    z_a = proj(0, GROUP_W)
    u = proj(GROUP_W, GROUP_W)
    z_b = proj(2 * GROUP_W, GROUP_W)
    o_refs, l_refs = (o0, o1, o2), (l0, l1, l2)

    def lse_tile(g, c, hh, n):
        row = l_refs[g][0, c, hh, pl.ds(n, 1), :]
        return jnp.transpose(jnp.broadcast_to(row, (Q_BLOCK, HEAD_DIM)))

    for g, (window, d) in enumerate(ATTN_PATTERNS):
        per = tm // d
        for hh in range(N_HEADS):
            for c in range(d):
                if d > 1:
                    oscr[g - 1, hh, pl.ds(c, per, stride=d), :] = o_refs[g][0, c, hh].astype(F32)
                if per >= Q_BLOCK:
                    for k in range(per // Q_BLOCK):
                        tile = lse_tile(g, c, hh, i * (per // Q_BLOCK) + k)
                        lscr[g, hh, pl.ds(c + k * Q_BLOCK * d, Q_BLOCK, stride=d), :] = tile
                else:
                    parts = Q_BLOCK // per
                    slot = (c * N_HEADS + hh) % ltmp.shape[0]
                    ltmp[slot] = lse_tile(g, c, hh, i // parts)
                    first = pl.multiple_of((i % parts) * per, per)
                    lscr[g, hh, pl.ds(c, per, stride=d), :] = ltmp[slot, pl.ds(first, per), :]
    a_cols = []
    for hh in range(N_HEADS):
        os_ = [o0[0, 0, hh].astype(F32), oscr[0, hh], oscr[1, hh]]
        ls_ = [lscr[0, hh], lscr[1, hh], lscr[2, hh]]
        mx = jnp.maximum(jnp.maximum(ls_[0], ls_[1]), ls_[2])
        es = [jnp.exp(l_ - mx) for l_ in ls_]
        den = es[0] + es[1] + es[2]
        a_cols.append((es[0] * os_[0] + es[1] * os_[1] + es[2] * os_[2]) / den)
    ya_in = (jnp.concatenate(a_cols, axis=-1) * _silu(z_a)).astype(BF16)
    shift_substep(1)

    @pl.when(step == 0)
    def _():
        _new_rows_copy(new_ref, dst_b, nsem).wait()

    g_a = proj(3 * GROUP_W, D_MODEL)
    ext[16:16 + tm, :] = u
    pos = i * tm + lax.broadcasted_iota(jnp.int32, (tm, 1), 0)
    mixed = []
    for g, w in enumerate(POOL_WINDOWS):
        lanes = pl.ds(g * HEAD_DIM, HEAD_DIM)
        acc = ext[pl.ds(16, tm), lanes]
        tok = acc
        for k in range(1, w):
            acc = acc + ext[pl.ds(16 - k, tm), lanes]
        cnt = jnp.minimum(w, pos + 1).astype(F32)
        mixed.append(acc / cnt - tok)
    pool_ref[0, 0] = ext[pl.ds(tm + 1, POOL_STATE), :]
    ext[0:16, :] = jnp.where(i == n_tiles - 1, 0.0, ext[tm:tm + 16, :])
    yb_in = (_pool_project(mixed, wp_ref, bp_ref, ps_ref) * _silu(z_b)).astype(BF16)
    shift_substep(2)

    g_b = proj(3 * GROUP_W + D_MODEL, D_MODEL)
    ya = jnp.dot(ya_in, wa_ref[...], preferred_element_type=F32)
    yb = jnp.dot(yb_in, wb_ref[...], preferred_element_type=F32)
    m = (_sigmoid(g_a) * ya + _sigmoid(g_b) * yb).astype(BF16)
    shift_substep(3)
    y_ref[0] = x + gate * jnp.dot(m, wo_ref[...], preferred_element_type=F32)


def _prompt_out(x, mod_p, norm_g, w_rest, ol, w_pool, b_pool, pool_scale, wa, wb, wo,
                new_rows, state_a, state_b, shifted_a, shift):
    B, L, _ = x.shape
    tm = ROW_TILE
    n_tiles = L // tm
    const2 = lambda b, i: (0, 0)
    one = pl.Buffered(1)
    in_specs = [
        pl.BlockSpec((1, tm, D_MODEL), lambda b, i: (b, i, 0)),
        pl.BlockSpec((1, 1, 3 * D_MODEL), lambda b, i: (b, 0, 0)),
        pl.BlockSpec((1, D_MODEL), const2),
        pl.BlockSpec((D_MODEL, REST_W), const2, pipeline_mode=one),
    ]
    for window, d in ATTN_PATTERNS:
        in_specs.append(pl.BlockSpec((1, d, N_HEADS, tm // d, HEAD_DIM), lambda b, i: (b, 0, 0, i, 0)))
        in_specs.append(pl.BlockSpec((1, d, N_HEADS, L // d // Q_BLOCK, Q_BLOCK), lambda b, i: (b, 0, 0, 0, 0)))
    in_specs += [
        pl.BlockSpec((len(POOL_WINDOWS), HEAD_DIM, HEAD_DIM), lambda b, i: (0, 0, 0)),
        pl.BlockSpec((len(POOL_WINDOWS), HEAD_DIM), const2),
        pl.BlockSpec((1, GROUP_W), const2),
        pl.BlockSpec((GROUP_W, D_MODEL), const2, pipeline_mode=one),
        pl.BlockSpec((GROUP_W, D_MODEL), const2, pipeline_mode=one),
        pl.BlockSpec((D_MODEL, D_MODEL), const2, pipeline_mode=one),
        pl.BlockSpec(new_rows.shape, lambda b, i: (0, 0, 0), pipeline_mode=one),
    ]
    any_spec = pl.BlockSpec(memory_space=pl.ANY)
    in_specs += [any_spec] * 3
    return pl.pallas_call(
        functools.partial(_out_body, n_tiles=n_tiles, shift=shift),
        grid=(B, n_tiles),
        in_specs=in_specs,
        out_specs=[pl.BlockSpec((1, tm, D_MODEL), lambda b, i: (b, i, 0)),
                   pl.BlockSpec((1, 1, POOL_STATE, GROUP_W), lambda b, i: (0, b, 0, 0)),
                   any_spec, any_spec],
        out_shape=[jax.ShapeDtypeStruct((B, L, D_MODEL), F32),
                   jax.ShapeDtypeStruct((1, B, POOL_STATE, GROUP_W), F32),
                   jax.ShapeDtypeStruct(state_a.shape, F32),
                   jax.ShapeDtypeStruct(state_b.shape, F32)],
        scratch_shapes=[pltpu.VMEM((tm + 16, GROUP_W), F32),
                        pltpu.VMEM((N_GROUPS - 1, N_HEADS, tm, HEAD_DIM), F32),
                        pltpu.VMEM((N_GROUPS, N_HEADS, tm, HEAD_DIM), F32),
                        pltpu.VMEM((8, Q_BLOCK, HEAD_DIM), F32)] + _shift_scratch(),
        input_output_aliases={len(in_specs) - 1: 2},
        compiler_params=pltpu.CompilerParams(
            dimension_semantics=("arbitrary", "arbitrary"), vmem_limit_bytes=OUT_VMEM_LIMIT),
        name="prompt_out",
    )(x, mod_p, norm_g, w_rest, *ol, w_pool, b_pool, pool_scale, wa, wb, wo,
      new_rows, state_a, state_b, shifted_a)


def _sample_proj_body(x_ref, mod_ref, ng_ref, w_ref, o_ref):
    h = _modulated_norm(x_ref[...], ng_ref[...], mod_ref[:, 0:D_MODEL], mod_ref[:, D_MODEL:2 * D_MODEL])
    o_ref[...] = jnp.dot(h, w_ref[...], preferred_element_type=F32)


def _sample_proj(xs, mod_s, norm_g, w):
    n, width = xs.shape[0], w.shape[1]
    tn = max(t for t in range(HEAD_DIM, width // 2 + 1, HEAD_DIM) if width % t == 0)
    return pl.pallas_call(
        _sample_proj_body,
        grid=(width // tn,),
        in_specs=[pl.BlockSpec((n, D_MODEL), lambda j: (0, 0)),
                  pl.BlockSpec((n, 3 * D_MODEL), lambda j: (0, 0)),
                  pl.BlockSpec((1, D_MODEL), lambda j: (0, 0)),
                  pl.BlockSpec((D_MODEL, tn), lambda j: (0, j))],
        out_specs=pl.BlockSpec((n, tn), lambda j: (0, j)),
        out_shape=jax.ShapeDtypeStruct((n, width), F32),
        name="sample_proj",
    )(xs, mod_s, norm_g, w)


def _sample_attn_body(p_ref, s0, s1, s2, qg_ref, kg_ref, a_ref, shifted0, n1, n2):
    bt = p_ref.shape[0]
    s_refs = (s0, s1, s2)
    last = N_BACK - 1
    for b in range(bt):
        os_, ls_ = [], []
        shifted0[b, 0:last] = s0[b, 1:N_BACK]
        for g in range(N_GROUPS):
            row = lambda t: pl.ds((t * N_GROUPS + g) * N_HEADS, N_HEADS)
            q4 = _head_norm(p_ref[b, row(0), :], qg_ref[g:g + 1, :]) * Q_SCALE
            k4 = _head_norm(p_ref[b, row(1), :], kg_ref[g:g + 1, :])
            v4 = p_ref[b, row(2), :]
            if g == 0:
                shifted0[b, last, 0:N_HEADS, :] = k4
                shifted0[b, last, N_HEADS:2 * N_HEADS, :] = v4
            else:
                (n1, n2)[g - 1][b, 0:N_HEADS, :] = k4
                (n1, n2)[g - 1][b, N_HEADS:2 * N_HEADS, :] = v4
            keys = s_refs[g][b, :, 0:N_HEADS, :]
            vals = s_refs[g][b, :, N_HEADS:2 * N_HEADS, :]
            s_old = jnp.sum(keys * q4[None], axis=-1, keepdims=True)
            s_new = jnp.sum(k4 * q4, axis=-1, keepdims=True)
            m = jnp.maximum(jnp.max(s_old, axis=0), s_new)
            p_old = jnp.exp(s_old - m[None])
            p_new = jnp.exp(s_new - m)
            l = jnp.sum(p_old, axis=0) + p_new
            o = jnp.sum(p_old * vals, axis=0) + p_new * v4
            os_.append(o / l)
            ls_.append(m + jnp.log(l))
        mx = jnp.maximum(jnp.maximum(ls_[0], ls_[1]), ls_[2])
        es = [jnp.exp(l_ - mx) for l_ in ls_]
        a_ref[b] = (es[0] * os_[0] + es[1] * os_[1] + es[2] * os_[2]) / (es[0] + es[1] + es[2])


def _sample_attn(p3, views, q_gain, k_gain):
    n = p3.shape[0]
    bt = 8
    in_specs = [pl.BlockSpec((bt, p3.shape[1], HEAD_DIM), lambda j: (j, 0, 0))]
    for v in views:
        in_specs.append(pl.BlockSpec((bt, N_BACK, 2 * N_HEADS, HEAD_DIM), lambda j: (j, 0, 0, 0)))
    in_specs += [pl.BlockSpec((N_GROUPS, HEAD_DIM), lambda j: (0, 0))] * 2
    new_spec = pl.BlockSpec((bt, 2 * N_HEADS, HEAD_DIM), lambda j: (j, 0, 0))
    new_shape = jax.ShapeDtypeStruct((n, 2 * N_HEADS, HEAD_DIM), F32)
    state_spec = pl.BlockSpec((bt, N_BACK, 2 * N_HEADS, HEAD_DIM), lambda j: (j, 0, 0, 0))
    return pl.pallas_call(
        _sample_attn_body,
        grid=(n // bt,),
        in_specs=in_specs,
        out_specs=[pl.BlockSpec((bt, N_HEADS, HEAD_DIM), lambda j: (j, 0, 0)), state_spec, new_spec, new_spec],
        out_shape=[jax.ShapeDtypeStruct((n, N_HEADS, HEAD_DIM), F32),
                   jax.ShapeDtypeStruct(views[0].shape, F32), new_shape, new_shape],
        compiler_params=pltpu.CompilerParams(vmem_limit_bytes=VMEM_LIMIT),
        name="sample_attn",
    )(p3, *views, q_gain, k_gain)


def _sample_out_body(x_ref, mod_ref, r_ref, a_ref, sp_ref, wp_ref, bp_ref, ps_ref, wa_ref, wb_ref, wo_ref,
                     y_ref, pool_ref):
    u = r_ref[:, GROUP_W:2 * GROUP_W]
    mixed = []
    for g, w in enumerate(POOL_WINDOWS):
        lanes = pl.ds(g * HEAD_DIM, HEAD_DIM)
        tok = r_ref[:, pl.ds(GROUP_W + g * HEAD_DIM, HEAD_DIM)]
        acc = tok
        for k in range(1, w):
            acc = acc + sp_ref[POOL_STATE - k, :, lanes]
        mixed.append(acc / float(min(w, PAST_LEN + 1)) - tok)
    p = _pool_project(mixed, wp_ref, bp_ref, ps_ref)
    for k in range(POOL_STATE - 1):
        pool_ref[k] = sp_ref[k + 1]
    pool_ref[POOL_STATE - 1] = u
    y_ref[...] = _merge_out(
        x_ref[...], mod_ref[:, 2 * D_MODEL:3 * D_MODEL], a_ref[...], p,
        r_ref[:, 0:GROUP_W], r_ref[:, 2 * GROUP_W:3 * GROUP_W],
        r_ref[:, 3 * GROUP_W:3 * GROUP_W + D_MODEL], r_ref[:, 3 * GROUP_W + D_MODEL:REST_W],
        wa_ref, wb_ref, wo_ref)


def _sample_out(xs, mod_s, rest, a_s, sp_t, w_pool, b_pool, pool_scale, wa, wb, wo):
    n = xs.shape[0]
    return pl.pallas_call(
        _sample_out_body,
        out_shape=[jax.ShapeDtypeStruct((n, D_MODEL), F32),
                   jax.ShapeDtypeStruct((POOL_STATE, n, GROUP_W), F32)],
        compiler_params=pltpu.CompilerParams(vmem_limit_bytes=VMEM_LIMIT),
        name="sample_out",
    )(xs, mod_s, rest, a_s, sp_t, w_pool, b_pool, pool_scale, wa, wb, wo)


def kernel(x_prompt, x_sample, state_kv_w128, state_kv_w512, state_kv_w2048, state_pool, c_prompt, c_sample,
           norm_g, w_ada, b_ada, w_in, q_gain, k_gain, w_pool, b_pool, pool_scale, w_a_out, w_b_out, w_out):
    B, L, _ = x_prompt.shape
    n_s = x_sample.shape[0]
    w_qkv = w_in[0, :, :QKV_W].astype(BF16)
    w_rest = w_in[0, :, QKV_W:].astype(BF16)
    wa, wb, wo, wp = (w[0].astype(BF16) for w in (w_a_out, w_b_out, w_out, w_pool))
    qg, kg, bp = q_gain[0], k_gain[0], b_pool[0]

    mod = _ada(jnp.concatenate([c_prompt, c_sample], axis=0), w_ada[0], b_ada)
    mod_p = mod[:B].reshape(B, 1, 3 * D_MODEL)
    mod_s = mod[B:]

    xs = x_sample[:, 0, :]
    qkv_s = _sample_proj(xs, mod_s, norm_g, w_qkv)
    rest_s = _sample_proj(xs, mod_s, norm_g, w_rest)
    states = (state_kv_w128, state_kv_w512, state_kv_w2048)
    views = [s.reshape(n_s, N_BACK, d * ROWS_PER_POS, HEAD_DIM) for s, (_, d) in zip(states, ATTN_PATTERNS)]
    a_s, shifted_128, new_512, new_2048 = _sample_attn(
        qkv_s.reshape(n_s, QKV_W // HEAD_DIM, HEAD_DIM), views, qg, kg)
    y_s, pool_t = _sample_out(xs, mod_s, rest_s, a_s.reshape(n_s, GROUP_W), state_pool[0].transpose(1, 0, 2),
                              wp, bp, pool_scale, wa, wb, wo)

    flat_2048, flat_512 = (s.reshape(n_s, s.shape[2] * ROWS_PER_POS, HEAD_DIM) for s in states[:0:-1])
    kinds = _shift_plan([flat_2048.shape, flat_512.shape])
    n_sub = B * (L // ROW_TILE) * SHIFT_SUBSTEPS
    n_chunks = kinds[-1][-1]
    assert n_sub <= kinds[0][4] and n_chunks <= 2 * n_sub
    *qkv, t0, t1, t2, part_2048 = _prompt_qkv(x_prompt, mod_p, norm_g, w_qkv, qg, kg, new_2048, flat_2048,
                                              (kinds, 0, n_sub))
    ol = []
    for g, (d_blk, h_blk) in enumerate(((1, 2), (1, 4), (4, 4))):
        ol += _prompt_attn(qkv[g], qkv[N_GROUPS + g], qkv[2 * N_GROUPS + g], d_blk, h_blk)
    y_p, pool_p, shifted_2048, shifted_512 = _prompt_out(
        x_prompt, mod_p, norm_g, w_rest, ol, wp, bp, pool_scale, wa, wb, wo,
        new_512, flat_2048, flat_512, part_2048, (kinds, n_sub, n_chunks))
    kv_s = [o.reshape(s.shape) for o, s in zip((shifted_128, shifted_512, shifted_2048), states)]
    kv_p = [t.reshape(1, B, t.shape[1], 2, N_HEADS, HEAD_DIM) for t in (t0, t1, t2)]

    return (y_p, y_s.reshape(n_s, 1, D_MODEL), kv_p[0], kv_p[1], kv_p[2], pool_p,
            kv_s[0], kv_s[1], kv_s[2], pool_t.transpose(1, 0, 2)[None])
```

```python
import functools

import jax
import jax.numpy as jnp
from jax import lax
from jax.experimental import pallas as pl
from jax.experimental.pallas import tpu as pltpu

F32 = jnp.float32
BF16 = jnp.bfloat16

D_MODEL = 1024
HEAD_DIM = 128
N_HEADS = 4
GROUP_W = N_HEADS * HEAD_DIM
ATTN_PATTERNS = ((128, 1), (512, 4), (2048, 16))
N_GROUPS = len(ATTN_PATTERNS)
N_BACK = 128
QKV_W = 3 * N_GROUPS * GROUP_W
REST_W = 3 * GROUP_W + 2 * D_MODEL
POOL_WINDOWS = (2, 4, 8, 16)
POOL_STATE = 15
PAST_LEN = 8192
EPS = 1e-6
Q_SCALE = HEAD_DIM ** -0.5
NEG = -1e30

ROW_TILE = 512
Q_BLOCK = 128
Q_UNROLL = 8
VMEM_LIMIT = 56 * 1024 * 1024
OUT_VMEM_LIMIT = 60 * 1024 * 1024


def _sigmoid(v):
    return 0.5 * jnp.tanh(0.5 * v) + 0.5


def _silu(v):
    return v * _sigmoid(v)


def _modulated_norm(x, norm_g, shift, scale):
    ms = jnp.mean(x * x, axis=-1, keepdims=True)
    return (x * lax.rsqrt(ms + EPS) * norm_g * (1.0 + scale) + shift).astype(BF16)


def _head_norm(r, gain):
    ms = jnp.mean(r * r, axis=-1, keepdims=True)
    return r * lax.rsqrt(ms + EPS) * gain


def _ada_body(c_ref, w_ref, b_ref, o_ref):
    s = _silu(c_ref[...]).astype(BF16)
    o_ref[...] = jnp.dot(s, w_ref[...].astype(BF16), preferred_element_type=F32) + b_ref[...]


def _ada(c_all, w_ada, b_ada):
    n = c_all.shape[0]
    return pl.pallas_call(
        _ada_body,
        grid=(3,),
        in_specs=[pl.BlockSpec((n, D_MODEL), lambda j: (0, 0)),
                  pl.BlockSpec((D_MODEL, D_MODEL), lambda j: (0, j)),
                  pl.BlockSpec((1, D_MODEL), lambda j: (0, j))],
        out_specs=pl.BlockSpec((n, D_MODEL), lambda j: (0, j)),
        out_shape=jax.ShapeDtypeStruct((n, 3 * D_MODEL), F32),
        name="ada",
    )(c_all, w_ada, b_ada)


ROWS_PER_POS = 2 * N_HEADS
SHIFT_NB = 8
SHIFT_ROWS = 712
SHIFT_SUBSTEPS = 4
SHIFT_AHEAD = 2
SHIFT_SLOTS = 2 * SHIFT_AHEAD
SHIFT_PRIORITY = 1


def _shift_plan(state_shapes):
    kinds, c0 = [], 0
    for idx, (n_batch, rows_total, _) in enumerate(state_shapes):
        keep = rows_total - ROWS_PER_POS
        pieces = min(p for p in range(1, keep)
                     if keep % (p * ROWS_PER_POS) == 0 and keep // p <= SHIFT_ROWS)
        n = (n_batch // SHIFT_NB) * pieces
        kinds.append((idx, keep // pieces, pieces, c0, c0 + n))
        c0 += n
    return kinds


def _shift_substep(j, lo, hi, last_slot, kinds, srcs, dsts, buf, rsem, wsem):
    def copy(jj, kind, write):
        idx, rows, pieces, c0, _ = kind
        local = jj - c0
        batches = pl.ds((local // pieces) * SHIFT_NB, SHIFT_NB)
        first = (local % pieces) * rows
        slot = jj % SHIFT_SLOTS
        stage = buf.at[slot, :, pl.ds(0, rows), :]
        if write:
            return pltpu.make_async_copy(stage, dsts[idx].at[batches, pl.ds(first, rows), :], wsem.at[slot])
        return pltpu.make_async_copy(srcs[idx].at[batches, pl.ds(first + ROWS_PER_POS, rows), :], stage, rsem.at[slot])

    spans = [(kind, max(kind[3], lo), min(kind[4], hi)) for kind in kinds]
    spans = [s for s in spans if s[1] < s[2]]

    def for_chunk(jj, also, fn):
        for kind, first, end in spans:
            cond = (jj >= first) & (jj < end)

            @pl.when(cond if also is None else cond & also)
            def _():
                fn(functools.partial(copy, jj, kind))

    steady = []
    for kind, first, end in spans:
        inner = (j >= first + SHIFT_AHEAD) & (j < end - SHIFT_AHEAD)
        steady.append(inner)

        @pl.when(inner)
        def _():
            copy(j, kind, False).wait()
            copy(j, kind, True).start(priority=SHIFT_PRIORITY)
            copy(j - SHIFT_AHEAD, kind, True).wait()
            copy(j + SHIFT_AHEAD, kind, False).start(priority=SHIFT_PRIORITY)

    @pl.when(jnp.logical_not(functools.reduce(jnp.logical_or, steady)))
    def _():
        for a in range(SHIFT_AHEAD):
            for_chunk(j + a, j == lo, lambda cp: cp(False).start(priority=SHIFT_PRIORITY))

        def landed(cp):
            cp(False).wait()
            cp(True).start(priority=SHIFT_PRIORITY)

        for_chunk(j, None, landed)
        for_chunk(j - SHIFT_AHEAD, None, lambda cp: cp(True).wait())
        for_chunk(j + SHIFT_AHEAD, None, lambda cp: cp(False).start(priority=SHIFT_PRIORITY))
        for a in range(SHIFT_AHEAD):
            for_chunk(j - a, j == last_slot, lambda cp: cp(True).wait())


def _new_rows_copy(new_ref, dst, nsem):
    keep = dst.shape[1] - ROWS_PER_POS
    return pltpu.make_async_copy(new_ref, dst.at[:, pl.ds(keep, ROWS_PER_POS), :], nsem)


def _shift_scratch():
    return [pltpu.VMEM((SHIFT_SLOTS, SHIFT_NB, SHIFT_ROWS, HEAD_DIM), F32),
            pltpu.SemaphoreType.DMA((SHIFT_SLOTS,)), pltpu.SemaphoreType.DMA((SHIFT_SLOTS,)),
            pltpu.SemaphoreType.DMA(())]


def _class_major_perm(tm, d):
    i = jnp.arange(tm)
    src_row = (i % (tm // d)) * d + i // (tm // d)
    return (src_row[:, None] == jnp.arange(tm)[None, :]).astype(BF16)


SHIFT_SITES = ((0, 1), (1, 1), (2, 0))


def _qkv_body(x_ref, mod_ref, ng_ref, w_ref, qg_ref, kg_ref, perm_ref, new_ref, src_ref, *refs, n_tiles, shift):
    qkv_refs = (refs[0:3], refs[3:6], refs[6:9])
    tail_refs = refs[9:12]
    dst_ref, buf, rsem, wsem, nsem = refs[12:17]
    i = pl.program_id(1)
    tm = x_ref.shape[1]
    step = pl.program_id(0) * n_tiles + i
    kinds, lo, hi = shift

    last_slot = lo + SHIFT_SUBSTEPS * pl.num_programs(0) * n_tiles - 1

    def shift_substep(k):
        _shift_substep(lo + SHIFT_SUBSTEPS * step + k, lo, hi, last_slot, kinds, [src_ref], [dst_ref],
                       buf, rsem, wsem)

    shift_substep(0)

    @pl.when(step == 0)
    def _():
        _new_rows_copy(new_ref, dst_ref, nsem).start()

    h = _modulated_norm(x_ref[0], ng_ref[...], mod_ref[0, :, 0:D_MODEL], mod_ref[0, :, D_MODEL:2 * D_MODEL])
    h_by_group = [h] + [jnp.dot(perm_ref[g - 1], h, preferred_element_type=F32).astype(BF16)
                        for g in range(1, N_GROUPS)]

    def heads(t, g, lhs):
        c = t * N_GROUPS + g
        res = jnp.dot(lhs, w_ref[:, c * GROUP_W:(c + 1) * GROUP_W], preferred_element_type=F32)
        out = []
        for hh in range(N_HEADS):
            r = res[:, hh * HEAD_DIM:(hh + 1) * HEAD_DIM]
            if t == 0:
                r = _head_norm(r, qg_ref[g:g + 1, :]) * Q_SCALE
            elif t == 1:
                r = _head_norm(r, kg_ref[g:g + 1, :])
            out.append(r)
        return out

    def write_tail(t, g, rs):
        window, d = ATTN_PATTERNS[g]
        per = tm // d
        for hh, r in enumerate(rs):
            row = (t - 1) * N_HEADS + hh
            if d == 1:
                tail_refs[g][0, :, row, :] = r if window >= tm else r[tm - window:, :]
            else:
                assert window >= tm
                for rr in range(d):
                    tail_refs[g][0, pl.ds(rr, per, stride=d), row, :] = r[rr * per:(rr + 1) * per, :]

    for t in range(3):
        for g, (window, d) in enumerate(ATTN_PATTERNS):
            if (t, g) in SHIFT_SITES:
                shift_substep(SHIFT_SITES.index((t, g)) + 1)
            if (t, g) == SHIFT_SITES[0]:
                @pl.when(step == 0)
                def _():
                    _new_rows_copy(new_ref, dst_ref, nsem).wait()

            rs = heads(t, g, h_by_group[g])
            per = tm // d
            for hh, r in enumerate(rs):
                for rr in range(d):
                    qkv_refs[t][g][0, rr, hh] = r[rr * per:(rr + 1) * per, :].astype(BF16)
            if t > 0:
                n_tail = max(window // tm, 1)

                @pl.when(i >= n_tiles - n_tail)
                def _():
                    write_tail(t, g, rs)


def _prompt_qkv(x, mod_p, norm_g, w_qkv, q_gain, k_gain, new_rows, state, shift):
    B, L, _ = x.shape
    tm = ROW_TILE
    n_tiles = L // tm
    const2 = lambda b, i: (0, 0)
    in_specs = [
        pl.BlockSpec((1, tm, D_MODEL), lambda b, i: (b, i, 0)),
        pl.BlockSpec((1, 1, 3 * D_MODEL), lambda b, i: (b, 0, 0)),
        pl.BlockSpec((1, D_MODEL), const2),
        pl.BlockSpec((D_MODEL, QKV_W), const2, pipeline_mode=pl.Buffered(1)),
        pl.BlockSpec((N_GROUPS, HEAD_DIM), const2),
        pl.BlockSpec((N_GROUPS, HEAD_DIM), const2),
        pl.BlockSpec((N_GROUPS - 1, tm, tm), lambda b, i: (0, 0, 0), pipeline_mode=pl.Buffered(1)),
        pl.BlockSpec(new_rows.shape, lambda b, i: (0, 0, 0), pipeline_mode=pl.Buffered(1)),
        pl.BlockSpec(memory_space=pl.ANY),
    ]
    perms = jnp.stack([_class_major_perm(tm, d) for _, d in ATTN_PATTERNS[1:]])
    out_specs, out_shape = [], []
    for t in range(3):
        for window, d in ATTN_PATTERNS:
            out_specs.append(pl.BlockSpec((1, d, N_HEADS, tm // d, HEAD_DIM), lambda b, i: (b, 0, 0, i, 0)))
            out_shape.append(jax.ShapeDtypeStruct((B, d, N_HEADS, L // d, HEAD_DIM), BF16))
    for window, d in ATTN_PATTERNS:
        w_eff = min(window, L)
        if w_eff >= tm:
            n_tail = w_eff // tm
            out_specs.append(pl.BlockSpec(
                (1, tm, 2 * N_HEADS, HEAD_DIM),
                lambda b, i, n_tail=n_tail: (b, jnp.maximum(i - (n_tiles - n_tail), 0), 0, 0)))
        else:
            out_specs.append(pl.BlockSpec((1, w_eff, 2 * N_HEADS, HEAD_DIM), lambda b, i: (b, 0, 0, 0)))
        out_shape.append(jax.ShapeDtypeStruct((B, w_eff, 2 * N_HEADS, HEAD_DIM), F32))
    out_specs.append(pl.BlockSpec(memory_space=pl.ANY))
    out_shape.append(jax.ShapeDtypeStruct(state.shape, F32))
    return pl.pallas_call(
        functools.partial(_qkv_body, n_tiles=n_tiles, shift=shift),
        grid=(B, n_tiles),
        in_specs=in_specs,
        out_specs=out_specs,
        out_shape=out_shape,
        scratch_shapes=_shift_scratch(),
        compiler_params=pltpu.CompilerParams(
            dimension_semantics=("arbitrary", "arbitrary"), vmem_limit_bytes=VMEM_LIMIT),
        name="prompt_qkv",
    )(x, mod_p, norm_g, w_qkv, q_gain, k_gain, perms, new_rows, state)


def _attn_body(q_ref, k_ref, v_ref, o_ref, l_ref, vx):
    _, d_blk, h_blk, M, _ = q_ref.shape
    n_q = M // Q_BLOCK
    for c in range(d_blk):
        for hh in range(h_blk):
            vx[c, hh, :, 0:HEAD_DIM] = v_ref[0, c, hh]
            vx[c, hh, :, HEAD_DIM:2 * HEAD_DIM] = jnp.ones((M, HEAD_DIM), BF16)
    ii = lax.broadcasted_iota(jnp.int32, (Q_BLOCK, 2 * Q_BLOCK), 0)
    jj = lax.broadcasted_iota(jnp.int32, (Q_BLOCK, 2 * Q_BLOCK), 1)
    dist = Q_BLOCK + ii - jj
    band_ok = (dist >= 0) & (dist <= N_BACK)
    causal_ok = (lax.broadcasted_iota(jnp.int32, (Q_BLOCK, Q_BLOCK), 1)
                 <= lax.broadcasted_iota(jnp.int32, (Q_BLOCK, Q_BLOCK), 0))
    nt = (((1,), (1,)), ((), ()))

    def block(c, hh, n, first):
        rows = pl.ds(pl.multiple_of(n * Q_BLOCK, Q_BLOCK), Q_BLOCK)
        if first:
            keys, ok = rows, causal_ok
        else:
            keys, ok = pl.ds(pl.multiple_of((n - 1) * Q_BLOCK, Q_BLOCK), 2 * Q_BLOCK), band_ok
        s = lax.dot_general(q_ref[0, c, hh, rows, :], k_ref[0, c, hh, keys, :], nt, preferred_element_type=F32)
        s = jnp.where(ok, s, NEG)
        m = jnp.max(s, axis=-1, keepdims=True)
        p = jnp.exp(s - m).astype(BF16)
        oe = jnp.dot(p, vx[c, hh, keys, :], preferred_element_type=F32)
        l = oe[:, HEAD_DIM:]
        o_ref[0, c, hh, rows, :] = oe[:, :HEAD_DIM] / l
        l_ref[0, c, hh, rows, :] = m + jnp.log(l)

    for c in range(d_blk):
        for hh in range(h_blk):
            block(c, hh, 0, True)
        if n_q <= 2:
            for n in range(1, n_q):
                for hh in range(h_blk):
                    block(c, hh, n, False)
        else:
            def step(n, carry, c=c):
                for hh in range(h_blk):
                    block(c, hh, n, False)
                return carry

            lax.fori_loop(1, n_q, step, 0, unroll=Q_UNROLL)


def _prompt_attn(q, k, v, d_blk, h_blk):
    B, d, H, M, E = q.shape
    blk = (1, d_blk, h_blk, M, E)
    spec = pl.BlockSpec(blk, lambda b, c, hh: (b, c, hh, 0, 0))
    shape = jax.ShapeDtypeStruct(q.shape, F32)
    return pl.pallas_call(
        _attn_body,
        grid=(B, d // d_blk, H // h_blk),
        in_specs=[spec, spec, spec],
        out_specs=[spec, spec],
        out_shape=[shape, shape],
        scratch_shapes=[pltpu.VMEM((d_blk, h_blk, M, 2 * E), BF16)],
        compiler_params=pltpu.CompilerParams(
            dimension_semantics=("arbitrary", "arbitrary", "arbitrary"), vmem_limit_bytes=VMEM_LIMIT),
        name="prompt_attn_d%d" % d,
    )(q, k, v)


def _pool_project(mixed, wp_ref, bp_ref, ps_ref):
    cols = []
    for g in range(len(POOL_WINDOWS)):
        lanes = slice(g * HEAD_DIM, (g + 1) * HEAD_DIM)
        y = jnp.dot(mixed[g].astype(BF16), wp_ref[g], preferred_element_type=F32) + bp_ref[g:g + 1, :]
        cols.append(y * ps_ref[:, lanes])
    return jnp.concatenate(cols, axis=-1)


def _merge_out(x, gate, a, p, z_a, z_b, g_a, g_b, wa_ref, wb_ref, wo_ref):
    ya = jnp.dot((a * _silu(z_a)).astype(BF16), wa_ref[...], preferred_element_type=F32)
    yb = jnp.dot((p * _silu(z_b)).astype(BF16), wb_ref[...], preferred_element_type=F32)
    m = _sigmoid(g_a) * ya + _sigmoid(g_b) * yb
    return x + gate * jnp.dot(m.astype(BF16), wo_ref[...], preferred_element_type=F32)


def _out_body(x_ref, mod_ref, ng_ref, w_ref, o0, l0, o1, l1, o2, l2, wp_ref, bp_ref, ps_ref,
              wa_ref, wb_ref, wo_ref, new_ref, src_a, src_b, dst_a_in, y_ref, pool_ref, dst_a, dst_b,
              ext, oscr, lscr, buf, rsem, wsem, nsem, *, n_tiles, shift):
    del dst_a_in
    i = pl.program_id(1)
    tm = x_ref.shape[1]
    step = pl.program_id(0) * n_tiles + i
    kinds, lo, hi = shift

    last_slot = lo + SHIFT_SUBSTEPS * pl.num_programs(0) * n_tiles - 1

    def shift_substep(k):
        _shift_substep(lo + SHIFT_SUBSTEPS * step + k, lo, hi, last_slot, kinds, [src_a, src_b], [dst_a, dst_b],
                       buf, rsem, wsem)

    shift_substep(0)

    @pl.when(step == 0)
    def _():
        _new_rows_copy(new_ref, dst_b, nsem).start()
        ext[0:16, :] = jnp.zeros((16, GROUP_W), F32)

    x = x_ref[0]
    h = _modulated_norm(x, ng_ref[...], mod_ref[0, :, 0:D_MODEL], mod_ref[0, :, D_MODEL:2 * D_MODEL])
    gate = mod_ref[0, :, 2 * D_MODEL:3 * D_MODEL]

    def proj(lo, width):
        return jnp.dot(h, w_ref[:, lo:lo + width], preferred_element_type=F32)

    z_a = proj(0, GROUP_W)
    u = proj(GROUP_W, GROUP_W)
    z_b = proj(2 * GROUP_W, GROUP_W)
    o_refs, l_refs = (o0, o1, o2), (l0, l1, l2)
    for g, (window, d) in enumerate(ATTN_PATTERNS):
        if d == 1:
            continue
        for hh in range(N_HEADS):
            for rr in range(d):
                rows = pl.ds(rr, tm // d, stride=d)
                oscr[g - 1, hh, rows, :] = o_refs[g][0, rr, hh]
                lscr[g - 1, hh, rows, :] = l_refs[g][0, rr, hh]
    a_cols = []
    for hh in range(N_HEADS):
        os_ = [o0[0, 0, hh], oscr[0, hh], oscr[1, hh]]
        ls_ = [l0[0, 0, hh], lscr[0, hh], lscr[1, hh]]
        mx = jnp.maximum(jnp.maximum(ls_[0], ls_[1]), ls_[2])
        es = [jnp.exp(l_ - mx) for l_ in ls_]
        den = es[0] + es[1] + es[2]
        a_cols.append((es[0] * os_[0] + es[1] * os_[1] + es[2] * os_[2]) / den)
    ya_in = (jnp.concatenate(a_cols, axis=-1) * _silu(z_a)).astype(BF16)
    shift_substep(1)

    @pl.when(step == 0)
    def _():
        _new_rows_copy(new_ref, dst_b, nsem).wait()

    g_a = proj(3 * GROUP_W, D_MODEL)
    ext[16:16 + tm, :] = u
    pos = i * tm + lax.broadcasted_iota(jnp.int32, (tm, 1), 0)
    mixed = []
    for g, w in enumerate(POOL_WINDOWS):
        lanes = pl.ds(g * HEAD_DIM, HEAD_DIM)
        acc = ext[pl.ds(16, tm), lanes]
        tok = acc
        for k in range(1, w):
            acc = acc + ext[pl.ds(16 - k, tm), lanes]
        cnt = jnp.minimum(w, pos + 1).astype(F32)
        mixed.append(acc / cnt - tok)
    pool_ref[0, 0] = ext[pl.ds(tm + 1, POOL_STATE), :]
    ext[0:16, :] = jnp.where(i == n_tiles - 1, 0.0, ext[tm:tm + 16, :])
    yb_in = (_pool_project(mixed, wp_ref, bp_ref, ps_ref) * _silu(z_b)).astype(BF16)
    shift_substep(2)

    g_b = proj(3 * GROUP_W + D_MODEL, D_MODEL)
    ya = jnp.dot(ya_in, wa_ref[...], preferred_element_type=F32)
    yb = jnp.dot(yb_in, wb_ref[...], preferred_element_type=F32)
    m = (_sigmoid(g_a) * ya + _sigmoid(g_b) * yb).astype(BF16)
    shift_substep(3)
    y_ref[0] = x + gate * jnp.dot(m, wo_ref[...], preferred_element_type=F32)


def _prompt_out(x, mod_p, norm_g, w_rest, ol, w_pool, b_pool, pool_scale, wa, wb, wo,
                new_rows, state_a, state_b, shifted_a, shift):
    B, L, _ = x.shape
    tm = ROW_TILE
    n_tiles = L // tm
    const2 = lambda b, i: (0, 0)
    one = pl.Buffered(1)
    in_specs = [
        pl.BlockSpec((1, tm, D_MODEL), lambda b, i: (b, i, 0)),
        pl.BlockSpec((1, 1, 3 * D_MODEL), lambda b, i: (b, 0, 0)),
        pl.BlockSpec((1, D_MODEL), const2),
        pl.BlockSpec((D_MODEL, REST_W), const2, pipeline_mode=one),
    ]
    for window, d in ATTN_PATTERNS:
        for _ in range(2):
            in_specs.append(pl.BlockSpec((1, d, N_HEADS, tm // d, HEAD_DIM), lambda b, i: (b, 0, 0, i, 0)))
    in_specs += [
        pl.BlockSpec((len(POOL_WINDOWS), HEAD_DIM, HEAD_DIM), lambda b, i: (0, 0, 0)),
        pl.BlockSpec((len(POOL_WINDOWS), HEAD_DIM), const2),
        pl.BlockSpec((1, GROUP_W), const2),
        pl.BlockSpec((GROUP_W, D_MODEL), const2, pipeline_mode=one),
        pl.BlockSpec((GROUP_W, D_MODEL), const2, pipeline_mode=one),
        pl.BlockSpec((D_MODEL, D_MODEL), const2, pipeline_mode=one),
        pl.BlockSpec(new_rows.shape, lambda b, i: (0, 0, 0), pipeline_mode=one),
    ]
    any_spec = pl.BlockSpec(memory_space=pl.ANY)
    in_specs += [any_spec] * 3
    return pl.pallas_call(
        functools.partial(_out_body, n_tiles=n_tiles, shift=shift),
        grid=(B, n_tiles),
        in_specs=in_specs,
        out_specs=[pl.BlockSpec((1, tm, D_MODEL), lambda b, i: (b, i, 0)),
                   pl.BlockSpec((1, 1, POOL_STATE, GROUP_W), lambda b, i: (0, b, 0, 0)),
                   any_spec, any_spec],
        out_shape=[jax.ShapeDtypeStruct((B, L, D_MODEL), F32),
                   jax.ShapeDtypeStruct((1, B, POOL_STATE, GROUP_W), F32),
                   jax.ShapeDtypeStruct(state_a.shape, F32),
                   jax.ShapeDtypeStruct(state_b.shape, F32)],
        scratch_shapes=[pltpu.VMEM((tm + 16, GROUP_W), F32),
                        pltpu.VMEM((2, N_HEADS, tm, HEAD_DIM), F32),
                        pltpu.VMEM((2, N_HEADS, tm, HEAD_DIM), F32)] + _shift_scratch(),
        input_output_aliases={len(in_specs) - 1: 2},
        compiler_params=pltpu.CompilerParams(
            dimension_semantics=("arbitrary", "arbitrary"), vmem_limit_bytes=OUT_VMEM_LIMIT),
        name="prompt_out",
    )(x, mod_p, norm_g, w_rest, *ol, w_pool, b_pool, pool_scale, wa, wb, wo,
      new_rows, state_a, state_b, shifted_a)


def _sample_proj_body(x_ref, mod_ref, ng_ref, w_ref, o_ref):
    h = _modulated_norm(x_ref[...], ng_ref[...], mod_ref[:, 0:D_MODEL], mod_ref[:, D_MODEL:2 * D_MODEL])
    o_ref[...] = jnp.dot(h, w_ref[...], preferred_element_type=F32)


def _sample_proj(xs, mod_s, norm_g, w):
    n, width = xs.shape[0], w.shape[1]
    tn = max(t for t in range(HEAD_DIM, width // 2 + 1, HEAD_DIM) if width % t == 0)
    return pl.pallas_call(
        _sample_proj_body,
        grid=(width // tn,),
        in_specs=[pl.BlockSpec((n, D_MODEL), lambda j: (0, 0)),
                  pl.BlockSpec((n, 3 * D_MODEL), lambda j: (0, 0)),
                  pl.BlockSpec((1, D_MODEL), lambda j: (0, 0)),
                  pl.BlockSpec((D_MODEL, tn), lambda j: (0, j))],
        out_specs=pl.BlockSpec((n, tn), lambda j: (0, j)),
        out_shape=jax.ShapeDtypeStruct((n, width), F32),
        name="sample_proj",
    )(xs, mod_s, norm_g, w)


def _sample_attn_body(p_ref, s0, s1, s2, qg_ref, kg_ref, a_ref, shifted0, n1, n2):
    bt = p_ref.shape[0]
    s_refs = (s0, s1, s2)
    last = N_BACK - 1
    for b in range(bt):
        os_, ls_ = [], []
        shifted0[b, 0:last] = s0[b, 1:N_BACK]
        for g in range(N_GROUPS):
            row = lambda t: pl.ds((t * N_GROUPS + g) * N_HEADS, N_HEADS)
            q4 = _head_norm(p_ref[b, row(0), :], qg_ref[g:g + 1, :]) * Q_SCALE
            k4 = _head_norm(p_ref[b, row(1), :], kg_ref[g:g + 1, :])
            v4 = p_ref[b, row(2), :]
            if g == 0:
                shifted0[b, last, 0:N_HEADS, :] = k4
                shifted0[b, last, N_HEADS:2 * N_HEADS, :] = v4
            else:
                (n1, n2)[g - 1][b, 0:N_HEADS, :] = k4
                (n1, n2)[g - 1][b, N_HEADS:2 * N_HEADS, :] = v4
            keys = s_refs[g][b, :, 0:N_HEADS, :]
            vals = s_refs[g][b, :, N_HEADS:2 * N_HEADS, :]
            s_old = jnp.sum(keys * q4[None], axis=-1, keepdims=True)
            s_new = jnp.sum(k4 * q4, axis=-1, keepdims=True)
            m = jnp.maximum(jnp.max(s_old, axis=0), s_new)
            p_old = jnp.exp(s_old - m[None])
            p_new = jnp.exp(s_new - m)
            l = jnp.sum(p_old, axis=0) + p_new
            o = jnp.sum(p_old * vals, axis=0) + p_new * v4
            os_.append(o / l)
            ls_.append(m + jnp.log(l))
        mx = jnp.maximum(jnp.maximum(ls_[0], ls_[1]), ls_[2])
        es = [jnp.exp(l_ - mx) for l_ in ls_]
        a_ref[b] = (es[0] * os_[0] + es[1] * os_[1] + es[2] * os_[2]) / (es[0] + es[1] + es[2])


def _sample_attn(p3, views, q_gain, k_gain):
    n = p3.shape[0]
    bt = 8
    in_specs = [pl.BlockSpec((bt, p3.shape[1], HEAD_DIM), lambda j: (j, 0, 0))]
    for v in views:
        in_specs.append(pl.BlockSpec((bt, N_BACK, 2 * N_HEADS, HEAD_DIM), lambda j: (j, 0, 0, 0)))
    in_specs += [pl.BlockSpec((N_GROUPS, HEAD_DIM), lambda j: (0, 0))] * 2
    new_spec = pl.BlockSpec((bt, 2 * N_HEADS, HEAD_DIM), lambda j: (j, 0, 0))
    new_shape = jax.ShapeDtypeStruct((n, 2 * N_HEADS, HEAD_DIM), F32)
    state_spec = pl.BlockSpec((bt, N_BACK, 2 * N_HEADS, HEAD_DIM), lambda j: (j, 0, 0, 0))
    return pl.pallas_call(
        _sample_attn_body,
        grid=(n // bt,),
        in_specs=in_specs,
        out_specs=[pl.BlockSpec((bt, N_HEADS, HEAD_DIM), lambda j: (j, 0, 0)), state_spec, new_spec, new_spec],
        out_shape=[jax.ShapeDtypeStruct((n, N_HEADS, HEAD_DIM), F32),
                   jax.ShapeDtypeStruct(views[0].shape, F32), new_shape, new_shape],
        compiler_params=pltpu.CompilerParams(vmem_limit_bytes=VMEM_LIMIT),
        name="sample_attn",
    )(p3, *views, q_gain, k_gain)


def _sample_out_body(x_ref, mod_ref, r_ref, a_ref, sp_ref, wp_ref, bp_ref, ps_ref, wa_ref, wb_ref, wo_ref,
                     y_ref, pool_ref):
    u = r_ref[:, GROUP_W:2 * GROUP_W]
    mixed = []
    for g, w in enumerate(POOL_WINDOWS):
        lanes = pl.ds(g * HEAD_DIM, HEAD_DIM)
        tok = r_ref[:, pl.ds(GROUP_W + g * HEAD_DIM, HEAD_DIM)]
        acc = tok
        for k in range(1, w):
            acc = acc + sp_ref[POOL_STATE - k, :, lanes]
        mixed.append(acc / float(min(w, PAST_LEN + 1)) - tok)
    p = _pool_project(mixed, wp_ref, bp_ref, ps_ref)
    for k in range(POOL_STATE - 1):
        pool_ref[k] = sp_ref[k + 1]
    pool_ref[POOL_STATE - 1] = u
    y_ref[...] = _merge_out(
        x_ref[...], mod_ref[:, 2 * D_MODEL:3 * D_MODEL], a_ref[...], p,
        r_ref[:, 0:GROUP_W], r_ref[:, 2 * GROUP_W:3 * GROUP_W],
        r_ref[:, 3 * GROUP_W:3 * GROUP_W + D_MODEL], r_ref[:, 3 * GROUP_W + D_MODEL:REST_W],
        wa_ref, wb_ref, wo_ref)


def _sample_out(xs, mod_s, rest, a_s, sp_t, w_pool, b_pool, pool_scale, wa, wb, wo):
    n = xs.shape[0]
    return pl.pallas_call(
        _sample_out_body,
        out_shape=[jax.ShapeDtypeStruct((n, D_MODEL), F32),
                   jax.ShapeDtypeStruct((POOL_STATE, n, GROUP_W), F32)],
        compiler_params=pltpu.CompilerParams(vmem_limit_bytes=VMEM_LIMIT),
        name="sample_out",
    )(xs, mod_s, rest, a_s, sp_t, w_pool, b_pool, pool_scale, wa, wb, wo)


def kernel(x_prompt, x_sample, state_kv_w128, state_kv_w512, state_kv_w2048, state_pool, c_prompt, c_sample,
           norm_g, w_ada, b_ada, w_in, q_gain, k_gain, w_pool, b_pool, pool_scale, w_a_out, w_b_out, w_out):
    B, L, _ = x_prompt.shape
    n_s = x_sample.shape[0]
    w_qkv = w_in[0, :, :QKV_W].astype(BF16)
    w_rest = w_in[0, :, QKV_W:].astype(BF16)
    wa, wb, wo, wp = (w[0].astype(BF16) for w in (w_a_out, w_b_out, w_out, w_pool))
    qg, kg, bp = q_gain[0], k_gain[0], b_pool[0]

    mod = _ada(jnp.concatenate([c_prompt, c_sample], axis=0), w_ada[0], b_ada)
    mod_p = mod[:B].reshape(B, 1, 3 * D_MODEL)
    mod_s = mod[B:]

    xs = x_sample[:, 0, :]
    qkv_s = _sample_proj(xs, mod_s, norm_g, w_qkv)
    rest_s = _sample_proj(xs, mod_s, norm_g, w_rest)
    states = (state_kv_w128, state_kv_w512, state_kv_w2048)
    views = [s.reshape(n_s, N_BACK, d * ROWS_PER_POS, HEAD_DIM) for s, (_, d) in zip(states, ATTN_PATTERNS)]
    a_s, shifted_128, new_512, new_2048 = _sample_attn(
        qkv_s.reshape(n_s, QKV_W // HEAD_DIM, HEAD_DIM), views, qg, kg)
    y_s, pool_t = _sample_out(xs, mod_s, rest_s, a_s.reshape(n_s, GROUP_W), state_pool[0].transpose(1, 0, 2),
                              wp, bp, pool_scale, wa, wb, wo)

    flat_2048, flat_512 = (s.reshape(n_s, s.shape[2] * ROWS_PER_POS, HEAD_DIM) for s in states[:0:-1])
    kinds = _shift_plan([flat_2048.shape, flat_512.shape])
    n_sub = B * (L // ROW_TILE) * SHIFT_SUBSTEPS
    n_chunks = kinds[-1][-1]
    assert n_sub <= kinds[0][4] and n_chunks <= 2 * n_sub
    *qkv, t0, t1, t2, part_2048 = _prompt_qkv(x_prompt, mod_p, norm_g, w_qkv, qg, kg, new_2048, flat_2048,
                                              (kinds, 0, n_sub))
    ol = []
    for g, (d_blk, h_blk) in enumerate(((1, 2), (1, 4), (4, 4))):
        ol += _prompt_attn(qkv[g], qkv[N_GROUPS + g], qkv[2 * N_GROUPS + g], d_blk, h_blk)
    y_p, pool_p, shifted_2048, shifted_512 = _prompt_out(
        x_prompt, mod_p, norm_g, w_rest, ol, wp, bp, pool_scale, wa, wb, wo,
        new_512, flat_2048, flat_512, part_2048, (kinds, n_sub, n_chunks))
    kv_s = [o.reshape(s.shape) for o, s in zip((shifted_128, shifted_512, shifted_2048), states)]
    kv_p = [t.reshape(1, B, t.shape[1], 2, N_HEADS, HEAD_DIM) for t in (t0, t1, t2)]

    return (y_p, y_s.reshape(n_s, 1, D_MODEL), kv_p[0], kv_p[1], kv_p[2], pool_p,
            kv_s[0], kv_s[1], kv_s[2], pool_t.transpose(1, 0, 2)[None])
```

```python
import functools

import jax
import jax.numpy as jnp
from jax import lax
from jax.experimental import pallas as pl
from jax.experimental.pallas import tpu as pltpu

F32 = jnp.float32
BF16 = jnp.bfloat16

D_MODEL = 1024
HEAD_DIM = 128
N_HEADS = 4
GROUP_W = N_HEADS * HEAD_DIM
ATTN_PATTERNS = ((128, 1), (512, 4), (2048, 16))
N_GROUPS = len(ATTN_PATTERNS)
N_BACK = 128
QKV_W = 3 * N_GROUPS * GROUP_W
REST_W = 3 * GROUP_W + 2 * D_MODEL
POOL_WINDOWS = (2, 4, 8, 16)
POOL_STATE = 15
PAST_LEN = 8192
EPS = 1e-6
Q_SCALE = HEAD_DIM ** -0.5
NEG = -1e30

ROW_TILE = 512
Q_BLOCK = 128
Q_UNROLL = 8
VMEM_LIMIT = 56 * 1024 * 1024
OUT_VMEM_LIMIT = 60 * 1024 * 1024


def _sigmoid(v):
    return 0.5 * jnp.tanh(0.5 * v) + 0.5


def _silu(v):
    return v * _sigmoid(v)


def _modulated_norm(x, norm_g, shift, scale):
    ms = jnp.mean(x * x, axis=-1, keepdims=True)
    return (x * lax.rsqrt(ms + EPS) * norm_g * (1.0 + scale) + shift).astype(BF16)


def _head_norm(r, gain):
    ms = jnp.mean(r * r, axis=-1, keepdims=True)
    return r * lax.rsqrt(ms + EPS) * gain


def _ada_body(c_ref, w_ref, b_ref, o_ref):
    s = _silu(c_ref[...]).astype(BF16)
    o_ref[...] = jnp.dot(s, w_ref[...].astype(BF16), preferred_element_type=F32) + b_ref[...]


def _ada(c_all, w_ada, b_ada):
    n = c_all.shape[0]
    return pl.pallas_call(
        _ada_body,
        grid=(3,),
        in_specs=[pl.BlockSpec((n, D_MODEL), lambda j: (0, 0)),
                  pl.BlockSpec((D_MODEL, D_MODEL), lambda j: (0, j)),
                  pl.BlockSpec((1, D_MODEL), lambda j: (0, j))],
        out_specs=pl.BlockSpec((n, D_MODEL), lambda j: (0, j)),
        out_shape=jax.ShapeDtypeStruct((n, 3 * D_MODEL), F32),
        name="ada",
    )(c_all, w_ada, b_ada)


ROWS_PER_POS = 2 * N_HEADS
SHIFT_NB = 8
SHIFT_ROWS = 712
SHIFT_SUBSTEPS = 4
QKV_SHIFT_LEAD = 3
OUT_SHIFT_LEAD = 2
SHIFT_PRIORITY = 1


def _shift_plan(state_shapes):
    kinds, c0 = [], 0
    for idx, (n_batch, rows_total, _) in enumerate(state_shapes):
        keep = rows_total - ROWS_PER_POS
        pieces = min(p for p in range(1, keep)
                     if keep % (p * ROWS_PER_POS) == 0 and keep // p <= SHIFT_ROWS)
        n = (n_batch // SHIFT_NB) * pieces
        kinds.append((idx, keep // pieces, pieces, c0, c0 + n))
        c0 += n
    return kinds


def _shift_substep(j, lo, hi, last_slot, kinds, srcs, dsts, buf, rsem, wsem):
    ahead = buf.shape[0] // 2

    def copy(jj, kind, write):
        idx, rows, pieces, c0, _ = kind
        local = jj - c0
        batches = pl.ds((local // pieces) * SHIFT_NB, SHIFT_NB)
        first = (local % pieces) * rows
        slot = jj % buf.shape[0]
        stage = buf.at[slot, :, pl.ds(0, rows), :]
        if write:
            return pltpu.make_async_copy(stage, dsts[idx].at[batches, pl.ds(first, rows), :], wsem.at[slot])
        return pltpu.make_async_copy(srcs[idx].at[batches, pl.ds(first + ROWS_PER_POS, rows), :], stage, rsem.at[slot])

    spans = [(kind, max(kind[3], lo), min(kind[4], hi)) for kind in kinds]
    spans = [s for s in spans if s[1] < s[2]]

    def for_chunk(jj, also, fn):
        for kind, first, end in spans:
            cond = (jj >= first) & (jj < end)

            @pl.when(cond if also is None else cond & also)
            def _():
                fn(functools.partial(copy, jj, kind))

    steady = []
    for kind, first, end in spans:
        inner = (j >= first + ahead) & (j < end - ahead)
        steady.append(inner)

        @pl.when(inner)
        def _():
            copy(j, kind, False).wait()
            copy(j, kind, True).start(priority=SHIFT_PRIORITY)
            copy(j - ahead, kind, True).wait()
            copy(j + ahead, kind, False).start(priority=SHIFT_PRIORITY)

    @pl.when(jnp.logical_not(functools.reduce(jnp.logical_or, steady)))
    def _():
        for a in range(ahead):
            for_chunk(j + a, j == lo, lambda cp: cp(False).start(priority=SHIFT_PRIORITY))

        def landed(cp):
            cp(False).wait()
            cp(True).start(priority=SHIFT_PRIORITY)

        for_chunk(j, None, landed)
        for_chunk(j - ahead, None, lambda cp: cp(True).wait())
        for_chunk(j + ahead, None, lambda cp: cp(False).start(priority=SHIFT_PRIORITY))
        for a in range(ahead):
            for_chunk(j - a, j == last_slot, lambda cp: cp(True).wait())


def _new_rows_copy(new_ref, dst, nsem):
    keep = dst.shape[1] - ROWS_PER_POS
    return pltpu.make_async_copy(new_ref, dst.at[:, pl.ds(keep, ROWS_PER_POS), :], nsem)


def _shift_scratch(lead):
    return [pltpu.VMEM((2 * lead, SHIFT_NB, SHIFT_ROWS, HEAD_DIM), F32),
            pltpu.SemaphoreType.DMA((2 * lead,)), pltpu.SemaphoreType.DMA((2 * lead,)),
            pltpu.SemaphoreType.DMA(())]


def _class_major_perm(tm, d):
    i = jnp.arange(tm)
    src_row = (i % (tm // d)) * d + i // (tm // d)
    return (src_row[:, None] == jnp.arange(tm)[None, :]).astype(BF16)


SHIFT_SITES = ((0, 1), (1, 1), (2, 0))


def _qkv_body(x_ref, mod_ref, ng_ref, w_ref, qg_ref, kg_ref, perm_ref, new_ref, src_ref, *refs, n_tiles, shift):
    qkv_refs = (refs[0:3], refs[3:6], refs[6:9])
    tail_refs = refs[9:12]
    dst_ref, buf, rsem, wsem, nsem = refs[12:17]
    i = pl.program_id(1)
    tm = x_ref.shape[1]
    step = pl.program_id(0) * n_tiles + i
    kinds, lo, hi = shift

    last_slot = lo + SHIFT_SUBSTEPS * pl.num_programs(0) * n_tiles - 1

    def shift_substep(k):
        _shift_substep(lo + SHIFT_SUBSTEPS * step + k, lo, hi, last_slot, kinds, [src_ref], [dst_ref],
                       buf, rsem, wsem)

    shift_substep(0)

    @pl.when(step == 0)
    def _():
        _new_rows_copy(new_ref, dst_ref, nsem).start()

    h = _modulated_norm(x_ref[0], ng_ref[...], mod_ref[0, :, 0:D_MODEL], mod_ref[0, :, D_MODEL:2 * D_MODEL])
    h_by_group = [h] + [jnp.dot(perm_ref[g - 1], h, preferred_element_type=F32).astype(BF16)
                        for g in range(1, N_GROUPS)]

    def heads(t, g, lhs):
        c = t * N_GROUPS + g
        res = jnp.dot(lhs, w_ref[:, c * GROUP_W:(c + 1) * GROUP_W], preferred_element_type=F32)
        out = []
        for hh in range(N_HEADS):
            r = res[:, hh * HEAD_DIM:(hh + 1) * HEAD_DIM]
            if t == 0:
                r = _head_norm(r, qg_ref[g:g + 1, :]) * Q_SCALE
            elif t == 1:
                r = _head_norm(r, kg_ref[g:g + 1, :])
            out.append(r)
        return out

    def write_tail(t, g, rs):
        window, d = ATTN_PATTERNS[g]
        per = tm // d
        for hh, r in enumerate(rs):
            row = (t - 1) * N_HEADS + hh
            if d == 1:
                tail_refs[g][0, :, row, :] = r if window >= tm else r[tm - window:, :]
            else:
                assert window >= tm
                for rr in range(d):
                    tail_refs[g][0, pl.ds(rr, per, stride=d), row, :] = r[rr * per:(rr + 1) * per, :]

    for t in range(3):
        for g, (window, d) in enumerate(ATTN_PATTERNS):
            if (t, g) in SHIFT_SITES:
                shift_substep(SHIFT_SITES.index((t, g)) + 1)
            if (t, g) == SHIFT_SITES[0]:
                @pl.when(step == 0)
                def _():
                    _new_rows_copy(new_ref, dst_ref, nsem).wait()

            rs = heads(t, g, h_by_group[g])
            per = tm // d
            for hh, r in enumerate(rs):
                for rr in range(d):
                    qkv_refs[t][g][0, rr, hh] = r[rr * per:(rr + 1) * per, :].astype(BF16)
            if t > 0:
                n_tail = max(window // tm, 1)

                @pl.when(i >= n_tiles - n_tail)
                def _():
                    write_tail(t, g, rs)


def _prompt_qkv(x, mod_p, norm_g, w_qkv, q_gain, k_gain, new_rows, state, shift):
    B, L, _ = x.shape
    tm = ROW_TILE
    n_tiles = L // tm
    const2 = lambda b, i: (0, 0)
    in_specs = [
        pl.BlockSpec((1, tm, D_MODEL), lambda b, i: (b, i, 0)),
        pl.BlockSpec((1, 1, 3 * D_MODEL), lambda b, i: (b, 0, 0)),
        pl.BlockSpec((1, D_MODEL), const2),
        pl.BlockSpec((D_MODEL, QKV_W), const2, pipeline_mode=pl.Buffered(1)),
        pl.BlockSpec((N_GROUPS, HEAD_DIM), const2),
        pl.BlockSpec((N_GROUPS, HEAD_DIM), const2),
        pl.BlockSpec((N_GROUPS - 1, tm, tm), lambda b, i: (0, 0, 0), pipeline_mode=pl.Buffered(1)),
        pl.BlockSpec(new_rows.shape, lambda b, i: (0, 0, 0), pipeline_mode=pl.Buffered(1)),
        pl.BlockSpec(memory_space=pl.ANY),
    ]
    perms = jnp.stack([_class_major_perm(tm, d) for _, d in ATTN_PATTERNS[1:]])
    out_specs, out_shape = [], []
    for t in range(3):
        for window, d in ATTN_PATTERNS:
            out_specs.append(pl.BlockSpec((1, d, N_HEADS, tm // d, HEAD_DIM), lambda b, i: (b, 0, 0, i, 0)))
            out_shape.append(jax.ShapeDtypeStruct((B, d, N_HEADS, L // d, HEAD_DIM), BF16))
    for window, d in ATTN_PATTERNS:
        w_eff = min(window, L)
        if w_eff >= tm:
            n_tail = w_eff // tm
            out_specs.append(pl.BlockSpec(
                (1, tm, 2 * N_HEADS, HEAD_DIM),
                lambda b, i, n_tail=n_tail: (b, jnp.maximum(i - (n_tiles - n_tail), 0), 0, 0)))
        else:
            out_specs.append(pl.BlockSpec((1, w_eff, 2 * N_HEADS, HEAD_DIM), lambda b, i: (b, 0, 0, 0)))
        out_shape.append(jax.ShapeDtypeStruct((B, w_eff, 2 * N_HEADS, HEAD_DIM), F32))
    out_specs.append(pl.BlockSpec(memory_space=pl.ANY))
    out_shape.append(jax.ShapeDtypeStruct(state.shape, F32))
    return pl.pallas_call(
        functools.partial(_qkv_body, n_tiles=n_tiles, shift=shift),
        grid=(B, n_tiles),
        in_specs=in_specs,
        out_specs=out_specs,
        out_shape=out_shape,
        scratch_shapes=_shift_scratch(QKV_SHIFT_LEAD),
        compiler_params=pltpu.CompilerParams(
            dimension_semantics=("arbitrary", "arbitrary"), vmem_limit_bytes=VMEM_LIMIT),
        name="prompt_qkv",
    )(x, mod_p, norm_g, w_qkv, q_gain, k_gain, perms, new_rows, state)


def _attn_body(q_ref, k_ref, v_ref, o_ref, l_ref, vx):
    _, d_blk, h_blk, M, _ = q_ref.shape
    n_q = M // Q_BLOCK
    for c in range(d_blk):
        for hh in range(h_blk):
            vx[c, hh, :, 0:HEAD_DIM] = v_ref[0, c, hh]
            vx[c, hh, :, HEAD_DIM:2 * HEAD_DIM] = jnp.ones((M, HEAD_DIM), BF16)
    ii = lax.broadcasted_iota(jnp.int32, (Q_BLOCK, 2 * Q_BLOCK), 0)
    jj = lax.broadcasted_iota(jnp.int32, (Q_BLOCK, 2 * Q_BLOCK), 1)
    dist = Q_BLOCK + ii - jj
    band_ok = (dist >= 0) & (dist <= N_BACK)
    causal_ok = (lax.broadcasted_iota(jnp.int32, (Q_BLOCK, Q_BLOCK), 1)
                 <= lax.broadcasted_iota(jnp.int32, (Q_BLOCK, Q_BLOCK), 0))
    nt = (((1,), (1,)), ((), ()))

    def block(c, hh, n, first):
        rows = pl.ds(pl.multiple_of(n * Q_BLOCK, Q_BLOCK), Q_BLOCK)
        if first:
            keys, ok = rows, causal_ok
        else:
            keys, ok = pl.ds(pl.multiple_of((n - 1) * Q_BLOCK, Q_BLOCK), 2 * Q_BLOCK), band_ok
        s = lax.dot_general(q_ref[0, c, hh, rows, :], k_ref[0, c, hh, keys, :], nt, preferred_element_type=F32)
        s = jnp.where(ok, s, NEG)
        m = jnp.max(s, axis=-1, keepdims=True)
        p = jnp.exp(s - m).astype(BF16)
        oe = jnp.dot(p, vx[c, hh, keys, :], preferred_element_type=F32)
        l = oe[:, HEAD_DIM:]
        o_ref[0, c, hh, rows, :] = oe[:, :HEAD_DIM] / l
        l_ref[0, c, hh, rows, :] = m + jnp.log(l)

    for c in range(d_blk):
        for hh in range(h_blk):
            block(c, hh, 0, True)
        if n_q <= 2:
            for n in range(1, n_q):
                for hh in range(h_blk):
                    block(c, hh, n, False)
        else:
            def step(n, carry, c=c):
                for hh in range(h_blk):
                    block(c, hh, n, False)
                return carry

            lax.fori_loop(1, n_q, step, 0, unroll=Q_UNROLL)


def _prompt_attn(q, k, v, d_blk, h_blk):
    B, d, H, M, E = q.shape
    blk = (1, d_blk, h_blk, M, E)
    spec = pl.BlockSpec(blk, lambda b, c, hh: (b, c, hh, 0, 0))
    shape = jax.ShapeDtypeStruct(q.shape, F32)
    return pl.pallas_call(
        _attn_body,
        grid=(B, d // d_blk, H // h_blk),
        in_specs=[spec, spec, spec],
        out_specs=[spec, spec],
        out_shape=[shape, shape],
        scratch_shapes=[pltpu.VMEM((d_blk, h_blk, M, 2 * E), BF16)],
        compiler_params=pltpu.CompilerParams(
            dimension_semantics=("arbitrary", "arbitrary", "arbitrary"), vmem_limit_bytes=VMEM_LIMIT),
        name="prompt_attn_d%d" % d,
    )(q, k, v)


def _pool_project(mixed, wp_ref, bp_ref, ps_ref):
    cols = []
    for g in range(len(POOL_WINDOWS)):
        lanes = slice(g * HEAD_DIM, (g + 1) * HEAD_DIM)
        y = jnp.dot(mixed[g].astype(BF16), wp_ref[g], preferred_element_type=F32) + bp_ref[g:g + 1, :]
        cols.append(y * ps_ref[:, lanes])
    return jnp.concatenate(cols, axis=-1)


def _merge_out(x, gate, a, p, z_a, z_b, g_a, g_b, wa_ref, wb_ref, wo_ref):
    ya = jnp.dot((a * _silu(z_a)).astype(BF16), wa_ref[...], preferred_element_type=F32)
    yb = jnp.dot((p * _silu(z_b)).astype(BF16), wb_ref[...], preferred_element_type=F32)
    m = _sigmoid(g_a) * ya + _sigmoid(g_b) * yb
    return x + gate * jnp.dot(m.astype(BF16), wo_ref[...], preferred_element_type=F32)


def _out_body(x_ref, mod_ref, ng_ref, w_ref, o0, l0, o1, l1, o2, l2, wp_ref, bp_ref, ps_ref,
              wa_ref, wb_ref, wo_ref, new_ref, src_a, src_b, dst_a_in, y_ref, pool_ref, dst_a, dst_b,
              ext, oscr, lscr, buf, rsem, wsem, nsem, *, n_tiles, shift):
    del dst_a_in
    i = pl.program_id(1)
    tm = x_ref.shape[1]
    step = pl.program_id(0) * n_tiles + i
    kinds, lo, hi = shift

    last_slot = lo + SHIFT_SUBSTEPS * pl.num_programs(0) * n_tiles - 1

    def shift_substep(k):
        _shift_substep(lo + SHIFT_SUBSTEPS * step + k, lo, hi, last_slot, kinds, [src_a, src_b], [dst_a, dst_b],
                       buf, rsem, wsem)

    shift_substep(0)

    @pl.when(step == 0)
    def _():
        _new_rows_copy(new_ref, dst_b, nsem).start()
        ext[0:16, :] = jnp.zeros((16, GROUP_W), F32)

    x = x_ref[0]
    h = _modulated_norm(x, ng_ref[...], mod_ref[0, :, 0:D_MODEL], mod_ref[0, :, D_MODEL:2 * D_MODEL])
    gate = mod_ref[0, :, 2 * D_MODEL:3 * D_MODEL]

    def proj(lo, width):
        return jnp.dot(h, w_ref[:, lo:lo + width], preferred_element_type=F32)

    z_a = proj(0, GROUP_W)
    u = proj(GROUP_W, GROUP_W)
    z_b = proj(2 * GROUP_W, GROUP_W)
    o_refs, l_refs = (o0, o1, o2), (l0, l1, l2)
    for g, (window, d) in enumerate(ATTN_PATTERNS):
        if d == 1:
            continue
        for hh in range(N_HEADS):
            for rr in range(d):
                rows = pl.ds(rr, tm // d, stride=d)
                oscr[g - 1, hh, rows, :] = o_refs[g][0, rr, hh]
                lscr[g - 1, hh, rows, :] = l_refs[g][0, rr, hh]
    a_cols = []
    for hh in range(N_HEADS):
        os_ = [o0[0, 0, hh], oscr[0, hh], oscr[1, hh]]
        ls_ = [l0[0, 0, hh], lscr[0, hh], lscr[1, hh]]
        mx = jnp.maximum(jnp.maximum(ls_[0], ls_[1]), ls_[2])
        es = [jnp.exp(l_ - mx) for l_ in ls_]
        den = es[0] + es[1] + es[2]
        a_cols.append((es[0] * os_[0] + es[1] * os_[1] + es[2] * os_[2]) / den)
    ya_in = (jnp.concatenate(a_cols, axis=-1) * _silu(z_a)).astype(BF16)
    shift_substep(1)

    @pl.when(step == 0)
    def _():
        _new_rows_copy(new_ref, dst_b, nsem).wait()

    g_a = proj(3 * GROUP_W, D_MODEL)
    ext[16:16 + tm, :] = u
    pos = i * tm + lax.broadcasted_iota(jnp.int32, (tm, 1), 0)
    mixed = []
    for g, w in enumerate(POOL_WINDOWS):
        lanes = pl.ds(g * HEAD_DIM, HEAD_DIM)
        acc = ext[pl.ds(16, tm), lanes]
        tok = acc
        for k in range(1, w):
            acc = acc + ext[pl.ds(16 - k, tm), lanes]
        cnt = jnp.minimum(w, pos + 1).astype(F32)
        mixed.append(acc / cnt - tok)
    pool_ref[0, 0] = ext[pl.ds(tm + 1, POOL_STATE), :]
    ext[0:16, :] = jnp.where(i == n_tiles - 1, 0.0, ext[tm:tm + 16, :])
    yb_in = (_pool_project(mixed, wp_ref, bp_ref, ps_ref) * _silu(z_b)).astype(BF16)
    shift_substep(2)

    g_b = proj(3 * GROUP_W + D_MODEL, D_MODEL)
    ya = jnp.dot(ya_in, wa_ref[...], preferred_element_type=F32)
    yb = jnp.dot(yb_in, wb_ref[...], preferred_element_type=F32)
    m = (_sigmoid(g_a) * ya + _sigmoid(g_b) * yb).astype(BF16)
    shift_substep(3)
    y_ref[0] = x + gate * jnp.dot(m, wo_ref[...], preferred_element_type=F32)


def _prompt_out(x, mod_p, norm_g, w_rest, ol, w_pool, b_pool, pool_scale, wa, wb, wo,
                new_rows, state_a, state_b, shifted_a, shift):
    B, L, _ = x.shape
    tm = ROW_TILE
    n_tiles = L // tm
    const2 = lambda b, i: (0, 0)
    one = pl.Buffered(1)
    in_specs = [
        pl.BlockSpec((1, tm, D_MODEL), lambda b, i: (b, i, 0)),
        pl.BlockSpec((1, 1, 3 * D_MODEL), lambda b, i: (b, 0, 0)),
        pl.BlockSpec((1, D_MODEL), const2),
        pl.BlockSpec((D_MODEL, REST_W), const2, pipeline_mode=one),
    ]
    for window, d in ATTN_PATTERNS:
        for _ in range(2):
            in_specs.append(pl.BlockSpec((1, d, N_HEADS, tm // d, HEAD_DIM), lambda b, i: (b, 0, 0, i, 0)))
    in_specs += [
        pl.BlockSpec((len(POOL_WINDOWS), HEAD_DIM, HEAD_DIM), lambda b, i: (0, 0, 0)),
        pl.BlockSpec((len(POOL_WINDOWS), HEAD_DIM), const2),
        pl.BlockSpec((1, GROUP_W), const2),
        pl.BlockSpec((GROUP_W, D_MODEL), const2, pipeline_mode=one),
        pl.BlockSpec((GROUP_W, D_MODEL), const2, pipeline_mode=one),
        pl.BlockSpec((D_MODEL, D_MODEL), const2, pipeline_mode=one),
        pl.BlockSpec(new_rows.shape, lambda b, i: (0, 0, 0), pipeline_mode=one),
    ]
    any_spec = pl.BlockSpec(memory_space=pl.ANY)
    in_specs += [any_spec] * 3
    return pl.pallas_call(
        functools.partial(_out_body, n_tiles=n_tiles, shift=shift),
        grid=(B, n_tiles),
        in_specs=in_specs,
        out_specs=[pl.BlockSpec((1, tm, D_MODEL), lambda b, i: (b, i, 0)),
                   pl.BlockSpec((1, 1, POOL_STATE, GROUP_W), lambda b, i: (0, b, 0, 0)),
                   any_spec, any_spec],
        out_shape=[jax.ShapeDtypeStruct((B, L, D_MODEL), F32),
                   jax.ShapeDtypeStruct((1, B, POOL_STATE, GROUP_W), F32),
                   jax.ShapeDtypeStruct(state_a.shape, F32),
                   jax.ShapeDtypeStruct(state_b.shape, F32)],
        scratch_shapes=[pltpu.VMEM((tm + 16, GROUP_W), F32),
                        pltpu.VMEM((2, N_HEADS, tm, HEAD_DIM), F32),
                        pltpu.VMEM((2, N_HEADS, tm, HEAD_DIM), F32)] + _shift_scratch(OUT_SHIFT_LEAD),
        input_output_aliases={len(in_specs) - 1: 2},
        compiler_params=pltpu.CompilerParams(
            dimension_semantics=("arbitrary", "arbitrary"), vmem_limit_bytes=OUT_VMEM_LIMIT),
        name="prompt_out",
    )(x, mod_p, norm_g, w_rest, *ol, w_pool, b_pool, pool_scale, wa, wb, wo,
      new_rows, state_a, state_b, shifted_a)


def _sample_proj_body(x_ref, mod_ref, ng_ref, w_ref, o_ref):
    h = _modulated_norm(x_ref[...], ng_ref[...], mod_ref[:, 0:D_MODEL], mod_ref[:, D_MODEL:2 * D_MODEL])
    o_ref[...] = jnp.dot(h, w_ref[...], preferred_element_type=F32)


def _sample_proj(xs, mod_s, norm_g, w):
    n, width = xs.shape[0], w.shape[1]
    tn = max(t for t in range(HEAD_DIM, width // 2 + 1, HEAD_DIM) if width % t == 0)
    return pl.pallas_call(
        _sample_proj_body,
        grid=(width // tn,),
        in_specs=[pl.BlockSpec((n, D_MODEL), lambda j: (0, 0)),
                  pl.BlockSpec((n, 3 * D_MODEL), lambda j: (0, 0)),
                  pl.BlockSpec((1, D_MODEL), lambda j: (0, 0)),
                  pl.BlockSpec((D_MODEL, tn), lambda j: (0, j))],
        out_specs=pl.BlockSpec((n, tn), lambda j: (0, j)),
        out_shape=jax.ShapeDtypeStruct((n, width), F32),
        name="sample_proj",
    )(xs, mod_s, norm_g, w)


def _sample_attn_body(p_ref, s0, s1, s2, qg_ref, kg_ref, a_ref, shifted0, n1, n2):
    bt = p_ref.shape[0]
    s_refs = (s0, s1, s2)
    last = N_BACK - 1
    for b in range(bt):
        os_, ls_ = [], []
        shifted0[b, 0:last] = s0[b, 1:N_BACK]
        for g in range(N_GROUPS):
            row = lambda t: pl.ds((t * N_GROUPS + g) * N_HEADS, N_HEADS)
            q4 = _head_norm(p_ref[b, row(0), :], qg_ref[g:g + 1, :]) * Q_SCALE
            k4 = _head_norm(p_ref[b, row(1), :], kg_ref[g:g + 1, :])
            v4 = p_ref[b, row(2), :]
            if g == 0:
                shifted0[b, last, 0:N_HEADS, :] = k4
                shifted0[b, last, N_HEADS:2 * N_HEADS, :] = v4
            else:
                (n1, n2)[g - 1][b, 0:N_HEADS, :] = k4
                (n1, n2)[g - 1][b, N_HEADS:2 * N_HEADS, :] = v4
            keys = s_refs[g][b, :, 0:N_HEADS, :]
            vals = s_refs[g][b, :, N_HEADS:2 * N_HEADS, :]
            s_old = jnp.sum(keys * q4[None], axis=-1, keepdims=True)
            s_new = jnp.sum(k4 * q4, axis=-1, keepdims=True)
            m = jnp.maximum(jnp.max(s_old, axis=0), s_new)
            p_old = jnp.exp(s_old - m[None])
            p_new = jnp.exp(s_new - m)
            l = jnp.sum(p_old, axis=0) + p_new
            o = jnp.sum(p_old * vals, axis=0) + p_new * v4
            os_.append(o / l)
            ls_.append(m + jnp.log(l))
        mx = jnp.maximum(jnp.maximum(ls_[0], ls_[1]), ls_[2])
        es = [jnp.exp(l_ - mx) for l_ in ls_]
        a_ref[b] = (es[0] * os_[0] + es[1] * os_[1] + es[2] * os_[2]) / (es[0] + es[1] + es[2])


def _sample_attn(p3, views, q_gain, k_gain):
    n = p3.shape[0]
    bt = 8
    in_specs = [pl.BlockSpec((bt, p3.shape[1], HEAD_DIM), lambda j: (j, 0, 0))]
    for v in views:
        in_specs.append(pl.BlockSpec((bt, N_BACK, 2 * N_HEADS, HEAD_DIM), lambda j: (j, 0, 0, 0)))
    in_specs += [pl.BlockSpec((N_GROUPS, HEAD_DIM), lambda j: (0, 0))] * 2
    new_spec = pl.BlockSpec((bt, 2 * N_HEADS, HEAD_DIM), lambda j: (j, 0, 0))
    new_shape = jax.ShapeDtypeStruct((n, 2 * N_HEADS, HEAD_DIM), F32)
    state_spec = pl.BlockSpec((bt, N_BACK, 2 * N_HEADS, HEAD_DIM), lambda j: (j, 0, 0, 0))
    return pl.pallas_call(
        _sample_attn_body,
        grid=(n // bt,),
        in_specs=in_specs,
        out_specs=[pl.BlockSpec((bt, N_HEADS, HEAD_DIM), lambda j: (j, 0, 0)), state_spec, new_spec, new_spec],
        out_shape=[jax.ShapeDtypeStruct((n, N_HEADS, HEAD_DIM), F32),
                   jax.ShapeDtypeStruct(views[0].shape, F32), new_shape, new_shape],
        compiler_params=pltpu.CompilerParams(vmem_limit_bytes=VMEM_LIMIT),
        name="sample_attn",
    )(p3, *views, q_gain, k_gain)


def _sample_out_body(x_ref, mod_ref, r_ref, a_ref, sp_ref, wp_ref, bp_ref, ps_ref, wa_ref, wb_ref, wo_ref,
                     y_ref, pool_ref):
    u = r_ref[:, GROUP_W:2 * GROUP_W]
    mixed = []
    for g, w in enumerate(POOL_WINDOWS):
        lanes = pl.ds(g * HEAD_DIM, HEAD_DIM)
        tok = r_ref[:, pl.ds(GROUP_W + g * HEAD_DIM, HEAD_DIM)]
        acc = tok
        for k in range(1, w):
            acc = acc + sp_ref[POOL_STATE - k, :, lanes]
        mixed.append(acc / float(min(w, PAST_LEN + 1)) - tok)
    p = _pool_project(mixed, wp_ref, bp_ref, ps_ref)
    for k in range(POOL_STATE - 1):
        pool_ref[k] = sp_ref[k + 1]
    pool_ref[POOL_STATE - 1] = u
    y_ref[...] = _merge_out(
        x_ref[...], mod_ref[:, 2 * D_MODEL:3 * D_MODEL], a_ref[...], p,
        r_ref[:, 0:GROUP_W], r_ref[:, 2 * GROUP_W:3 * GROUP_W],
        r_ref[:, 3 * GROUP_W:3 * GROUP_W + D_MODEL], r_ref[:, 3 * GROUP_W + D_MODEL:REST_W],
        wa_ref, wb_ref, wo_ref)


def _sample_out(xs, mod_s, rest, a_s, sp_t, w_pool, b_pool, pool_scale, wa, wb, wo):
    n = xs.shape[0]
    return pl.pallas_call(
        _sample_out_body,
        out_shape=[jax.ShapeDtypeStruct((n, D_MODEL), F32),
                   jax.ShapeDtypeStruct((POOL_STATE, n, GROUP_W), F32)],
        compiler_params=pltpu.CompilerParams(vmem_limit_bytes=VMEM_LIMIT),
        name="sample_out",
    )(xs, mod_s, rest, a_s, sp_t, w_pool, b_pool, pool_scale, wa, wb, wo)


def kernel(x_prompt, x_sample, state_kv_w128, state_kv_w512, state_kv_w2048, state_pool, c_prompt, c_sample,
           norm_g, w_ada, b_ada, w_in, q_gain, k_gain, w_pool, b_pool, pool_scale, w_a_out, w_b_out, w_out):
    B, L, _ = x_prompt.shape
    n_s = x_sample.shape[0]
    w_qkv = w_in[0, :, :QKV_W].astype(BF16)
    w_rest = w_in[0, :, QKV_W:].astype(BF16)
    wa, wb, wo, wp = (w[0].astype(BF16) for w in (w_a_out, w_b_out, w_out, w_pool))
    qg, kg, bp = q_gain[0], k_gain[0], b_pool[0]

    mod = _ada(jnp.concatenate([c_prompt, c_sample], axis=0), w_ada[0], b_ada)
    mod_p = mod[:B].reshape(B, 1, 3 * D_MODEL)
    mod_s = mod[B:]

    xs = x_sample[:, 0, :]
    qkv_s = _sample_proj(xs, mod_s, norm_g, w_qkv)
    rest_s = _sample_proj(xs, mod_s, norm_g, w_rest)
    states = (state_kv_w128, state_kv_w512, state_kv_w2048)
    views = [s.reshape(n_s, N_BACK, d * ROWS_PER_POS, HEAD_DIM) for s, (_, d) in zip(states, ATTN_PATTERNS)]
    a_s, shifted_128, new_512, new_2048 = _sample_attn(
        qkv_s.reshape(n_s, QKV_W // HEAD_DIM, HEAD_DIM), views, qg, kg)
    y_s, pool_t = _sample_out(xs, mod_s, rest_s, a_s.reshape(n_s, GROUP_W), state_pool[0].transpose(1, 0, 2),
                              wp, bp, pool_scale, wa, wb, wo)

    flat_2048, flat_512 = (s.reshape(n_s, s.shape[2] * ROWS_PER_POS, HEAD_DIM) for s in states[:0:-1])
    kinds = _shift_plan([flat_2048.shape, flat_512.shape])
    n_sub = B * (L // ROW_TILE) * SHIFT_SUBSTEPS
    n_chunks = kinds[-1][-1]
    assert n_sub <= kinds[0][4] and n_chunks <= 2 * n_sub
    *qkv, t0, t1, t2, part_2048 = _prompt_qkv(x_prompt, mod_p, norm_g, w_qkv, qg, kg, new_2048, flat_2048,
                                              (kinds, 0, n_sub))
    ol = []
    for g, (d_blk, h_blk) in enumerate(((1, 2), (1, 4), (4, 4))):
        ol += _prompt_attn(qkv[g], qkv[N_GROUPS + g], qkv[2 * N_GROUPS + g], d_blk, h_blk)
    y_p, pool_p, shifted_2048, shifted_512 = _prompt_out(
        x_prompt, mod_p, norm_g, w_rest, ol, wp, bp, pool_scale, wa, wb, wo,
        new_512, flat_2048, flat_512, part_2048, (kinds, n_sub, n_chunks))
    kv_s = [o.reshape(s.shape) for o, s in zip((shifted_128, shifted_512, shifted_2048), states)]
    kv_p = [t.reshape(1, B, t.shape[1], 2, N_HEADS, HEAD_DIM) for t in (t0, t1, t2)]

    return (y_p, y_s.reshape(n_s, 1, D_MODEL), kv_p[0], kv_p[1], kv_p[2], pool_p,
            kv_s[0], kv_s[1], kv_s[2], pool_t.transpose(1, 0, 2)[None])
```

```python
import functools

import jax
import jax.numpy as jnp
from jax import lax
from jax.experimental import pallas as pl
from jax.experimental.pallas import tpu as pltpu

F32 = jnp.float32
BF16 = jnp.bfloat16

D_MODEL = 1024
HEAD_DIM = 128
N_HEADS = 4
GROUP_W = N_HEADS * HEAD_DIM
ATTN_PATTERNS = ((128, 1), (512, 4), (2048, 16))
N_GROUPS = len(ATTN_PATTERNS)
N_BACK = 128
QKV_W = 3 * N_GROUPS * GROUP_W
REST_W = 3 * GROUP_W + 2 * D_MODEL
POOL_WINDOWS = (2, 4, 8, 16)
POOL_STATE = 15
PAST_LEN = 8192
EPS = 1e-6
Q_SCALE = HEAD_DIM ** -0.5
NEG = -1e30

ROW_TILE = 512
Q_BLOCK = 128
Q_UNROLL = 16
VMEM_LIMIT = 56 * 1024 * 1024
OUT_VMEM_LIMIT = 60 * 1024 * 1024


def _sigmoid(v):
    return 0.5 * jnp.tanh(0.5 * v) + 0.5


def _silu(v):
    return v * _sigmoid(v)


def _modulated_norm(x, norm_g, shift, scale):
    ms = jnp.mean(x * x, axis=-1, keepdims=True)
    return (x * lax.rsqrt(ms + EPS) * norm_g * (1.0 + scale) + shift).astype(BF16)


def _head_norm(r, gain):
    ms = jnp.mean(r * r, axis=-1, keepdims=True)
    return r * lax.rsqrt(ms + EPS) * gain


def _ada_body(c_ref, w_ref, b_ref, o_ref):
    s = _silu(c_ref[...]).astype(BF16)
    o_ref[...] = jnp.dot(s, w_ref[...].astype(BF16), preferred_element_type=F32) + b_ref[...]


def _ada(c_all, w_ada, b_ada):
    n = c_all.shape[0]
    return pl.pallas_call(
        _ada_body,
        grid=(3,),
        in_specs=[pl.BlockSpec((n, D_MODEL), lambda j: (0, 0)),
                  pl.BlockSpec((D_MODEL, D_MODEL), lambda j: (0, j)),
                  pl.BlockSpec((1, D_MODEL), lambda j: (0, j))],
        out_specs=pl.BlockSpec((n, D_MODEL), lambda j: (0, j)),
        out_shape=jax.ShapeDtypeStruct((n, 3 * D_MODEL), F32),
        name="ada",
    )(c_all, w_ada, b_ada)


ROWS_PER_POS = 2 * N_HEADS
SHIFT_NB = 8
SHIFT_ROWS = 712
SHIFT_SUBSTEPS = 4
QKV_SHIFT_LEAD = 3
OUT_SHIFT_LEAD = 2
SHIFT_PRIORITY = 1


def _shift_plan(state_shapes):
    kinds, c0 = [], 0
    for idx, (n_batch, rows_total, _) in enumerate(state_shapes):
        keep = rows_total - ROWS_PER_POS
        pieces = min(p for p in range(1, keep)
                     if keep % (p * ROWS_PER_POS) == 0 and keep // p <= SHIFT_ROWS)
        n = (n_batch // SHIFT_NB) * pieces
        kinds.append((idx, keep // pieces, pieces, c0, c0 + n))
        c0 += n
    return kinds


def _shift_substep(j, lo, hi, last_slot, kinds, srcs, dsts, buf, rsem, wsem):
    ahead = buf.shape[0] // 2

    def copy(jj, kind, write):
        idx, rows, pieces, c0, _ = kind
        local = jj - c0
        batches = pl.ds((local // pieces) * SHIFT_NB, SHIFT_NB)
        first = (local % pieces) * rows
        slot = jj % buf.shape[0]
        stage = buf.at[slot, :, pl.ds(0, rows), :]
        if write:
            return pltpu.make_async_copy(stage, dsts[idx].at[batches, pl.ds(first, rows), :], wsem.at[slot])
        return pltpu.make_async_copy(srcs[idx].at[batches, pl.ds(first + ROWS_PER_POS, rows), :], stage, rsem.at[slot])

    spans = [(kind, max(kind[3], lo), min(kind[4], hi)) for kind in kinds]
    spans = [s for s in spans if s[1] < s[2]]

    def for_chunk(jj, also, fn):
        for kind, first, end in spans:
            cond = (jj >= first) & (jj < end)

            @pl.when(cond if also is None else cond & also)
            def _():
                fn(functools.partial(copy, jj, kind))

    steady = []
    for kind, first, end in spans:
        inner = (j >= first + ahead) & (j < end - ahead)
        steady.append(inner)

        @pl.when(inner)
        def _():
            copy(j, kind, False).wait()
            copy(j, kind, True).start(priority=SHIFT_PRIORITY)
            copy(j - ahead, kind, True).wait()
            copy(j + ahead, kind, False).start(priority=SHIFT_PRIORITY)

    @pl.when(jnp.logical_not(functools.reduce(jnp.logical_or, steady)))
    def _():
        for a in range(ahead):
            for_chunk(j + a, j == lo, lambda cp: cp(False).start(priority=SHIFT_PRIORITY))

        def landed(cp):
            cp(False).wait()
            cp(True).start(priority=SHIFT_PRIORITY)

        for_chunk(j, None, landed)
        for_chunk(j - ahead, None, lambda cp: cp(True).wait())
        for_chunk(j + ahead, None, lambda cp: cp(False).start(priority=SHIFT_PRIORITY))
        for a in range(ahead):
            for_chunk(j - a, j == last_slot, lambda cp: cp(True).wait())


def _new_rows_copy(new_ref, dst, nsem):
    keep = dst.shape[1] - ROWS_PER_POS
    return pltpu.make_async_copy(new_ref, dst.at[:, pl.ds(keep, ROWS_PER_POS), :], nsem)


def _shift_scratch(lead):
    return [pltpu.VMEM((2 * lead, SHIFT_NB, SHIFT_ROWS, HEAD_DIM), F32),
            pltpu.SemaphoreType.DMA((2 * lead,)), pltpu.SemaphoreType.DMA((2 * lead,)),
            pltpu.SemaphoreType.DMA(())]


def _class_major_perm(tm, d):
    i = jnp.arange(tm)
    src_row = (i % (tm // d)) * d + i // (tm // d)
    return (src_row[:, None] == jnp.arange(tm)[None, :]).astype(BF16)


SHIFT_SITES = ((0, 1), (1, 1), (2, 0))


def _qkv_body(x_ref, mod_ref, ng_ref, w_ref, qg_ref, kg_ref, perm_ref, new_ref, src_ref, *refs, n_tiles, shift):
    qkv_refs = (refs[0:3], refs[3:6], refs[6:9])
    tail_refs = refs[9:12]
    dst_ref, buf, rsem, wsem, nsem = refs[12:17]
    i = pl.program_id(1)
    tm = x_ref.shape[1]
    step = pl.program_id(0) * n_tiles + i
    kinds, lo, hi = shift

    last_slot = lo + SHIFT_SUBSTEPS * pl.num_programs(0) * n_tiles - 1

    def shift_substep(k):
        _shift_substep(lo + SHIFT_SUBSTEPS * step + k, lo, hi, last_slot, kinds, [src_ref], [dst_ref],
                       buf, rsem, wsem)

    shift_substep(0)

    @pl.when(step == 0)
    def _():
        _new_rows_copy(new_ref, dst_ref, nsem).start()

    h = _modulated_norm(x_ref[0], ng_ref[...], mod_ref[0, :, 0:D_MODEL], mod_ref[0, :, D_MODEL:2 * D_MODEL])
    h_by_group = [h] + [jnp.dot(perm_ref[g - 1], h, preferred_element_type=F32).astype(BF16)
                        for g in range(1, N_GROUPS)]

    def heads(t, g, lhs):
        c = t * N_GROUPS + g
        res = jnp.dot(lhs, w_ref[:, c * GROUP_W:(c + 1) * GROUP_W], preferred_element_type=F32)
        out = []
        for hh in range(N_HEADS):
            r = res[:, hh * HEAD_DIM:(hh + 1) * HEAD_DIM]
            if t == 0:
                r = _head_norm(r, qg_ref[g:g + 1, :]) * Q_SCALE
            elif t == 1:
                r = _head_norm(r, kg_ref[g:g + 1, :])
            out.append(r)
        return out

    def write_tail(t, g, rs):
        window, d = ATTN_PATTERNS[g]
        per = tm // d
        for hh, r in enumerate(rs):
            row = (t - 1) * N_HEADS + hh
            if d == 1:
                tail_refs[g][0, :, row, :] = r if window >= tm else r[tm - window:, :]
            else:
                assert window >= tm
                for rr in range(d):
                    tail_refs[g][0, pl.ds(rr, per, stride=d), row, :] = r[rr * per:(rr + 1) * per, :]

    for t in range(3):
        for g, (window, d) in enumerate(ATTN_PATTERNS):
            if (t, g) in SHIFT_SITES:
                shift_substep(SHIFT_SITES.index((t, g)) + 1)
            if (t, g) == SHIFT_SITES[0]:
                @pl.when(step == 0)
                def _():
                    _new_rows_copy(new_ref, dst_ref, nsem).wait()

            rs = heads(t, g, h_by_group[g])
            per = tm // d
            for hh, r in enumerate(rs):
                for rr in range(d):
                    qkv_refs[t][g][0, rr, hh] = r[rr * per:(rr + 1) * per, :].astype(BF16)
            if t > 0:
                n_tail = max(window // tm, 1)

                @pl.when(i >= n_tiles - n_tail)
                def _():
                    write_tail(t, g, rs)


def _prompt_qkv(x, mod_p, norm_g, w_qkv, q_gain, k_gain, new_rows, state, shift):
    B, L, _ = x.shape
    tm = ROW_TILE
    n_tiles = L // tm
    const2 = lambda b, i: (0, 0)
    in_specs = [
        pl.BlockSpec((1, tm, D_MODEL), lambda b, i: (b, i, 0)),
        pl.BlockSpec((1, 1, 3 * D_MODEL), lambda b, i: (b, 0, 0)),
        pl.BlockSpec((1, D_MODEL), const2),
        pl.BlockSpec((D_MODEL, QKV_W), const2, pipeline_mode=pl.Buffered(1)),
        pl.BlockSpec((N_GROUPS, HEAD_DIM), const2),
        pl.BlockSpec((N_GROUPS, HEAD_DIM), const2),
        pl.BlockSpec((N_GROUPS - 1, tm, tm), lambda b, i: (0, 0, 0), pipeline_mode=pl.Buffered(1)),
        pl.BlockSpec(new_rows.shape, lambda b, i: (0, 0, 0), pipeline_mode=pl.Buffered(1)),
        pl.BlockSpec(memory_space=pl.ANY),
    ]
    perms = jnp.stack([_class_major_perm(tm, d) for _, d in ATTN_PATTERNS[1:]])
    out_specs, out_shape = [], []
    for t in range(3):
        for window, d in ATTN_PATTERNS:
            out_specs.append(pl.BlockSpec((1, d, N_HEADS, tm // d, HEAD_DIM), lambda b, i: (b, 0, 0, i, 0)))
            out_shape.append(jax.ShapeDtypeStruct((B, d, N_HEADS, L // d, HEAD_DIM), BF16))
    for window, d in ATTN_PATTERNS:
        w_eff = min(window, L)
        if w_eff >= tm:
            n_tail = w_eff // tm
            out_specs.append(pl.BlockSpec(
                (1, tm, 2 * N_HEADS, HEAD_DIM),
                lambda b, i, n_tail=n_tail: (b, jnp.maximum(i - (n_tiles - n_tail), 0), 0, 0)))
        else:
            out_specs.append(pl.BlockSpec((1, w_eff, 2 * N_HEADS, HEAD_DIM), lambda b, i: (b, 0, 0, 0)))
        out_shape.append(jax.ShapeDtypeStruct((B, w_eff, 2 * N_HEADS, HEAD_DIM), F32))
    out_specs.append(pl.BlockSpec(memory_space=pl.ANY))
    out_shape.append(jax.ShapeDtypeStruct(state.shape, F32))
    return pl.pallas_call(
        functools.partial(_qkv_body, n_tiles=n_tiles, shift=shift),
        grid=(B, n_tiles),
        in_specs=in_specs,
        out_specs=out_specs,
        out_shape=out_shape,
        scratch_shapes=_shift_scratch(QKV_SHIFT_LEAD),
        compiler_params=pltpu.CompilerParams(
            dimension_semantics=("arbitrary", "arbitrary"), vmem_limit_bytes=VMEM_LIMIT),
        name="prompt_qkv",
    )(x, mod_p, norm_g, w_qkv, q_gain, k_gain, perms, new_rows, state)


def _attn_body(q_ref, k_ref, v_ref, o_ref, l_ref, vx):
    _, d_blk, h_blk, M, _ = q_ref.shape
    n_q = M // Q_BLOCK
    for c in range(d_blk):
        for hh in range(h_blk):
            vx[c, hh, :, 0:HEAD_DIM] = v_ref[0, c, hh]
            vx[c, hh, :, HEAD_DIM:2 * HEAD_DIM] = jnp.ones((M, HEAD_DIM), BF16)
    ii = lax.broadcasted_iota(jnp.int32, (Q_BLOCK, 2 * Q_BLOCK), 0)
    jj = lax.broadcasted_iota(jnp.int32, (Q_BLOCK, 2 * Q_BLOCK), 1)
    dist = Q_BLOCK + ii - jj
    band_ok = (dist >= 0) & (dist <= N_BACK)
    causal_ok = (lax.broadcasted_iota(jnp.int32, (Q_BLOCK, Q_BLOCK), 1)
                 <= lax.broadcasted_iota(jnp.int32, (Q_BLOCK, Q_BLOCK), 0))
    nt = (((1,), (1,)), ((), ()))

    def block(c, hh, n, first):
        rows = pl.ds(pl.multiple_of(n * Q_BLOCK, Q_BLOCK), Q_BLOCK)
        if first:
            keys, ok = rows, causal_ok
        else:
            keys, ok = pl.ds(pl.multiple_of((n - 1) * Q_BLOCK, Q_BLOCK), 2 * Q_BLOCK), band_ok
        s = lax.dot_general(q_ref[0, c, hh, rows, :], k_ref[0, c, hh, keys, :], nt, preferred_element_type=F32)
        s = jnp.where(ok, s, NEG)
        m = jnp.max(s, axis=-1, keepdims=True)
        p = jnp.exp(s - m).astype(BF16)
        oe = jnp.dot(p, vx[c, hh, keys, :], preferred_element_type=F32)
        l = oe[:, HEAD_DIM:]
        o_ref[0, c, hh, rows, :] = oe[:, :HEAD_DIM] / l
        l_ref[0, c, hh, rows, :] = m + jnp.log(l)

    for c in range(d_blk):
        for hh in range(h_blk):
            block(c, hh, 0, True)
        if n_q <= 2:
            for n in range(1, n_q):
                for hh in range(h_blk):
                    block(c, hh, n, False)
        else:
            def step(n, carry, c=c):
                for hh in range(h_blk):
                    block(c, hh, n, False)
                return carry

            lax.fori_loop(1, n_q, step, 0, unroll=Q_UNROLL)


def _prompt_attn(q, k, v, d_blk, h_blk):
    B, d, H, M, E = q.shape
    blk = (1, d_blk, h_blk, M, E)
    spec = pl.BlockSpec(blk, lambda b, c, hh: (b, c, hh, 0, 0))
    shape = jax.ShapeDtypeStruct(q.shape, F32)
    return pl.pallas_call(
        _attn_body,
        grid=(B, d // d_blk, H // h_blk),
        in_specs=[spec, spec, spec],
        out_specs=[spec, spec],
        out_shape=[shape, shape],
        scratch_shapes=[pltpu.VMEM((d_blk, h_blk, M, 2 * E), BF16)],
        compiler_params=pltpu.CompilerParams(
            dimension_semantics=("arbitrary", "arbitrary", "arbitrary"), vmem_limit_bytes=VMEM_LIMIT),
        name="prompt_attn_d%d" % d,
    )(q, k, v)


def _pool_project(mixed, wp_ref, bp_ref, ps_ref):
    cols = []
    for g in range(len(POOL_WINDOWS)):
        lanes = slice(g * HEAD_DIM, (g + 1) * HEAD_DIM)
        y = jnp.dot(mixed[g].astype(BF16), wp_ref[g], preferred_element_type=F32) + bp_ref[g:g + 1, :]
        cols.append(y * ps_ref[:, lanes])
    return jnp.concatenate(cols, axis=-1)


def _merge_out(x, gate, a, p, z_a, z_b, g_a, g_b, wa_ref, wb_ref, wo_ref):
    ya = jnp.dot((a * _silu(z_a)).astype(BF16), wa_ref[...], preferred_element_type=F32)
    yb = jnp.dot((p * _silu(z_b)).astype(BF16), wb_ref[...], preferred_element_type=F32)
    m = _sigmoid(g_a) * ya + _sigmoid(g_b) * yb
    return x + gate * jnp.dot(m.astype(BF16), wo_ref[...], preferred_element_type=F32)


def _out_body(x_ref, mod_ref, ng_ref, w_ref, o0, l0, o1, l1, o2, l2, wp_ref, bp_ref, ps_ref,
              wa_ref, wb_ref, wo_ref, new_ref, src_a, src_b, dst_a_in, y_ref, pool_ref, dst_a, dst_b,
              ext, oscr, lscr, buf, rsem, wsem, nsem, *, n_tiles, shift):
    del dst_a_in
    i = pl.program_id(1)
    tm = x_ref.shape[1]
    step = pl.program_id(0) * n_tiles + i
    kinds, lo, hi = shift

    last_slot = lo + SHIFT_SUBSTEPS * pl.num_programs(0) * n_tiles - 1

    def shift_substep(k):
        _shift_substep(lo + SHIFT_SUBSTEPS * step + k, lo, hi, last_slot, kinds, [src_a, src_b], [dst_a, dst_b],
                       buf, rsem, wsem)

    shift_substep(0)

    @pl.when(step == 0)
    def _():
        _new_rows_copy(new_ref, dst_b, nsem).start()
        ext[0:16, :] = jnp.zeros((16, GROUP_W), F32)

    x = x_ref[0]
    h = _modulated_norm(x, ng_ref[...], mod_ref[0, :, 0:D_MODEL], mod_ref[0, :, D_MODEL:2 * D_MODEL])
    gate = mod_ref[0, :, 2 * D_MODEL:3 * D_MODEL]

    def proj(lo, width):
        return jnp.dot(h, w_ref[:, lo:lo + width], preferred_element_type=F32)

    z_a = proj(0, GROUP_W)
    u = proj(GROUP_W, GROUP_W)
    z_b = proj(2 * GROUP_W, GROUP_W)
    o_refs, l_refs = (o0, o1, o2), (l0, l1, l2)
    for g, (window, d) in enumerate(ATTN_PATTERNS):
        if d == 1:
            continue
        for hh in range(N_HEADS):
            for rr in range(d):
                rows = pl.ds(rr, tm // d, stride=d)
                oscr[g - 1, hh, rows, :] = o_refs[g][0, rr, hh]
                lscr[g - 1, hh, rows, :] = l_refs[g][0, rr, hh]
    a_cols = []
    for hh in range(N_HEADS):
        os_ = [o0[0, 0, hh], oscr[0, hh], oscr[1, hh]]
        ls_ = [l0[0, 0, hh], lscr[0, hh], lscr[1, hh]]
        mx = jnp.maximum(jnp.maximum(ls_[0], ls_[1]), ls_[2])
        es = [jnp.exp(l_ - mx) for l_ in ls_]
        den = es[0] + es[1] + es[2]
        a_cols.append((es[0] * os_[0] + es[1] * os_[1] + es[2] * os_[2]) / den)
    ya_in = (jnp.concatenate(a_cols, axis=-1) * _silu(z_a)).astype(BF16)
    shift_substep(1)

    @pl.when(step == 0)
    def _():
        _new_rows_copy(new_ref, dst_b, nsem).wait()

    g_a = proj(3 * GROUP_W, D_MODEL)
    ext[16:16 + tm, :] = u
    pos = i * tm + lax.broadcasted_iota(jnp.int32, (tm, 1), 0)
    mixed = []
    for g, w in enumerate(POOL_WINDOWS):
        lanes = pl.ds(g * HEAD_DIM, HEAD_DIM)
        acc = ext[pl.ds(16, tm), lanes]
        tok = acc
        for k in range(1, w):
            acc = acc + ext[pl.ds(16 - k, tm), lanes]
        cnt = jnp.minimum(w, pos + 1).astype(F32)
        mixed.append(acc / cnt - tok)
    pool_ref[0, 0] = ext[pl.ds(tm + 1, POOL_STATE), :]
    ext[0:16, :] = jnp.where(i == n_tiles - 1, 0.0, ext[tm:tm + 16, :])
    yb_in = (_pool_project(mixed, wp_ref, bp_ref, ps_ref) * _silu(z_b)).astype(BF16)
    shift_substep(2)

    g_b = proj(3 * GROUP_W + D_MODEL, D_MODEL)
    ya = jnp.dot(ya_in, wa_ref[...], preferred_element_type=F32)
    yb = jnp.dot(yb_in, wb_ref[...], preferred_element_type=F32)
    m = (_sigmoid(g_a) * ya + _sigmoid(g_b) * yb).astype(BF16)
    shift_substep(3)
    y_ref[0] = x + gate * jnp.dot(m, wo_ref[...], preferred_element_type=F32)


def _prompt_out(x, mod_p, norm_g, w_rest, ol, w_pool, b_pool, pool_scale, wa, wb, wo,
                new_rows, state_a, state_b, shifted_a, shift):
    B, L, _ = x.shape
    tm = ROW_TILE
    n_tiles = L // tm
    const2 = lambda b, i: (0, 0)
    one = pl.Buffered(1)
    in_specs = [
        pl.BlockSpec((1, tm, D_MODEL), lambda b, i: (b, i, 0)),
        pl.BlockSpec((1, 1, 3 * D_MODEL), lambda b, i: (b, 0, 0)),
        pl.BlockSpec((1, D_MODEL), const2),
        pl.BlockSpec((D_MODEL, REST_W), const2, pipeline_mode=one),
    ]
    for window, d in ATTN_PATTERNS:
        for _ in range(2):
            in_specs.append(pl.BlockSpec((1, d, N_HEADS, tm // d, HEAD_DIM), lambda b, i: (b, 0, 0, i, 0)))
    in_specs += [
        pl.BlockSpec((len(POOL_WINDOWS), HEAD_DIM, HEAD_DIM), lambda b, i: (0, 0, 0)),
        pl.BlockSpec((len(POOL_WINDOWS), HEAD_DIM), const2),
        pl.BlockSpec((1, GROUP_W), const2),
        pl.BlockSpec((GROUP_W, D_MODEL), const2, pipeline_mode=one),
        pl.BlockSpec((GROUP_W, D_MODEL), const2, pipeline_mode=one),
        pl.BlockSpec((D_MODEL, D_MODEL), const2, pipeline_mode=one),
        pl.BlockSpec(new_rows.shape, lambda b, i: (0, 0, 0), pipeline_mode=one),
    ]
    any_spec = pl.BlockSpec(memory_space=pl.ANY)
    in_specs += [any_spec] * 3
    return pl.pallas_call(
        functools.partial(_out_body, n_tiles=n_tiles, shift=shift),
        grid=(B, n_tiles),
        in_specs=in_specs,
        out_specs=[pl.BlockSpec((1, tm, D_MODEL), lambda b, i: (b, i, 0)),
                   pl.BlockSpec((1, 1, POOL_STATE, GROUP_W), lambda b, i: (0, b, 0, 0)),
                   any_spec, any_spec],
        out_shape=[jax.ShapeDtypeStruct((B, L, D_MODEL), F32),
                   jax.ShapeDtypeStruct((1, B, POOL_STATE, GROUP_W), F32),
                   jax.ShapeDtypeStruct(state_a.shape, F32),
                   jax.ShapeDtypeStruct(state_b.shape, F32)],
        scratch_shapes=[pltpu.VMEM((tm + 16, GROUP_W), F32),
                        pltpu.VMEM((2, N_HEADS, tm, HEAD_DIM), F32),
                        pltpu.VMEM((2, N_HEADS, tm, HEAD_DIM), F32)] + _shift_scratch(OUT_SHIFT_LEAD),
        input_output_aliases={len(in_specs) - 1: 2},
        compiler_params=pltpu.CompilerParams(
            dimension_semantics=("arbitrary", "arbitrary"), vmem_limit_bytes=OUT_VMEM_LIMIT),
        name="prompt_out",
    )(x, mod_p, norm_g, w_rest, *ol, w_pool, b_pool, pool_scale, wa, wb, wo,
      new_rows, state_a, state_b, shifted_a)


def _sample_proj_body(x_ref, mod_ref, ng_ref, w_ref, o_ref):
    h = _modulated_norm(x_ref[...], ng_ref[...], mod_ref[:, 0:D_MODEL], mod_ref[:, D_MODEL:2 * D_MODEL])
    o_ref[...] = jnp.dot(h, w_ref[...], preferred_element_type=F32)


def _sample_proj(xs, mod_s, norm_g, w):
    n, width = xs.shape[0], w.shape[1]
    tn = max(t for t in range(HEAD_DIM, width // 2 + 1, HEAD_DIM) if width % t == 0)
    return pl.pallas_call(
        _sample_proj_body,
        grid=(width // tn,),
        in_specs=[pl.BlockSpec((n, D_MODEL), lambda j: (0, 0)),
                  pl.BlockSpec((n, 3 * D_MODEL), lambda j: (0, 0)),
                  pl.BlockSpec((1, D_MODEL), lambda j: (0, 0)),
                  pl.BlockSpec((D_MODEL, tn), lambda j: (0, j))],
        out_specs=pl.BlockSpec((n, tn), lambda j: (0, j)),
        out_shape=jax.ShapeDtypeStruct((n, width), F32),
        name="sample_proj",
    )(xs, mod_s, norm_g, w)


def _sample_attn_body(p_ref, s0, s1, s2, qg_ref, kg_ref, a_ref, shifted0, n1, n2):
    bt = p_ref.shape[0]
    s_refs = (s0, s1, s2)
    last = N_BACK - 1
    for b in range(bt):
        os_, ls_ = [], []
        shifted0[b, 0:last] = s0[b, 1:N_BACK]
        for g in range(N_GROUPS):
            row = lambda t: pl.ds((t * N_GROUPS + g) * N_HEADS, N_HEADS)
            q4 = _head_norm(p_ref[b, row(0), :], qg_ref[g:g + 1, :]) * Q_SCALE
            k4 = _head_norm(p_ref[b, row(1), :], kg_ref[g:g + 1, :])
            v4 = p_ref[b, row(2), :]
            if g == 0:
                shifted0[b, last, 0:N_HEADS, :] = k4
                shifted0[b, last, N_HEADS:2 * N_HEADS, :] = v4
            else:
                (n1, n2)[g - 1][b, 0:N_HEADS, :] = k4
                (n1, n2)[g - 1][b, N_HEADS:2 * N_HEADS, :] = v4
            keys = s_refs[g][b, :, 0:N_HEADS, :]
            vals = s_refs[g][b, :, N_HEADS:2 * N_HEADS, :]
            s_old = jnp.sum(keys * q4[None], axis=-1, keepdims=True)
            s_new = jnp.sum(k4 * q4, axis=-1, keepdims=True)
            m = jnp.maximum(jnp.max(s_old, axis=0), s_new)
            p_old = jnp.exp(s_old - m[None])
            p_new = jnp.exp(s_new - m)
            l = jnp.sum(p_old, axis=0) + p_new
            o = jnp.sum(p_old * vals, axis=0) + p_new * v4
            os_.append(o / l)
            ls_.append(m + jnp.log(l))
        mx = jnp.maximum(jnp.maximum(ls_[0], ls_[1]), ls_[2])
        es = [jnp.exp(l_ - mx) for l_ in ls_]
        a_ref[b] = (es[0] * os_[0] + es[1] * os_[1] + es[2] * os_[2]) / (es[0] + es[1] + es[2])


def _sample_attn(p3, views, q_gain, k_gain):
    n = p3.shape[0]
    bt = 8
    in_specs = [pl.BlockSpec((bt, p3.shape[1], HEAD_DIM), lambda j: (j, 0, 0))]
    for v in views:
        in_specs.append(pl.BlockSpec((bt, N_BACK, 2 * N_HEADS, HEAD_DIM), lambda j: (j, 0, 0, 0)))
    in_specs += [pl.BlockSpec((N_GROUPS, HEAD_DIM), lambda j: (0, 0))] * 2
    new_spec = pl.BlockSpec((bt, 2 * N_HEADS, HEAD_DIM), lambda j: (j, 0, 0))
    new_shape = jax.ShapeDtypeStruct((n, 2 * N_HEADS, HEAD_DIM), F32)
    state_spec = pl.BlockSpec((bt, N_BACK, 2 * N_HEADS, HEAD_DIM), lambda j: (j, 0, 0, 0))
    return pl.pallas_call(
        _sample_attn_body,
        grid=(n // bt,),
        in_specs=in_specs,
        out_specs=[pl.BlockSpec((bt, N_HEADS, HEAD_DIM), lambda j: (j, 0, 0)), state_spec, new_spec, new_spec],
        out_shape=[jax.ShapeDtypeStruct((n, N_HEADS, HEAD_DIM), F32),
                   jax.ShapeDtypeStruct(views[0].shape, F32), new_shape, new_shape],
        compiler_params=pltpu.CompilerParams(vmem_limit_bytes=VMEM_LIMIT),
        name="sample_attn",
    )(p3, *views, q_gain, k_gain)


def _sample_out_body(x_ref, mod_ref, r_ref, a_ref, sp_ref, wp_ref, bp_ref, ps_ref, wa_ref, wb_ref, wo_ref,
                     y_ref, pool_ref):
    u = r_ref[:, GROUP_W:2 * GROUP_W]
    mixed = []
    for g, w in enumerate(POOL_WINDOWS):
        lanes = pl.ds(g * HEAD_DIM, HEAD_DIM)
        tok = r_ref[:, pl.ds(GROUP_W + g * HEAD_DIM, HEAD_DIM)]
        acc = tok
        for k in range(1, w):
            acc = acc + sp_ref[POOL_STATE - k, :, lanes]
        mixed.append(acc / float(min(w, PAST_LEN + 1)) - tok)
    p = _pool_project(mixed, wp_ref, bp_ref, ps_ref)
    for k in range(POOL_STATE - 1):
        pool_ref[k] = sp_ref[k + 1]
    pool_ref[POOL_STATE - 1] = u
    y_ref[...] = _merge_out(
        x_ref[...], mod_ref[:, 2 * D_MODEL:3 * D_MODEL], a_ref[...], p,
        r_ref[:, 0:GROUP_W], r_ref[:, 2 * GROUP_W:3 * GROUP_W],
        r_ref[:, 3 * GROUP_W:3 * GROUP_W + D_MODEL], r_ref[:, 3 * GROUP_W + D_MODEL:REST_W],
        wa_ref, wb_ref, wo_ref)


def _sample_out(xs, mod_s, rest, a_s, sp_t, w_pool, b_pool, pool_scale, wa, wb, wo):
    n = xs.shape[0]
    return pl.pallas_call(
        _sample_out_body,
        out_shape=[jax.ShapeDtypeStruct((n, D_MODEL), F32),
                   jax.ShapeDtypeStruct((POOL_STATE, n, GROUP_W), F32)],
        compiler_params=pltpu.CompilerParams(vmem_limit_bytes=VMEM_LIMIT),
        name="sample_out",
    )(xs, mod_s, rest, a_s, sp_t, w_pool, b_pool, pool_scale, wa, wb, wo)


def kernel(x_prompt, x_sample, state_kv_w128, state_kv_w512, state_kv_w2048, state_pool, c_prompt, c_sample,
           norm_g, w_ada, b_ada, w_in, q_gain, k_gain, w_pool, b_pool, pool_scale, w_a_out, w_b_out, w_out):
    B, L, _ = x_prompt.shape
    n_s = x_sample.shape[0]
    w_qkv = w_in[0, :, :QKV_W].astype(BF16)
    w_rest = w_in[0, :, QKV_W:].astype(BF16)
    wa, wb, wo, wp = (w[0].astype(BF16) for w in (w_a_out, w_b_out, w_out, w_pool))
    qg, kg, bp = q_gain[0], k_gain[0], b_pool[0]

    mod = _ada(jnp.concatenate([c_prompt, c_sample], axis=0), w_ada[0], b_ada)
    mod_p = mod[:B].reshape(B, 1, 3 * D_MODEL)
    mod_s = mod[B:]

    xs = x_sample[:, 0, :]
    qkv_s = _sample_proj(xs, mod_s, norm_g, w_qkv)
    rest_s = _sample_proj(xs, mod_s, norm_g, w_rest)
    states = (state_kv_w128, state_kv_w512, state_kv_w2048)
    views = [s.reshape(n_s, N_BACK, d * ROWS_PER_POS, HEAD_DIM) for s, (_, d) in zip(states, ATTN_PATTERNS)]
    a_s, shifted_128, new_512, new_2048 = _sample_attn(
        qkv_s.reshape(n_s, QKV_W // HEAD_DIM, HEAD_DIM), views, qg, kg)
    y_s, pool_t = _sample_out(xs, mod_s, rest_s, a_s.reshape(n_s, GROUP_W), state_pool[0].transpose(1, 0, 2),
                              wp, bp, pool_scale, wa, wb, wo)

    flat_2048, flat_512 = (s.reshape(n_s, s.shape[2] * ROWS_PER_POS, HEAD_DIM) for s in states[:0:-1])
    kinds = _shift_plan([flat_2048.shape, flat_512.shape])
    n_sub = B * (L // ROW_TILE) * SHIFT_SUBSTEPS
    n_chunks = kinds[-1][-1]
    assert n_sub <= kinds[0][4] and n_chunks <= 2 * n_sub
    *qkv, t0, t1, t2, part_2048 = _prompt_qkv(x_prompt, mod_p, norm_g, w_qkv, qg, kg, new_2048, flat_2048,
                                              (kinds, 0, n_sub))
    ol = []
    for g, (d_blk, h_blk) in enumerate(((1, 2), (2, 4), (8, 4))):
        ol += _prompt_attn(qkv[g], qkv[N_GROUPS + g], qkv[2 * N_GROUPS + g], d_blk, h_blk)
    y_p, pool_p, shifted_2048, shifted_512 = _prompt_out(
        x_prompt, mod_p, norm_g, w_rest, ol, wp, bp, pool_scale, wa, wb, wo,
        new_512, flat_2048, flat_512, part_2048, (kinds, n_sub, n_chunks))
    kv_s = [o.reshape(s.shape) for o, s in zip((shifted_128, shifted_512, shifted_2048), states)]
    kv_p = [t.reshape(1, B, t.shape[1], 2, N_HEADS, HEAD_DIM) for t in (t0, t1, t2)]

    return (y_p, y_s.reshape(n_s, 1, D_MODEL), kv_p[0], kv_p[1], kv_p[2], pool_p,
            kv_s[0], kv_s[1], kv_s[2], pool_t.transpose(1, 0, 2)[None])
```

```python
import functools

import jax
import jax.numpy as jnp
from jax import lax
from jax.experimental import pallas as pl
from jax.experimental.pallas import tpu as pltpu

F32 = jnp.float32
BF16 = jnp.bfloat16

D_MODEL = 1024
HEAD_DIM = 128
N_HEADS = 4
GROUP_W = N_HEADS * HEAD_DIM
ATTN_PATTERNS = ((128, 1), (512, 4), (2048, 16))
N_GROUPS = len(ATTN_PATTERNS)
N_BACK = 128
QKV_W = 3 * N_GROUPS * GROUP_W
REST_W = 3 * GROUP_W + 2 * D_MODEL
POOL_WINDOWS = (2, 4, 8, 16)
POOL_STATE = 15
PAST_LEN = 8192
EPS = 1e-6
Q_SCALE = HEAD_DIM ** -0.5
NEG = -1e30

ROW_TILE = 512
Q_BLOCK = 128
Q_UNROLL = 16
VMEM_LIMIT = 56 * 1024 * 1024
OUT_VMEM_LIMIT = 60 * 1024 * 1024


def _sigmoid(v):
    return 0.5 * jnp.tanh(0.5 * v) + 0.5


def _silu(v):
    return v * _sigmoid(v)


def _modulated_norm(x, norm_g, shift, scale):
    ms = jnp.mean(x * x, axis=-1, keepdims=True)
    return (x * lax.rsqrt(ms + EPS) * norm_g * (1.0 + scale) + shift).astype(BF16)


def _head_norm(r, gain):
    ms = jnp.mean(r * r, axis=-1, keepdims=True)
    return r * lax.rsqrt(ms + EPS) * gain


def _ada_body(c_ref, w_ref, b_ref, o_ref):
    s = _silu(c_ref[...]).astype(BF16)
    o_ref[...] = jnp.dot(s, w_ref[...].astype(BF16), preferred_element_type=F32) + b_ref[...]


def _ada(c_all, w_ada, b_ada):
    n = c_all.shape[0]
    return pl.pallas_call(
        _ada_body,
        grid=(3,),
        in_specs=[pl.BlockSpec((n, D_MODEL), lambda j: (0, 0)),
                  pl.BlockSpec((D_MODEL, D_MODEL), lambda j: (0, j)),
                  pl.BlockSpec((1, D_MODEL), lambda j: (0, j))],
        out_specs=pl.BlockSpec((n, D_MODEL), lambda j: (0, j)),
        out_shape=jax.ShapeDtypeStruct((n, 3 * D_MODEL), F32),
        name="ada",
    )(c_all, w_ada, b_ada)


ROWS_PER_POS = 2 * N_HEADS
SHIFT_NB = 8
SHIFT_ROWS = 712
SHIFT_SUBSTEPS = 4
QKV_SHIFT_LEAD = 3
OUT_SHIFT_LEAD = 2
SHIFT_PRIORITY = 1


def _shift_plan(state_shapes):
    kinds, c0 = [], 0
    for idx, (n_batch, rows_total, _) in enumerate(state_shapes):
        keep = rows_total - ROWS_PER_POS
        pieces = min(p for p in range(1, keep)
                     if keep % (p * ROWS_PER_POS) == 0 and keep // p <= SHIFT_ROWS)
        n = (n_batch // SHIFT_NB) * pieces
        kinds.append((idx, keep // pieces, pieces, c0, c0 + n))
        c0 += n
    return kinds


def _shift_substep(j, lo, hi, last_slot, kinds, srcs, dsts, buf, rsem, wsem):
    ahead = buf.shape[0] // 2

    def copy(jj, kind, write):
        idx, rows, pieces, c0, _ = kind
        local = jj - c0
        batches = pl.ds((local // pieces) * SHIFT_NB, SHIFT_NB)
        first = (local % pieces) * rows
        slot = jj % buf.shape[0]
        stage = buf.at[slot, :, pl.ds(0, rows), :]
        if write:
            return pltpu.make_async_copy(stage, dsts[idx].at[batches, pl.ds(first, rows), :], wsem.at[slot])
        return pltpu.make_async_copy(srcs[idx].at[batches, pl.ds(first + ROWS_PER_POS, rows), :], stage, rsem.at[slot])

    spans = [(kind, max(kind[3], lo), min(kind[4], hi)) for kind in kinds]
    spans = [s for s in spans if s[1] < s[2]]

    def for_chunk(jj, also, fn):
        for kind, first, end in spans:
            cond = (jj >= first) & (jj < end)

            @pl.when(cond if also is None else cond & also)
            def _():
                fn(functools.partial(copy, jj, kind))

    steady = []
    for kind, first, end in spans:
        inner = (j >= first + ahead) & (j < end - ahead)
        steady.append(inner)

        @pl.when(inner)
        def _():
            copy(j, kind, False).wait()
            copy(j, kind, True).start(priority=SHIFT_PRIORITY)
            copy(j - ahead, kind, True).wait()
            copy(j + ahead, kind, False).start(priority=SHIFT_PRIORITY)

    @pl.when(jnp.logical_not(functools.reduce(jnp.logical_or, steady)))
    def _():
        for a in range(ahead):
            for_chunk(j + a, j == lo, lambda cp: cp(False).start(priority=SHIFT_PRIORITY))

        def landed(cp):
            cp(False).wait()
            cp(True).start(priority=SHIFT_PRIORITY)

        for_chunk(j, None, landed)
        for_chunk(j - ahead, None, lambda cp: cp(True).wait())
        for_chunk(j + ahead, None, lambda cp: cp(False).start(priority=SHIFT_PRIORITY))
        for a in range(ahead):
            for_chunk(j - a, j == last_slot, lambda cp: cp(True).wait())


def _new_rows_copy(new_ref, dst, nsem):
    keep = dst.shape[1] - ROWS_PER_POS
    return pltpu.make_async_copy(new_ref, dst.at[:, pl.ds(keep, ROWS_PER_POS), :], nsem)


def _shift_scratch(lead):
    return [pltpu.VMEM((2 * lead, SHIFT_NB, SHIFT_ROWS, HEAD_DIM), F32),
            pltpu.SemaphoreType.DMA((2 * lead,)), pltpu.SemaphoreType.DMA((2 * lead,)),
            pltpu.SemaphoreType.DMA(())]


def _class_major_perm(tm, d):
    i = jnp.arange(tm)
    src_row = (i % (tm // d)) * d + i // (tm // d)
    return (src_row[:, None] == jnp.arange(tm)[None, :]).astype(BF16)


SHIFT_SITE = (1, 1)


def _qkv_body(x_ref, mod_ref, ng_ref, w_ref, qg_ref, kg_ref, perm_ref, new_ref, src_ref, *refs, n_tiles, shift):
    qkv_refs = (refs[0:3], refs[3:6], refs[6:9])
    tail_refs = refs[9:12]
    dst_ref, buf, rsem, wsem, nsem = refs[12:17]
    i = pl.program_id(1)
    tm = x_ref.shape[1]
    step = pl.program_id(0) * n_tiles + i
    kinds, lo, hi = shift

    last_slot = lo + SHIFT_SUBSTEPS * pl.num_programs(0) * n_tiles - 1

    def shift_substep(k):
        _shift_substep(lo + SHIFT_SUBSTEPS * step + k, lo, hi, last_slot, kinds, [src_ref], [dst_ref],
                       buf, rsem, wsem)

    for k in range(SHIFT_SUBSTEPS // 2):
        shift_substep(k)

    @pl.when(step == 0)
    def _():
        _new_rows_copy(new_ref, dst_ref, nsem).start()

    h = _modulated_norm(x_ref[0], ng_ref[...], mod_ref[0, :, 0:D_MODEL], mod_ref[0, :, D_MODEL:2 * D_MODEL])
    h_by_group = [h] + [jnp.dot(perm_ref[g - 1], h, preferred_element_type=F32).astype(BF16)
                        for g in range(1, N_GROUPS)]

    def heads(t, g, lhs):
        c = t * N_GROUPS + g
        res = jnp.dot(lhs, w_ref[:, c * GROUP_W:(c + 1) * GROUP_W], preferred_element_type=F32)
        out = []
        for hh in range(N_HEADS):
            r = res[:, hh * HEAD_DIM:(hh + 1) * HEAD_DIM]
            if t == 0:
                r = _head_norm(r, qg_ref[g:g + 1, :]) * Q_SCALE
            elif t == 1:
                r = _head_norm(r, kg_ref[g:g + 1, :])
            out.append(r)
        return out

    def write_tail(t, g, rs):
        window, d = ATTN_PATTERNS[g]
        per = tm // d
        for hh, r in enumerate(rs):
            row = (t - 1) * N_HEADS + hh
            if d == 1:
                tail_refs[g][0, :, row, :] = r if window >= tm else r[tm - window:, :]
            else:
                assert window >= tm
                for rr in range(d):
                    tail_refs[g][0, pl.ds(rr, per, stride=d), row, :] = r[rr * per:(rr + 1) * per, :]

    kept = {}
    for t in range(3):
        for g, (window, d) in enumerate(ATTN_PATTERNS):
            if (t, g) == SHIFT_SITE:
                for k in range(SHIFT_SUBSTEPS // 2, SHIFT_SUBSTEPS):
                    shift_substep(k)

                @pl.when(step == 0)
                def _():
                    _new_rows_copy(new_ref, dst_ref, nsem).wait()

            rs = heads(t, g, h_by_group[g])
            per = tm // d
            for hh, r in enumerate(rs):
                for rr in range(d):
                    qkv_refs[t][g][0, rr, hh] = r[rr * per:(rr + 1) * per, :].astype(BF16)
            if t > 0:
                kept.setdefault(max(window // tm, 1), []).append((t, g, rs))

    for n_tail, items in sorted(kept.items()):
        @pl.when(i >= n_tiles - n_tail)
        def _():
            for t, g, rs in items:
                write_tail(t, g, rs)


def _prompt_qkv(x, mod_p, norm_g, w_qkv, q_gain, k_gain, new_rows, state, shift):
    B, L, _ = x.shape
    tm = ROW_TILE
    n_tiles = L // tm
    const2 = lambda b, i: (0, 0)
    in_specs = [
        pl.BlockSpec((1, tm, D_MODEL), lambda b, i: (b, i, 0)),
        pl.BlockSpec((1, 1, 3 * D_MODEL), lambda b, i: (b, 0, 0)),
        pl.BlockSpec((1, D_MODEL), const2),
        pl.BlockSpec((D_MODEL, QKV_W), const2, pipeline_mode=pl.Buffered(1)),
        pl.BlockSpec((N_GROUPS, HEAD_DIM), const2),
        pl.BlockSpec((N_GROUPS, HEAD_DIM), const2),
        pl.BlockSpec((N_GROUPS - 1, tm, tm), lambda b, i: (0, 0, 0), pipeline_mode=pl.Buffered(1)),
        pl.BlockSpec(new_rows.shape, lambda b, i: (0, 0, 0), pipeline_mode=pl.Buffered(1)),
        pl.BlockSpec(memory_space=pl.ANY),
    ]
    perms = jnp.stack([_class_major_perm(tm, d) for _, d in ATTN_PATTERNS[1:]])
    out_specs, out_shape = [], []
    for t in range(3):
        for window, d in ATTN_PATTERNS:
            out_specs.append(pl.BlockSpec((1, d, N_HEADS, tm // d, HEAD_DIM), lambda b, i: (b, 0, 0, i, 0)))
            out_shape.append(jax.ShapeDtypeStruct((B, d, N_HEADS, L // d, HEAD_DIM), BF16))
    for window, d in ATTN_PATTERNS:
        w_eff = min(window, L)
        if w_eff >= tm:
            n_tail = w_eff // tm
            out_specs.append(pl.BlockSpec(
                (1, tm, 2 * N_HEADS, HEAD_DIM),
                lambda b, i, n_tail=n_tail: (b, jnp.maximum(i - (n_tiles - n_tail), 0), 0, 0)))
        else:
            out_specs.append(pl.BlockSpec((1, w_eff, 2 * N_HEADS, HEAD_DIM), lambda b, i: (b, 0, 0, 0)))
        out_shape.append(jax.ShapeDtypeStruct((B, w_eff, 2 * N_HEADS, HEAD_DIM), F32))
    out_specs.append(pl.BlockSpec(memory_space=pl.ANY))
    out_shape.append(jax.ShapeDtypeStruct(state.shape, F32))
    return pl.pallas_call(
        functools.partial(_qkv_body, n_tiles=n_tiles, shift=shift),
        grid=(B, n_tiles),
        in_specs=in_specs,
        out_specs=out_specs,
        out_shape=out_shape,
        scratch_shapes=_shift_scratch(QKV_SHIFT_LEAD),
        compiler_params=pltpu.CompilerParams(
            dimension_semantics=("arbitrary", "arbitrary"), vmem_limit_bytes=OUT_VMEM_LIMIT),
        name="prompt_qkv",
    )(x, mod_p, norm_g, w_qkv, q_gain, k_gain, perms, new_rows, state)


def _attn_body(q_ref, k_ref, v_ref, o_ref, l_ref, vx):
    _, d_blk, h_blk, M, _ = q_ref.shape
    n_q = M // Q_BLOCK
    for c in range(d_blk):
        for hh in range(h_blk):
            vx[c, hh, :, 0:HEAD_DIM] = v_ref[0, c, hh]
            vx[c, hh, :, HEAD_DIM:2 * HEAD_DIM] = jnp.ones((M, HEAD_DIM), BF16)
    ii = lax.broadcasted_iota(jnp.int32, (Q_BLOCK, 2 * Q_BLOCK), 0)
    jj = lax.broadcasted_iota(jnp.int32, (Q_BLOCK, 2 * Q_BLOCK), 1)
    dist = Q_BLOCK + ii - jj
    band_ok = (dist >= 0) & (dist <= N_BACK)
    causal_ok = (lax.broadcasted_iota(jnp.int32, (Q_BLOCK, Q_BLOCK), 1)
                 <= lax.broadcasted_iota(jnp.int32, (Q_BLOCK, Q_BLOCK), 0))
    nt = (((1,), (1,)), ((), ()))

    def block(c, hh, n, first):
        rows = pl.ds(pl.multiple_of(n * Q_BLOCK, Q_BLOCK), Q_BLOCK)
        if first:
            keys, ok = rows, causal_ok
        else:
            keys, ok = pl.ds(pl.multiple_of((n - 1) * Q_BLOCK, Q_BLOCK), 2 * Q_BLOCK), band_ok
        s = lax.dot_general(q_ref[0, c, hh, rows, :], k_ref[0, c, hh, keys, :], nt, preferred_element_type=F32)
        s = jnp.where(ok, s, NEG)
        m = jnp.max(s, axis=-1, keepdims=True)
        p = jnp.exp(s - m).astype(BF16)
        oe = jnp.dot(p, vx[c, hh, keys, :], preferred_element_type=F32)
        l = oe[:, HEAD_DIM:]
        o_ref[0, c, hh, rows, :] = oe[:, :HEAD_DIM] / l
        l_ref[0, c, hh, rows, :] = m + jnp.log(l)

    for c in range(d_blk):
        for hh in range(h_blk):
            block(c, hh, 0, True)
        if n_q <= 2:
            for n in range(1, n_q):
                for hh in range(h_blk):
                    block(c, hh, n, False)
        else:
            def step(n, carry, c=c):
                for hh in range(h_blk):
                    block(c, hh, n, False)
                return carry

            lax.fori_loop(1, n_q, step, 0, unroll=Q_UNROLL)


def _prompt_attn(q, k, v, d_blk, h_blk):
    B, d, H, M, E = q.shape
    blk = (1, d_blk, h_blk, M, E)
    spec = pl.BlockSpec(blk, lambda b, c, hh: (b, c, hh, 0, 0))
    shape = jax.ShapeDtypeStruct(q.shape, F32)
    return pl.pallas_call(
        _attn_body,
        grid=(B, d // d_blk, H // h_blk),
        in_specs=[spec, spec, spec],
        out_specs=[spec, spec],
        out_shape=[shape, shape],
        scratch_shapes=[pltpu.VMEM((d_blk, h_blk, M, 2 * E), BF16)],
        compiler_params=pltpu.CompilerParams(
            dimension_semantics=("arbitrary", "arbitrary", "arbitrary"), vmem_limit_bytes=VMEM_LIMIT),
        name="prompt_attn_d%d" % d,
    )(q, k, v)


def _pool_project(mixed, wp_ref, bp_ref, ps_ref):
    cols = []
    for g in range(len(POOL_WINDOWS)):
        lanes = slice(g * HEAD_DIM, (g + 1) * HEAD_DIM)
        y = jnp.dot(mixed[g].astype(BF16), wp_ref[g], preferred_element_type=F32) + bp_ref[g:g + 1, :]
        cols.append(y * ps_ref[:, lanes])
    return jnp.concatenate(cols, axis=-1)


def _merge_out(x, gate, a, p, z_a, z_b, g_a, g_b, wa_ref, wb_ref, wo_ref):
    ya = jnp.dot((a * _silu(z_a)).astype(BF16), wa_ref[...], preferred_element_type=F32)
    yb = jnp.dot((p * _silu(z_b)).astype(BF16), wb_ref[...], preferred_element_type=F32)
    m = _sigmoid(g_a) * ya + _sigmoid(g_b) * yb
    return x + gate * jnp.dot(m.astype(BF16), wo_ref[...], preferred_element_type=F32)


def _out_body(x_ref, mod_ref, ng_ref, w_ref, o0, l0, o1, l1, o2, l2, wp_ref, bp_ref, ps_ref,
              wa_ref, wb_ref, wo_ref, new_ref, src_a, src_b, dst_a_in, y_ref, pool_ref, dst_a, dst_b,
              ext, oscr, lscr, buf, rsem, wsem, nsem, *, n_tiles, shift):
    del dst_a_in
    i = pl.program_id(1)
    tm = x_ref.shape[1]
    step = pl.program_id(0) * n_tiles + i
    kinds, lo, hi = shift

    last_slot = lo + SHIFT_SUBSTEPS * pl.num_programs(0) * n_tiles - 1

    def shift_substep(k):
        _shift_substep(lo + SHIFT_SUBSTEPS * step + k, lo, hi, last_slot, kinds, [src_a, src_b], [dst_a, dst_b],
                       buf, rsem, wsem)

    for k in range(SHIFT_SUBSTEPS // 2):
        shift_substep(k)

    @pl.when(step == 0)
    def _():
        _new_rows_copy(new_ref, dst_b, nsem).start()
        ext[0:16, :] = jnp.zeros((16, GROUP_W), F32)

    x = x_ref[0]
    h = _modulated_norm(x, ng_ref[...], mod_ref[0, :, 0:D_MODEL], mod_ref[0, :, D_MODEL:2 * D_MODEL])
    gate = mod_ref[0, :, 2 * D_MODEL:3 * D_MODEL]

    def proj(lo, width):
        return jnp.dot(h, w_ref[:, lo:lo + width], preferred_element_type=F32)

    z_a = proj(0, GROUP_W)
    u = proj(GROUP_W, GROUP_W)
    z_b = proj(2 * GROUP_W, GROUP_W)
    o_refs, l_refs = (o0, o1, o2), (l0, l1, l2)
    for g, (window, d) in enumerate(ATTN_PATTERNS):
        if d == 1:
            continue
        for hh in range(N_HEADS):
            for rr in range(d):
                rows = pl.ds(rr, tm // d, stride=d)
                oscr[g - 1, hh, rows, :] = o_refs[g][0, rr, hh]
                lscr[g - 1, hh, rows, :] = l_refs[g][0, rr, hh]
    a_cols = []
    for hh in range(N_HEADS):
        os_ = [o0[0, 0, hh], oscr[0, hh], oscr[1, hh]]
        ls_ = [l0[0, 0, hh], lscr[0, hh], lscr[1, hh]]
        mx = jnp.maximum(jnp.maximum(ls_[0], ls_[1]), ls_[2])
        es = [jnp.exp(l_ - mx) for l_ in ls_]
        den = es[0] + es[1] + es[2]
        a_cols.append((es[0] * os_[0] + es[1] * os_[1] + es[2] * os_[2]) / den)
    ya_in = (jnp.concatenate(a_cols, axis=-1) * _silu(z_a)).astype(BF16)

    g_a = proj(3 * GROUP_W, D_MODEL)
    ext[16:16 + tm, :] = u
    pos = i * tm + lax.broadcasted_iota(jnp.int32, (tm, 1), 0)
    mixed = []
    for g, w in enumerate(POOL_WINDOWS):
        lanes = pl.ds(g * HEAD_DIM, HEAD_DIM)
        acc = ext[pl.ds(16, tm), lanes]
        tok = acc
        for k in range(1, w):
            acc = acc + ext[pl.ds(16 - k, tm), lanes]
        cnt = jnp.minimum(w, pos + 1).astype(F32)
        mixed.append(acc / cnt - tok)
    pool_ref[0, 0] = ext[pl.ds(tm + 1, POOL_STATE), :]
    ext[0:16, :] = jnp.where(i == n_tiles - 1, 0.0, ext[tm:tm + 16, :])
    yb_in = (_pool_project(mixed, wp_ref, bp_ref, ps_ref) * _silu(z_b)).astype(BF16)
    for k in range(SHIFT_SUBSTEPS // 2, SHIFT_SUBSTEPS):
        shift_substep(k)

    @pl.when(step == 0)
    def _():
        _new_rows_copy(new_ref, dst_b, nsem).wait()

    g_b = proj(3 * GROUP_W + D_MODEL, D_MODEL)
    ya = jnp.dot(ya_in, wa_ref[...], preferred_element_type=F32)
    yb = jnp.dot(yb_in, wb_ref[...], preferred_element_type=F32)
    m = (_sigmoid(g_a) * ya + _sigmoid(g_b) * yb).astype(BF16)
    y_ref[0] = x + gate * jnp.dot(m, wo_ref[...], preferred_element_type=F32)


def _prompt_out(x, mod_p, norm_g, w_rest, ol, w_pool, b_pool, pool_scale, wa, wb, wo,
                new_rows, state_a, state_b, shifted_a, shift):
    B, L, _ = x.shape
    tm = ROW_TILE
    n_tiles = L // tm
    const2 = lambda b, i: (0, 0)
    one = pl.Buffered(1)
    in_specs = [
        pl.BlockSpec((1, tm, D_MODEL), lambda b, i: (b, i, 0)),
        pl.BlockSpec((1, 1, 3 * D_MODEL), lambda b, i: (b, 0, 0)),
        pl.BlockSpec((1, D_MODEL), const2),
        pl.BlockSpec((D_MODEL, REST_W), const2, pipeline_mode=one),
    ]
    for window, d in ATTN_PATTERNS:
        for _ in range(2):
            in_specs.append(pl.BlockSpec((1, d, N_HEADS, tm // d, HEAD_DIM), lambda b, i: (b, 0, 0, i, 0)))
    in_specs += [
        pl.BlockSpec((len(POOL_WINDOWS), HEAD_DIM, HEAD_DIM), lambda b, i: (0, 0, 0)),
        pl.BlockSpec((len(POOL_WINDOWS), HEAD_DIM), const2),
        pl.BlockSpec((1, GROUP_W), const2),
        pl.BlockSpec((GROUP_W, D_MODEL), const2, pipeline_mode=one),
        pl.BlockSpec((GROUP_W, D_MODEL), const2, pipeline_mode=one),
        pl.BlockSpec((D_MODEL, D_MODEL), const2, pipeline_mode=one),
        pl.BlockSpec(new_rows.shape, lambda b, i: (0, 0, 0), pipeline_mode=one),
    ]
    any_spec = pl.BlockSpec(memory_space=pl.ANY)
    in_specs += [any_spec] * 3
    return pl.pallas_call(
        functools.partial(_out_body, n_tiles=n_tiles, shift=shift),
        grid=(B, n_tiles),
        in_specs=in_specs,
        out_specs=[pl.BlockSpec((1, tm, D_MODEL), lambda b, i: (b, i, 0)),
                   pl.BlockSpec((1, 1, POOL_STATE, GROUP_W), lambda b, i: (0, b, 0, 0)),
                   any_spec, any_spec],
        out_shape=[jax.ShapeDtypeStruct((B, L, D_MODEL), F32),
                   jax.ShapeDtypeStruct((1, B, POOL_STATE, GROUP_W), F32),
                   jax.ShapeDtypeStruct(state_a.shape, F32),
                   jax.ShapeDtypeStruct(state_b.shape, F32)],
        scratch_shapes=[pltpu.VMEM((tm + 16, GROUP_W), F32),
                        pltpu.VMEM((2, N_HEADS, tm, HEAD_DIM), F32),
                        pltpu.VMEM((2, N_HEADS, tm, HEAD_DIM), F32)] + _shift_scratch(OUT_SHIFT_LEAD),
        input_output_aliases={len(in_specs) - 1: 2},
        compiler_params=pltpu.CompilerParams(
            dimension_semantics=("arbitrary", "arbitrary"), vmem_limit_bytes=OUT_VMEM_LIMIT),
        name="prompt_out",
    )(x, mod_p, norm_g, w_rest, *ol, w_pool, b_pool, pool_scale, wa, wb, wo,
      new_rows, state_a, state_b, shifted_a)


def _sample_proj_body(x_ref, mod_ref, ng_ref, w_ref, o_ref):
    h = _modulated_norm(x_ref[...], ng_ref[...], mod_ref[:, 0:D_MODEL], mod_ref[:, D_MODEL:2 * D_MODEL])
    o_ref[...] = jnp.dot(h, w_ref[...], preferred_element_type=F32)


def _sample_proj(xs, mod_s, norm_g, w):
    n, width = xs.shape[0], w.shape[1]
    tn = max(t for t in range(HEAD_DIM, width // 2 + 1, HEAD_DIM) if width % t == 0)
    return pl.pallas_call(
        _sample_proj_body,
        grid=(width // tn,),
        in_specs=[pl.BlockSpec((n, D_MODEL), lambda j: (0, 0)),
                  pl.BlockSpec((n, 3 * D_MODEL), lambda j: (0, 0)),
                  pl.BlockSpec((1, D_MODEL), lambda j: (0, 0)),
                  pl.BlockSpec((D_MODEL, tn), lambda j: (0, j))],
        out_specs=pl.BlockSpec((n, tn), lambda j: (0, j)),
        out_shape=jax.ShapeDtypeStruct((n, width), F32),
        name="sample_proj",
    )(xs, mod_s, norm_g, w)


def _sample_attn_body(p_ref, s0, s1, s2, qg_ref, kg_ref, a_ref, shifted0, n1, n2):
    bt = p_ref.shape[0]
    s_refs = (s0, s1, s2)
    last = N_BACK - 1
    for b in range(bt):
        os_, ls_ = [], []
        shifted0[b, 0:last] = s0[b, 1:N_BACK]
        for g in range(N_GROUPS):
            row = lambda t: pl.ds((t * N_GROUPS + g) * N_HEADS, N_HEADS)
            q4 = _head_norm(p_ref[b, row(0), :], qg_ref[g:g + 1, :]) * Q_SCALE
            k4 = _head_norm(p_ref[b, row(1), :], kg_ref[g:g + 1, :])
            v4 = p_ref[b, row(2), :]
            if g == 0:
                shifted0[b, last, 0:N_HEADS, :] = k4
                shifted0[b, last, N_HEADS:2 * N_HEADS, :] = v4
            else:
                (n1, n2)[g - 1][b, 0:N_HEADS, :] = k4
                (n1, n2)[g - 1][b, N_HEADS:2 * N_HEADS, :] = v4
            keys = s_refs[g][b, :, 0:N_HEADS, :]
            vals = s_refs[g][b, :, N_HEADS:2 * N_HEADS, :]
            s_old = jnp.sum(keys * q4[None], axis=-1, keepdims=True)
            s_new = jnp.sum(k4 * q4, axis=-1, keepdims=True)
            m = jnp.maximum(jnp.max(s_old, axis=0), s_new)
            p_old = jnp.exp(s_old - m[None])
            p_new = jnp.exp(s_new - m)
            l = jnp.sum(p_old, axis=0) + p_new
            o = jnp.sum(p_old * vals, axis=0) + p_new * v4
            os_.append(o / l)
            ls_.append(m + jnp.log(l))
        mx = jnp.maximum(jnp.maximum(ls_[0], ls_[1]), ls_[2])
        es = [jnp.exp(l_ - mx) for l_ in ls_]
        a_ref[b] = (es[0] * os_[0] + es[1] * os_[1] + es[2] * os_[2]) / (es[0] + es[1] + es[2])


def _sample_attn(p3, views, q_gain, k_gain):
    n = p3.shape[0]
    bt = 8
    in_specs = [pl.BlockSpec((bt, p3.shape[1], HEAD_DIM), lambda j: (j, 0, 0))]
    for v in views:
        in_specs.append(pl.BlockSpec((bt, N_BACK, 2 * N_HEADS, HEAD_DIM), lambda j: (j, 0, 0, 0)))
    in_specs += [pl.BlockSpec((N_GROUPS, HEAD_DIM), lambda j: (0, 0))] * 2
    new_spec = pl.BlockSpec((bt, 2 * N_HEADS, HEAD_DIM), lambda j: (j, 0, 0))
    new_shape = jax.ShapeDtypeStruct((n, 2 * N_HEADS, HEAD_DIM), F32)
    state_spec = pl.BlockSpec((bt, N_BACK, 2 * N_HEADS, HEAD_DIM), lambda j: (j, 0, 0, 0))
    return pl.pallas_call(
        _sample_attn_body,
        grid=(n // bt,),
        in_specs=in_specs,
        out_specs=[pl.BlockSpec((bt, N_HEADS, HEAD_DIM), lambda j: (j, 0, 0)), state_spec, new_spec, new_spec],
        out_shape=[jax.ShapeDtypeStruct((n, N_HEADS, HEAD_DIM), F32),
                   jax.ShapeDtypeStruct(views[0].shape, F32), new_shape, new_shape],
        compiler_params=pltpu.CompilerParams(vmem_limit_bytes=VMEM_LIMIT),
        name="sample_attn",
    )(p3, *views, q_gain, k_gain)


def _sample_out_body(x_ref, mod_ref, r_ref, a_ref, sp_ref, wp_ref, bp_ref, ps_ref, wa_ref, wb_ref, wo_ref,
                     y_ref, pool_ref):
    u = r_ref[:, GROUP_W:2 * GROUP_W]
    mixed = []
    for g, w in enumerate(POOL_WINDOWS):
        lanes = pl.ds(g * HEAD_DIM, HEAD_DIM)
        tok = r_ref[:, pl.ds(GROUP_W + g * HEAD_DIM, HEAD_DIM)]
        acc = tok
        for k in range(1, w):
            acc = acc + sp_ref[POOL_STATE - k, :, lanes]
        mixed.append(acc / float(min(w, PAST_LEN + 1)) - tok)
    p = _pool_project(mixed, wp_ref, bp_ref, ps_ref)
    for k in range(POOL_STATE - 1):
        pool_ref[k] = sp_ref[k + 1]
    pool_ref[POOL_STATE - 1] = u
    y_ref[...] = _merge_out(
        x_ref[...], mod_ref[:, 2 * D_MODEL:3 * D_MODEL], a_ref[...], p,
        r_ref[:, 0:GROUP_W], r_ref[:, 2 * GROUP_W:3 * GROUP_W],
        r_ref[:, 3 * GROUP_W:3 * GROUP_W + D_MODEL], r_ref[:, 3 * GROUP_W + D_MODEL:REST_W],
        wa_ref, wb_ref, wo_ref)


def _sample_out(xs, mod_s, rest, a_s, sp_t, w_pool, b_pool, pool_scale, wa, wb, wo):
    n = xs.shape[0]
    return pl.pallas_call(
        _sample_out_body,
        out_shape=[jax.ShapeDtypeStruct((n, D_MODEL), F32),
                   jax.ShapeDtypeStruct((POOL_STATE, n, GROUP_W), F32)],
        compiler_params=pltpu.CompilerParams(vmem_limit_bytes=VMEM_LIMIT),
        name="sample_out",
    )(xs, mod_s, rest, a_s, sp_t, w_pool, b_pool, pool_scale, wa, wb, wo)


def kernel(x_prompt, x_sample, state_kv_w128, state_kv_w512, state_kv_w2048, state_pool, c_prompt, c_sample,
           norm_g, w_ada, b_ada, w_in, q_gain, k_gain, w_pool, b_pool, pool_scale, w_a_out, w_b_out, w_out):
    B, L, _ = x_prompt.shape
    n_s = x_sample.shape[0]
    w_qkv = w_in[0, :, :QKV_W].astype(BF16)
    w_rest = w_in[0, :, QKV_W:].astype(BF16)
    wa, wb, wo, wp = (w[0].astype(BF16) for w in (w_a_out, w_b_out, w_out, w_pool))
    qg, kg, bp = q_gain[0], k_gain[0], b_pool[0]

    mod = _ada(jnp.concatenate([c_prompt, c_sample], axis=0), w_ada[0], b_ada)
    mod_p = mod[:B].reshape(B, 1, 3 * D_MODEL)
    mod_s = mod[B:]

    xs = x_sample[:, 0, :]
    qkv_s = _sample_proj(xs, mod_s, norm_g, w_qkv)
    rest_s = _sample_proj(xs, mod_s, norm_g, w_rest)
    states = (state_kv_w128, state_kv_w512, state_kv_w2048)
    views = [s.reshape(n_s, N_BACK, d * ROWS_PER_POS, HEAD_DIM) for s, (_, d) in zip(states, ATTN_PATTERNS)]
    a_s, shifted_128, new_512, new_2048 = _sample_attn(
        qkv_s.reshape(n_s, QKV_W // HEAD_DIM, HEAD_DIM), views, qg, kg)
    y_s, pool_t = _sample_out(xs, mod_s, rest_s, a_s.reshape(n_s, GROUP_W), state_pool[0].transpose(1, 0, 2),
                              wp, bp, pool_scale, wa, wb, wo)

    flat_2048, flat_512 = (s.reshape(n_s, s.shape[2] * ROWS_PER_POS, HEAD_DIM) for s in states[:0:-1])
    kinds = _shift_plan([flat_2048.shape, flat_512.shape])
    n_sub = B * (L // ROW_TILE) * SHIFT_SUBSTEPS
    n_chunks = kinds[-1][-1]
    assert n_sub <= kinds[0][4] and n_chunks <= 2 * n_sub
    *qkv, t0, t1, t2, part_2048 = _prompt_qkv(x_prompt, mod_p, norm_g, w_qkv, qg, kg, new_2048, flat_2048,
                                              (kinds, 0, n_sub))
    ol = []
    for g, (d_blk, h_blk) in enumerate(((1, 2), (2, 4), (8, 4))):
        ol += _prompt_attn(qkv[g], qkv[N_GROUPS + g], qkv[2 * N_GROUPS + g], d_blk, h_blk)
    y_p, pool_p, shifted_2048, shifted_512 = _prompt_out(
        x_prompt, mod_p, norm_g, w_rest, ol, wp, bp, pool_scale, wa, wb, wo,
        new_512, flat_2048, flat_512, part_2048, (kinds, n_sub, n_chunks))
    kv_s = [o.reshape(s.shape) for o, s in zip((shifted_128, shifted_512, shifted_2048), states)]
    kv_p = [t.reshape(1, B, t.shape[1], 2, N_HEADS, HEAD_DIM) for t in (t0, t1, t2)]

    return (y_p, y_s.reshape(n_s, 1, D_MODEL), kv_p[0], kv_p[1], kv_p[2], pool_p,
            kv_s[0], kv_s[1], kv_s[2], pool_t.transpose(1, 0, 2)[None])
```

```python
import functools

import jax
import jax.numpy as jnp
from jax import lax
from jax.experimental import pallas as pl
from jax.experimental.pallas import tpu as pltpu

F32 = jnp.float32
BF16 = jnp.bfloat16

D_MODEL = 1024
HEAD_DIM = 128
N_HEADS = 4
GROUP_W = N_HEADS * HEAD_DIM
ATTN_PATTERNS = ((128, 1), (512, 4), (2048, 16))
N_GROUPS = len(ATTN_PATTERNS)
N_BACK = 128
QKV_W = 3 * N_GROUPS * GROUP_W
REST_W = 3 * GROUP_W + 2 * D_MODEL
POOL_WINDOWS = (2, 4, 8, 16)
POOL_STATE = 15
PAST_LEN = 8192
EPS = 1e-6
Q_SCALE = HEAD_DIM ** -0.5
NEG = -1e30

ROW_TILE = 512
Q_BLOCK = 128
Q_UNROLL = 16
VMEM_LIMIT = 56 * 1024 * 1024
OUT_VMEM_LIMIT = 60 * 1024 * 1024


def _sigmoid(v):
    return 0.5 * jnp.tanh(0.5 * v) + 0.5


def _silu(v):
    return v * _sigmoid(v)


def _modulated_norm(x, norm_g, shift, scale):
    ms = jnp.mean(x * x, axis=-1, keepdims=True)
    return (x * lax.rsqrt(ms + EPS) * norm_g * (1.0 + scale) + shift).astype(BF16)


def _head_norm(r, gain):
    ms = jnp.mean(r * r, axis=-1, keepdims=True)
    return r * lax.rsqrt(ms + EPS) * gain


def _ada_body(c_ref, w_ref, b_ref, o_ref):
    s = _silu(c_ref[...]).astype(BF16)
    o_ref[...] = jnp.dot(s, w_ref[...].astype(BF16), preferred_element_type=F32) + b_ref[...]


def _ada(c_all, w_ada, b_ada):
    n = c_all.shape[0]
    return pl.pallas_call(
        _ada_body,
        grid=(3,),
        in_specs=[pl.BlockSpec((n, D_MODEL), lambda j: (0, 0)),
                  pl.BlockSpec((D_MODEL, D_MODEL), lambda j: (0, j)),
                  pl.BlockSpec((1, D_MODEL), lambda j: (0, j))],
        out_specs=pl.BlockSpec((n, D_MODEL), lambda j: (0, j)),
        out_shape=jax.ShapeDtypeStruct((n, 3 * D_MODEL), F32),
        name="ada",
    )(c_all, w_ada, b_ada)


ROWS_PER_POS = 2 * N_HEADS
SHIFT_NB = 8
SHIFT_ROWS = 712
SHIFT_SUBSTEPS = 4
QKV_SHIFT_LEAD = 3
OUT_SHIFT_LEAD = 2
SHIFT_PRIORITY = 1


def _shift_plan(state_shapes):
    kinds, c0 = [], 0
    for idx, (n_batch, rows_total, _) in enumerate(state_shapes):
        keep = rows_total - ROWS_PER_POS
        pieces = min(p for p in range(1, keep)
                     if keep % (p * ROWS_PER_POS) == 0 and keep // p <= SHIFT_ROWS)
        n = (n_batch // SHIFT_NB) * pieces
        kinds.append((idx, keep // pieces, pieces, c0, c0 + n))
        c0 += n
    return kinds


def _shift_substep(j, lo, hi, last_slot, kinds, srcs, dsts, buf, rsem, wsem):
    ahead = buf.shape[0] // 2

    def copy(jj, kind, write):
        idx, rows, pieces, c0, _ = kind
        local = jj - c0
        batches = pl.ds((local // pieces) * SHIFT_NB, SHIFT_NB)
        first = (local % pieces) * rows
        slot = jj % buf.shape[0]
        stage = buf.at[slot, :, pl.ds(0, rows), :]
        if write:
            return pltpu.make_async_copy(stage, dsts[idx].at[batches, pl.ds(first, rows), :], wsem.at[slot])
        return pltpu.make_async_copy(srcs[idx].at[batches, pl.ds(first + ROWS_PER_POS, rows), :], stage, rsem.at[slot])

    spans = [(kind, max(kind[3], lo), min(kind[4], hi)) for kind in kinds]
    spans = [s for s in spans if s[1] < s[2]]

    def for_chunk(jj, also, fn):
        for kind, first, end in spans:
            cond = (jj >= first) & (jj < end)

            @pl.when(cond if also is None else cond & also)
            def _():
                fn(functools.partial(copy, jj, kind))

    steady = []
    for kind, first, end in spans:
        inner = (j >= first + ahead) & (j < end - ahead)
        steady.append(inner)

        @pl.when(inner)
        def _():
            copy(j, kind, False).wait()
            copy(j, kind, True).start(priority=SHIFT_PRIORITY)
            copy(j - ahead, kind, True).wait()
            copy(j + ahead, kind, False).start(priority=SHIFT_PRIORITY)

    @pl.when(jnp.logical_not(functools.reduce(jnp.logical_or, steady)))
    def _():
        for a in range(ahead):
            for_chunk(j + a, j == lo, lambda cp: cp(False).start(priority=SHIFT_PRIORITY))

        def landed(cp):
            cp(False).wait()
            cp(True).start(priority=SHIFT_PRIORITY)

        for_chunk(j, None, landed)
        for_chunk(j - ahead, None, lambda cp: cp(True).wait())
        for_chunk(j + ahead, None, lambda cp: cp(False).start(priority=SHIFT_PRIORITY))
        for a in range(ahead):
            for_chunk(j - a, j == last_slot, lambda cp: cp(True).wait())


def _new_rows_copy(new_ref, dst, nsem):
    keep = dst.shape[1] - ROWS_PER_POS
    return pltpu.make_async_copy(new_ref, dst.at[:, pl.ds(keep, ROWS_PER_POS), :], nsem)


def _shift_scratch(lead):
    return [pltpu.VMEM((2 * lead, SHIFT_NB, SHIFT_ROWS, HEAD_DIM), F32),
            pltpu.SemaphoreType.DMA((2 * lead,)), pltpu.SemaphoreType.DMA((2 * lead,)),
            pltpu.SemaphoreType.DMA(())]


def _class_major_perm(tm, d):
    i = jnp.arange(tm)
    src_row = (i % (tm // d)) * d + i // (tm // d)
    return (src_row[:, None] == jnp.arange(tm)[None, :]).astype(BF16)


SHIFT_SITES = ((0, 1), (1, 1), (2, 0))


def _qkv_body(x_ref, mod_ref, ng_ref, w_ref, qg_ref, kg_ref, perm_ref, new_ref, src_ref, *refs, n_tiles, shift):
    qkv_refs = (refs[0:3], refs[3:6], refs[6:9])
    tail_refs = refs[9:12]
    dst_ref, buf, rsem, wsem, nsem = refs[12:17]
    i = pl.program_id(1)
    tm = x_ref.shape[1]
    step = pl.program_id(0) * n_tiles + i
    kinds, lo, hi = shift

    last_slot = lo + SHIFT_SUBSTEPS * pl.num_programs(0) * n_tiles - 1

    def shift_substep(k):
        _shift_substep(lo + SHIFT_SUBSTEPS * step + k, lo, hi, last_slot, kinds, [src_ref], [dst_ref],
                       buf, rsem, wsem)

    shift_substep(0)

    @pl.when(step == 0)
    def _():
        _new_rows_copy(new_ref, dst_ref, nsem).start()

    h = _modulated_norm(x_ref[0], ng_ref[...], mod_ref[0, :, 0:D_MODEL], mod_ref[0, :, D_MODEL:2 * D_MODEL])
    h_by_group = [h] + [jnp.dot(perm_ref[g - 1], h, preferred_element_type=F32).astype(BF16)
                        for g in range(1, N_GROUPS)]

    def heads(t, g, lhs):
        c = t * N_GROUPS + g
        res = jnp.dot(lhs, w_ref[:, c * GROUP_W:(c + 1) * GROUP_W], preferred_element_type=F32)
        out = []
        for hh in range(N_HEADS):
            r = res[:, hh * HEAD_DIM:(hh + 1) * HEAD_DIM]
            if t == 0:
                r = _head_norm(r, qg_ref[g:g + 1, :]) * Q_SCALE
            elif t == 1:
                r = _head_norm(r, kg_ref[g:g + 1, :])
            out.append(r)
        return out

    def write_tail(t, g, rs):
        window, d = ATTN_PATTERNS[g]
        per = tm // d
        for hh, r in enumerate(rs):
            row = (t - 1) * N_HEADS + hh
            if d == 1:
                tail_refs[g][0, :, row, :] = r if window >= tm else r[tm - window:, :]
            else:
                assert window >= tm
                for rr in range(d):
                    tail_refs[g][0, pl.ds(rr, per, stride=d), row, :] = r[rr * per:(rr + 1) * per, :]

    kept = {}
    for t in range(3):
        for g, (window, d) in enumerate(ATTN_PATTERNS):
            if (t, g) in SHIFT_SITES:
                shift_substep(SHIFT_SITES.index((t, g)) + 1)
            if (t, g) == SHIFT_SITES[0]:
                @pl.when(step == 0)
                def _():
                    _new_rows_copy(new_ref, dst_ref, nsem).wait()

            rs = heads(t, g, h_by_group[g])
            per = tm // d
            for hh, r in enumerate(rs):
                for rr in range(d):
                    qkv_refs[t][g][0, rr, hh] = r[rr * per:(rr + 1) * per, :].astype(BF16)
            if t > 0:
                kept.setdefault(max(window // tm, 1), []).append((t, g, rs))

    for n_tail, items in sorted(kept.items()):
        @pl.when(i >= n_tiles - n_tail)
        def _():
            for t, g, rs in items:
                write_tail(t, g, rs)


def _prompt_qkv(x, mod_p, norm_g, w_qkv, q_gain, k_gain, new_rows, state, shift):
    B, L, _ = x.shape
    tm = ROW_TILE
    n_tiles = L // tm
    const2 = lambda b, i: (0, 0)
    in_specs = [
        pl.BlockSpec((1, tm, D_MODEL), lambda b, i: (b, i, 0)),
        pl.BlockSpec((1, 1, 3 * D_MODEL), lambda b, i: (b, 0, 0)),
        pl.BlockSpec((1, D_MODEL), const2),
        pl.BlockSpec((D_MODEL, QKV_W), const2, pipeline_mode=pl.Buffered(1)),
        pl.BlockSpec((N_GROUPS, HEAD_DIM), const2),
        pl.BlockSpec((N_GROUPS, HEAD_DIM), const2),
        pl.BlockSpec((N_GROUPS - 1, tm, tm), lambda b, i: (0, 0, 0), pipeline_mode=pl.Buffered(1)),
        pl.BlockSpec(new_rows.shape, lambda b, i: (0, 0, 0), pipeline_mode=pl.Buffered(1)),
        pl.BlockSpec(memory_space=pl.ANY),
    ]
    perms = jnp.stack([_class_major_perm(tm, d) for _, d in ATTN_PATTERNS[1:]])
    out_specs, out_shape = [], []
    for t in range(3):
        for window, d in ATTN_PATTERNS:
            out_specs.append(pl.BlockSpec((1, d, N_HEADS, tm // d, HEAD_DIM), lambda b, i: (b, 0, 0, i, 0)))
            out_shape.append(jax.ShapeDtypeStruct((B, d, N_HEADS, L // d, HEAD_DIM), BF16))
    for window, d in ATTN_PATTERNS:
        w_eff = min(window, L)
        if w_eff >= tm:
            n_tail = w_eff // tm
            out_specs.append(pl.BlockSpec(
                (1, tm, 2 * N_HEADS, HEAD_DIM),
                lambda b, i, n_tail=n_tail: (b, jnp.maximum(i - (n_tiles - n_tail), 0), 0, 0)))
        else:
            out_specs.append(pl.BlockSpec((1, w_eff, 2 * N_HEADS, HEAD_DIM), lambda b, i: (b, 0, 0, 0)))
        out_shape.append(jax.ShapeDtypeStruct((B, w_eff, 2 * N_HEADS, HEAD_DIM), F32))
    out_specs.append(pl.BlockSpec(memory_space=pl.ANY))
    out_shape.append(jax.ShapeDtypeStruct(state.shape, F32))
    return pl.pallas_call(
        functools.partial(_qkv_body, n_tiles=n_tiles, shift=shift),
        grid=(B, n_tiles),
        in_specs=in_specs,
        out_specs=out_specs,
        out_shape=out_shape,
        scratch_shapes=_shift_scratch(QKV_SHIFT_LEAD),
        compiler_params=pltpu.CompilerParams(
            dimension_semantics=("arbitrary", "arbitrary"), vmem_limit_bytes=OUT_VMEM_LIMIT),
        name="prompt_qkv",
    )(x, mod_p, norm_g, w_qkv, q_gain, k_gain, perms, new_rows, state)


def _attn_body(q_ref, k_ref, v_ref, o_ref, l_ref, vx):
    _, d_blk, h_blk, M, _ = q_ref.shape
    n_q = M // Q_BLOCK
    for c in range(d_blk):
        for hh in range(h_blk):
            vx[c, hh, :, 0:HEAD_DIM] = v_ref[0, c, hh]
            vx[c, hh, :, HEAD_DIM:2 * HEAD_DIM] = jnp.ones((M, HEAD_DIM), BF16)
    ii = lax.broadcasted_iota(jnp.int32, (Q_BLOCK, 2 * Q_BLOCK), 0)
    jj = lax.broadcasted_iota(jnp.int32, (Q_BLOCK, 2 * Q_BLOCK), 1)
    dist = Q_BLOCK + ii - jj
    band_ok = (dist >= 0) & (dist <= N_BACK)
    causal_ok = (lax.broadcasted_iota(jnp.int32, (Q_BLOCK, Q_BLOCK), 1)
                 <= lax.broadcasted_iota(jnp.int32, (Q_BLOCK, Q_BLOCK), 0))
    nt = (((1,), (1,)), ((), ()))

    def block(c, hh, n, first):
        rows = pl.ds(pl.multiple_of(n * Q_BLOCK, Q_BLOCK), Q_BLOCK)
        if first:
            keys, ok = rows, causal_ok
        else:
            keys, ok = pl.ds(pl.multiple_of((n - 1) * Q_BLOCK, Q_BLOCK), 2 * Q_BLOCK), band_ok
        s = lax.dot_general(q_ref[0, c, hh, rows, :], k_ref[0, c, hh, keys, :], nt, preferred_element_type=F32)
        s = jnp.where(ok, s, NEG)
        m = jnp.max(s, axis=-1, keepdims=True)
        p = jnp.exp(s - m).astype(BF16)
        oe = jnp.dot(p, vx[c, hh, keys, :], preferred_element_type=F32)
        l = oe[:, HEAD_DIM:]
        o_ref[0, c, hh, rows, :] = (oe[:, :HEAD_DIM] / l).astype(BF16)
        l_ref[0, c, hh, rows, :] = m + jnp.log(l)

    for c in range(d_blk):
        for hh in range(h_blk):
            block(c, hh, 0, True)
        if n_q <= 2:
            for n in range(1, n_q):
                for hh in range(h_blk):
                    block(c, hh, n, False)
        else:
            def step(n, carry, c=c):
                for hh in range(h_blk):
                    block(c, hh, n, False)
                return carry

            lax.fori_loop(1, n_q, step, 0, unroll=Q_UNROLL)


def _prompt_attn(q, k, v, d_blk, h_blk):
    B, d, H, M, E = q.shape
    blk = (1, d_blk, h_blk, M, E)
    spec = pl.BlockSpec(blk, lambda b, c, hh: (b, c, hh, 0, 0))
    return pl.pallas_call(
        _attn_body,
        grid=(B, d // d_blk, H // h_blk),
        in_specs=[spec, spec, spec],
        out_specs=[spec, spec],
        out_shape=[jax.ShapeDtypeStruct(q.shape, BF16), jax.ShapeDtypeStruct(q.shape, F32)],
        scratch_shapes=[pltpu.VMEM((d_blk, h_blk, M, 2 * E), BF16)],
        compiler_params=pltpu.CompilerParams(
            dimension_semantics=("arbitrary", "arbitrary", "arbitrary"), vmem_limit_bytes=VMEM_LIMIT),
        name="prompt_attn_d%d" % d,
    )(q, k, v)


def _pool_project(mixed, wp_ref, bp_ref, ps_ref):
    cols = []
    for g in range(len(POOL_WINDOWS)):
        lanes = slice(g * HEAD_DIM, (g + 1) * HEAD_DIM)
        y = jnp.dot(mixed[g].astype(BF16), wp_ref[g], preferred_element_type=F32) + bp_ref[g:g + 1, :]
        cols.append(y * ps_ref[:, lanes])
    return jnp.concatenate(cols, axis=-1)


def _merge_out(x, gate, a, p, z_a, z_b, g_a, g_b, wa_ref, wb_ref, wo_ref):
    ya = jnp.dot((a * _silu(z_a)).astype(BF16), wa_ref[...], preferred_element_type=F32)
    yb = jnp.dot((p * _silu(z_b)).astype(BF16), wb_ref[...], preferred_element_type=F32)
    m = _sigmoid(g_a) * ya + _sigmoid(g_b) * yb
    return x + gate * jnp.dot(m.astype(BF16), wo_ref[...], preferred_element_type=F32)


def _out_body(x_ref, mod_ref, ng_ref, w_ref, o0, l0, o1, l1, o2, l2, wp_ref, bp_ref, ps_ref,
              wa_ref, wb_ref, wo_ref, new_ref, src_a, src_b, dst_a_in, y_ref, pool_ref, dst_a, dst_b,
              ext, oscr, lscr, buf, rsem, wsem, nsem, *, n_tiles, shift):
    del dst_a_in
    i = pl.program_id(1)
    tm = x_ref.shape[1]
    step = pl.program_id(0) * n_tiles + i
    kinds, lo, hi = shift

    last_slot = lo + SHIFT_SUBSTEPS * pl.num_programs(0) * n_tiles - 1

    def shift_substep(k):
        _shift_substep(lo + SHIFT_SUBSTEPS * step + k, lo, hi, last_slot, kinds, [src_a, src_b], [dst_a, dst_b],
                       buf, rsem, wsem)

    shift_substep(0)

    @pl.when(step == 0)
    def _():
        _new_rows_copy(new_ref, dst_b, nsem).start()
        ext[0:16, :] = jnp.zeros((16, GROUP_W), F32)

    x = x_ref[0]
    h = _modulated_norm(x, ng_ref[...], mod_ref[0, :, 0:D_MODEL], mod_ref[0, :, D_MODEL:2 * D_MODEL])
    gate = mod_ref[0, :, 2 * D_MODEL:3 * D_MODEL]

    def proj(lo, width):
        return jnp.dot(h, w_ref[:, lo:lo + width], preferred_element_type=F32)

    z_a = proj(0, GROUP_W)
    u = proj(GROUP_W, GROUP_W)
    z_b = proj(2 * GROUP_W, GROUP_W)
    o_refs, l_refs = (o0, o1, o2), (l0, l1, l2)
    for g, (window, d) in enumerate(ATTN_PATTERNS):
        if d == 1:
            continue
        for hh in range(N_HEADS):
            for rr in range(d):
                rows = pl.ds(rr, tm // d, stride=d)
                oscr[g - 1, hh, rows, :] = o_refs[g][0, rr, hh].astype(F32)
                lscr[g - 1, hh, rows, :] = l_refs[g][0, rr, hh]
    a_cols = []
    for hh in range(N_HEADS):
        os_ = [o0[0, 0, hh].astype(F32), oscr[0, hh], oscr[1, hh]]
        ls_ = [l0[0, 0, hh], lscr[0, hh], lscr[1, hh]]
        mx = jnp.maximum(jnp.maximum(ls_[0], ls_[1]), ls_[2])
        es = [jnp.exp(l_ - mx) for l_ in ls_]
        den = es[0] + es[1] + es[2]
        a_cols.append((es[0] * os_[0] + es[1] * os_[1] + es[2] * os_[2]) / den)
    ya_in = (jnp.concatenate(a_cols, axis=-1) * _silu(z_a)).astype(BF16)
    shift_substep(1)

    @pl.when(step == 0)
    def _():
        _new_rows_copy(new_ref, dst_b, nsem).wait()

    g_a = proj(3 * GROUP_W, D_MODEL)
    ext[16:16 + tm, :] = u
    pos = i * tm + lax.broadcasted_iota(jnp.int32, (tm, 1), 0)
    mixed = []
    for g, w in enumerate(POOL_WINDOWS):
        lanes = pl.ds(g * HEAD_DIM, HEAD_DIM)
        acc = ext[pl.ds(16, tm), lanes]
        tok = acc
        for k in range(1, w):
            acc = acc + ext[pl.ds(16 - k, tm), lanes]
        cnt = jnp.minimum(w, pos + 1).astype(F32)
        mixed.append(acc / cnt - tok)
    pool_ref[0, 0] = ext[pl.ds(tm + 1, POOL_STATE), :]
    ext[0:16, :] = jnp.where(i == n_tiles - 1, 0.0, ext[tm:tm + 16, :])
    yb_in = (_pool_project(mixed, wp_ref, bp_ref, ps_ref) * _silu(z_b)).astype(BF16)
    shift_substep(2)

    g_b = proj(3 * GROUP_W + D_MODEL, D_MODEL)
    ya = jnp.dot(ya_in, wa_ref[...], preferred_element_type=F32)
    yb = jnp.dot(yb_in, wb_ref[...], preferred_element_type=F32)
    m = (_sigmoid(g_a) * ya + _sigmoid(g_b) * yb).astype(BF16)
    shift_substep(3)
    y_ref[0] = x + gate * jnp.dot(m, wo_ref[...], preferred_element_type=F32)


def _prompt_out(x, mod_p, norm_g, w_rest, ol, w_pool, b_pool, pool_scale, wa, wb, wo,
                new_rows, state_a, state_b, shifted_a, shift):
    B, L, _ = x.shape
    tm = ROW_TILE
    n_tiles = L // tm
    const2 = lambda b, i: (0, 0)
    one = pl.Buffered(1)
    in_specs = [
        pl.BlockSpec((1, tm, D_MODEL), lambda b, i: (b, i, 0)),
        pl.BlockSpec((1, 1, 3 * D_MODEL), lambda b, i: (b, 0, 0)),
        pl.BlockSpec((1, D_MODEL), const2),
        pl.BlockSpec((D_MODEL, REST_W), const2, pipeline_mode=one),
    ]
    for window, d in ATTN_PATTERNS:
        for _ in range(2):
            in_specs.append(pl.BlockSpec((1, d, N_HEADS, tm // d, HEAD_DIM), lambda b, i: (b, 0, 0, i, 0)))
    in_specs += [
        pl.BlockSpec((len(POOL_WINDOWS), HEAD_DIM, HEAD_DIM), lambda b, i: (0, 0, 0)),
        pl.BlockSpec((len(POOL_WINDOWS), HEAD_DIM), const2),
        pl.BlockSpec((1, GROUP_W), const2),
        pl.BlockSpec((GROUP_W, D_MODEL), const2, pipeline_mode=one),
        pl.BlockSpec((GROUP_W, D_MODEL), const2, pipeline_mode=one),
        pl.BlockSpec((D_MODEL, D_MODEL), const2, pipeline_mode=one),
        pl.BlockSpec(new_rows.shape, lambda b, i: (0, 0, 0), pipeline_mode=one),
    ]
    any_spec = pl.BlockSpec(memory_space=pl.ANY)
    in_specs += [any_spec] * 3
    return pl.pallas_call(
        functools.partial(_out_body, n_tiles=n_tiles, shift=shift),
        grid=(B, n_tiles),
        in_specs=in_specs,
        out_specs=[pl.BlockSpec((1, tm, D_MODEL), lambda b, i: (b, i, 0)),
                   pl.BlockSpec((1, 1, POOL_STATE, GROUP_W), lambda b, i: (0, b, 0, 0)),
                   any_spec, any_spec],
        out_shape=[jax.ShapeDtypeStruct((B, L, D_MODEL), F32),
                   jax.ShapeDtypeStruct((1, B, POOL_STATE, GROUP_W), F32),
                   jax.ShapeDtypeStruct(state_a.shape, F32),
                   jax.ShapeDtypeStruct(state_b.shape, F32)],
        scratch_shapes=[pltpu.VMEM((tm + 16, GROUP_W), F32),
                        pltpu.VMEM((2, N_HEADS, tm, HEAD_DIM), F32),
                        pltpu.VMEM((2, N_HEADS, tm, HEAD_DIM), F32)] + _shift_scratch(OUT_SHIFT_LEAD),
        input_output_aliases={len(in_specs) - 1: 2},
        compiler_params=pltpu.CompilerParams(
            dimension_semantics=("arbitrary", "arbitrary"), vmem_limit_bytes=OUT_VMEM_LIMIT),
        name="prompt_out",
    )(x, mod_p, norm_g, w_rest, *ol, w_pool, b_pool, pool_scale, wa, wb, wo,
      new_rows, state_a, state_b, shifted_a)


def _sample_proj_body(x_ref, mod_ref, ng_ref, w_ref, o_ref):
    h = _modulated_norm(x_ref[...], ng_ref[...], mod_ref[:, 0:D_MODEL], mod_ref[:, D_MODEL:2 * D_MODEL])
    o_ref[...] = jnp.dot(h, w_ref[...], preferred_element_type=F32)


def _sample_proj(xs, mod_s, norm_g, w):
    n, width = xs.shape[0], w.shape[1]
    tn = max(t for t in range(HEAD_DIM, width // 2 + 1, HEAD_DIM) if width % t == 0)
    return pl.pallas_call(
        _sample_proj_body,
        grid=(width // tn,),
        in_specs=[pl.BlockSpec((n, D_MODEL), lambda j: (0, 0)),
                  pl.BlockSpec((n, 3 * D_MODEL), lambda j: (0, 0)),
                  pl.BlockSpec((1, D_MODEL), lambda j: (0, 0)),
                  pl.BlockSpec((D_MODEL, tn), lambda j: (0, j))],
        out_specs=pl.BlockSpec((n, tn), lambda j: (0, j)),
        out_shape=jax.ShapeDtypeStruct((n, width), F32),
        name="sample_proj",
    )(xs, mod_s, norm_g, w)


def _sample_attn_body(p_ref, s0, s1, s2, qg_ref, kg_ref, a_ref, shifted0, n1, n2):
    bt = p_ref.shape[0]
    s_refs = (s0, s1, s2)
    last = N_BACK - 1
    for b in range(bt):
        os_, ls_ = [], []
        shifted0[b, 0:last] = s0[b, 1:N_BACK]
        for g in range(N_GROUPS):
            row = lambda t: pl.ds((t * N_GROUPS + g) * N_HEADS, N_HEADS)
            q4 = _head_norm(p_ref[b, row(0), :], qg_ref[g:g + 1, :]) * Q_SCALE
            k4 = _head_norm(p_ref[b, row(1), :], kg_ref[g:g + 1, :])
            v4 = p_ref[b, row(2), :]
            if g == 0:
                shifted0[b, last, 0:N_HEADS, :] = k4
                shifted0[b, last, N_HEADS:2 * N_HEADS, :] = v4
            else:
                (n1, n2)[g - 1][b, 0:N_HEADS, :] = k4
                (n1, n2)[g - 1][b, N_HEADS:2 * N_HEADS, :] = v4
            keys = s_refs[g][b, :, 0:N_HEADS, :]
            vals = s_refs[g][b, :, N_HEADS:2 * N_HEADS, :]
            s_old = jnp.sum(keys * q4[None], axis=-1, keepdims=True)
            s_new = jnp.sum(k4 * q4, axis=-1, keepdims=True)
            m = jnp.maximum(jnp.max(s_old, axis=0), s_new)
            p_old = jnp.exp(s_old - m[None])
            p_new = jnp.exp(s_new - m)
            l = jnp.sum(p_old, axis=0) + p_new
            o = jnp.sum(p_old * vals, axis=0) + p_new * v4
            os_.append(o / l)
            ls_.append(m + jnp.log(l))
        mx = jnp.maximum(jnp.maximum(ls_[0], ls_[1]), ls_[2])
        es = [jnp.exp(l_ - mx) for l_ in ls_]
        a_ref[b] = (es[0] * os_[0] + es[1] * os_[1] + es[2] * os_[2]) / (es[0] + es[1] + es[2])


def _sample_attn(p3, views, q_gain, k_gain):
    n = p3.shape[0]
    bt = 8
    in_specs = [pl.BlockSpec((bt, p3.shape[1], HEAD_DIM), lambda j: (j, 0, 0))]
    for v in views:
        in_specs.append(pl.BlockSpec((bt, N_BACK, 2 * N_HEADS, HEAD_DIM), lambda j: (j, 0, 0, 0)))
    in_specs += [pl.BlockSpec((N_GROUPS, HEAD_DIM), lambda j: (0, 0))] * 2
    new_spec = pl.BlockSpec((bt, 2 * N_HEADS, HEAD_DIM), lambda j: (j, 0, 0))
    new_shape = jax.ShapeDtypeStruct((n, 2 * N_HEADS, HEAD_DIM), F32)
    state_spec = pl.BlockSpec((bt, N_BACK, 2 * N_HEADS, HEAD_DIM), lambda j: (j, 0, 0, 0))
    return pl.pallas_call(
        _sample_attn_body,
        grid=(n // bt,),
        in_specs=in_specs,
        out_specs=[pl.BlockSpec((bt, N_HEADS, HEAD_DIM), lambda j: (j, 0, 0)), state_spec, new_spec, new_spec],
        out_shape=[jax.ShapeDtypeStruct((n, N_HEADS, HEAD_DIM), F32),
                   jax.ShapeDtypeStruct(views[0].shape, F32), new_shape, new_shape],
        compiler_params=pltpu.CompilerParams(vmem_limit_bytes=VMEM_LIMIT),
        name="sample_attn",
    )(p3, *views, q_gain, k_gain)


def _sample_out_body(x_ref, mod_ref, r_ref, a_ref, sp_ref, wp_ref, bp_ref, ps_ref, wa_ref, wb_ref, wo_ref,
                     y_ref, pool_ref):
    u = r_ref[:, GROUP_W:2 * GROUP_W]
    mixed = []
    for g, w in enumerate(POOL_WINDOWS):
        lanes = pl.ds(g * HEAD_DIM, HEAD_DIM)
        tok = r_ref[:, pl.ds(GROUP_W + g * HEAD_DIM, HEAD_DIM)]
        acc = tok
        for k in range(1, w):
            acc = acc + sp_ref[POOL_STATE - k, :, lanes]
        mixed.append(acc / float(min(w, PAST_LEN + 1)) - tok)
    p = _pool_project(mixed, wp_ref, bp_ref, ps_ref)
    for k in range(POOL_STATE - 1):
        pool_ref[k] = sp_ref[k + 1]
    pool_ref[POOL_STATE - 1] = u
    y_ref[...] = _merge_out(
        x_ref[...], mod_ref[:, 2 * D_MODEL:3 * D_MODEL], a_ref[...], p,
        r_ref[:, 0:GROUP_W], r_ref[:, 2 * GROUP_W:3 * GROUP_W],
        r_ref[:, 3 * GROUP_W:3 * GROUP_W + D_MODEL], r_ref[:, 3 * GROUP_W + D_MODEL:REST_W],
        wa_ref, wb_ref, wo_ref)


def _sample_out(xs, mod_s, rest, a_s, sp_t, w_pool, b_pool, pool_scale, wa, wb, wo):
    n = xs.shape[0]
    return pl.pallas_call(
        _sample_out_body,
        out_shape=[jax.ShapeDtypeStruct((n, D_MODEL), F32),
                   jax.ShapeDtypeStruct((POOL_STATE, n, GROUP_W), F32)],
        compiler_params=pltpu.CompilerParams(vmem_limit_bytes=VMEM_LIMIT),
        name="sample_out",
    )(xs, mod_s, rest, a_s, sp_t, w_pool, b_pool, pool_scale, wa, wb, wo)


def kernel(x_prompt, x_sample, state_kv_w128, state_kv_w512, state_kv_w2048, state_pool, c_prompt, c_sample,
           norm_g, w_ada, b_ada, w_in, q_gain, k_gain, w_pool, b_pool, pool_scale, w_a_out, w_b_out, w_out):
    B, L, _ = x_prompt.shape
    n_s = x_sample.shape[0]
    w_qkv = w_in[0, :, :QKV_W].astype(BF16)
    w_rest = w_in[0, :, QKV_W:].astype(BF16)
    wa, wb, wo, wp = (w[0].astype(BF16) for w in (w_a_out, w_b_out, w_out, w_pool))
    qg, kg, bp = q_gain[0], k_gain[0], b_pool[0]

    mod = _ada(jnp.concatenate([c_prompt, c_sample], axis=0), w_ada[0], b_ada)
    mod_p = mod[:B].reshape(B, 1, 3 * D_MODEL)
    mod_s = mod[B:]

    xs = x_sample[:, 0, :]
    qkv_s = _sample_proj(xs, mod_s, norm_g, w_qkv)
    rest_s = _sample_proj(xs, mod_s, norm_g, w_rest)
    states = (state_kv_w128, state_kv_w512, state_kv_w2048)
    views = [s.reshape(n_s, N_BACK, d * ROWS_PER_POS, HEAD_DIM) for s, (_, d) in zip(states, ATTN_PATTERNS)]
    a_s, shifted_128, new_512, new_2048 = _sample_attn(
        qkv_s.reshape(n_s, QKV_W // HEAD_DIM, HEAD_DIM), views, qg, kg)
    y_s, pool_t = _sample_out(xs, mod_s, rest_s, a_s.reshape(n_s, GROUP_W), state_pool[0].transpose(1, 0, 2),
                              wp, bp, pool_scale, wa, wb, wo)

    flat_2048, flat_512 = (s.reshape(n_s, s.shape[2] * ROWS_PER_POS, HEAD_DIM) for s in states[:0:-1])
    kinds = _shift_plan([flat_2048.shape, flat_512.shape])
    n_sub = B * (L // ROW_TILE) * SHIFT_SUBSTEPS
    n_chunks = kinds[-1][-1]
    assert n_sub <= kinds[0][4] and n_chunks <= 2 * n_sub
    *qkv, t0, t1, t2, part_2048 = _prompt_qkv(x_prompt, mod_p, norm_g, w_qkv, qg, kg, new_2048, flat_2048,
                                              (kinds, 0, n_sub))
    ol = []
    for g, (d_blk, h_blk) in enumerate(((1, 2), (2, 4), (8, 4))):
        ol += _prompt_attn(qkv[g], qkv[N_GROUPS + g], qkv[2 * N_GROUPS + g], d_blk, h_blk)
    y_p, pool_p, shifted_2048, shifted_512 = _prompt_out(
        x_prompt, mod_p, norm_g, w_rest, ol, wp, bp, pool_scale, wa, wb, wo,
        new_512, flat_2048, flat_512, part_2048, (kinds, n_sub, n_chunks))
    kv_s = [o.reshape(s.shape) for o, s in zip((shifted_128, shifted_512, shifted_2048), states)]
    kv_p = [t.reshape(1, B, t.shape[1], 2, N_HEADS, HEAD_DIM) for t in (t0, t1, t2)]

    return (y_p, y_s.reshape(n_s, 1, D_MODEL), kv_p[0], kv_p[1], kv_p[2], pool_p,
            kv_s[0], kv_s[1], kv_s[2], pool_t.transpose(1, 0, 2)[None])
```

```python
import functools

import jax
import jax.numpy as jnp
from jax import lax
from jax.experimental import pallas as pl
from jax.experimental.pallas import tpu as pltpu

F32 = jnp.float32
BF16 = jnp.bfloat16

D_MODEL = 1024
HEAD_DIM = 128
N_HEADS = 4
GROUP_W = N_HEADS * HEAD_DIM
ATTN_PATTERNS = ((128, 1), (512, 4), (2048, 16))
N_GROUPS = len(ATTN_PATTERNS)
N_BACK = 128
QKV_W = 3 * N_GROUPS * GROUP_W
REST_W = 3 * GROUP_W + 2 * D_MODEL
POOL_WINDOWS = (2, 4, 8, 16)
POOL_STATE = 15
PAST_LEN = 8192
EPS = 1e-6
Q_SCALE = HEAD_DIM ** -0.5
NEG = -1e30

ROW_TILE = 512
Q_BLOCK = 128
Q_UNROLL = 16
VMEM_LIMIT = 56 * 1024 * 1024
OUT_VMEM_LIMIT = 60 * 1024 * 1024


def _sigmoid(v):
    return 0.5 * jnp.tanh(0.5 * v) + 0.5


def _silu(v):
    return v * _sigmoid(v)


def _modulated_norm(x, norm_g, shift, scale):
    ms = jnp.mean(x * x, axis=-1, keepdims=True)
    return (x * lax.rsqrt(ms + EPS) * norm_g * (1.0 + scale) + shift).astype(BF16)


def _head_norm(r, gain):
    ms = jnp.mean(r * r, axis=-1, keepdims=True)
    return r * lax.rsqrt(ms + EPS) * gain


def _ada_body(c_ref, w_ref, b_ref, o_ref):
    s = _silu(c_ref[...]).astype(BF16)
    o_ref[...] = jnp.dot(s, w_ref[...].astype(BF16), preferred_element_type=F32) + b_ref[...]


def _ada(c_all, w_ada, b_ada):
    n = c_all.shape[0]
    return pl.pallas_call(
        _ada_body,
        grid=(3,),
        in_specs=[pl.BlockSpec((n, D_MODEL), lambda j: (0, 0)),
                  pl.BlockSpec((D_MODEL, D_MODEL), lambda j: (0, j)),
                  pl.BlockSpec((1, D_MODEL), lambda j: (0, j))],
        out_specs=pl.BlockSpec((n, D_MODEL), lambda j: (0, j)),
        out_shape=jax.ShapeDtypeStruct((n, 3 * D_MODEL), F32),
        name="ada",
    )(c_all, w_ada, b_ada)


ROWS_PER_POS = 2 * N_HEADS
SHIFT_NB = 8
SHIFT_ROWS = 712
SHIFT_SUBSTEPS = 4
QKV_SHIFT_LEAD = 3
OUT_SHIFT_LEAD = 3
SHIFT_PRIORITY = 1


def _shift_plan(state_shapes):
    kinds, c0 = [], 0
    for idx, (n_batch, rows_total, _) in enumerate(state_shapes):
        keep = rows_total - ROWS_PER_POS
        pieces = min(p for p in range(1, keep)
                     if keep % (p * ROWS_PER_POS) == 0 and keep // p <= SHIFT_ROWS)
        n = (n_batch // SHIFT_NB) * pieces
        kinds.append((idx, keep // pieces, pieces, c0, c0 + n))
        c0 += n
    return kinds


def _shift_substep(j, lo, hi, last_slot, kinds, srcs, dsts, buf, rsem, wsem):
    ahead = buf.shape[0] // 2

    def copy(jj, kind, write):
        idx, rows, pieces, c0, _ = kind
        local = jj - c0
        batches = pl.ds((local // pieces) * SHIFT_NB, SHIFT_NB)
        first = (local % pieces) * rows
        slot = jj % buf.shape[0]
        stage = buf.at[slot, :, pl.ds(0, rows), :]
        if write:
            return pltpu.make_async_copy(stage, dsts[idx].at[batches, pl.ds(first, rows), :], wsem.at[slot])
        return pltpu.make_async_copy(srcs[idx].at[batches, pl.ds(first + ROWS_PER_POS, rows), :], stage, rsem.at[slot])

    spans = [(kind, max(kind[3], lo), min(kind[4], hi)) for kind in kinds]
    spans = [s for s in spans if s[1] < s[2]]

    def for_chunk(jj, also, fn):
        for kind, first, end in spans:
            cond = (jj >= first) & (jj < end)

            @pl.when(cond if also is None else cond & also)
            def _():
                fn(functools.partial(copy, jj, kind))

    steady = []
    for kind, first, end in spans:
        inner = (j >= first + ahead) & (j < end - ahead)
        steady.append(inner)

        @pl.when(inner)
        def _():
            copy(j, kind, False).wait()
            copy(j, kind, True).start(priority=SHIFT_PRIORITY)
            copy(j - ahead, kind, True).wait()
            copy(j + ahead, kind, False).start(priority=SHIFT_PRIORITY)

    @pl.when(jnp.logical_not(functools.reduce(jnp.logical_or, steady)))
    def _():
        for a in range(ahead):
            for_chunk(j + a, j == lo, lambda cp: cp(False).start(priority=SHIFT_PRIORITY))

        def landed(cp):
            cp(False).wait()
            cp(True).start(priority=SHIFT_PRIORITY)

        for_chunk(j, None, landed)
        for_chunk(j - ahead, None, lambda cp: cp(True).wait())
        for_chunk(j + ahead, None, lambda cp: cp(False).start(priority=SHIFT_PRIORITY))
        for a in range(ahead):
            for_chunk(j - a, j == last_slot, lambda cp: cp(True).wait())


def _new_rows_copy(new_ref, dst, nsem):
    keep = dst.shape[1] - ROWS_PER_POS
    return pltpu.make_async_copy(new_ref, dst.at[:, pl.ds(keep, ROWS_PER_POS), :], nsem)


def _shift_scratch(lead):
    return [pltpu.VMEM((2 * lead, SHIFT_NB, SHIFT_ROWS, HEAD_DIM), F32),
            pltpu.SemaphoreType.DMA((2 * lead,)), pltpu.SemaphoreType.DMA((2 * lead,)),
            pltpu.SemaphoreType.DMA(())]


def _class_major_perm(tm, d):
    i = jnp.arange(tm)
    src_row = (i % (tm // d)) * d + i // (tm // d)
    return (src_row[:, None] == jnp.arange(tm)[None, :]).astype(BF16)


SHIFT_SITES = ((0, 1), (1, 1), (2, 0))


def _qkv_body(x_ref, mod_ref, ng_ref, w_ref, qg_ref, kg_ref, perm_ref, new_ref, src_ref, *refs, n_tiles, shift):
    qkv_refs = (refs[0:3], refs[3:6], refs[6:9])
    tail_refs = refs[9:12]
    dst_ref, buf, rsem, wsem, nsem = refs[12:17]
    i = pl.program_id(1)
    tm = x_ref.shape[1]
    step = pl.program_id(0) * n_tiles + i
    kinds, lo, hi = shift

    last_slot = lo + SHIFT_SUBSTEPS * pl.num_programs(0) * n_tiles - 1

    def shift_substep(k):
        _shift_substep(lo + SHIFT_SUBSTEPS * step + k, lo, hi, last_slot, kinds, [src_ref], [dst_ref],
                       buf, rsem, wsem)

    shift_substep(0)

    @pl.when(step == 0)
    def _():
        _new_rows_copy(new_ref, dst_ref, nsem).start()

    h = _modulated_norm(x_ref[0], ng_ref[...], mod_ref[0, :, 0:D_MODEL], mod_ref[0, :, D_MODEL:2 * D_MODEL])
    h_by_group = [h] + [jnp.dot(perm_ref[g - 1], h, preferred_element_type=F32).astype(BF16)
                        for g in range(1, N_GROUPS)]

    def heads(t, g, lhs):
        c = t * N_GROUPS + g
        res = jnp.dot(lhs, w_ref[:, c * GROUP_W:(c + 1) * GROUP_W], preferred_element_type=F32)
        out = []
        for hh in range(N_HEADS):
            r = res[:, hh * HEAD_DIM:(hh + 1) * HEAD_DIM]
            if t == 0:
                r = _head_norm(r, qg_ref[g:g + 1, :]) * Q_SCALE
            elif t == 1:
                r = _head_norm(r, kg_ref[g:g + 1, :])
            out.append(r)
        return out

    def write_tail(t, g, rs):
        window, d = ATTN_PATTERNS[g]
        per = tm // d
        for hh, r in enumerate(rs):
            row = (t - 1) * N_HEADS + hh
            if d == 1:
                tail_refs[g][0, :, row, :] = r if window >= tm else r[tm - window:, :]
            else:
                assert window >= tm
                for rr in range(d):
                    tail_refs[g][0, pl.ds(rr, per, stride=d), row, :] = r[rr * per:(rr + 1) * per, :]

    kept = {}
    for t in range(3):
        for g, (window, d) in enumerate(ATTN_PATTERNS):
            if (t, g) in SHIFT_SITES:
                shift_substep(SHIFT_SITES.index((t, g)) + 1)
            if (t, g) == SHIFT_SITES[0]:
                @pl.when(step == 0)
                def _():
                    _new_rows_copy(new_ref, dst_ref, nsem).wait()

            rs = heads(t, g, h_by_group[g])
            per = tm // d
            for hh, r in enumerate(rs):
                for rr in range(d):
                    qkv_refs[t][g][0, rr, hh] = r[rr * per:(rr + 1) * per, :].astype(BF16)
            if t > 0:
                kept.setdefault(max(window // tm, 1), []).append((t, g, rs))

    for n_tail, items in sorted(kept.items()):
        @pl.when(i >= n_tiles - n_tail)
        def _():
            for t, g, rs in items:
                write_tail(t, g, rs)


def _prompt_qkv(x, mod_p, norm_g, w_qkv, q_gain, k_gain, new_rows, state, shift):
    B, L, _ = x.shape
    tm = ROW_TILE
    n_tiles = L // tm
    const2 = lambda b, i: (0, 0)
    in_specs = [
        pl.BlockSpec((1, tm, D_MODEL), lambda b, i: (b, i, 0)),
        pl.BlockSpec((1, 1, 3 * D_MODEL), lambda b, i: (b, 0, 0)),
        pl.BlockSpec((1, D_MODEL), const2),
        pl.BlockSpec((D_MODEL, QKV_W), const2, pipeline_mode=pl.Buffered(1)),
        pl.BlockSpec((N_GROUPS, HEAD_DIM), const2),
        pl.BlockSpec((N_GROUPS, HEAD_DIM), const2),
        pl.BlockSpec((N_GROUPS - 1, tm, tm), lambda b, i: (0, 0, 0), pipeline_mode=pl.Buffered(1)),
        pl.BlockSpec(new_rows.shape, lambda b, i: (0, 0, 0), pipeline_mode=pl.Buffered(1)),
        pl.BlockSpec(memory_space=pl.ANY),
    ]
    perms = jnp.stack([_class_major_perm(tm, d) for _, d in ATTN_PATTERNS[1:]])
    out_specs, out_shape = [], []
    for t in range(3):
        for window, d in ATTN_PATTERNS:
            out_specs.append(pl.BlockSpec((1, d, N_HEADS, tm // d, HEAD_DIM), lambda b, i: (b, 0, 0, i, 0)))
            out_shape.append(jax.ShapeDtypeStruct((B, d, N_HEADS, L // d, HEAD_DIM), BF16))
    for window, d in ATTN_PATTERNS:
        w_eff = min(window, L)
        if w_eff >= tm:
            n_tail = w_eff // tm
            out_specs.append(pl.BlockSpec(
                (1, tm, 2 * N_HEADS, HEAD_DIM),
                lambda b, i, n_tail=n_tail: (b, jnp.maximum(i - (n_tiles - n_tail), 0), 0, 0)))
        else:
            out_specs.append(pl.BlockSpec((1, w_eff, 2 * N_HEADS, HEAD_DIM), lambda b, i: (b, 0, 0, 0)))
        out_shape.append(jax.ShapeDtypeStruct((B, w_eff, 2 * N_HEADS, HEAD_DIM), F32))
    out_specs.append(pl.BlockSpec(memory_space=pl.ANY))
    out_shape.append(jax.ShapeDtypeStruct(state.shape, F32))
    return pl.pallas_call(
        functools.partial(_qkv_body, n_tiles=n_tiles, shift=shift),
        grid=(B, n_tiles),
        in_specs=in_specs,
        out_specs=out_specs,
        out_shape=out_shape,
        scratch_shapes=_shift_scratch(QKV_SHIFT_LEAD),
        compiler_params=pltpu.CompilerParams(
            dimension_semantics=("arbitrary", "arbitrary"), vmem_limit_bytes=OUT_VMEM_LIMIT),
        name="prompt_qkv",
    )(x, mod_p, norm_g, w_qkv, q_gain, k_gain, perms, new_rows, state)


def _attn_body(q_ref, k_ref, v_ref, o_ref, l_ref, vx):
    _, d_blk, h_blk, M, _ = q_ref.shape
    n_q = M // Q_BLOCK
    for c in range(d_blk):
        for hh in range(h_blk):
            vx[c, hh, :, 0:HEAD_DIM] = v_ref[0, c, hh]
            vx[c, hh, :, HEAD_DIM:2 * HEAD_DIM] = jnp.ones((M, HEAD_DIM), BF16)
    ii = lax.broadcasted_iota(jnp.int32, (Q_BLOCK, 2 * Q_BLOCK), 0)
    jj = lax.broadcasted_iota(jnp.int32, (Q_BLOCK, 2 * Q_BLOCK), 1)
    dist = Q_BLOCK + ii - jj
    band_ok = (dist >= 0) & (dist <= N_BACK)
    causal_ok = (lax.broadcasted_iota(jnp.int32, (Q_BLOCK, Q_BLOCK), 1)
                 <= lax.broadcasted_iota(jnp.int32, (Q_BLOCK, Q_BLOCK), 0))
    nt = (((1,), (1,)), ((), ()))

    def block(c, hh, n, first):
        rows = pl.ds(pl.multiple_of(n * Q_BLOCK, Q_BLOCK), Q_BLOCK)
        if first:
            keys, ok = rows, causal_ok
        else:
            keys, ok = pl.ds(pl.multiple_of((n - 1) * Q_BLOCK, Q_BLOCK), 2 * Q_BLOCK), band_ok
        s = lax.dot_general(q_ref[0, c, hh, rows, :], k_ref[0, c, hh, keys, :], nt, preferred_element_type=F32)
        s = jnp.where(ok, s, NEG)
        m = jnp.max(s, axis=-1, keepdims=True)
        p = jnp.exp(s - m).astype(BF16)
        oe = jnp.dot(p, vx[c, hh, keys, :], preferred_element_type=F32)
        l = oe[:, HEAD_DIM:]
        o_ref[0, c, hh, rows, :] = (oe[:, :HEAD_DIM] / l).astype(BF16)
        l_ref[0, c, hh, rows, :] = m + jnp.log(l)

    for c in range(d_blk):
        for hh in range(h_blk):
            block(c, hh, 0, True)
        if n_q <= 2:
            for n in range(1, n_q):
                for hh in range(h_blk):
                    block(c, hh, n, False)
        else:
            def step(n, carry, c=c):
                for hh in range(h_blk):
                    block(c, hh, n, False)
                return carry

            lax.fori_loop(1, n_q, step, 0, unroll=Q_UNROLL)


def _prompt_attn(q, k, v, d_blk, h_blk):
    B, d, H, M, E = q.shape
    blk = (1, d_blk, h_blk, M, E)
    spec = pl.BlockSpec(blk, lambda b, c, hh: (b, c, hh, 0, 0))
    return pl.pallas_call(
        _attn_body,
        grid=(B, d // d_blk, H // h_blk),
        in_specs=[spec, spec, spec],
        out_specs=[spec, spec],
        out_shape=[jax.ShapeDtypeStruct(q.shape, BF16), jax.ShapeDtypeStruct(q.shape, F32)],
        scratch_shapes=[pltpu.VMEM((d_blk, h_blk, M, 2 * E), BF16)],
        compiler_params=pltpu.CompilerParams(
            dimension_semantics=("arbitrary", "arbitrary", "arbitrary"), vmem_limit_bytes=VMEM_LIMIT),
        name="prompt_attn_d%d" % d,
    )(q, k, v)


def _pool_project(mixed, wp_ref, bp_ref, ps_ref):
    cols = []
    for g in range(len(POOL_WINDOWS)):
        lanes = slice(g * HEAD_DIM, (g + 1) * HEAD_DIM)
        y = jnp.dot(mixed[g].astype(BF16), wp_ref[g], preferred_element_type=F32) + bp_ref[g:g + 1, :]
        cols.append(y * ps_ref[:, lanes])
    return jnp.concatenate(cols, axis=-1)


def _merge_out(x, gate, a, p, z_a, z_b, g_a, g_b, wa_ref, wb_ref, wo_ref):
    ya = jnp.dot((a * _silu(z_a)).astype(BF16), wa_ref[...], preferred_element_type=F32)
    yb = jnp.dot((p * _silu(z_b)).astype(BF16), wb_ref[...], preferred_element_type=F32)
    m = _sigmoid(g_a) * ya + _sigmoid(g_b) * yb
    return x + gate * jnp.dot(m.astype(BF16), wo_ref[...], preferred_element_type=F32)


def _out_body(x_ref, mod_ref, ng_ref, w_ref, o0, l0, o1, l1, o2, l2, wp_ref, bp_ref, ps_ref,
              wa_ref, wb_ref, wo_ref, new_ref, src_a, src_b, dst_a_in, y_ref, pool_ref, dst_a, dst_b,
              ext, oscr, lscr, buf, rsem, wsem, nsem, *, n_tiles, shift):
    del dst_a_in
    i = pl.program_id(1)
    tm = x_ref.shape[1]
    step = pl.program_id(0) * n_tiles + i
    kinds, lo, hi = shift

    last_slot = lo + SHIFT_SUBSTEPS * pl.num_programs(0) * n_tiles - 1

    def shift_substep(k):
        _shift_substep(lo + SHIFT_SUBSTEPS * step + k, lo, hi, last_slot, kinds, [src_a, src_b], [dst_a, dst_b],
                       buf, rsem, wsem)

    shift_substep(0)

    @pl.when(step == 0)
    def _():
        _new_rows_copy(new_ref, dst_b, nsem).start()
        ext[0:16, :] = jnp.zeros((16, GROUP_W), F32)

    x = x_ref[0]
    h = _modulated_norm(x, ng_ref[...], mod_ref[0, :, 0:D_MODEL], mod_ref[0, :, D_MODEL:2 * D_MODEL])
    gate = mod_ref[0, :, 2 * D_MODEL:3 * D_MODEL]

    def proj(lo, width):
        return jnp.dot(h, w_ref[:, lo:lo + width], preferred_element_type=F32)

    z_a = proj(0, GROUP_W)
    u = proj(GROUP_W, GROUP_W)
    z_b = proj(2 * GROUP_W, GROUP_W)
    o_refs, l_refs = (o0, o1, o2), (l0, l1, l2)
    for g, (window, d) in enumerate(ATTN_PATTERNS):
        if d == 1:
            continue
        for hh in range(N_HEADS):
            for rr in range(d):
                rows = pl.ds(rr, tm // d, stride=d)
                oscr[g - 1, hh, rows, :] = o_refs[g][0, rr, hh].astype(F32)
                lscr[g - 1, hh, rows, :] = l_refs[g][0, rr, hh]
    a_cols = []
    for hh in range(N_HEADS):
        os_ = [o0[0, 0, hh].astype(F32), oscr[0, hh], oscr[1, hh]]
        ls_ = [l0[0, 0, hh], lscr[0, hh], lscr[1, hh]]
        mx = jnp.maximum(jnp.maximum(ls_[0], ls_[1]), ls_[2])
        es = [jnp.exp(l_ - mx) for l_ in ls_]
        den = es[0] + es[1] + es[2]
        a_cols.append((es[0] * os_[0] + es[1] * os_[1] + es[2] * os_[2]) / den)
    ya_in = (jnp.concatenate(a_cols, axis=-1) * _silu(z_a)).astype(BF16)
    shift_substep(1)

    @pl.when(step == 0)
    def _():
        _new_rows_copy(new_ref, dst_b, nsem).wait()

    g_a = proj(3 * GROUP_W, D_MODEL)
    ext[16:16 + tm, :] = u
    pos = i * tm + lax.broadcasted_iota(jnp.int32, (tm, 1), 0)
    mixed = []
    for g, w in enumerate(POOL_WINDOWS):
        lanes = pl.ds(g * HEAD_DIM, HEAD_DIM)
        acc = ext[pl.ds(16, tm), lanes]
        tok = acc
        for k in range(1, w):
            acc = acc + ext[pl.ds(16 - k, tm), lanes]
        cnt = jnp.minimum(w, pos + 1).astype(F32)
        mixed.append(acc / cnt - tok)
    pool_ref[0, 0] = ext[pl.ds(tm + 1, POOL_STATE), :]
    ext[0:16, :] = jnp.where(i == n_tiles - 1, 0.0, ext[tm:tm + 16, :])
    yb_in = (_pool_project(mixed, wp_ref, bp_ref, ps_ref) * _silu(z_b)).astype(BF16)
    shift_substep(2)

    g_b = proj(3 * GROUP_W + D_MODEL, D_MODEL)
    ya = jnp.dot(ya_in, wa_ref[...], preferred_element_type=F32)
    yb = jnp.dot(yb_in, wb_ref[...], preferred_element_type=F32)
    m = (_sigmoid(g_a) * ya + _sigmoid(g_b) * yb).astype(BF16)
    shift_substep(3)
    y_ref[0] = x + gate * jnp.dot(m, wo_ref[...], preferred_element_type=F32)


def _prompt_out(x, mod_p, norm_g, w_rest, ol, w_pool, b_pool, pool_scale, wa, wb, wo,
                new_rows, state_a, state_b, shifted_a, shift):
    B, L, _ = x.shape
    tm = ROW_TILE
    n_tiles = L // tm
    const2 = lambda b, i: (0, 0)
    one = pl.Buffered(1)
    in_specs = [
        pl.BlockSpec((1, tm, D_MODEL), lambda b, i: (b, i, 0)),
        pl.BlockSpec((1, 1, 3 * D_MODEL), lambda b, i: (b, 0, 0)),
        pl.BlockSpec((1, D_MODEL), const2),
        pl.BlockSpec((D_MODEL, REST_W), const2, pipeline_mode=one),
    ]
    for window, d in ATTN_PATTERNS:
        for _ in range(2):
            in_specs.append(pl.BlockSpec((1, d, N_HEADS, tm // d, HEAD_DIM), lambda b, i: (b, 0, 0, i, 0)))
    in_specs += [
        pl.BlockSpec((len(POOL_WINDOWS), HEAD_DIM, HEAD_DIM), lambda b, i: (0, 0, 0)),
        pl.BlockSpec((len(POOL_WINDOWS), HEAD_DIM), const2),
        pl.BlockSpec((1, GROUP_W), const2),
        pl.BlockSpec((GROUP_W, D_MODEL), const2, pipeline_mode=one),
        pl.BlockSpec((GROUP_W, D_MODEL), const2, pipeline_mode=one),
        pl.BlockSpec((D_MODEL, D_MODEL), const2, pipeline_mode=one),
        pl.BlockSpec(new_rows.shape, lambda b, i: (0, 0, 0), pipeline_mode=one),
    ]
    any_spec = pl.BlockSpec(memory_space=pl.ANY)
    in_specs += [any_spec] * 3
    return pl.pallas_call(
        functools.partial(_out_body, n_tiles=n_tiles, shift=shift),
        grid=(B, n_tiles),
        in_specs=in_specs,
        out_specs=[pl.BlockSpec((1, tm, D_MODEL), lambda b, i: (b, i, 0)),
                   pl.BlockSpec((1, 1, POOL_STATE, GROUP_W), lambda b, i: (0, b, 0, 0)),
                   any_spec, any_spec],
        out_shape=[jax.ShapeDtypeStruct((B, L, D_MODEL), F32),
                   jax.ShapeDtypeStruct((1, B, POOL_STATE, GROUP_W), F32),
                   jax.ShapeDtypeStruct(state_a.shape, F32),
                   jax.ShapeDtypeStruct(state_b.shape, F32)],
        scratch_shapes=[pltpu.VMEM((tm + 16, GROUP_W), F32),
                        pltpu.VMEM((2, N_HEADS, tm, HEAD_DIM), F32),
                        pltpu.VMEM((2, N_HEADS, tm, HEAD_DIM), F32)] + _shift_scratch(OUT_SHIFT_LEAD),
        input_output_aliases={len(in_specs) - 1: 2},
        compiler_params=pltpu.CompilerParams(
            dimension_semantics=("arbitrary", "arbitrary"), vmem_limit_bytes=OUT_VMEM_LIMIT),
        name="prompt_out",
    )(x, mod_p, norm_g, w_rest, *ol, w_pool, b_pool, pool_scale, wa, wb, wo,
      new_rows, state_a, state_b, shifted_a)


def _sample_proj_body(x_ref, mod_ref, ng_ref, w_ref, o_ref):
    h = _modulated_norm(x_ref[...], ng_ref[...], mod_ref[:, 0:D_MODEL], mod_ref[:, D_MODEL:2 * D_MODEL])
    o_ref[...] = jnp.dot(h, w_ref[...], preferred_element_type=F32)


def _sample_proj(xs, mod_s, norm_g, w):
    n, width = xs.shape[0], w.shape[1]
    tn = max(t for t in range(HEAD_DIM, width // 2 + 1, HEAD_DIM) if width % t == 0)
    return pl.pallas_call(
        _sample_proj_body,
        grid=(width // tn,),
        in_specs=[pl.BlockSpec((n, D_MODEL), lambda j: (0, 0)),
                  pl.BlockSpec((n, 3 * D_MODEL), lambda j: (0, 0)),
                  pl.BlockSpec((1, D_MODEL), lambda j: (0, 0)),
                  pl.BlockSpec((D_MODEL, tn), lambda j: (0, j))],
        out_specs=pl.BlockSpec((n, tn), lambda j: (0, j)),
        out_shape=jax.ShapeDtypeStruct((n, width), F32),
        name="sample_proj",
    )(xs, mod_s, norm_g, w)


def _sample_attn_body(p_ref, s0, s1, s2, qg_ref, kg_ref, a_ref, shifted0, n1, n2):
    bt = p_ref.shape[0]
    s_refs = (s0, s1, s2)
    last = N_BACK - 1
    for b in range(bt):
        os_, ls_ = [], []
        shifted0[b, 0:last] = s0[b, 1:N_BACK]
        for g in range(N_GROUPS):
            row = lambda t: pl.ds((t * N_GROUPS + g) * N_HEADS, N_HEADS)
            q4 = _head_norm(p_ref[b, row(0), :], qg_ref[g:g + 1, :]) * Q_SCALE
            k4 = _head_norm(p_ref[b, row(1), :], kg_ref[g:g + 1, :])
            v4 = p_ref[b, row(2), :]
            if g == 0:
                shifted0[b, last, 0:N_HEADS, :] = k4
                shifted0[b, last, N_HEADS:2 * N_HEADS, :] = v4
            else:
                (n1, n2)[g - 1][b, 0:N_HEADS, :] = k4
                (n1, n2)[g - 1][b, N_HEADS:2 * N_HEADS, :] = v4
            keys = s_refs[g][b, :, 0:N_HEADS, :]
            vals = s_refs[g][b, :, N_HEADS:2 * N_HEADS, :]
            s_old = jnp.sum(keys * q4[None], axis=-1, keepdims=True)
            s_new = jnp.sum(k4 * q4, axis=-1, keepdims=True)
            m = jnp.maximum(jnp.max(s_old, axis=0), s_new)
            p_old = jnp.exp(s_old - m[None])
            p_new = jnp.exp(s_new - m)
            l = jnp.sum(p_old, axis=0) + p_new
            o = jnp.sum(p_old * vals, axis=0) + p_new * v4
            os_.append(o / l)
            ls_.append(m + jnp.log(l))
        mx = jnp.maximum(jnp.maximum(ls_[0], ls_[1]), ls_[2])
        es = [jnp.exp(l_ - mx) for l_ in ls_]
        a_ref[b] = (es[0] * os_[0] + es[1] * os_[1] + es[2] * os_[2]) / (es[0] + es[1] + es[2])


def _sample_attn(p3, views, q_gain, k_gain):
    n = p3.shape[0]
    bt = 8
    in_specs = [pl.BlockSpec((bt, p3.shape[1], HEAD_DIM), lambda j: (j, 0, 0))]
    for v in views:
        in_specs.append(pl.BlockSpec((bt, N_BACK, 2 * N_HEADS, HEAD_DIM), lambda j: (j, 0, 0, 0)))
    in_specs += [pl.BlockSpec((N_GROUPS, HEAD_DIM), lambda j: (0, 0))] * 2
    new_spec = pl.BlockSpec((bt, 2 * N_HEADS, HEAD_DIM), lambda j: (j, 0, 0))
    new_shape = jax.ShapeDtypeStruct((n, 2 * N_HEADS, HEAD_DIM), F32)
    state_spec = pl.BlockSpec((bt, N_BACK, 2 * N_HEADS, HEAD_DIM), lambda j: (j, 0, 0, 0))
    return pl.pallas_call(
        _sample_attn_body,
        grid=(n // bt,),
        in_specs=in_specs,
        out_specs=[pl.BlockSpec((bt, N_HEADS, HEAD_DIM), lambda j: (j, 0, 0)), state_spec, new_spec, new_spec],
        out_shape=[jax.ShapeDtypeStruct((n, N_HEADS, HEAD_DIM), F32),
                   jax.ShapeDtypeStruct(views[0].shape, F32), new_shape, new_shape],
        compiler_params=pltpu.CompilerParams(vmem_limit_bytes=VMEM_LIMIT),
        name="sample_attn",
    )(p3, *views, q_gain, k_gain)


def _sample_out_body(x_ref, mod_ref, r_ref, a_ref, sp_ref, wp_ref, bp_ref, ps_ref, wa_ref, wb_ref, wo_ref,
                     y_ref, pool_ref):
    u = r_ref[:, GROUP_W:2 * GROUP_W]
    mixed = []
    for g, w in enumerate(POOL_WINDOWS):
        lanes = pl.ds(g * HEAD_DIM, HEAD_DIM)
        tok = r_ref[:, pl.ds(GROUP_W + g * HEAD_DIM, HEAD_DIM)]
        acc = tok
        for k in range(1, w):
            acc = acc + sp_ref[POOL_STATE - k, :, lanes]
        mixed.append(acc / float(min(w, PAST_LEN + 1)) - tok)
    p = _pool_project(mixed, wp_ref, bp_ref, ps_ref)
    for k in range(POOL_STATE - 1):
        pool_ref[k] = sp_ref[k + 1]
    pool_ref[POOL_STATE - 1] = u
    y_ref[...] = _merge_out(
        x_ref[...], mod_ref[:, 2 * D_MODEL:3 * D_MODEL], a_ref[...], p,
        r_ref[:, 0:GROUP_W], r_ref[:, 2 * GROUP_W:3 * GROUP_W],
        r_ref[:, 3 * GROUP_W:3 * GROUP_W + D_MODEL], r_ref[:, 3 * GROUP_W + D_MODEL:REST_W],
        wa_ref, wb_ref, wo_ref)


def _sample_out(xs, mod_s, rest, a_s, sp_t, w_pool, b_pool, pool_scale, wa, wb, wo):
    n = xs.shape[0]
    return pl.pallas_call(
        _sample_out_body,
        out_shape=[jax.ShapeDtypeStruct((n, D_MODEL), F32),
                   jax.ShapeDtypeStruct((POOL_STATE, n, GROUP_W), F32)],
        compiler_params=pltpu.CompilerParams(vmem_limit_bytes=VMEM_LIMIT),
        name="sample_out",
    )(xs, mod_s, rest, a_s, sp_t, w_pool, b_pool, pool_scale, wa, wb, wo)


def kernel(x_prompt, x_sample, state_kv_w128, state_kv_w512, state_kv_w2048, state_pool, c_prompt, c_sample,
           norm_g, w_ada, b_ada, w_in, q_gain, k_gain, w_pool, b_pool, pool_scale, w_a_out, w_b_out, w_out):
    B, L, _ = x_prompt.shape
    n_s = x_sample.shape[0]
    w_qkv = w_in[0, :, :QKV_W].astype(BF16)
    w_rest = w_in[0, :, QKV_W:].astype(BF16)
    wa, wb, wo, wp = (w[0].astype(BF16) for w in (w_a_out, w_b_out, w_out, w_pool))
    qg, kg, bp = q_gain[0], k_gain[0], b_pool[0]

    mod = _ada(jnp.concatenate([c_prompt, c_sample], axis=0), w_ada[0], b_ada)
    mod_p = mod[:B].reshape(B, 1, 3 * D_MODEL)
    mod_s = mod[B:]

    xs = x_sample[:, 0, :]
    qkv_s = _sample_proj(xs, mod_s, norm_g, w_qkv)
    rest_s = _sample_proj(xs, mod_s, norm_g, w_rest)
    states = (state_kv_w128, state_kv_w512, state_kv_w2048)
    views = [s.reshape(n_s, N_BACK, d * ROWS_PER_POS, HEAD_DIM) for s, (_, d) in zip(states, ATTN_PATTERNS)]
    a_s, shifted_128, new_512, new_2048 = _sample_attn(
        qkv_s.reshape(n_s, QKV_W // HEAD_DIM, HEAD_DIM), views, qg, kg)
    y_s, pool_t = _sample_out(xs, mod_s, rest_s, a_s.reshape(n_s, GROUP_W), state_pool[0].transpose(1, 0, 2),
                              wp, bp, pool_scale, wa, wb, wo)

    flat_2048, flat_512 = (s.reshape(n_s, s.shape[2] * ROWS_PER_POS, HEAD_DIM) for s in states[:0:-1])
    kinds = _shift_plan([flat_2048.shape, flat_512.shape])
    n_sub = B * (L // ROW_TILE) * SHIFT_SUBSTEPS
    n_chunks = kinds[-1][-1]
    assert n_sub <= kinds[0][4] and n_chunks <= 2 * n_sub
    *qkv, t0, t1, t2, part_2048 = _prompt_qkv(x_prompt, mod_p, norm_g, w_qkv, qg, kg, new_2048, flat_2048,
                                              (kinds, 0, n_sub))
    ol = []
    for g, (d_blk, h_blk) in enumerate(((1, 2), (2, 4), (8, 4))):
        ol += _prompt_attn(qkv[g], qkv[N_GROUPS + g], qkv[2 * N_GROUPS + g], d_blk, h_blk)
    y_p, pool_p, shifted_2048, shifted_512 = _prompt_out(
        x_prompt, mod_p, norm_g, w_rest, ol, wp, bp, pool_scale, wa, wb, wo,
        new_512, flat_2048, flat_512, part_2048, (kinds, n_sub, n_chunks))
    kv_s = [o.reshape(s.shape) for o, s in zip((shifted_128, shifted_512, shifted_2048), states)]
    kv_p = [t.reshape(1, B, t.shape[1], 2, N_HEADS, HEAD_DIM) for t in (t0, t1, t2)]

    return (y_p, y_s.reshape(n_s, 1, D_MODEL), kv_p[0], kv_p[1], kv_p[2], pool_p,
            kv_s[0], kv_s[1], kv_s[2], pool_t.transpose(1, 0, 2)[None])
```

```python
import functools

import jax
import jax.numpy as jnp
from jax import lax
from jax.experimental import pallas as pl
from jax.experimental.pallas import tpu as pltpu

F32 = jnp.float32
BF16 = jnp.bfloat16

D_MODEL = 1024
HEAD_DIM = 128
N_HEADS = 4
GROUP_W = N_HEADS * HEAD_DIM
ATTN_PATTERNS = ((128, 1), (512, 4), (2048, 16))
N_GROUPS = len(ATTN_PATTERNS)
N_BACK = 128
QKV_W = 3 * N_GROUPS * GROUP_W
REST_W = 3 * GROUP_W + 2 * D_MODEL
POOL_WINDOWS = (2, 4, 8, 16)
POOL_STATE = 15
PAST_LEN = 8192
EPS = 1e-6
Q_SCALE = HEAD_DIM ** -0.5
NEG = -1e30

ROW_TILE = 512
Q_BLOCK = 128
Q_UNROLL = 16
VMEM_LIMIT = 56 * 1024 * 1024
OUT_VMEM_LIMIT = 60 * 1024 * 1024


def _sigmoid(v):
    return 0.5 * jnp.tanh(0.5 * v) + 0.5


def _silu(v):
    return v * _sigmoid(v)


def _modulated_norm(x, norm_g, shift, scale):
    ms = jnp.mean(x * x, axis=-1, keepdims=True)
    return (x * lax.rsqrt(ms + EPS) * norm_g * (1.0 + scale) + shift).astype(BF16)


def _head_norm(r, gain):
    ms = jnp.mean(r * r, axis=-1, keepdims=True)
    return r * lax.rsqrt(ms + EPS) * gain


def _ada_body(c_ref, w_ref, b_ref, o_ref):
    s = _silu(c_ref[...]).astype(BF16)
    o_ref[...] = jnp.dot(s, w_ref[...].astype(BF16), preferred_element_type=F32) + b_ref[...]


def _ada(c_all, w_ada, b_ada):
    n = c_all.shape[0]
    return pl.pallas_call(
        _ada_body,
        grid=(3,),
        in_specs=[pl.BlockSpec((n, D_MODEL), lambda j: (0, 0)),
                  pl.BlockSpec((D_MODEL, D_MODEL), lambda j: (0, j)),
                  pl.BlockSpec((1, D_MODEL), lambda j: (0, j))],
        out_specs=pl.BlockSpec((n, D_MODEL), lambda j: (0, j)),
        out_shape=jax.ShapeDtypeStruct((n, 3 * D_MODEL), F32),
        name="ada",
    )(c_all, w_ada, b_ada)


ROWS_PER_POS = 2 * N_HEADS
SHIFT_NB = 8
SHIFT_ROWS = 712
SHIFT_SUBSTEPS = 4
QKV_SHIFT_LEAD = 3
OUT_SHIFT_LEAD = 3
SHIFT_PRIORITY = 1


def _shift_plan(state_shapes):
    kinds, c0 = [], 0
    for idx, (n_batch, rows_total, _) in enumerate(state_shapes):
        keep = rows_total - ROWS_PER_POS
        pieces = min(p for p in range(1, keep)
                     if keep % (p * ROWS_PER_POS) == 0 and keep // p <= SHIFT_ROWS)
        n = (n_batch // SHIFT_NB) * pieces
        kinds.append((idx, keep // pieces, pieces, c0, c0 + n))
        c0 += n
    return kinds


def _shift_substep(j, lo, hi, last_slot, kinds, srcs, dsts, buf, rsem, wsem):
    ahead = buf.shape[0] // 2

    def copy(jj, kind, write):
        idx, rows, pieces, c0, _ = kind
        local = jj - c0
        batches = pl.ds((local // pieces) * SHIFT_NB, SHIFT_NB)
        first = (local % pieces) * rows
        slot = jj % buf.shape[0]
        stage = buf.at[slot, :, pl.ds(0, rows), :]
        if write:
            return pltpu.make_async_copy(stage, dsts[idx].at[batches, pl.ds(first, rows), :], wsem.at[slot])
        return pltpu.make_async_copy(srcs[idx].at[batches, pl.ds(first + ROWS_PER_POS, rows), :], stage, rsem.at[slot])

    spans = [(kind, max(kind[3], lo), min(kind[4], hi)) for kind in kinds]
    spans = [s for s in spans if s[1] < s[2]]

    def for_chunk(jj, also, fn):
        for kind, first, end in spans:
            cond = (jj >= first) & (jj < end)

            @pl.when(cond if also is None else cond & also)
            def _():
                fn(functools.partial(copy, jj, kind))

    steady = []
    for kind, first, end in spans:
        inner = (j >= first + ahead) & (j < end - ahead)
        steady.append(inner)

        @pl.when(inner)
        def _():
            copy(j, kind, False).wait()
            copy(j, kind, True).start(priority=SHIFT_PRIORITY)
            copy(j - ahead, kind, True).wait()
            copy(j + ahead, kind, False).start(priority=SHIFT_PRIORITY)

    @pl.when(jnp.logical_not(functools.reduce(jnp.logical_or, steady)))
    def _():
        for a in range(ahead):
            for_chunk(j + a, j == lo, lambda cp: cp(False).start(priority=SHIFT_PRIORITY))

        def landed(cp):
            cp(False).wait()
            cp(True).start(priority=SHIFT_PRIORITY)

        for_chunk(j, None, landed)
        for_chunk(j - ahead, None, lambda cp: cp(True).wait())
        for_chunk(j + ahead, None, lambda cp: cp(False).start(priority=SHIFT_PRIORITY))
        for a in range(ahead):
            for_chunk(j - a, j == last_slot, lambda cp: cp(True).wait())


def _new_rows_copy(new_ref, dst, nsem):
    keep = dst.shape[1] - ROWS_PER_POS
    return pltpu.make_async_copy(new_ref, dst.at[:, pl.ds(keep, ROWS_PER_POS), :], nsem)


def _shift_scratch(lead):
    return [pltpu.VMEM((2 * lead, SHIFT_NB, SHIFT_ROWS, HEAD_DIM), F32),
            pltpu.SemaphoreType.DMA((2 * lead,)), pltpu.SemaphoreType.DMA((2 * lead,)),
            pltpu.SemaphoreType.DMA(())]


def _class_major_perm(tm, d):
    i = jnp.arange(tm)
    src_row = (i % (tm // d)) * d + i // (tm // d)
    return (src_row[:, None] == jnp.arange(tm)[None, :]).astype(BF16)


SHIFT_SITES = ((0, 1), (1, 1), (2, 0))


def _qkv_body(x_ref, mod_ref, ng_ref, w_ref, qg_ref, kg_ref, perm_ref, new_ref, src_ref, *refs, n_tiles, shift):
    qkv_refs = (refs[0:3], refs[3:6], refs[6:9])
    tail_refs = refs[9:12]
    dst_ref, buf, rsem, wsem, nsem = refs[12:17]
    i = pl.program_id(1)
    tm = x_ref.shape[1]
    step = pl.program_id(0) * n_tiles + i
    kinds, lo, hi = shift

    last_slot = lo + SHIFT_SUBSTEPS * pl.num_programs(0) * n_tiles - 1

    def shift_substep(k):
        _shift_substep(lo + SHIFT_SUBSTEPS * step + k, lo, hi, last_slot, kinds, [src_ref], [dst_ref],
                       buf, rsem, wsem)

    shift_substep(0)

    @pl.when(step == 0)
    def _():
        _new_rows_copy(new_ref, dst_ref, nsem).start()

    h = _modulated_norm(x_ref[0], ng_ref[...], mod_ref[0, :, 0:D_MODEL], mod_ref[0, :, D_MODEL:2 * D_MODEL])
    h_by_group = [h] + [jnp.dot(perm_ref[g - 1], h, preferred_element_type=F32).astype(BF16)
                        for g in range(1, N_GROUPS)]

    def heads(t, g, lhs):
        c = t * N_GROUPS + g
        res = jnp.dot(lhs, w_ref[:, c * GROUP_W:(c + 1) * GROUP_W], preferred_element_type=F32)
        out = []
        for hh in range(N_HEADS):
            r = res[:, hh * HEAD_DIM:(hh + 1) * HEAD_DIM]
            if t == 0:
                r = _head_norm(r, qg_ref[g:g + 1, :]) * Q_SCALE
            elif t == 1:
                r = _head_norm(r, kg_ref[g:g + 1, :])
            out.append(r)
        return out

    def write_tail(t, g, rs):
        window, d = ATTN_PATTERNS[g]
        per = tm // d
        for hh, r in enumerate(rs):
            row = (t - 1) * N_HEADS + hh
            if d == 1:
                tail_refs[g][0, :, row, :] = r if window >= tm else r[tm - window:, :]
            else:
                assert window >= tm
                for rr in range(d):
                    tail_refs[g][0, pl.ds(rr, per, stride=d), row, :] = r[rr * per:(rr + 1) * per, :]

    kept = {}
    for t in range(3):
        for g, (window, d) in enumerate(ATTN_PATTERNS):
            if (t, g) in SHIFT_SITES:
                shift_substep(SHIFT_SITES.index((t, g)) + 1)
            if (t, g) == SHIFT_SITES[0]:
                @pl.when(step == 0)
                def _():
                    _new_rows_copy(new_ref, dst_ref, nsem).wait()

            rs = heads(t, g, h_by_group[g])
            per = tm // d
            for hh, r in enumerate(rs):
                for rr in range(d):
                    qkv_refs[t][g][0, rr, hh] = r[rr * per:(rr + 1) * per, :].astype(BF16)
            if t > 0:
                kept.setdefault(max(window // tm, 1), []).append((t, g, rs))

    for n_tail, items in sorted(kept.items()):
        @pl.when(i >= n_tiles - n_tail)
        def _():
            for t, g, rs in items:
                write_tail(t, g, rs)


def _prompt_qkv(x, mod_p, norm_g, w_qkv, q_gain, k_gain, new_rows, state, shift):
    B, L, _ = x.shape
    tm = ROW_TILE
    n_tiles = L // tm
    const2 = lambda b, i: (0, 0)
    in_specs = [
        pl.BlockSpec((1, tm, D_MODEL), lambda b, i: (b, i, 0)),
        pl.BlockSpec((1, 1, 3 * D_MODEL), lambda b, i: (b, 0, 0)),
        pl.BlockSpec((1, D_MODEL), const2),
        pl.BlockSpec((D_MODEL, QKV_W), const2, pipeline_mode=pl.Buffered(1)),
        pl.BlockSpec((N_GROUPS, HEAD_DIM), const2),
        pl.BlockSpec((N_GROUPS, HEAD_DIM), const2),
        pl.BlockSpec((N_GROUPS - 1, tm, tm), lambda b, i: (0, 0, 0), pipeline_mode=pl.Buffered(1)),
        pl.BlockSpec(new_rows.shape, lambda b, i: (0, 0, 0), pipeline_mode=pl.Buffered(1)),
        pl.BlockSpec(memory_space=pl.ANY),
    ]
    perms = jnp.stack([_class_major_perm(tm, d) for _, d in ATTN_PATTERNS[1:]])
    out_specs, out_shape = [], []
    for t in range(3):
        for window, d in ATTN_PATTERNS:
            out_specs.append(pl.BlockSpec((1, d, N_HEADS, tm // d, HEAD_DIM), lambda b, i: (b, 0, 0, i, 0)))
            out_shape.append(jax.ShapeDtypeStruct((B, d, N_HEADS, L // d, HEAD_DIM), BF16))
    for window, d in ATTN_PATTERNS:
        w_eff = min(window, L)
        if w_eff >= tm:
            n_tail = w_eff // tm
            out_specs.append(pl.BlockSpec(
                (1, tm, 2 * N_HEADS, HEAD_DIM),
                lambda b, i, n_tail=n_tail: (b, jnp.maximum(i - (n_tiles - n_tail), 0), 0, 0)))
        else:
            out_specs.append(pl.BlockSpec((1, w_eff, 2 * N_HEADS, HEAD_DIM), lambda b, i: (b, 0, 0, 0)))
        out_shape.append(jax.ShapeDtypeStruct((B, w_eff, 2 * N_HEADS, HEAD_DIM), F32))
    out_specs.append(pl.BlockSpec(memory_space=pl.ANY))
    out_shape.append(jax.ShapeDtypeStruct(state.shape, F32))
    return pl.pallas_call(
        functools.partial(_qkv_body, n_tiles=n_tiles, shift=shift),
        grid=(B, n_tiles),
        in_specs=in_specs,
        out_specs=out_specs,
        out_shape=out_shape,
        scratch_shapes=_shift_scratch(QKV_SHIFT_LEAD),
        compiler_params=pltpu.CompilerParams(
            dimension_semantics=("arbitrary", "arbitrary"), vmem_limit_bytes=OUT_VMEM_LIMIT),
        name="prompt_qkv",
    )(x, mod_p, norm_g, w_qkv, q_gain, k_gain, perms, new_rows, state)


def _attn_body(q_ref, k_ref, v_ref, o_ref, l_ref, vx):
    _, d_blk, h_blk, M, _ = q_ref.shape
    n_q = M // Q_BLOCK
    for c in range(d_blk):
        for hh in range(h_blk):
            vx[c, hh, :, 0:HEAD_DIM] = v_ref[0, c, hh]
            vx[c, hh, :, HEAD_DIM:2 * HEAD_DIM] = jnp.ones((M, HEAD_DIM), BF16)
    ii = lax.broadcasted_iota(jnp.int32, (Q_BLOCK, 2 * Q_BLOCK), 0)
    jj = lax.broadcasted_iota(jnp.int32, (Q_BLOCK, 2 * Q_BLOCK), 1)
    dist = Q_BLOCK + ii - jj
    band_ok = (dist >= 0) & (dist <= N_BACK)
    causal_ok = (lax.broadcasted_iota(jnp.int32, (Q_BLOCK, Q_BLOCK), 1)
                 <= lax.broadcasted_iota(jnp.int32, (Q_BLOCK, Q_BLOCK), 0))
    nt = (((1,), (1,)), ((), ()))

    def block(c, hh, n, first):
        rows = pl.ds(pl.multiple_of(n * Q_BLOCK, Q_BLOCK), Q_BLOCK)
        if first:
            keys, ok = rows, causal_ok
        else:
            keys, ok = pl.ds(pl.multiple_of((n - 1) * Q_BLOCK, Q_BLOCK), 2 * Q_BLOCK), band_ok
        s = lax.dot_general(q_ref[0, c, hh, rows, :], k_ref[0, c, hh, keys, :], nt, preferred_element_type=F32)
        s = jnp.where(ok, s, NEG)
        m = jnp.max(s, axis=-1, keepdims=True)
        p = jnp.exp(s - m).astype(BF16)
        oe = jnp.dot(p, vx[c, hh, keys, :], preferred_element_type=F32)
        l = oe[:, HEAD_DIM:]
        o_ref[0, c, hh, rows, :] = (oe[:, :HEAD_DIM] / l).astype(BF16)
        return m + jnp.log(l)

    low_half = lax.broadcasted_iota(jnp.int32, (Q_BLOCK, HEAD_DIM), 1) < HEAD_DIM // 2

    def blocks(c, n, first):
        rows = pl.ds(pl.multiple_of(n * Q_BLOCK, Q_BLOCK), Q_BLOCK)
        for pair in range(h_blk // 2):
            lse = [block(c, 2 * pair + k, n, first) for k in range(2)]
            l_ref[0, c, pair, rows, :] = jnp.where(low_half, lse[0], lse[1])

    for c in range(d_blk):
        blocks(c, 0, True)
        if n_q <= 2:
            for n in range(1, n_q):
                blocks(c, n, False)
        else:
            def step(n, carry, c=c):
                blocks(c, n, False)
                return carry

            lax.fori_loop(1, n_q, step, 0, unroll=Q_UNROLL)


def _prompt_attn(q, k, v, d_blk, h_blk):
    B, d, H, M, E = q.shape
    blk = (1, d_blk, h_blk, M, E)
    spec = pl.BlockSpec(blk, lambda b, c, hh: (b, c, hh, 0, 0))
    return pl.pallas_call(
        _attn_body,
        grid=(B, d // d_blk, H // h_blk),
        in_specs=[spec, spec, spec],
        out_specs=[spec, pl.BlockSpec((1, d_blk, h_blk // 2, M, E), lambda b, c, hh: (b, c, hh, 0, 0))],
        out_shape=[jax.ShapeDtypeStruct(q.shape, BF16), jax.ShapeDtypeStruct((B, d, H // 2, M, E), F32)],
        scratch_shapes=[pltpu.VMEM((d_blk, h_blk, M, 2 * E), BF16)],
        compiler_params=pltpu.CompilerParams(
            dimension_semantics=("arbitrary", "arbitrary", "arbitrary"), vmem_limit_bytes=VMEM_LIMIT),
        name="prompt_attn_d%d" % d,
    )(q, k, v)


def _pool_project(mixed, wp_ref, bp_ref, ps_ref):
    cols = []
    for g in range(len(POOL_WINDOWS)):
        lanes = slice(g * HEAD_DIM, (g + 1) * HEAD_DIM)
        y = jnp.dot(mixed[g].astype(BF16), wp_ref[g], preferred_element_type=F32) + bp_ref[g:g + 1, :]
        cols.append(y * ps_ref[:, lanes])
    return jnp.concatenate(cols, axis=-1)


def _merge_out(x, gate, a, p, z_a, z_b, g_a, g_b, wa_ref, wb_ref, wo_ref):
    ya = jnp.dot((a * _silu(z_a)).astype(BF16), wa_ref[...], preferred_element_type=F32)
    yb = jnp.dot((p * _silu(z_b)).astype(BF16), wb_ref[...], preferred_element_type=F32)
    m = _sigmoid(g_a) * ya + _sigmoid(g_b) * yb
    return x + gate * jnp.dot(m.astype(BF16), wo_ref[...], preferred_element_type=F32)


def _out_body(x_ref, mod_ref, ng_ref, w_ref, o0, l0, o1, l1, o2, l2, wp_ref, bp_ref, ps_ref,
              wa_ref, wb_ref, wo_ref, new_ref, src_a, src_b, dst_a_in, y_ref, pool_ref, dst_a, dst_b,
              ext, oscr, lscr, buf, rsem, wsem, nsem, *, n_tiles, shift):
    del dst_a_in
    i = pl.program_id(1)
    tm = x_ref.shape[1]
    step = pl.program_id(0) * n_tiles + i
    kinds, lo, hi = shift

    last_slot = lo + SHIFT_SUBSTEPS * pl.num_programs(0) * n_tiles - 1

    def shift_substep(k):
        _shift_substep(lo + SHIFT_SUBSTEPS * step + k, lo, hi, last_slot, kinds, [src_a, src_b], [dst_a, dst_b],
                       buf, rsem, wsem)

    shift_substep(0)

    @pl.when(step == 0)
    def _():
        _new_rows_copy(new_ref, dst_b, nsem).start()
        ext[0:16, :] = jnp.zeros((16, GROUP_W), F32)

    x = x_ref[0]
    h = _modulated_norm(x, ng_ref[...], mod_ref[0, :, 0:D_MODEL], mod_ref[0, :, D_MODEL:2 * D_MODEL])
    gate = mod_ref[0, :, 2 * D_MODEL:3 * D_MODEL]

    def proj(lo, width):
        return jnp.dot(h, w_ref[:, lo:lo + width], preferred_element_type=F32)

    z_a = proj(0, GROUP_W)
    u = proj(GROUP_W, GROUP_W)
    z_b = proj(2 * GROUP_W, GROUP_W)
    o_refs, l_refs = (o0, o1, o2), (l0, l1, l2)
    half = HEAD_DIM // 2

    def head_lse(g, c, hh):
        packed = l_refs[g][0, c, hh // 2]
        low = lax.broadcasted_iota(jnp.int32, packed.shape, 1) < half
        return jnp.where(low == (hh % 2 == 0), packed, pltpu.roll(packed, half, axis=1))

    for g, (window, d) in enumerate(ATTN_PATTERNS):
        if d == 1:
            continue
        for hh in range(N_HEADS):
            for rr in range(d):
                rows = pl.ds(rr, tm // d, stride=d)
                oscr[g - 1, hh, rows, :] = o_refs[g][0, rr, hh].astype(F32)
                lscr[g - 1, hh, rows, :] = head_lse(g, rr, hh)
    a_cols = []
    for hh in range(N_HEADS):
        os_ = [o0[0, 0, hh].astype(F32), oscr[0, hh], oscr[1, hh]]
        ls_ = [head_lse(0, 0, hh), lscr[0, hh], lscr[1, hh]]
        mx = jnp.maximum(jnp.maximum(ls_[0], ls_[1]), ls_[2])
        es = [jnp.exp(l_ - mx) for l_ in ls_]
        den = es[0] + es[1] + es[2]
        a_cols.append((es[0] * os_[0] + es[1] * os_[1] + es[2] * os_[2]) / den)
    ya_in = (jnp.concatenate(a_cols, axis=-1) * _silu(z_a)).astype(BF16)
    shift_substep(1)

    @pl.when(step == 0)
    def _():
        _new_rows_copy(new_ref, dst_b, nsem).wait()

    g_a = proj(3 * GROUP_W, D_MODEL)
    ext[16:16 + tm, :] = u
    pos = i * tm + lax.broadcasted_iota(jnp.int32, (tm, 1), 0)
    mixed = []
    for g, w in enumerate(POOL_WINDOWS):
        lanes = pl.ds(g * HEAD_DIM, HEAD_DIM)
        acc = ext[pl.ds(16, tm), lanes]
        tok = acc
        for k in range(1, w):
            acc = acc + ext[pl.ds(16 - k, tm), lanes]
        cnt = jnp.minimum(w, pos + 1).astype(F32)
        mixed.append(acc / cnt - tok)
    pool_ref[0, 0] = ext[pl.ds(tm + 1, POOL_STATE), :]
    ext[0:16, :] = jnp.where(i == n_tiles - 1, 0.0, ext[tm:tm + 16, :])
    yb_in = (_pool_project(mixed, wp_ref, bp_ref, ps_ref) * _silu(z_b)).astype(BF16)
    shift_substep(2)

    g_b = proj(3 * GROUP_W + D_MODEL, D_MODEL)
    ya = jnp.dot(ya_in, wa_ref[...], preferred_element_type=F32)
    yb = jnp.dot(yb_in, wb_ref[...], preferred_element_type=F32)
    m = (_sigmoid(g_a) * ya + _sigmoid(g_b) * yb).astype(BF16)
    shift_substep(3)
    y_ref[0] = x + gate * jnp.dot(m, wo_ref[...], preferred_element_type=F32)


def _prompt_out(x, mod_p, norm_g, w_rest, ol, w_pool, b_pool, pool_scale, wa, wb, wo,
                new_rows, state_a, state_b, shifted_a, shift):
    B, L, _ = x.shape
    tm = ROW_TILE
    n_tiles = L // tm
    const2 = lambda b, i: (0, 0)
    one = pl.Buffered(1)
    in_specs = [
        pl.BlockSpec((1, tm, D_MODEL), lambda b, i: (b, i, 0)),
        pl.BlockSpec((1, 1, 3 * D_MODEL), lambda b, i: (b, 0, 0)),
        pl.BlockSpec((1, D_MODEL), const2),
        pl.BlockSpec((D_MODEL, REST_W), const2, pipeline_mode=one),
    ]
    for window, d in ATTN_PATTERNS:
        for heads in (N_HEADS, N_HEADS // 2):
            in_specs.append(pl.BlockSpec((1, d, heads, tm // d, HEAD_DIM), lambda b, i: (b, 0, 0, i, 0)))
    in_specs += [
        pl.BlockSpec((len(POOL_WINDOWS), HEAD_DIM, HEAD_DIM), lambda b, i: (0, 0, 0)),
        pl.BlockSpec((len(POOL_WINDOWS), HEAD_DIM), const2),
        pl.BlockSpec((1, GROUP_W), const2),
        pl.BlockSpec((GROUP_W, D_MODEL), const2, pipeline_mode=one),
        pl.BlockSpec((GROUP_W, D_MODEL), const2, pipeline_mode=one),
        pl.BlockSpec((D_MODEL, D_MODEL), const2, pipeline_mode=one),
        pl.BlockSpec(new_rows.shape, lambda b, i: (0, 0, 0), pipeline_mode=one),
    ]
    any_spec = pl.BlockSpec(memory_space=pl.ANY)
    in_specs += [any_spec] * 3
    return pl.pallas_call(
        functools.partial(_out_body, n_tiles=n_tiles, shift=shift),
        grid=(B, n_tiles),
        in_specs=in_specs,
        out_specs=[pl.BlockSpec((1, tm, D_MODEL), lambda b, i: (b, i, 0)),
                   pl.BlockSpec((1, 1, POOL_STATE, GROUP_W), lambda b, i: (0, b, 0, 0)),
                   any_spec, any_spec],
        out_shape=[jax.ShapeDtypeStruct((B, L, D_MODEL), F32),
                   jax.ShapeDtypeStruct((1, B, POOL_STATE, GROUP_W), F32),
                   jax.ShapeDtypeStruct(state_a.shape, F32),
                   jax.ShapeDtypeStruct(state_b.shape, F32)],
        scratch_shapes=[pltpu.VMEM((tm + 16, GROUP_W), F32),
                        pltpu.VMEM((2, N_HEADS, tm, HEAD_DIM), F32),
                        pltpu.VMEM((2, N_HEADS, tm, HEAD_DIM), F32)] + _shift_scratch(OUT_SHIFT_LEAD),
        input_output_aliases={len(in_specs) - 1: 2},
        compiler_params=pltpu.CompilerParams(
            dimension_semantics=("arbitrary", "arbitrary"), vmem_limit_bytes=OUT_VMEM_LIMIT),
        name="prompt_out",
    )(x, mod_p, norm_g, w_rest, *ol, w_pool, b_pool, pool_scale, wa, wb, wo,
      new_rows, state_a, state_b, shifted_a)


def _sample_proj_body(x_ref, mod_ref, ng_ref, w_ref, o_ref):
    h = _modulated_norm(x_ref[...], ng_ref[...], mod_ref[:, 0:D_MODEL], mod_ref[:, D_MODEL:2 * D_MODEL])
    o_ref[...] = jnp.dot(h, w_ref[...], preferred_element_type=F32)


def _sample_proj(xs, mod_s, norm_g, w):
    n, width = xs.shape[0], w.shape[1]
    tn = max(t for t in range(HEAD_DIM, width // 2 + 1, HEAD_DIM) if width % t == 0)
    return pl.pallas_call(
        _sample_proj_body,
        grid=(width // tn,),
        in_specs=[pl.BlockSpec((n, D_MODEL), lambda j: (0, 0)),
                  pl.BlockSpec((n, 3 * D_MODEL), lambda j: (0, 0)),
                  pl.BlockSpec((1, D_MODEL), lambda j: (0, 0)),
                  pl.BlockSpec((D_MODEL, tn), lambda j: (0, j))],
        out_specs=pl.BlockSpec((n, tn), lambda j: (0, j)),
        out_shape=jax.ShapeDtypeStruct((n, width), F32),
        name="sample_proj",
    )(xs, mod_s, norm_g, w)


def _sample_attn_body(p_ref, s0, s1, s2, qg_ref, kg_ref, a_ref, shifted0, n1, n2):
    bt = p_ref.shape[0]
    s_refs = (s0, s1, s2)
    last = N_BACK - 1
    for b in range(bt):
        os_, ls_ = [], []
        shifted0[b, 0:last] = s0[b, 1:N_BACK]
        for g in range(N_GROUPS):
            row = lambda t: pl.ds((t * N_GROUPS + g) * N_HEADS, N_HEADS)
            q4 = _head_norm(p_ref[b, row(0), :], qg_ref[g:g + 1, :]) * Q_SCALE
            k4 = _head_norm(p_ref[b, row(1), :], kg_ref[g:g + 1, :])
            v4 = p_ref[b, row(2), :]
            if g == 0:
                shifted0[b, last, 0:N_HEADS, :] = k4
                shifted0[b, last, N_HEADS:2 * N_HEADS, :] = v4
            else:
                (n1, n2)[g - 1][b, 0:N_HEADS, :] = k4
                (n1, n2)[g - 1][b, N_HEADS:2 * N_HEADS, :] = v4
            keys = s_refs[g][b, :, 0:N_HEADS, :]
            vals = s_refs[g][b, :, N_HEADS:2 * N_HEADS, :]
            s_old = jnp.sum(keys * q4[None], axis=-1, keepdims=True)
            s_new = jnp.sum(k4 * q4, axis=-1, keepdims=True)
            m = jnp.maximum(jnp.max(s_old, axis=0), s_new)
            p_old = jnp.exp(s_old - m[None])
            p_new = jnp.exp(s_new - m)
            l = jnp.sum(p_old, axis=0) + p_new
            o = jnp.sum(p_old * vals, axis=0) + p_new * v4
            os_.append(o / l)
            ls_.append(m + jnp.log(l))
        mx = jnp.maximum(jnp.maximum(ls_[0], ls_[1]), ls_[2])
        es = [jnp.exp(l_ - mx) for l_ in ls_]
        a_ref[b] = (es[0] * os_[0] + es[1] * os_[1] + es[2] * os_[2]) / (es[0] + es[1] + es[2])


def _sample_attn(p3, views, q_gain, k_gain):
    n = p3.shape[0]
    bt = 8
    in_specs = [pl.BlockSpec((bt, p3.shape[1], HEAD_DIM), lambda j: (j, 0, 0))]
    for v in views:
        in_specs.append(pl.BlockSpec((bt, N_BACK, 2 * N_HEADS, HEAD_DIM), lambda j: (j, 0, 0, 0)))
    in_specs += [pl.BlockSpec((N_GROUPS, HEAD_DIM), lambda j: (0, 0))] * 2
    new_spec = pl.BlockSpec((bt, 2 * N_HEADS, HEAD_DIM), lambda j: (j, 0, 0))
    new_shape = jax.ShapeDtypeStruct((n, 2 * N_HEADS, HEAD_DIM), F32)
    state_spec = pl.BlockSpec((bt, N_BACK, 2 * N_HEADS, HEAD_DIM), lambda j: (j, 0, 0, 0))
    return pl.pallas_call(
        _sample_attn_body,
        grid=(n // bt,),
        in_specs=in_specs,
        out_specs=[pl.BlockSpec((bt, N_HEADS, HEAD_DIM), lambda j: (j, 0, 0)), state_spec, new_spec, new_spec],
        out_shape=[jax.ShapeDtypeStruct((n, N_HEADS, HEAD_DIM), F32),
                   jax.ShapeDtypeStruct(views[0].shape, F32), new_shape, new_shape],
        compiler_params=pltpu.CompilerParams(vmem_limit_bytes=VMEM_LIMIT),
        name="sample_attn",
    )(p3, *views, q_gain, k_gain)


def _sample_out_body(x_ref, mod_ref, r_ref, a_ref, sp_ref, wp_ref, bp_ref, ps_ref, wa_ref, wb_ref, wo_ref,
                     y_ref, pool_ref):
    u = r_ref[:, GROUP_W:2 * GROUP_W]
    mixed = []
    for g, w in enumerate(POOL_WINDOWS):
        lanes = pl.ds(g * HEAD_DIM, HEAD_DIM)
        tok = r_ref[:, pl.ds(GROUP_W + g * HEAD_DIM, HEAD_DIM)]
        acc = tok
        for k in range(1, w):
            acc = acc + sp_ref[POOL_STATE - k, :, lanes]
        mixed.append(acc / float(min(w, PAST_LEN + 1)) - tok)
    p = _pool_project(mixed, wp_ref, bp_ref, ps_ref)
    for k in range(POOL_STATE - 1):
        pool_ref[k] = sp_ref[k + 1]
    pool_ref[POOL_STATE - 1] = u
    y_ref[...] = _merge_out(
        x_ref[...], mod_ref[:, 2 * D_MODEL:3 * D_MODEL], a_ref[...], p,
        r_ref[:, 0:GROUP_W], r_ref[:, 2 * GROUP_W:3 * GROUP_W],
        r_ref[:, 3 * GROUP_W:3 * GROUP_W + D_MODEL], r_ref[:, 3 * GROUP_W + D_MODEL:REST_W],
        wa_ref, wb_ref, wo_ref)


def _sample_out(xs, mod_s, rest, a_s, sp_t, w_pool, b_pool, pool_scale, wa, wb, wo):
    n = xs.shape[0]
    return pl.pallas_call(
        _sample_out_body,
        out_shape=[jax.ShapeDtypeStruct((n, D_MODEL), F32),
                   jax.ShapeDtypeStruct((POOL_STATE, n, GROUP_W), F32)],
        compiler_params=pltpu.CompilerParams(vmem_limit_bytes=VMEM_LIMIT),
        name="sample_out",
    )(xs, mod_s, rest, a_s, sp_t, w_pool, b_pool, pool_scale, wa, wb, wo)


def kernel(x_prompt, x_sample, state_kv_w128, state_kv_w512, state_kv_w2048, state_pool, c_prompt, c_sample,
           norm_g, w_ada, b_ada, w_in, q_gain, k_gain, w_pool, b_pool, pool_scale, w_a_out, w_b_out, w_out):
    B, L, _ = x_prompt.shape
    n_s = x_sample.shape[0]
    w_qkv = w_in[0, :, :QKV_W].astype(BF16)
    w_rest = w_in[0, :, QKV_W:].astype(BF16)
    wa, wb, wo, wp = (w[0].astype(BF16) for w in (w_a_out, w_b_out, w_out, w_pool))
    qg, kg, bp = q_gain[0], k_gain[0], b_pool[0]

    mod = _ada(jnp.concatenate([c_prompt, c_sample], axis=0), w_ada[0], b_ada)
    mod_p = mod[:B].reshape(B, 1, 3 * D_MODEL)
    mod_s = mod[B:]

    xs = x_sample[:, 0, :]
    qkv_s = _sample_proj(xs, mod_s, norm_g, w_qkv)
    rest_s = _sample_proj(xs, mod_s, norm_g, w_rest)
    states = (state_kv_w128, state_kv_w512, state_kv_w2048)
    views = [s.reshape(n_s, N_BACK, d * ROWS_PER_POS, HEAD_DIM) for s, (_, d) in zip(states, ATTN_PATTERNS)]
    a_s, shifted_128, new_512, new_2048 = _sample_attn(
        qkv_s.reshape(n_s, QKV_W // HEAD_DIM, HEAD_DIM), views, qg, kg)
    y_s, pool_t = _sample_out(xs, mod_s, rest_s, a_s.reshape(n_s, GROUP_W), state_pool[0].transpose(1, 0, 2),
                              wp, bp, pool_scale, wa, wb, wo)

    flat_2048, flat_512 = (s.reshape(n_s, s.shape[2] * ROWS_PER_POS, HEAD_DIM) for s in states[:0:-1])
    kinds = _shift_plan([flat_2048.shape, flat_512.shape])
    n_sub = B * (L // ROW_TILE) * SHIFT_SUBSTEPS
    n_chunks = kinds[-1][-1]
    assert n_sub <= kinds[0][4] and n_chunks <= 2 * n_sub
    *qkv, t0, t1, t2, part_2048 = _prompt_qkv(x_prompt, mod_p, norm_g, w_qkv, qg, kg, new_2048, flat_2048,
                                              (kinds, 0, n_sub))
    ol = []
    for g, (d_blk, h_blk) in enumerate(((1, 2), (2, 4), (8, 4))):
        ol += _prompt_attn(qkv[g], qkv[N_GROUPS + g], qkv[2 * N_GROUPS + g], d_blk, h_blk)
    y_p, pool_p, shifted_2048, shifted_512 = _prompt_out(
        x_prompt, mod_p, norm_g, w_rest, ol, wp, bp, pool_scale, wa, wb, wo,
        new_512, flat_2048, flat_512, part_2048, (kinds, n_sub, n_chunks))
    kv_s = [o.reshape(s.shape) for o, s in zip((shifted_128, shifted_512, shifted_2048), states)]
    kv_p = [t.reshape(1, B, t.shape[1], 2, N_HEADS, HEAD_DIM) for t in (t0, t1, t2)]

    return (y_p, y_s.reshape(n_s, 1, D_MODEL), kv_p[0], kv_p[1], kv_p[2], pool_p,
            kv_s[0], kv_s[1], kv_s[2], pool_t.transpose(1, 0, 2)[None])
```

```python
import functools

import jax
import jax.numpy as jnp
from jax import lax
from jax.experimental import pallas as pl
from jax.experimental.pallas import tpu as pltpu

F32 = jnp.float32
BF16 = jnp.bfloat16

D_MODEL = 1024
HEAD_DIM = 128
N_HEADS = 4
GROUP_W = N_HEADS * HEAD_DIM
ATTN_PATTERNS = ((128, 1), (512, 4), (2048, 16))
N_GROUPS = len(ATTN_PATTERNS)
N_BACK = 128
QKV_W = 3 * N_GROUPS * GROUP_W
REST_W = 3 * GROUP_W + 2 * D_MODEL
POOL_WINDOWS = (2, 4, 8, 16)
POOL_STATE = 15
PAST_LEN = 8192
EPS = 1e-6
Q_SCALE = HEAD_DIM ** -0.5
NEG = -1e30

ROW_TILE = 512
Q_BLOCK = 128
Q_UNROLL = 16
VMEM_LIMIT = 56 * 1024 * 1024
HOST_VMEM_LIMIT = 60 * 1024 * 1024


def _sigmoid(v):
    return 0.5 * jnp.tanh(0.5 * v) + 0.5


def _silu(v):
    return v * _sigmoid(v)


def _modulated_norm(x, norm_g, shift, scale):
    ms = jnp.mean(x * x, axis=-1, keepdims=True)
    return (x * lax.rsqrt(ms + EPS) * norm_g * (1.0 + scale) + shift).astype(BF16)


def _head_norm(r, gain):
    ms = jnp.mean(r * r, axis=-1, keepdims=True)
    return r * lax.rsqrt(ms + EPS) * gain


def _ada_body(c_ref, w_ref, b_ref, o_ref):
    s = _silu(c_ref[...]).astype(BF16)
    o_ref[...] = jnp.dot(s, w_ref[...].astype(BF16), preferred_element_type=F32) + b_ref[...]


def _ada(c_all, w_ada, b_ada):
    n = c_all.shape[0]
    return pl.pallas_call(
        _ada_body,
        grid=(3,),
        in_specs=[pl.BlockSpec((n, D_MODEL), lambda j: (0, 0)),
                  pl.BlockSpec((D_MODEL, D_MODEL), lambda j: (0, j)),
                  pl.BlockSpec((1, D_MODEL), lambda j: (0, j))],
        out_specs=pl.BlockSpec((n, D_MODEL), lambda j: (0, j)),
        out_shape=jax.ShapeDtypeStruct((n, 3 * D_MODEL), F32),
        name="ada",
    )(c_all, w_ada, b_ada)


ROWS_PER_POS = 2 * N_HEADS
SHIFT_NB = 8
SHIFT_ROWS = 712
SHIFT_SUBSTEPS = 4
QKV_SHIFT_LEAD = 3
OUT_SHIFT_LEAD = 3
SHIFT_PRIORITY = 1


def _shift_plan(state_shapes):
    kinds, c0 = [], 0
    for idx, (n_batch, rows_total, _) in enumerate(state_shapes):
        keep = rows_total - ROWS_PER_POS
        pieces = min(p for p in range(1, keep)
                     if keep % (p * ROWS_PER_POS) == 0 and keep // p <= SHIFT_ROWS)
        n = (n_batch // SHIFT_NB) * pieces
        kinds.append((idx, keep // pieces, pieces, c0, c0 + n))
        c0 += n
    return kinds


def _shift_substep(j, lo, hi, last_slot, kinds, srcs, dsts, buf, rsem, wsem):
    ahead = buf.shape[0] // 2

    def copy(jj, kind, write):
        idx, rows, pieces, c0, _ = kind
        local = jj - c0
        batches = pl.ds((local // pieces) * SHIFT_NB, SHIFT_NB)
        first = (local % pieces) * rows
        slot = jj % buf.shape[0]
        stage = buf.at[slot, :, pl.ds(0, rows), :]
        if write:
            return pltpu.make_async_copy(stage, dsts[idx].at[batches, pl.ds(first, rows), :], wsem.at[slot])
        return pltpu.make_async_copy(srcs[idx].at[batches, pl.ds(first + ROWS_PER_POS, rows), :], stage, rsem.at[slot])

    spans = [(kind, max(kind[3], lo), min(kind[4], hi)) for kind in kinds]
    spans = [s for s in spans if s[1] < s[2]]

    def for_chunk(jj, also, fn):
        for kind, first, end in spans:
            cond = (jj >= first) & (jj < end)

            @pl.when(cond if also is None else cond & also)
            def _():
                fn(functools.partial(copy, jj, kind))

    steady = []
    for kind, first, end in spans:
        inner = (j >= first + ahead) & (j < end - ahead)
        steady.append(inner)

        @pl.when(inner)
        def _():
            copy(j, kind, False).wait()
            copy(j, kind, True).start(priority=SHIFT_PRIORITY)
            copy(j - ahead, kind, True).wait()
            copy(j + ahead, kind, False).start(priority=SHIFT_PRIORITY)

    @pl.when(jnp.logical_not(functools.reduce(jnp.logical_or, steady)))
    def _():
        for a in range(ahead):
            for_chunk(j + a, j == lo, lambda cp: cp(False).start(priority=SHIFT_PRIORITY))

        def landed(cp):
            cp(False).wait()
            cp(True).start(priority=SHIFT_PRIORITY)

        for_chunk(j, None, landed)
        for_chunk(j - ahead, None, lambda cp: cp(True).wait())
        for_chunk(j + ahead, None, lambda cp: cp(False).start(priority=SHIFT_PRIORITY))
        for a in range(ahead):
            for_chunk(j - a, j == last_slot, lambda cp: cp(True).wait())


def _new_rows_copy(new_ref, dst, nsem):
    keep = dst.shape[1] - ROWS_PER_POS
    return pltpu.make_async_copy(new_ref, dst.at[:, pl.ds(keep, ROWS_PER_POS), :], nsem)


def _shift_scratch(lead):
    return [pltpu.VMEM((2 * lead, SHIFT_NB, SHIFT_ROWS, HEAD_DIM), F32),
            pltpu.SemaphoreType.DMA((2 * lead,)), pltpu.SemaphoreType.DMA((2 * lead,)),
            pltpu.SemaphoreType.DMA(())]


def _class_major_perm(tm, d):
    i = jnp.arange(tm)
    src_row = (i % (tm // d)) * d + i // (tm // d)
    return (src_row[:, None] == jnp.arange(tm)[None, :]).astype(BF16)


SHIFT_SITES = ((0, 1), (1, 1), (2, 0))


def _qkv_body(x_ref, mod_ref, ng_ref, w_ref, qg_ref, kg_ref, perm_ref, new_ref, src_ref, *refs, n_tiles, shift):
    qkv_refs = (refs[0:3], refs[3:6], refs[6:9])
    tail_refs = refs[9:12]
    dst_ref, buf, rsem, wsem, nsem = refs[12:17]
    i = pl.program_id(1)
    tm = x_ref.shape[1]
    step = pl.program_id(0) * n_tiles + i
    kinds, lo, hi = shift

    last_slot = lo + SHIFT_SUBSTEPS * pl.num_programs(0) * n_tiles - 1

    def shift_substep(k):
        _shift_substep(lo + SHIFT_SUBSTEPS * step + k, lo, hi, last_slot, kinds, [src_ref], [dst_ref],
                       buf, rsem, wsem)

    shift_substep(0)

    @pl.when(step == 0)
    def _():
        _new_rows_copy(new_ref, dst_ref, nsem).start()

    h = _modulated_norm(x_ref[0], ng_ref[...], mod_ref[0, :, 0:D_MODEL], mod_ref[0, :, D_MODEL:2 * D_MODEL])
    h_by_group = [h] + [jnp.dot(perm_ref[g - 1], h, preferred_element_type=F32).astype(BF16)
                        for g in range(1, N_GROUPS)]

    def heads(t, g, lhs):
        c = t * N_GROUPS + g
        res = jnp.dot(lhs, w_ref[:, c * GROUP_W:(c + 1) * GROUP_W], preferred_element_type=F32)
        out = []
        for hh in range(N_HEADS):
            r = res[:, hh * HEAD_DIM:(hh + 1) * HEAD_DIM]
            if t == 0:
                r = _head_norm(r, qg_ref[g:g + 1, :]) * Q_SCALE
            elif t == 1:
                r = _head_norm(r, kg_ref[g:g + 1, :])
            out.append(r)
        return out

    def write_tail(t, g, rs):
        window, d = ATTN_PATTERNS[g]
        per = tm // d
        for hh, r in enumerate(rs):
            row = (t - 1) * N_HEADS + hh
            if d == 1:
                tail_refs[g][0, :, row, :] = r if window >= tm else r[tm - window:, :]
            else:
                assert window >= tm
                for rr in range(d):
                    tail_refs[g][0, pl.ds(rr, per, stride=d), row, :] = r[rr * per:(rr + 1) * per, :]

    kept = {}
    for t in range(3):
        for g, (window, d) in enumerate(ATTN_PATTERNS):
            if (t, g) in SHIFT_SITES:
                shift_substep(SHIFT_SITES.index((t, g)) + 1)
            if (t, g) == SHIFT_SITES[0]:
                @pl.when(step == 0)
                def _():
                    _new_rows_copy(new_ref, dst_ref, nsem).wait()

            rs = heads(t, g, h_by_group[g])
            per = tm // d
            for hh, r in enumerate(rs):
                for rr in range(d):
                    qkv_refs[t][g][0, rr, hh] = r[rr * per:(rr + 1) * per, :].astype(BF16)
            if t > 0:
                kept.setdefault(max(window // tm, 1), []).append((t, g, rs))

    for n_tail, items in sorted(kept.items()):
        @pl.when(i >= n_tiles - n_tail)
        def _():
            for t, g, rs in items:
                write_tail(t, g, rs)


def _prompt_qkv(x, mod_p, norm_g, w_qkv, q_gain, k_gain, new_rows, state, shift):
    B, L, _ = x.shape
    tm = ROW_TILE
    n_tiles = L // tm
    const2 = lambda b, i: (0, 0)
    in_specs = [
        pl.BlockSpec((1, tm, D_MODEL), lambda b, i: (b, i, 0)),
        pl.BlockSpec((1, 1, 3 * D_MODEL), lambda b, i: (b, 0, 0)),
        pl.BlockSpec((1, D_MODEL), const2),
        pl.BlockSpec((D_MODEL, QKV_W), const2, pipeline_mode=pl.Buffered(1)),
        pl.BlockSpec((N_GROUPS, HEAD_DIM), const2),
        pl.BlockSpec((N_GROUPS, HEAD_DIM), const2),
        pl.BlockSpec((N_GROUPS - 1, tm, tm), lambda b, i: (0, 0, 0), pipeline_mode=pl.Buffered(1)),
        pl.BlockSpec(new_rows.shape, lambda b, i: (0, 0, 0), pipeline_mode=pl.Buffered(1)),
        pl.BlockSpec(memory_space=pl.ANY),
    ]
    perms = jnp.stack([_class_major_perm(tm, d) for _, d in ATTN_PATTERNS[1:]])
    out_specs, out_shape = [], []
    for t in range(3):
        for window, d in ATTN_PATTERNS:
            out_specs.append(pl.BlockSpec((1, d, N_HEADS, tm // d, HEAD_DIM), lambda b, i: (b, 0, 0, i, 0)))
            out_shape.append(jax.ShapeDtypeStruct((B, d, N_HEADS, L // d, HEAD_DIM), BF16))
    for window, d in ATTN_PATTERNS:
        w_eff = min(window, L)
        if w_eff >= tm:
            n_tail = w_eff // tm
            out_specs.append(pl.BlockSpec(
                (1, tm, 2 * N_HEADS, HEAD_DIM),
                lambda b, i, n_tail=n_tail: (b, jnp.maximum(i - (n_tiles - n_tail), 0), 0, 0)))
        else:
            out_specs.append(pl.BlockSpec((1, w_eff, 2 * N_HEADS, HEAD_DIM), lambda b, i: (b, 0, 0, 0)))
        out_shape.append(jax.ShapeDtypeStruct((B, w_eff, 2 * N_HEADS, HEAD_DIM), F32))
    out_specs.append(pl.BlockSpec(memory_space=pl.ANY))
    out_shape.append(jax.ShapeDtypeStruct(state.shape, F32))
    return pl.pallas_call(
        functools.partial(_qkv_body, n_tiles=n_tiles, shift=shift),
        grid=(B, n_tiles),
        in_specs=in_specs,
        out_specs=out_specs,
        out_shape=out_shape,
        scratch_shapes=_shift_scratch(QKV_SHIFT_LEAD),
        compiler_params=pltpu.CompilerParams(
            dimension_semantics=("arbitrary", "arbitrary"), vmem_limit_bytes=HOST_VMEM_LIMIT),
        name="prompt_qkv",
    )(x, mod_p, norm_g, w_qkv, q_gain, k_gain, perms, new_rows, state)


def _attn_body(q_ref, k_ref, v_ref, o_ref, l_ref, vx):
    _, d_blk, h_blk, M, _ = q_ref.shape
    n_q = M // Q_BLOCK
    for c in range(d_blk):
        for hh in range(h_blk):
            vx[c, hh, :, 0:HEAD_DIM] = v_ref[0, c, hh]
            vx[c, hh, :, HEAD_DIM:2 * HEAD_DIM] = jnp.ones((M, HEAD_DIM), BF16)
    ii = lax.broadcasted_iota(jnp.int32, (Q_BLOCK, 2 * Q_BLOCK), 0)
    jj = lax.broadcasted_iota(jnp.int32, (Q_BLOCK, 2 * Q_BLOCK), 1)
    dist = Q_BLOCK + ii - jj
    band_ok = (dist >= 0) & (dist <= N_BACK)
    causal_ok = (lax.broadcasted_iota(jnp.int32, (Q_BLOCK, Q_BLOCK), 1)
                 <= lax.broadcasted_iota(jnp.int32, (Q_BLOCK, Q_BLOCK), 0))
    nt = (((1,), (1,)), ((), ()))

    def block(c, hh, n, first):
        rows = pl.ds(pl.multiple_of(n * Q_BLOCK, Q_BLOCK), Q_BLOCK)
        if first:
            keys, ok = rows, causal_ok
        else:
            keys, ok = pl.ds(pl.multiple_of((n - 1) * Q_BLOCK, Q_BLOCK), 2 * Q_BLOCK), band_ok
        s = lax.dot_general(q_ref[0, c, hh, rows, :], k_ref[0, c, hh, keys, :], nt, preferred_element_type=F32)
        s = jnp.where(ok, s, NEG)
        m = jnp.max(s, axis=-1, keepdims=True)
        p = jnp.exp(s - m).astype(BF16)
        oe = jnp.dot(p, vx[c, hh, keys, :], preferred_element_type=F32)
        l = oe[:, HEAD_DIM:]
        o_ref[0, c, hh, rows, :] = (oe[:, :HEAD_DIM] / l).astype(BF16)
        return m + jnp.log(l)

    low_half = lax.broadcasted_iota(jnp.int32, (Q_BLOCK, HEAD_DIM), 1) < HEAD_DIM // 2

    def blocks(c, n, first):
        rows = pl.ds(pl.multiple_of(n * Q_BLOCK, Q_BLOCK), Q_BLOCK)
        for pair in range(h_blk // 2):
            lse = [block(c, 2 * pair + k, n, first) for k in range(2)]
            l_ref[0, c, pair, rows, :] = jnp.where(low_half, lse[0], lse[1])

    for c in range(d_blk):
        blocks(c, 0, True)
        if n_q <= 2:
            for n in range(1, n_q):
                blocks(c, n, False)
        else:
            def step(n, carry, c=c):
                blocks(c, n, False)
                return carry

            lax.fori_loop(1, n_q, step, 0, unroll=Q_UNROLL)


def _prompt_attn(q, k, v, d_blk, h_blk):
    B, d, H, M, E = q.shape
    blk = (1, d_blk, h_blk, M, E)
    spec = pl.BlockSpec(blk, lambda b, c, hh: (b, c, hh, 0, 0))
    return pl.pallas_call(
        _attn_body,
        grid=(B, d // d_blk, H // h_blk),
        in_specs=[spec, spec, spec],
        out_specs=[spec, pl.BlockSpec((1, d_blk, h_blk // 2, M, E), lambda b, c, hh: (b, c, hh, 0, 0))],
        out_shape=[jax.ShapeDtypeStruct(q.shape, BF16), jax.ShapeDtypeStruct((B, d, H // 2, M, E), F32)],
        scratch_shapes=[pltpu.VMEM((d_blk, h_blk, M, 2 * E), BF16)],
        compiler_params=pltpu.CompilerParams(
            dimension_semantics=("arbitrary", "arbitrary", "arbitrary"), vmem_limit_bytes=VMEM_LIMIT),
        name="prompt_attn_d%d" % d,
    )(q, k, v)


def _pool_project(mixed, wp_ref, bp_ref, ps_ref):
    cols = []
    for g in range(len(POOL_WINDOWS)):
        lanes = slice(g * HEAD_DIM, (g + 1) * HEAD_DIM)
        y = jnp.dot(mixed[g].astype(BF16), wp_ref[g], preferred_element_type=F32) + bp_ref[g:g + 1, :]
        cols.append(y * ps_ref[:, lanes])
    return jnp.concatenate(cols, axis=-1)


def _merge_out(x, gate, a, p, z_a, z_b, g_a, g_b, wa_ref, wb_ref, wo_ref):
    ya = jnp.dot((a * _silu(z_a)).astype(BF16), wa_ref[...], preferred_element_type=F32)
    yb = jnp.dot((p * _silu(z_b)).astype(BF16), wb_ref[...], preferred_element_type=F32)
    m = _sigmoid(g_a) * ya + _sigmoid(g_b) * yb
    return x + gate * jnp.dot(m.astype(BF16), wo_ref[...], preferred_element_type=F32)


def _out_body(x_ref, mod_ref, ng_ref, w_ref, o0, l0, o1, l1, o2, l2, wp_ref, bp_ref, ps_ref,
              wa_ref, wb_ref, wo_ref, new_ref, src_a, src_b, dst_a_in, y_ref, pool_ref, dst_a, dst_b,
              ext, oscr, lscr, buf, rsem, wsem, nsem, *, n_tiles, shift):
    del dst_a_in
    i = pl.program_id(1)
    tm = x_ref.shape[1]
    step = pl.program_id(0) * n_tiles + i
    kinds, lo, hi = shift

    last_slot = lo + SHIFT_SUBSTEPS * pl.num_programs(0) * n_tiles - 1

    def shift_substep(k):
        _shift_substep(lo + SHIFT_SUBSTEPS * step + k, lo, hi, last_slot, kinds, [src_a, src_b], [dst_a, dst_b],
                       buf, rsem, wsem)

    shift_substep(0)

    @pl.when(step == 0)
    def _():
        _new_rows_copy(new_ref, dst_b, nsem).start()
        ext[0:16, :] = jnp.zeros((16, GROUP_W), F32)

    x = x_ref[0]
    h = _modulated_norm(x, ng_ref[...], mod_ref[0, :, 0:D_MODEL], mod_ref[0, :, D_MODEL:2 * D_MODEL])
    gate = mod_ref[0, :, 2 * D_MODEL:3 * D_MODEL]

    def proj(lo, width):
        return jnp.dot(h, w_ref[:, lo:lo + width], preferred_element_type=F32)

    z_a = proj(0, GROUP_W)
    u = proj(GROUP_W, GROUP_W)
    z_b = proj(2 * GROUP_W, GROUP_W)
    o_refs, l_refs = (o0, o1, o2), (l0, l1, l2)
    half = HEAD_DIM // 2

    def head_lse(g, c, hh):
        packed = l_refs[g][0, c, hh // 2]
        low = lax.broadcasted_iota(jnp.int32, packed.shape, 1) < half
        return jnp.where(low == (hh % 2 == 0), packed, pltpu.roll(packed, half, axis=1))

    for g, (window, d) in enumerate(ATTN_PATTERNS):
        if d == 1:
            continue
        for hh in range(N_HEADS):
            for rr in range(d):
                rows = pl.ds(rr, tm // d, stride=d)
                oscr[g - 1, hh, rows, :] = o_refs[g][0, rr, hh].astype(F32)
                lscr[g - 1, hh, rows, :] = head_lse(g, rr, hh)
    a_cols = []
    for hh in range(N_HEADS):
        os_ = [o0[0, 0, hh].astype(F32), oscr[0, hh], oscr[1, hh]]
        ls_ = [head_lse(0, 0, hh), lscr[0, hh], lscr[1, hh]]
        mx = jnp.maximum(jnp.maximum(ls_[0], ls_[1]), ls_[2])
        es = [jnp.exp(l_ - mx) for l_ in ls_]
        den = es[0] + es[1] + es[2]
        a_cols.append((es[0] * os_[0] + es[1] * os_[1] + es[2] * os_[2]) / den)
    ya_in = (jnp.concatenate(a_cols, axis=-1) * _silu(z_a)).astype(BF16)
    shift_substep(1)

    @pl.when(step == 0)
    def _():
        _new_rows_copy(new_ref, dst_b, nsem).wait()

    g_a = proj(3 * GROUP_W, D_MODEL)
    ext[16:16 + tm, :] = u
    pos = i * tm + lax.broadcasted_iota(jnp.int32, (tm, 1), 0)
    mixed = []
    for g, w in enumerate(POOL_WINDOWS):
        lanes = pl.ds(g * HEAD_DIM, HEAD_DIM)
        acc = ext[pl.ds(16, tm), lanes]
        tok = acc
        for k in range(1, w):
            acc = acc + ext[pl.ds(16 - k, tm), lanes]
        cnt = jnp.minimum(w, pos + 1).astype(F32)
        mixed.append(acc / cnt - tok)
    pool_ref[0, 0] = ext[pl.ds(tm + 1, POOL_STATE), :]
    ext[0:16, :] = jnp.where(i == n_tiles - 1, 0.0, ext[tm:tm + 16, :])
    yb_in = (_pool_project(mixed, wp_ref, bp_ref, ps_ref) * _silu(z_b)).astype(BF16)
    shift_substep(2)

    g_b = proj(3 * GROUP_W + D_MODEL, D_MODEL)
    ya = jnp.dot(ya_in, wa_ref[...], preferred_element_type=F32)
    yb = jnp.dot(yb_in, wb_ref[...], preferred_element_type=F32)
    m = (_sigmoid(g_a) * ya + _sigmoid(g_b) * yb).astype(BF16)
    shift_substep(3)
    y_ref[0] = x + gate * jnp.dot(m, wo_ref[...], preferred_element_type=F32)


def _prompt_out(x, mod_p, norm_g, w_rest, ol, w_pool, b_pool, pool_scale, wa, wb, wo,
                new_rows, state_a, state_b, shifted_a, shift):
    B, L, _ = x.shape
    tm = ROW_TILE
    n_tiles = L // tm
    const2 = lambda b, i: (0, 0)
    one = pl.Buffered(1)
    in_specs = [
        pl.BlockSpec((1, tm, D_MODEL), lambda b, i: (b, i, 0)),
        pl.BlockSpec((1, 1, 3 * D_MODEL), lambda b, i: (b, 0, 0)),
        pl.BlockSpec((1, D_MODEL), const2),
        pl.BlockSpec((D_MODEL, REST_W), const2, pipeline_mode=one),
    ]
    for window, d in ATTN_PATTERNS:
        for heads in (N_HEADS, N_HEADS // 2):
            in_specs.append(pl.BlockSpec((1, d, heads, tm // d, HEAD_DIM), lambda b, i: (b, 0, 0, i, 0)))
    in_specs += [
        pl.BlockSpec((len(POOL_WINDOWS), HEAD_DIM, HEAD_DIM), lambda b, i: (0, 0, 0)),
        pl.BlockSpec((len(POOL_WINDOWS), HEAD_DIM), const2),
        pl.BlockSpec((1, GROUP_W), const2),
        pl.BlockSpec((GROUP_W, D_MODEL), const2, pipeline_mode=one),
        pl.BlockSpec((GROUP_W, D_MODEL), const2, pipeline_mode=one),
        pl.BlockSpec((D_MODEL, D_MODEL), const2, pipeline_mode=one),
        pl.BlockSpec(new_rows.shape, lambda b, i: (0, 0, 0), pipeline_mode=one),
    ]
    any_spec = pl.BlockSpec(memory_space=pl.ANY)
    in_specs += [any_spec] * 3
    return pl.pallas_call(
        functools.partial(_out_body, n_tiles=n_tiles, shift=shift),
        grid=(B, n_tiles),
        in_specs=in_specs,
        out_specs=[pl.BlockSpec((1, tm, D_MODEL), lambda b, i: (b, i, 0)),
                   pl.BlockSpec((1, 1, POOL_STATE, GROUP_W), lambda b, i: (0, b, 0, 0)),
                   any_spec, any_spec],
        out_shape=[jax.ShapeDtypeStruct((B, L, D_MODEL), F32),
                   jax.ShapeDtypeStruct((1, B, POOL_STATE, GROUP_W), F32),
                   jax.ShapeDtypeStruct(state_a.shape, F32),
                   jax.ShapeDtypeStruct(state_b.shape, F32)],
        scratch_shapes=[pltpu.VMEM((tm + 16, GROUP_W), F32),
                        pltpu.VMEM((2, N_HEADS, tm, HEAD_DIM), F32),
                        pltpu.VMEM((2, N_HEADS, tm, HEAD_DIM), F32)] + _shift_scratch(OUT_SHIFT_LEAD),
        input_output_aliases={len(in_specs) - 1: 2},
        compiler_params=pltpu.CompilerParams(
            dimension_semantics=("arbitrary", "arbitrary"), vmem_limit_bytes=HOST_VMEM_LIMIT),
        name="prompt_out",
    )(x, mod_p, norm_g, w_rest, *ol, w_pool, b_pool, pool_scale, wa, wb, wo,
      new_rows, state_a, state_b, shifted_a)


def _sample_proj_body(x_ref, mod_ref, ng_ref, w_ref, o_ref):
    h = _modulated_norm(x_ref[...], ng_ref[...], mod_ref[:, 0:D_MODEL], mod_ref[:, D_MODEL:2 * D_MODEL])
    o_ref[...] = jnp.dot(h, w_ref[...], preferred_element_type=F32)


def _sample_proj(xs, mod_s, norm_g, w):
    n, width = xs.shape[0], w.shape[1]
    tn = max(t for t in range(HEAD_DIM, width // 2 + 1, HEAD_DIM) if width % t == 0)
    return pl.pallas_call(
        _sample_proj_body,
        grid=(width // tn,),
        in_specs=[pl.BlockSpec((n, D_MODEL), lambda j: (0, 0)),
                  pl.BlockSpec((n, 3 * D_MODEL), lambda j: (0, 0)),
                  pl.BlockSpec((1, D_MODEL), lambda j: (0, 0)),
                  pl.BlockSpec((D_MODEL, tn), lambda j: (0, j))],
        out_specs=pl.BlockSpec((n, tn), lambda j: (0, j)),
        out_shape=jax.ShapeDtypeStruct((n, width), F32),
        name="sample_proj",
    )(xs, mod_s, norm_g, w)


def _sample_attn_body(p_ref, s0, s1, s2, qg_ref, kg_ref, a_ref, shifted0, n1, n2):
    bt = p_ref.shape[0]
    s_refs = (s0, s1, s2)
    last = N_BACK - 1
    for b in range(bt):
        os_, ls_ = [], []
        shifted0[b, 0:last] = s0[b, 1:N_BACK]
        for g in range(N_GROUPS):
            row = lambda t: pl.ds((t * N_GROUPS + g) * N_HEADS, N_HEADS)
            q4 = _head_norm(p_ref[b, row(0), :], qg_ref[g:g + 1, :]) * Q_SCALE
            k4 = _head_norm(p_ref[b, row(1), :], kg_ref[g:g + 1, :])
            v4 = p_ref[b, row(2), :]
            if g == 0:
                shifted0[b, last, 0:N_HEADS, :] = k4
                shifted0[b, last, N_HEADS:2 * N_HEADS, :] = v4
            else:
                (n1, n2)[g - 1][b, 0:N_HEADS, :] = k4
                (n1, n2)[g - 1][b, N_HEADS:2 * N_HEADS, :] = v4
            keys = s_refs[g][b, :, 0:N_HEADS, :]
            vals = s_refs[g][b, :, N_HEADS:2 * N_HEADS, :]
            s_old = jnp.sum(keys * q4[None], axis=-1, keepdims=True)
            s_new = jnp.sum(k4 * q4, axis=-1, keepdims=True)
            m = jnp.maximum(jnp.max(s_old, axis=0), s_new)
            p_old = jnp.exp(s_old - m[None])
            p_new = jnp.exp(s_new - m)
            l = jnp.sum(p_old, axis=0) + p_new
            o = jnp.sum(p_old * vals, axis=0) + p_new * v4
            os_.append(o / l)
            ls_.append(m + jnp.log(l))
        mx = jnp.maximum(jnp.maximum(ls_[0], ls_[1]), ls_[2])
        es = [jnp.exp(l_ - mx) for l_ in ls_]
        a_ref[b] = (es[0] * os_[0] + es[1] * os_[1] + es[2] * os_[2]) / (es[0] + es[1] + es[2])


def _sample_attn(p3, views, q_gain, k_gain):
    n = p3.shape[0]
    bt = 8
    in_specs = [pl.BlockSpec((bt, p3.shape[1], HEAD_DIM), lambda j: (j, 0, 0))]
    for v in views:
        in_specs.append(pl.BlockSpec((bt, N_BACK, 2 * N_HEADS, HEAD_DIM), lambda j: (j, 0, 0, 0)))
    in_specs += [pl.BlockSpec((N_GROUPS, HEAD_DIM), lambda j: (0, 0))] * 2
    new_spec = pl.BlockSpec((bt, 2 * N_HEADS, HEAD_DIM), lambda j: (j, 0, 0))
    new_shape = jax.ShapeDtypeStruct((n, 2 * N_HEADS, HEAD_DIM), F32)
    state_spec = pl.BlockSpec((bt, N_BACK, 2 * N_HEADS, HEAD_DIM), lambda j: (j, 0, 0, 0))
    return pl.pallas_call(
        _sample_attn_body,
        grid=(n // bt,),
        in_specs=in_specs,
        out_specs=[pl.BlockSpec((bt, N_HEADS, HEAD_DIM), lambda j: (j, 0, 0)), state_spec, new_spec, new_spec],
        out_shape=[jax.ShapeDtypeStruct((n, N_HEADS, HEAD_DIM), F32),
                   jax.ShapeDtypeStruct(views[0].shape, F32), new_shape, new_shape],
        compiler_params=pltpu.CompilerParams(vmem_limit_bytes=VMEM_LIMIT),
        name="sample_attn",
    )(p3, *views, q_gain, k_gain)


def _sample_out_body(x_ref, mod_ref, r_ref, a_ref, sp_ref, wp_ref, bp_ref, ps_ref, wa_ref, wb_ref, wo_ref,
                     y_ref, pool_ref):
    u = r_ref[:, GROUP_W:2 * GROUP_W]
    mixed = []
    for g, w in enumerate(POOL_WINDOWS):
        lanes = pl.ds(g * HEAD_DIM, HEAD_DIM)
        tok = r_ref[:, pl.ds(GROUP_W + g * HEAD_DIM, HEAD_DIM)]
        acc = tok
        for k in range(1, w):
            acc = acc + sp_ref[POOL_STATE - k, :, lanes]
        mixed.append(acc / float(min(w, PAST_LEN + 1)) - tok)
    p = _pool_project(mixed, wp_ref, bp_ref, ps_ref)
    for k in range(POOL_STATE - 1):
        pool_ref[k] = sp_ref[k + 1]
    pool_ref[POOL_STATE - 1] = u
    y_ref[...] = _merge_out(
        x_ref[...], mod_ref[:, 2 * D_MODEL:3 * D_MODEL], a_ref[...], p,
        r_ref[:, 0:GROUP_W], r_ref[:, 2 * GROUP_W:3 * GROUP_W],
        r_ref[:, 3 * GROUP_W:3 * GROUP_W + D_MODEL], r_ref[:, 3 * GROUP_W + D_MODEL:REST_W],
        wa_ref, wb_ref, wo_ref)


def _sample_out(xs, mod_s, rest, a_s, sp_t, w_pool, b_pool, pool_scale, wa, wb, wo):
    n = xs.shape[0]
    return pl.pallas_call(
        _sample_out_body,
        out_shape=[jax.ShapeDtypeStruct((n, D_MODEL), F32),
                   jax.ShapeDtypeStruct((POOL_STATE, n, GROUP_W), F32)],
        compiler_params=pltpu.CompilerParams(vmem_limit_bytes=VMEM_LIMIT),
        name="sample_out",
    )(xs, mod_s, rest, a_s, sp_t, w_pool, b_pool, pool_scale, wa, wb, wo)


def kernel(x_prompt, x_sample, state_kv_w128, state_kv_w512, state_kv_w2048, state_pool, c_prompt, c_sample,
           norm_g, w_ada, b_ada, w_in, q_gain, k_gain, w_pool, b_pool, pool_scale, w_a_out, w_b_out, w_out):
    B, L, _ = x_prompt.shape
    n_s = x_sample.shape[0]
    w_qkv = w_in[0, :, :QKV_W].astype(BF16)
    w_rest = w_in[0, :, QKV_W:].astype(BF16)
    wa, wb, wo, wp = (w[0].astype(BF16) for w in (w_a_out, w_b_out, w_out, w_pool))
    qg, kg, bp = q_gain[0], k_gain[0], b_pool[0]

    mod = _ada(jnp.concatenate([c_prompt, c_sample], axis=0), w_ada[0], b_ada)
    mod_p = mod[:B].reshape(B, 1, 3 * D_MODEL)
    mod_s = mod[B:]

    xs = x_sample[:, 0, :]
    qkv_s = _sample_proj(xs, mod_s, norm_g, w_qkv)
    rest_s = _sample_proj(xs, mod_s, norm_g, w_rest)
    states = (state_kv_w128, state_kv_w512, state_kv_w2048)
    views = [s.reshape(n_s, N_BACK, d * ROWS_PER_POS, HEAD_DIM) for s, (_, d) in zip(states, ATTN_PATTERNS)]
    a_s, shifted_128, new_512, new_2048 = _sample_attn(
        qkv_s.reshape(n_s, QKV_W // HEAD_DIM, HEAD_DIM), views, qg, kg)
    y_s, pool_t = _sample_out(xs, mod_s, rest_s, a_s.reshape(n_s, GROUP_W), state_pool[0].transpose(1, 0, 2),
                              wp, bp, pool_scale, wa, wb, wo)

    flat_2048, flat_512 = (s.reshape(n_s, s.shape[2] * ROWS_PER_POS, HEAD_DIM) for s in states[:0:-1])
    kinds = _shift_plan([flat_2048.shape, flat_512.shape])
    n_sub = B * (L // ROW_TILE) * SHIFT_SUBSTEPS
    n_chunks = kinds[-1][-1]
    assert n_sub <= kinds[0][4] and n_chunks <= 2 * n_sub
    *qkv, t0, t1, t2, part_2048 = _prompt_qkv(x_prompt, mod_p, norm_g, w_qkv, qg, kg, new_2048, flat_2048,
                                              (kinds, 0, n_sub))
    ol = []
    for g, (d_blk, h_blk) in enumerate(((1, 2), (2, 4), (8, 4))):
        ol += _prompt_attn(qkv[g], qkv[N_GROUPS + g], qkv[2 * N_GROUPS + g], d_blk, h_blk)
    y_p, pool_p, shifted_2048, shifted_512 = _prompt_out(
        x_prompt, mod_p, norm_g, w_rest, ol, wp, bp, pool_scale, wa, wb, wo,
        new_512, flat_2048, flat_512, part_2048, (kinds, n_sub, n_chunks))
    kv_s = [o.reshape(s.shape) for o, s in zip((shifted_128, shifted_512, shifted_2048), states)]
    kv_p = [t.reshape(1, B, t.shape[1], 2, N_HEADS, HEAD_DIM) for t in (t0, t1, t2)]

    return (y_p, y_s.reshape(n_s, 1, D_MODEL), kv_p[0], kv_p[1], kv_p[2], pool_p,
            kv_s[0], kv_s[1], kv_s[2], pool_t.transpose(1, 0, 2)[None])
```

```python
import functools

import jax
import jax.numpy as jnp
from jax import lax
from jax.experimental import pallas as pl
from jax.experimental.pallas import tpu as pltpu

F32 = jnp.float32
BF16 = jnp.bfloat16

D_MODEL = 1024
HEAD_DIM = 128
N_HEADS = 4
GROUP_W = N_HEADS * HEAD_DIM
ATTN_PATTERNS = ((128, 1), (512, 4), (2048, 16))
N_GROUPS = len(ATTN_PATTERNS)
N_BACK = 128
QKV_W = 3 * N_GROUPS * GROUP_W
REST_W = 3 * GROUP_W + 2 * D_MODEL
POOL_WINDOWS = (2, 4, 8, 16)
POOL_STATE = 15
PAST_LEN = 8192
EPS = 1e-6
Q_SCALE = HEAD_DIM ** -0.5
NEG = -1e30

ROW_TILE = 512
Q_BLOCK = 128
Q_UNROLL = 16
VMEM_LIMIT = 56 * 1024 * 1024
HOST_VMEM_LIMIT = 60 * 1024 * 1024


def _sigmoid(v):
    return 0.5 * jnp.tanh(0.5 * v) + 0.5


def _silu(v):
    return v * _sigmoid(v)


def _modulated_norm(x, norm_g, shift, scale):
    ms = jnp.mean(x * x, axis=-1, keepdims=True)
    return (x * lax.rsqrt(ms + EPS) * norm_g * (1.0 + scale) + shift).astype(BF16)


def _head_norm(r, gain):
    ms = jnp.mean(r * r, axis=-1, keepdims=True)
    return r * lax.rsqrt(ms + EPS) * gain


def _ada_body(c_ref, w_ref, b_ref, o_ref):
    s = _silu(c_ref[...]).astype(BF16)
    o_ref[...] = jnp.dot(s, w_ref[...].astype(BF16), preferred_element_type=F32) + b_ref[...]


def _ada(c_all, w_ada, b_ada):
    n = c_all.shape[0]
    return pl.pallas_call(
        _ada_body,
        grid=(3,),
        in_specs=[pl.BlockSpec((n, D_MODEL), lambda j: (0, 0)),
                  pl.BlockSpec((D_MODEL, D_MODEL), lambda j: (0, j)),
                  pl.BlockSpec((1, D_MODEL), lambda j: (0, j))],
        out_specs=pl.BlockSpec((n, D_MODEL), lambda j: (0, j)),
        out_shape=jax.ShapeDtypeStruct((n, 3 * D_MODEL), F32),
        name="ada",
    )(c_all, w_ada, b_ada)


ROWS_PER_POS = 2 * N_HEADS
SHIFT_NB = 8
SHIFT_ROWS = 712
SHIFT_SUBSTEPS = 4
QKV_SHIFT_LEAD = 3
OUT_SHIFT_LEAD = 3
SHIFT_PRIORITY = 1


def _shift_plan(state_shapes):
    kinds, c0 = [], 0
    for idx, (n_batch, rows_total, _) in enumerate(state_shapes):
        keep = rows_total - ROWS_PER_POS
        pieces = min(p for p in range(1, keep)
                     if keep % (p * ROWS_PER_POS) == 0 and keep // p <= SHIFT_ROWS)
        n = (n_batch // SHIFT_NB) * pieces
        kinds.append((idx, keep // pieces, pieces, c0, c0 + n))
        c0 += n
    return kinds


def _shift_substep(j, lo, hi, last_slot, kinds, srcs, dsts, buf, rsem, wsem):
    ahead = buf.shape[0] // 2

    def copy(jj, kind, write):
        idx, rows, pieces, c0, _ = kind
        local = jj - c0
        batches = pl.ds((local // pieces) * SHIFT_NB, SHIFT_NB)
        first = (local % pieces) * rows
        slot = jj % buf.shape[0]
        stage = buf.at[slot, :, pl.ds(0, rows), :]
        if write:
            return pltpu.make_async_copy(stage, dsts[idx].at[batches, pl.ds(first, rows), :], wsem.at[slot])
        return pltpu.make_async_copy(srcs[idx].at[batches, pl.ds(first + ROWS_PER_POS, rows), :], stage, rsem.at[slot])

    spans = [(kind, max(kind[3], lo), min(kind[4], hi)) for kind in kinds]
    spans = [s for s in spans if s[1] < s[2]]

    def for_chunk(jj, also, fn):
        for kind, first, end in spans:
            cond = (jj >= first) & (jj < end)

            @pl.when(cond if also is None else cond & also)
            def _():
                fn(functools.partial(copy, jj, kind))

    steady = []
    for kind, first, end in spans:
        inner = (j >= first + ahead) & (j < end - ahead)
        steady.append(inner)

        @pl.when(inner)
        def _():
            copy(j, kind, False).wait()
            copy(j, kind, True).start(priority=SHIFT_PRIORITY)
            copy(j - ahead, kind, True).wait()
            copy(j + ahead, kind, False).start(priority=SHIFT_PRIORITY)

    @pl.when(jnp.logical_not(functools.reduce(jnp.logical_or, steady)))
    def _():
        for a in range(ahead):
            for_chunk(j + a, j == lo, lambda cp: cp(False).start(priority=SHIFT_PRIORITY))

        def landed(cp):
            cp(False).wait()
            cp(True).start(priority=SHIFT_PRIORITY)

        for_chunk(j, None, landed)
        for_chunk(j - ahead, None, lambda cp: cp(True).wait())
        for_chunk(j + ahead, None, lambda cp: cp(False).start(priority=SHIFT_PRIORITY))
        for a in range(ahead):
            for_chunk(j - a, j == last_slot, lambda cp: cp(True).wait())


def _new_rows_copy(new_ref, dst, nsem):
    keep = dst.shape[1] - ROWS_PER_POS
    return pltpu.make_async_copy(new_ref, dst.at[:, pl.ds(keep, ROWS_PER_POS), :], nsem)


def _shift_scratch(lead):
    return [pltpu.VMEM((2 * lead, SHIFT_NB, SHIFT_ROWS, HEAD_DIM), F32),
            pltpu.SemaphoreType.DMA((2 * lead,)), pltpu.SemaphoreType.DMA((2 * lead,)),
            pltpu.SemaphoreType.DMA(())]


def _class_major_perm(tm, d):
    i = jnp.arange(tm)
    src_row = (i % (tm // d)) * d + i // (tm // d)
    return (src_row[:, None] == jnp.arange(tm)[None, :]).astype(BF16)


SHIFT_SITES = ((0, 1), (1, 1), (2, 0))


def _qkv_body(x_ref, mod_ref, ng_ref, w_ref, qg_ref, kg_ref, perm_ref, new_ref, src_ref, *refs, n_tiles, shift):
    qkv_refs = (refs[0:3], refs[3:6], refs[6:9])
    tail_refs = refs[9:12]
    dst_ref, buf, rsem, wsem, nsem = refs[12:17]
    i = pl.program_id(1)
    tm = x_ref.shape[1]
    step = pl.program_id(0) * n_tiles + i
    kinds, lo, hi = shift

    last_slot = lo + SHIFT_SUBSTEPS * pl.num_programs(0) * n_tiles - 1

    def shift_substep(k):
        _shift_substep(lo + SHIFT_SUBSTEPS * step + k, lo, hi, last_slot, kinds, [src_ref], [dst_ref],
                       buf, rsem, wsem)

    shift_substep(0)

    @pl.when(step == 0)
    def _():
        _new_rows_copy(new_ref, dst_ref, nsem).start()

    h = _modulated_norm(x_ref[0], ng_ref[...], mod_ref[0, :, 0:D_MODEL], mod_ref[0, :, D_MODEL:2 * D_MODEL])
    h_by_group = [h] + [jnp.dot(perm_ref[g - 1], h, preferred_element_type=F32).astype(BF16)
                        for g in range(1, N_GROUPS)]

    def heads(t, g, lhs):
        c = t * N_GROUPS + g
        res = jnp.dot(lhs, w_ref[:, c * GROUP_W:(c + 1) * GROUP_W], preferred_element_type=F32)
        out = []
        for hh in range(N_HEADS):
            r = res[:, hh * HEAD_DIM:(hh + 1) * HEAD_DIM]
            if t == 0:
                r = _head_norm(r, qg_ref[g:g + 1, :]) * Q_SCALE
            elif t == 1:
                r = _head_norm(r, kg_ref[g:g + 1, :])
            out.append(r)
        return out

    def write_tail(t, g, rs):
        window, d = ATTN_PATTERNS[g]
        per = tm // d
        for hh, r in enumerate(rs):
            row = (t - 1) * N_HEADS + hh
            if d == 1:
                tail_refs[g][0, :, row, :] = r if window >= tm else r[tm - window:, :]
            else:
                assert window >= tm
                for rr in range(d):
                    tail_refs[g][0, pl.ds(rr, per, stride=d), row, :] = r[rr * per:(rr + 1) * per, :]

    kept = {}
    for t in range(3):
        for g, (window, d) in enumerate(ATTN_PATTERNS):
            if (t, g) in SHIFT_SITES:
                shift_substep(SHIFT_SITES.index((t, g)) + 1)
            if (t, g) == SHIFT_SITES[0]:
                @pl.when(step == 0)
                def _():
                    _new_rows_copy(new_ref, dst_ref, nsem).wait()

            rs = heads(t, g, h_by_group[g])
            per = tm // d
            for hh, r in enumerate(rs):
                for rr in range(d):
                    qkv_refs[t][g][0, rr, hh] = r[rr * per:(rr + 1) * per, :].astype(BF16)
            if t > 0:
                kept.setdefault(max(window // tm, 1), []).append((t, g, rs))

    for n_tail, items in sorted(kept.items()):
        @pl.when(i >= n_tiles - n_tail)
        def _():
            for t, g, rs in items:
                write_tail(t, g, rs)


def _prompt_qkv(x, mod_p, norm_g, w_qkv, q_gain, k_gain, new_rows, state, shift):
    B, L, _ = x.shape
    tm = ROW_TILE
    n_tiles = L // tm
    const2 = lambda b, i: (0, 0)
    in_specs = [
        pl.BlockSpec((1, tm, D_MODEL), lambda b, i: (b, i, 0)),
        pl.BlockSpec((1, 1, 3 * D_MODEL), lambda b, i: (b, 0, 0)),
        pl.BlockSpec((1, D_MODEL), const2),
        pl.BlockSpec((D_MODEL, QKV_W), const2, pipeline_mode=pl.Buffered(1)),
        pl.BlockSpec((N_GROUPS, HEAD_DIM), const2),
        pl.BlockSpec((N_GROUPS, HEAD_DIM), const2),
        pl.BlockSpec((N_GROUPS - 1, tm, tm), lambda b, i: (0, 0, 0), pipeline_mode=pl.Buffered(1)),
        pl.BlockSpec(new_rows.shape, lambda b, i: (0, 0, 0), pipeline_mode=pl.Buffered(1)),
        pl.BlockSpec(memory_space=pl.ANY),
    ]
    perms = jnp.stack([_class_major_perm(tm, d) for _, d in ATTN_PATTERNS[1:]])
    out_specs, out_shape = [], []
    for t in range(3):
        for window, d in ATTN_PATTERNS:
            out_specs.append(pl.BlockSpec((1, d, N_HEADS, tm // d, HEAD_DIM), lambda b, i: (b, 0, 0, i, 0)))
            out_shape.append(jax.ShapeDtypeStruct((B, d, N_HEADS, L // d, HEAD_DIM), BF16))
    for window, d in ATTN_PATTERNS:
        w_eff = min(window, L)
        if w_eff >= tm:
            n_tail = w_eff // tm
            out_specs.append(pl.BlockSpec(
                (1, tm, 2 * N_HEADS, HEAD_DIM),
                lambda b, i, n_tail=n_tail: (b, jnp.maximum(i - (n_tiles - n_tail), 0), 0, 0)))
        else:
            out_specs.append(pl.BlockSpec((1, w_eff, 2 * N_HEADS, HEAD_DIM), lambda b, i: (b, 0, 0, 0)))
        out_shape.append(jax.ShapeDtypeStruct((B, w_eff, 2 * N_HEADS, HEAD_DIM), F32))
    out_specs.append(pl.BlockSpec(memory_space=pl.ANY))
    out_shape.append(jax.ShapeDtypeStruct(state.shape, F32))
    return pl.pallas_call(
        functools.partial(_qkv_body, n_tiles=n_tiles, shift=shift),
        grid=(B, n_tiles),
        in_specs=in_specs,
        out_specs=out_specs,
        out_shape=out_shape,
        scratch_shapes=_shift_scratch(QKV_SHIFT_LEAD),
        compiler_params=pltpu.CompilerParams(
            dimension_semantics=("arbitrary", "arbitrary"), vmem_limit_bytes=HOST_VMEM_LIMIT),
        name="prompt_qkv",
    )(x, mod_p, norm_g, w_qkv, q_gain, k_gain, perms, new_rows, state)


def _attn_body(q_ref, k_ref, v_ref, o_ref, l_ref, vx):
    _, d_blk, h_blk, M, _ = q_ref.shape
    n_q = M // Q_BLOCK
    for c in range(d_blk):
        for hh in range(h_blk):
            vx[c, hh, :, 0:HEAD_DIM] = v_ref[0, c, hh]
            vx[c, hh, :, HEAD_DIM:2 * HEAD_DIM] = jnp.ones((M, HEAD_DIM), BF16)
    ii = lax.broadcasted_iota(jnp.int32, (Q_BLOCK, 2 * Q_BLOCK), 0)
    jj = lax.broadcasted_iota(jnp.int32, (Q_BLOCK, 2 * Q_BLOCK), 1)
    dist = Q_BLOCK + ii - jj
    band_ok = (dist >= 0) & (dist <= N_BACK)
    causal_ok = (lax.broadcasted_iota(jnp.int32, (Q_BLOCK, Q_BLOCK), 1)
                 <= lax.broadcasted_iota(jnp.int32, (Q_BLOCK, Q_BLOCK), 0))
    nt = (((1,), (1,)), ((), ()))

    def block(c, hh, n, first):
        rows = pl.ds(pl.multiple_of(n * Q_BLOCK, Q_BLOCK), Q_BLOCK)
        if first:
            keys, ok = rows, causal_ok
        else:
            keys, ok = pl.ds(pl.multiple_of((n - 1) * Q_BLOCK, Q_BLOCK), 2 * Q_BLOCK), band_ok
        s = lax.dot_general(q_ref[0, c, hh, rows, :], k_ref[0, c, hh, keys, :], nt, preferred_element_type=F32)
        s = jnp.where(ok, s, NEG)
        m = jnp.max(s, axis=-1, keepdims=True)
        p = jnp.exp(s - m).astype(BF16)
        oe = jnp.dot(p, vx[c, hh, keys, :], preferred_element_type=F32)
        l = oe[:, HEAD_DIM:]
        o_ref[0, c, hh, rows, :] = (oe[:, :HEAD_DIM] / l).astype(BF16)
        return m + jnp.log(l)

    low_half = lax.broadcasted_iota(jnp.int32, (Q_BLOCK, HEAD_DIM), 1) < HEAD_DIM // 2

    def blocks(c, n, first):
        rows = pl.ds(pl.multiple_of(n * Q_BLOCK, Q_BLOCK), Q_BLOCK)
        for pair in range(h_blk // 2):
            lse = [block(c, 2 * pair + k, n, first) for k in range(2)]
            l_ref[0, c, pair, rows, :] = jnp.where(low_half, lse[0], lse[1])

    for c in range(d_blk):
        blocks(c, 0, True)
        if n_q <= 2:
            for n in range(1, n_q):
                blocks(c, n, False)
        else:
            def step(n, carry, c=c):
                blocks(c, n, False)
                return carry

            lax.fori_loop(1, n_q, step, 0, unroll=Q_UNROLL)


def _prompt_attn(q, k, v, d_blk, h_blk):
    B, d, H, M, E = q.shape
    blk = (1, d_blk, h_blk, M, E)
    spec = pl.BlockSpec(blk, lambda b, c, hh: (b, c, hh, 0, 0))
    return pl.pallas_call(
        _attn_body,
        grid=(B, d // d_blk, H // h_blk),
        in_specs=[spec, spec, spec],
        out_specs=[spec, pl.BlockSpec((1, d_blk, h_blk // 2, M, E), lambda b, c, hh: (b, c, hh, 0, 0))],
        out_shape=[jax.ShapeDtypeStruct(q.shape, BF16), jax.ShapeDtypeStruct((B, d, H // 2, M, E), F32)],
        scratch_shapes=[pltpu.VMEM((d_blk, h_blk, M, 2 * E), BF16)],
        compiler_params=pltpu.CompilerParams(
            dimension_semantics=("arbitrary", "arbitrary", "arbitrary"), vmem_limit_bytes=VMEM_LIMIT),
        name="prompt_attn_d%d" % d,
    )(q, k, v)


def _pool_project(mixed, wp_ref, bp_ref, ps_ref):
    cols = []
    for g in range(len(POOL_WINDOWS)):
        lanes = slice(g * HEAD_DIM, (g + 1) * HEAD_DIM)
        y = jnp.dot(mixed[g].astype(BF16), wp_ref[g], preferred_element_type=F32) + bp_ref[g:g + 1, :]
        cols.append(y * ps_ref[:, lanes])
    return jnp.concatenate(cols, axis=-1)


def _merge_out(x, gate, a, p, z_a, z_b, g_a, g_b, wa_ref, wb_ref, wo_ref):
    ya = jnp.dot((a * _silu(z_a)).astype(BF16), wa_ref[...], preferred_element_type=F32)
    yb = jnp.dot((p * _silu(z_b)).astype(BF16), wb_ref[...], preferred_element_type=F32)
    m = _sigmoid(g_a) * ya + _sigmoid(g_b) * yb
    return x + gate * jnp.dot(m.astype(BF16), wo_ref[...], preferred_element_type=F32)


def _out_body(x_ref, mod_ref, ng_ref, w_ref, o0, l0, o1, l1, o2, l2, wp_ref, bp_ref, ps_ref,
              wa_ref, wb_ref, wo_ref, new_ref, src_a, src_b, dst_a_in, y_ref, pool_ref, dst_a, dst_b,
              ext, oscr, lscr, buf, rsem, wsem, nsem, *, n_tiles, shift):
    del dst_a_in
    i = pl.program_id(1)
    tm = x_ref.shape[1]
    step = pl.program_id(0) * n_tiles + i
    kinds, lo, hi = shift

    last_slot = lo + SHIFT_SUBSTEPS * pl.num_programs(0) * n_tiles - 1

    def shift_substep(k):
        _shift_substep(lo + SHIFT_SUBSTEPS * step + k, lo, hi, last_slot, kinds, [src_a, src_b], [dst_a, dst_b],
                       buf, rsem, wsem)

    shift_substep(0)

    @pl.when(step == 0)
    def _():
        _new_rows_copy(new_ref, dst_b, nsem).start()
        ext[0:16, :] = jnp.zeros((16, GROUP_W), F32)

    x = x_ref[0]
    h = _modulated_norm(x, ng_ref[...], mod_ref[0, :, 0:D_MODEL], mod_ref[0, :, D_MODEL:2 * D_MODEL])
    gate = mod_ref[0, :, 2 * D_MODEL:3 * D_MODEL]

    def proj(lo, width):
        return jnp.dot(h, w_ref[:, lo:lo + width], preferred_element_type=F32)

    z_a = proj(0, GROUP_W)
    u = proj(GROUP_W, GROUP_W)
    z_b = proj(2 * GROUP_W, GROUP_W)
    o_refs, l_refs = (o0, o1, o2), (l0, l1, l2)
    half = HEAD_DIM // 2

    def head_lse(g, c, hh):
        packed = l_refs[g][0, c, hh // 2]
        low = lax.broadcasted_iota(jnp.int32, packed.shape, 1) < half
        return jnp.where(low == (hh % 2 == 0), packed, pltpu.roll(packed, half, axis=1))

    for g, (window, d) in enumerate(ATTN_PATTERNS):
        if d == 1:
            continue
        for hh in range(N_HEADS):
            for rr in range(d):
                rows = pl.ds(rr, tm // d, stride=d)
                oscr[g - 1, hh, rows, :] = o_refs[g][0, rr, hh].astype(F32)
                lscr[g - 1, hh, rows, :] = head_lse(g, rr, hh)
    a_cols = []
    for hh in range(N_HEADS):
        os_ = [o0[0, 0, hh].astype(F32), oscr[0, hh], oscr[1, hh]]
        ls_ = [head_lse(0, 0, hh), lscr[0, hh], lscr[1, hh]]
        mx = jnp.maximum(jnp.maximum(ls_[0], ls_[1]), ls_[2])
        es = [jnp.exp(l_ - mx) for l_ in ls_]
        den = es[0] + es[1] + es[2]
        a_cols.append((es[0] * os_[0] + es[1] * os_[1] + es[2] * os_[2]) / den)
    ya_in = (jnp.concatenate(a_cols, axis=-1) * _silu(z_a)).astype(BF16)
    shift_substep(1)

    @pl.when(step == 0)
    def _():
        _new_rows_copy(new_ref, dst_b, nsem).wait()

    g_a = proj(3 * GROUP_W, D_MODEL)
    ext[16:16 + tm, :] = u
    pos = i * tm + lax.broadcasted_iota(jnp.int32, (tm, 1), 0)
    mixed = []
    for g, w in enumerate(POOL_WINDOWS):
        lanes = pl.ds(g * HEAD_DIM, HEAD_DIM)
        acc = ext[pl.ds(16, tm), lanes]
        tok = acc
        for k in range(1, w):
            acc = acc + ext[pl.ds(16 - k, tm), lanes]
        cnt = jnp.minimum(w, pos + 1).astype(F32)
        mixed.append(acc / cnt - tok)
    pool_ref[0, 0] = ext[pl.ds(tm + 1, POOL_STATE), :]
    ext[0:16, :] = jnp.where(i == n_tiles - 1, 0.0, ext[tm:tm + 16, :])
    yb_in = (_pool_project(mixed, wp_ref, bp_ref, ps_ref) * _silu(z_b)).astype(BF16)
    shift_substep(2)

    g_b = proj(3 * GROUP_W + D_MODEL, D_MODEL)
    ya = jnp.dot(ya_in, wa_ref[...], preferred_element_type=F32)
    yb = jnp.dot(yb_in, wb_ref[...], preferred_element_type=F32)
    m = (_sigmoid(g_a) * ya + _sigmoid(g_b) * yb).astype(BF16)
    shift_substep(3)
    y_ref[0] = x + gate * jnp.dot(m, wo_ref[...], preferred_element_type=F32)


def _prompt_out(x, mod_p, norm_g, w_rest, ol, w_pool, b_pool, pool_scale, wa, wb, wo,
                new_rows, state_a, state_b, shifted_a, shift):
    B, L, _ = x.shape
    tm = ROW_TILE
    n_tiles = L // tm
    const2 = lambda b, i: (0, 0)
    one = pl.Buffered(1)
    in_specs = [
        pl.BlockSpec((1, tm, D_MODEL), lambda b, i: (b, i, 0)),
        pl.BlockSpec((1, 1, 3 * D_MODEL), lambda b, i: (b, 0, 0)),
        pl.BlockSpec((1, D_MODEL), const2),
        pl.BlockSpec((D_MODEL, REST_W), const2, pipeline_mode=one),
    ]
    for window, d in ATTN_PATTERNS:
        for heads in (N_HEADS, N_HEADS // 2):
            in_specs.append(pl.BlockSpec((1, d, heads, tm // d, HEAD_DIM), lambda b, i: (b, 0, 0, i, 0)))
    in_specs += [
        pl.BlockSpec((len(POOL_WINDOWS), HEAD_DIM, HEAD_DIM), lambda b, i: (0, 0, 0)),
        pl.BlockSpec((len(POOL_WINDOWS), HEAD_DIM), const2),
        pl.BlockSpec((1, GROUP_W), const2),
        pl.BlockSpec((GROUP_W, D_MODEL), const2, pipeline_mode=one),
        pl.BlockSpec((GROUP_W, D_MODEL), const2, pipeline_mode=one),
        pl.BlockSpec((D_MODEL, D_MODEL), const2, pipeline_mode=one),
        pl.BlockSpec(new_rows.shape, lambda b, i: (0, 0, 0), pipeline_mode=one),
    ]
    any_spec = pl.BlockSpec(memory_space=pl.ANY)
    in_specs += [any_spec] * 3
    return pl.pallas_call(
        functools.partial(_out_body, n_tiles=n_tiles, shift=shift),
        grid=(B, n_tiles),
        in_specs=in_specs,
        out_specs=[pl.BlockSpec((1, tm, D_MODEL), lambda b, i: (b, i, 0)),
                   pl.BlockSpec((1, 1, POOL_STATE, GROUP_W), lambda b, i: (0, b, 0, 0)),
                   any_spec, any_spec],
        out_shape=[jax.ShapeDtypeStruct((B, L, D_MODEL), F32),
                   jax.ShapeDtypeStruct((1, B, POOL_STATE, GROUP_W), F32),
                   jax.ShapeDtypeStruct(state_a.shape, F32),
                   jax.ShapeDtypeStruct(state_b.shape, F32)],
        scratch_shapes=[pltpu.VMEM((tm + 16, GROUP_W), F32),
                        pltpu.VMEM((2, N_HEADS, tm, HEAD_DIM), F32),
                        pltpu.VMEM((2, N_HEADS, tm, HEAD_DIM), F32)] + _shift_scratch(OUT_SHIFT_LEAD),
        input_output_aliases={len(in_specs) - 1: 2},
        compiler_params=pltpu.CompilerParams(
            dimension_semantics=("arbitrary", "arbitrary"), vmem_limit_bytes=HOST_VMEM_LIMIT),
        name="prompt_out",
    )(x, mod_p, norm_g, w_rest, *ol, w_pool, b_pool, pool_scale, wa, wb, wo,
      new_rows, state_a, state_b, shifted_a)


def _sample_proj_body(x_ref, mod_ref, ng_ref, w_ref, o_ref):
    h = _modulated_norm(x_ref[...], ng_ref[...], mod_ref[:, 0:D_MODEL], mod_ref[:, D_MODEL:2 * D_MODEL])
    o_ref[...] = jnp.dot(h, w_ref[...], preferred_element_type=F32)


def _sample_proj(xs, mod_s, norm_g, w):
    n, width = xs.shape[0], w.shape[1]
    tn = max(t for t in range(HEAD_DIM, width // 2 + 1, HEAD_DIM) if width % t == 0)
    return pl.pallas_call(
        _sample_proj_body,
        grid=(width // tn,),
        in_specs=[pl.BlockSpec((n, D_MODEL), lambda j: (0, 0)),
                  pl.BlockSpec((n, 3 * D_MODEL), lambda j: (0, 0)),
                  pl.BlockSpec((1, D_MODEL), lambda j: (0, 0)),
                  pl.BlockSpec((D_MODEL, tn), lambda j: (0, j))],
        out_specs=pl.BlockSpec((n, tn), lambda j: (0, j)),
        out_shape=jax.ShapeDtypeStruct((n, width), F32),
        name="sample_proj",
    )(xs, mod_s, norm_g, w)


def _sample_attn_body(p_ref, s0, s1, s2, qg_ref, kg_ref, a_ref, shifted0, n1, n2):
    bt = p_ref.shape[0]
    s_refs = (s0, s1, s2)
    last = N_BACK - 1
    for b in range(bt):
        os_, ls_ = [], []
        shifted0[b, 0:last] = s0[b, 1:N_BACK]
        for g in range(N_GROUPS):
            row = lambda t: pl.ds((t * N_GROUPS + g) * N_HEADS, N_HEADS)
            q4 = _head_norm(p_ref[b, row(0), :], qg_ref[g:g + 1, :]) * Q_SCALE
            k4 = _head_norm(p_ref[b, row(1), :], kg_ref[g:g + 1, :])
            v4 = p_ref[b, row(2), :]
            if g == 0:
                shifted0[b, last, 0:N_HEADS, :] = k4
                shifted0[b, last, N_HEADS:2 * N_HEADS, :] = v4
            else:
                (n1, n2)[g - 1][b, 0:N_HEADS, :] = k4
                (n1, n2)[g - 1][b, N_HEADS:2 * N_HEADS, :] = v4
            keys = s_refs[g][b, :, 0:N_HEADS, :]
            vals = s_refs[g][b, :, N_HEADS:2 * N_HEADS, :]
            s_old = jnp.sum(keys * q4[None], axis=-1, keepdims=True)
            s_new = jnp.sum(k4 * q4, axis=-1, keepdims=True)
            m = jnp.maximum(jnp.max(s_old, axis=0), s_new)
            p_old = jnp.exp(s_old - m[None])
            p_new = jnp.exp(s_new - m)
            l = jnp.sum(p_old, axis=0) + p_new
            o = jnp.sum(p_old * vals, axis=0) + p_new * v4
            os_.append(o / l)
            ls_.append(m + jnp.log(l))
        mx = jnp.maximum(jnp.maximum(ls_[0], ls_[1]), ls_[2])
        es = [jnp.exp(l_ - mx) for l_ in ls_]
        a_ref[b] = (es[0] * os_[0] + es[1] * os_[1] + es[2] * os_[2]) / (es[0] + es[1] + es[2])


def _sample_attn(p3, views, q_gain, k_gain):
    n = p3.shape[0]
    bt = 8
    in_specs = [pl.BlockSpec((bt, p3.shape[1], HEAD_DIM), lambda j: (j, 0, 0))]
    for v in views:
        in_specs.append(pl.BlockSpec((bt, N_BACK, 2 * N_HEADS, HEAD_DIM), lambda j: (j, 0, 0, 0)))
    in_specs += [pl.BlockSpec((N_GROUPS, HEAD_DIM), lambda j: (0, 0))] * 2
    new_spec = pl.BlockSpec((bt, 2 * N_HEADS, HEAD_DIM), lambda j: (j, 0, 0))
    new_shape = jax.ShapeDtypeStruct((n, 2 * N_HEADS, HEAD_DIM), F32)
    state_spec = pl.BlockSpec((bt, N_BACK, 2 * N_HEADS, HEAD_DIM), lambda j: (j, 0, 0, 0))
    return pl.pallas_call(
        _sample_attn_body,
        grid=(n // bt,),
        in_specs=in_specs,
        out_specs=[pl.BlockSpec((bt, N_HEADS, HEAD_DIM), lambda j: (j, 0, 0)), state_spec, new_spec, new_spec],
        out_shape=[jax.ShapeDtypeStruct((n, N_HEADS, HEAD_DIM), F32),
                   jax.ShapeDtypeStruct(views[0].shape, F32), new_shape, new_shape],
        compiler_params=pltpu.CompilerParams(vmem_limit_bytes=VMEM_LIMIT),
        name="sample_attn",
    )(p3, *views, q_gain, k_gain)


def _sample_out_body(x_ref, mod_ref, r_ref, a_ref, sp_ref, wp_ref, bp_ref, ps_ref, wa_ref, wb_ref, wo_ref,
                     y_ref, pool_ref):
    u = r_ref[:, GROUP_W:2 * GROUP_W]
    mixed = []
    for g, w in enumerate(POOL_WINDOWS):
        lanes = pl.ds(g * HEAD_DIM, HEAD_DIM)
        tok = r_ref[:, pl.ds(GROUP_W + g * HEAD_DIM, HEAD_DIM)]
        acc = tok
        for k in range(1, w):
            acc = acc + sp_ref[POOL_STATE - k, :, lanes]
        mixed.append(acc / float(min(w, PAST_LEN + 1)) - tok)
    p = _pool_project(mixed, wp_ref, bp_ref, ps_ref)
    for k in range(POOL_STATE - 1):
        pool_ref[k] = sp_ref[k + 1]
    pool_ref[POOL_STATE - 1] = u
    y_ref[...] = _merge_out(
        x_ref[...], mod_ref[:, 2 * D_MODEL:3 * D_MODEL], a_ref[...], p,
        r_ref[:, 0:GROUP_W], r_ref[:, 2 * GROUP_W:3 * GROUP_W],
        r_ref[:, 3 * GROUP_W:3 * GROUP_W + D_MODEL], r_ref[:, 3 * GROUP_W + D_MODEL:REST_W],
        wa_ref, wb_ref, wo_ref)


def _sample_out(xs, mod_s, rest, a_s, sp_t, w_pool, b_pool, pool_scale, wa, wb, wo):
    n = xs.shape[0]
    return pl.pallas_call(
        _sample_out_body,
        out_shape=[jax.ShapeDtypeStruct((n, D_MODEL), F32),
                   jax.ShapeDtypeStruct((POOL_STATE, n, GROUP_W), F32)],
        compiler_params=pltpu.CompilerParams(vmem_limit_bytes=VMEM_LIMIT),
        name="sample_out",
    )(xs, mod_s, rest, a_s, sp_t, w_pool, b_pool, pool_scale, wa, wb, wo)


def kernel(x_prompt, x_sample, state_kv_w128, state_kv_w512, state_kv_w2048, state_pool, c_prompt, c_sample,
           norm_g, w_ada, b_ada, w_in, q_gain, k_gain, w_pool, b_pool, pool_scale, w_a_out, w_b_out, w_out):
    B, L, _ = x_prompt.shape
    n_s = x_sample.shape[0]
    w_qkv = w_in[0, :, :QKV_W].astype(BF16)
    w_rest = w_in[0, :, QKV_W:].astype(BF16)
    wa, wb, wo, wp = (w[0].astype(BF16) for w in (w_a_out, w_b_out, w_out, w_pool))
    qg, kg, bp = q_gain[0], k_gain[0], b_pool[0]

    mod = _ada(jnp.concatenate([c_prompt, c_sample], axis=0), w_ada[0], b_ada)
    mod_p = mod[:B].reshape(B, 1, 3 * D_MODEL)
    mod_s = mod[B:]

    xs = x_sample[:, 0, :]
    qkv_s = _sample_proj(xs, mod_s, norm_g, w_qkv)
    rest_s = _sample_proj(xs, mod_s, norm_g, w_rest)
    states = (state_kv_w128, state_kv_w512, state_kv_w2048)
    views = [s.reshape(n_s, N_BACK, d * ROWS_PER_POS, HEAD_DIM) for s, (_, d) in zip(states, ATTN_PATTERNS)]
    a_s, shifted_128, new_512, new_2048 = _sample_attn(
        qkv_s.reshape(n_s, QKV_W // HEAD_DIM, HEAD_DIM), views, qg, kg)
    y_s, pool_t = _sample_out(xs, mod_s, rest_s, a_s.reshape(n_s, GROUP_W), state_pool[0].transpose(1, 0, 2),
                              wp, bp, pool_scale, wa, wb, wo)

    flat_2048, flat_512 = (s.reshape(n_s, s.shape[2] * ROWS_PER_POS, HEAD_DIM) for s in states[:0:-1])
    kinds = _shift_plan([flat_2048.shape, flat_512.shape])
    n_sub = B * (L // ROW_TILE) * SHIFT_SUBSTEPS
    n_chunks = kinds[-1][-1]
    split = n_chunks - n_sub
    assert 0 < split <= min(n_sub, kinds[0][4])
    *qkv, t0, t1, t2, part_2048 = _prompt_qkv(x_prompt, mod_p, norm_g, w_qkv, qg, kg, new_2048, flat_2048,
                                              (kinds, 0, split))
    ol = []
    for g, (d_blk, h_blk) in enumerate(((1, 2), (2, 4), (8, 4))):
        ol += _prompt_attn(qkv[g], qkv[N_GROUPS + g], qkv[2 * N_GROUPS + g], d_blk, h_blk)
    y_p, pool_p, shifted_2048, shifted_512 = _prompt_out(
        x_prompt, mod_p, norm_g, w_rest, ol, wp, bp, pool_scale, wa, wb, wo,
        new_512, flat_2048, flat_512, part_2048, (kinds, split, n_chunks))
    kv_s = [o.reshape(s.shape) for o, s in zip((shifted_128, shifted_512, shifted_2048), states)]
    kv_p = [t.reshape(1, B, t.shape[1], 2, N_HEADS, HEAD_DIM) for t in (t0, t1, t2)]

    return (y_p, y_s.reshape(n_s, 1, D_MODEL), kv_p[0], kv_p[1], kv_p[2], pool_p,
            kv_s[0], kv_s[1], kv_s[2], pool_t.transpose(1, 0, 2)[None])
```

```python
import functools

import jax
import jax.numpy as jnp
from jax import lax
from jax.experimental import pallas as pl
from jax.experimental.pallas import tpu as pltpu

F32 = jnp.float32
BF16 = jnp.bfloat16

D_MODEL = 1024
HEAD_DIM = 128
N_HEADS = 4
GROUP_W = N_HEADS * HEAD_DIM
ATTN_PATTERNS = ((128, 1), (512, 4), (2048, 16))
N_GROUPS = len(ATTN_PATTERNS)
N_BACK = 128
QKV_W = 3 * N_GROUPS * GROUP_W
REST_W = 3 * GROUP_W + 2 * D_MODEL
POOL_WINDOWS = (2, 4, 8, 16)
POOL_STATE = 15
PAST_LEN = 8192
EPS = 1e-6
Q_SCALE = HEAD_DIM ** -0.5
NEG = -1e30

ROW_TILE = 512
Q_BLOCK = 128
Q_UNROLL = 16
VMEM_LIMIT = 56 * 1024 * 1024
HOST_VMEM_LIMIT = 60 * 1024 * 1024


def _sigmoid(v):
    return 0.5 * jnp.tanh(0.5 * v) + 0.5


def _silu(v):
    return v * _sigmoid(v)


def _modulated_norm(x, norm_g, shift, scale):
    ms = jnp.mean(x * x, axis=-1, keepdims=True)
    return (x * lax.rsqrt(ms + EPS) * norm_g * (1.0 + scale) + shift).astype(BF16)


def _head_norm(r, gain):
    ms = jnp.mean(r * r, axis=-1, keepdims=True)
    return r * lax.rsqrt(ms + EPS) * gain


def _ada_body(c_ref, w_ref, b_ref, o_ref):
    s = _silu(c_ref[...]).astype(BF16)
    o_ref[...] = jnp.dot(s, w_ref[...].astype(BF16), preferred_element_type=F32) + b_ref[...]


def _ada(c_all, w_ada, b_ada):
    n = c_all.shape[0]
    return pl.pallas_call(
        _ada_body,
        grid=(3,),
        in_specs=[pl.BlockSpec((n, D_MODEL), lambda j: (0, 0)),
                  pl.BlockSpec((D_MODEL, D_MODEL), lambda j: (0, j)),
                  pl.BlockSpec((1, D_MODEL), lambda j: (0, j))],
        out_specs=pl.BlockSpec((n, D_MODEL), lambda j: (0, j)),
        out_shape=jax.ShapeDtypeStruct((n, 3 * D_MODEL), F32),
        name="ada",
    )(c_all, w_ada, b_ada)


ROWS_PER_POS = 2 * N_HEADS
SHIFT_NB = 8
SHIFT_ROWS = 712
SHIFT_SUBSTEPS = 4
QKV_SHIFT_LEAD = 3
OUT_SHIFT_LEAD = 3
SHIFT_READ_PRIORITY = 0
SHIFT_WRITE_PRIORITY = 1


def _shift_plan(state_shapes):
    kinds, c0 = [], 0
    for idx, (n_batch, rows_total, _) in enumerate(state_shapes):
        keep = rows_total - ROWS_PER_POS
        pieces = min(p for p in range(1, keep)
                     if keep % (p * ROWS_PER_POS) == 0 and keep // p <= SHIFT_ROWS)
        n = (n_batch // SHIFT_NB) * pieces
        kinds.append((idx, keep // pieces, pieces, c0, c0 + n))
        c0 += n
    return kinds


def _shift_substep(j, lo, hi, last_slot, kinds, srcs, dsts, buf, rsem, wsem):
    ahead = buf.shape[0] // 2

    def copy(jj, kind, write):
        idx, rows, pieces, c0, _ = kind
        local = jj - c0
        batches = pl.ds((local // pieces) * SHIFT_NB, SHIFT_NB)
        first = (local % pieces) * rows
        slot = jj % buf.shape[0]
        stage = buf.at[slot, :, pl.ds(0, rows), :]
        if write:
            return pltpu.make_async_copy(stage, dsts[idx].at[batches, pl.ds(first, rows), :], wsem.at[slot])
        return pltpu.make_async_copy(srcs[idx].at[batches, pl.ds(first + ROWS_PER_POS, rows), :], stage, rsem.at[slot])

    spans = [(kind, max(kind[3], lo), min(kind[4], hi)) for kind in kinds]
    spans = [s for s in spans if s[1] < s[2]]

    def for_chunk(jj, also, fn):
        for kind, first, end in spans:
            cond = (jj >= first) & (jj < end)

            @pl.when(cond if also is None else cond & also)
            def _():
                fn(functools.partial(copy, jj, kind))

    steady = []
    for kind, first, end in spans:
        inner = (j >= first + ahead) & (j < end - ahead)
        steady.append(inner)

        @pl.when(inner)
        def _():
            copy(j, kind, False).wait()
            copy(j, kind, True).start(priority=SHIFT_WRITE_PRIORITY)
            copy(j - ahead, kind, True).wait()
            copy(j + ahead, kind, False).start(priority=SHIFT_READ_PRIORITY)

    @pl.when(jnp.logical_not(functools.reduce(jnp.logical_or, steady)))
    def _():
        for a in range(ahead):
            for_chunk(j + a, j == lo, lambda cp: cp(False).start(priority=SHIFT_READ_PRIORITY))

        def landed(cp):
            cp(False).wait()
            cp(True).start(priority=SHIFT_WRITE_PRIORITY)

        for_chunk(j, None, landed)
        for_chunk(j - ahead, None, lambda cp: cp(True).wait())
        for_chunk(j + ahead, None, lambda cp: cp(False).start(priority=SHIFT_READ_PRIORITY))
        for a in range(ahead):
            for_chunk(j - a, j == last_slot, lambda cp: cp(True).wait())


def _new_rows_copy(new_ref, dst, nsem):
    keep = dst.shape[1] - ROWS_PER_POS
    return pltpu.make_async_copy(new_ref, dst.at[:, pl.ds(keep, ROWS_PER_POS), :], nsem)


def _shift_scratch(lead):
    return [pltpu.VMEM((2 * lead, SHIFT_NB, SHIFT_ROWS, HEAD_DIM), F32),
            pltpu.SemaphoreType.DMA((2 * lead,)), pltpu.SemaphoreType.DMA((2 * lead,)),
            pltpu.SemaphoreType.DMA(())]


def _class_major_perm(tm, d):
    i = jnp.arange(tm)
    src_row = (i % (tm // d)) * d + i // (tm // d)
    return (src_row[:, None] == jnp.arange(tm)[None, :]).astype(BF16)


SHIFT_SITES = ((0, 1), (1, 1), (2, 0))


def _qkv_body(x_ref, mod_ref, ng_ref, w_ref, qg_ref, kg_ref, perm_ref, new_ref, src_ref, *refs, n_tiles, shift):
    qkv_refs = (refs[0:3], refs[3:6], refs[6:9])
    tail_refs = refs[9:12]
    dst_ref, buf, rsem, wsem, nsem = refs[12:17]
    i = pl.program_id(1)
    tm = x_ref.shape[1]
    step = pl.program_id(0) * n_tiles + i
    kinds, lo, hi = shift

    last_slot = lo + SHIFT_SUBSTEPS * pl.num_programs(0) * n_tiles - 1

    def shift_substep(k):
        _shift_substep(lo + SHIFT_SUBSTEPS * step + k, lo, hi, last_slot, kinds, [src_ref], [dst_ref],
                       buf, rsem, wsem)

    shift_substep(0)

    @pl.when(step == 0)
    def _():
        _new_rows_copy(new_ref, dst_ref, nsem).start()

    h = _modulated_norm(x_ref[0], ng_ref[...], mod_ref[0, :, 0:D_MODEL], mod_ref[0, :, D_MODEL:2 * D_MODEL])
    h_by_group = [h] + [jnp.dot(perm_ref[g - 1], h, preferred_element_type=F32).astype(BF16)
                        for g in range(1, N_GROUPS)]

    def heads(t, g, lhs):
        c = t * N_GROUPS + g
        res = jnp.dot(lhs, w_ref[:, c * GROUP_W:(c + 1) * GROUP_W], preferred_element_type=F32)
        out = []
        for hh in range(N_HEADS):
            r = res[:, hh * HEAD_DIM:(hh + 1) * HEAD_DIM]
            if t == 0:
                r = _head_norm(r, qg_ref[g:g + 1, :]) * Q_SCALE
            elif t == 1:
                r = _head_norm(r, kg_ref[g:g + 1, :])
            out.append(r)
        return out

    def write_tail(t, g, rs):
        window, d = ATTN_PATTERNS[g]
        per = tm // d
        for hh, r in enumerate(rs):
            row = (t - 1) * N_HEADS + hh
            if d == 1:
                tail_refs[g][0, :, row, :] = r if window >= tm else r[tm - window:, :]
            else:
                assert window >= tm
                for rr in range(d):
                    tail_refs[g][0, pl.ds(rr, per, stride=d), row, :] = r[rr * per:(rr + 1) * per, :]

    kept = {}
    for t in range(3):
        for g, (window, d) in enumerate(ATTN_PATTERNS):
            if (t, g) in SHIFT_SITES:
                shift_substep(SHIFT_SITES.index((t, g)) + 1)
            if (t, g) == SHIFT_SITES[0]:
                @pl.when(step == 0)
                def _():
                    _new_rows_copy(new_ref, dst_ref, nsem).wait()

            rs = heads(t, g, h_by_group[g])
            per = tm // d
            for hh, r in enumerate(rs):
                for rr in range(d):
                    qkv_refs[t][g][0, rr, hh] = r[rr * per:(rr + 1) * per, :].astype(BF16)
            if t > 0:
                kept.setdefault(max(window // tm, 1), []).append((t, g, rs))

    for n_tail, items in sorted(kept.items()):
        @pl.when(i >= n_tiles - n_tail)
        def _():
            for t, g, rs in items:
                write_tail(t, g, rs)


def _prompt_qkv(x, mod_p, norm_g, w_qkv, q_gain, k_gain, new_rows, state, shift):
    B, L, _ = x.shape
    tm = ROW_TILE
    n_tiles = L // tm
    const2 = lambda b, i: (0, 0)
    in_specs = [
        pl.BlockSpec((1, tm, D_MODEL), lambda b, i: (b, i, 0)),
        pl.BlockSpec((1, 1, 3 * D_MODEL), lambda b, i: (b, 0, 0)),
        pl.BlockSpec((1, D_MODEL), const2),
        pl.BlockSpec((D_MODEL, QKV_W), const2, pipeline_mode=pl.Buffered(1)),
        pl.BlockSpec((N_GROUPS, HEAD_DIM), const2),
        pl.BlockSpec((N_GROUPS, HEAD_DIM), const2),
        pl.BlockSpec((N_GROUPS - 1, tm, tm), lambda b, i: (0, 0, 0), pipeline_mode=pl.Buffered(1)),
        pl.BlockSpec(new_rows.shape, lambda b, i: (0, 0, 0), pipeline_mode=pl.Buffered(1)),
        pl.BlockSpec(memory_space=pl.ANY),
    ]
    perms = jnp.stack([_class_major_perm(tm, d) for _, d in ATTN_PATTERNS[1:]])
    out_specs, out_shape = [], []
    for t in range(3):
        for window, d in ATTN_PATTERNS:
            out_specs.append(pl.BlockSpec((1, d, N_HEADS, tm // d, HEAD_DIM), lambda b, i: (b, 0, 0, i, 0)))
            out_shape.append(jax.ShapeDtypeStruct((B, d, N_HEADS, L // d, HEAD_DIM), BF16))
    for window, d in ATTN_PATTERNS:
        w_eff = min(window, L)
        if w_eff >= tm:
            n_tail = w_eff // tm
            out_specs.append(pl.BlockSpec(
                (1, tm, 2 * N_HEADS, HEAD_DIM),
                lambda b, i, n_tail=n_tail: (b, jnp.maximum(i - (n_tiles - n_tail), 0), 0, 0)))
        else:
            out_specs.append(pl.BlockSpec((1, w_eff, 2 * N_HEADS, HEAD_DIM), lambda b, i: (b, 0, 0, 0)))
        out_shape.append(jax.ShapeDtypeStruct((B, w_eff, 2 * N_HEADS, HEAD_DIM), F32))
    out_specs.append(pl.BlockSpec(memory_space=pl.ANY))
    out_shape.append(jax.ShapeDtypeStruct(state.shape, F32))
    return pl.pallas_call(
        functools.partial(_qkv_body, n_tiles=n_tiles, shift=shift),
        grid=(B, n_tiles),
        in_specs=in_specs,
        out_specs=out_specs,
        out_shape=out_shape,
        scratch_shapes=_shift_scratch(QKV_SHIFT_LEAD),
        compiler_params=pltpu.CompilerParams(
            dimension_semantics=("arbitrary", "arbitrary"), vmem_limit_bytes=HOST_VMEM_LIMIT),
        name="prompt_qkv",
    )(x, mod_p, norm_g, w_qkv, q_gain, k_gain, perms, new_rows, state)


def _attn_body(q_ref, k_ref, v_ref, o_ref, l_ref, vx):
    _, d_blk, h_blk, M, _ = q_ref.shape
    n_q = M // Q_BLOCK
    for c in range(d_blk):
        for hh in range(h_blk):
            vx[c, hh, :, 0:HEAD_DIM] = v_ref[0, c, hh]
            vx[c, hh, :, HEAD_DIM:2 * HEAD_DIM] = jnp.ones((M, HEAD_DIM), BF16)
    ii = lax.broadcasted_iota(jnp.int32, (Q_BLOCK, 2 * Q_BLOCK), 0)
    jj = lax.broadcasted_iota(jnp.int32, (Q_BLOCK, 2 * Q_BLOCK), 1)
    dist = Q_BLOCK + ii - jj
    band_ok = (dist >= 0) & (dist <= N_BACK)
    causal_ok = (lax.broadcasted_iota(jnp.int32, (Q_BLOCK, Q_BLOCK), 1)
                 <= lax.broadcasted_iota(jnp.int32, (Q_BLOCK, Q_BLOCK), 0))
    nt = (((1,), (1,)), ((), ()))

    def block(c, hh, n, first):
        rows = pl.ds(pl.multiple_of(n * Q_BLOCK, Q_BLOCK), Q_BLOCK)
        if first:
            keys, ok = rows, causal_ok
        else:
            keys, ok = pl.ds(pl.multiple_of((n - 1) * Q_BLOCK, Q_BLOCK), 2 * Q_BLOCK), band_ok
        s = lax.dot_general(q_ref[0, c, hh, rows, :], k_ref[0, c, hh, keys, :], nt, preferred_element_type=F32)
        s = jnp.where(ok, s, NEG)
        m = jnp.max(s, axis=-1, keepdims=True)
        p = jnp.exp(s - m).astype(BF16)
        oe = jnp.dot(p, vx[c, hh, keys, :], preferred_element_type=F32)
        l = oe[:, HEAD_DIM:]
        o_ref[0, c, hh, rows, :] = (oe[:, :HEAD_DIM] / l).astype(BF16)
        return m + jnp.log(l)

    low_half = lax.broadcasted_iota(jnp.int32, (Q_BLOCK, HEAD_DIM), 1) < HEAD_DIM // 2

    def blocks(c, n, first):
        rows = pl.ds(pl.multiple_of(n * Q_BLOCK, Q_BLOCK), Q_BLOCK)
        for pair in range(h_blk // 2):
            lse = [block(c, 2 * pair + k, n, first) for k in range(2)]
            l_ref[0, c, pair, rows, :] = jnp.where(low_half, lse[0], lse[1])

    for c in range(d_blk):
        blocks(c, 0, True)
        if n_q <= 2:
            for n in range(1, n_q):
                blocks(c, n, False)
        else:
            def step(n, carry, c=c):
                blocks(c, n, False)
                return carry

            lax.fori_loop(1, n_q, step, 0, unroll=Q_UNROLL)


def _prompt_attn(q, k, v, d_blk, h_blk):
    B, d, H, M, E = q.shape
    blk = (1, d_blk, h_blk, M, E)
    spec = pl.BlockSpec(blk, lambda b, c, hh: (b, c, hh, 0, 0))
    return pl.pallas_call(
        _attn_body,
        grid=(B, d // d_blk, H // h_blk),
        in_specs=[spec, spec, spec],
        out_specs=[spec, pl.BlockSpec((1, d_blk, h_blk // 2, M, E), lambda b, c, hh: (b, c, hh, 0, 0))],
        out_shape=[jax.ShapeDtypeStruct(q.shape, BF16), jax.ShapeDtypeStruct((B, d, H // 2, M, E), F32)],
        scratch_shapes=[pltpu.VMEM((d_blk, h_blk, M, 2 * E), BF16)],
        compiler_params=pltpu.CompilerParams(
            dimension_semantics=("arbitrary", "arbitrary", "arbitrary"), vmem_limit_bytes=VMEM_LIMIT),
        name="prompt_attn_d%d" % d,
    )(q, k, v)


def _pool_project(mixed, wp_ref, bp_ref, ps_ref):
    cols = []
    for g in range(len(POOL_WINDOWS)):
        lanes = slice(g * HEAD_DIM, (g + 1) * HEAD_DIM)
        y = jnp.dot(mixed[g].astype(BF16), wp_ref[g], preferred_element_type=F32) + bp_ref[g:g + 1, :]
        cols.append(y * ps_ref[:, lanes])
    return jnp.concatenate(cols, axis=-1)


def _merge_out(x, gate, a, p, z_a, z_b, g_a, g_b, wa_ref, wb_ref, wo_ref):
    ya = jnp.dot((a * _silu(z_a)).astype(BF16), wa_ref[...], preferred_element_type=F32)
    yb = jnp.dot((p * _silu(z_b)).astype(BF16), wb_ref[...], preferred_element_type=F32)
    m = _sigmoid(g_a) * ya + _sigmoid(g_b) * yb
    return x + gate * jnp.dot(m.astype(BF16), wo_ref[...], preferred_element_type=F32)


def _out_body(x_ref, mod_ref, ng_ref, w_ref, o0, l0, o1, l1, o2, l2, wp_ref, bp_ref, ps_ref,
              wa_ref, wb_ref, wo_ref, new_ref, src_a, src_b, dst_a_in, y_ref, pool_ref, dst_a, dst_b,
              ext, oscr, lscr, buf, rsem, wsem, nsem, *, n_tiles, shift):
    del dst_a_in
    i = pl.program_id(1)
    tm = x_ref.shape[1]
    step = pl.program_id(0) * n_tiles + i
    kinds, lo, hi = shift

    last_slot = lo + SHIFT_SUBSTEPS * pl.num_programs(0) * n_tiles - 1

    def shift_substep(k):
        _shift_substep(lo + SHIFT_SUBSTEPS * step + k, lo, hi, last_slot, kinds, [src_a, src_b], [dst_a, dst_b],
                       buf, rsem, wsem)

    shift_substep(0)

    @pl.when(step == 0)
    def _():
        _new_rows_copy(new_ref, dst_b, nsem).start()
        ext[0:16, :] = jnp.zeros((16, GROUP_W), F32)

    x = x_ref[0]
    h = _modulated_norm(x, ng_ref[...], mod_ref[0, :, 0:D_MODEL], mod_ref[0, :, D_MODEL:2 * D_MODEL])
    gate = mod_ref[0, :, 2 * D_MODEL:3 * D_MODEL]

    def proj(lo, width):
        return jnp.dot(h, w_ref[:, lo:lo + width], preferred_element_type=F32)

    z_a = proj(0, GROUP_W)
    u = proj(GROUP_W, GROUP_W)
    z_b = proj(2 * GROUP_W, GROUP_W)
    o_refs, l_refs = (o0, o1, o2), (l0, l1, l2)
    half = HEAD_DIM // 2

    def head_lse(g, c, hh):
        packed = l_refs[g][0, c, hh // 2]
        low = lax.broadcasted_iota(jnp.int32, packed.shape, 1) < half
        return jnp.where(low == (hh % 2 == 0), packed, pltpu.roll(packed, half, axis=1))

    for g, (window, d) in enumerate(ATTN_PATTERNS):
        if d == 1:
            continue
        for hh in range(N_HEADS):
            for rr in range(d):
                rows = pl.ds(rr, tm // d, stride=d)
                oscr[g - 1, hh, rows, :] = o_refs[g][0, rr, hh].astype(F32)
                lscr[g - 1, hh, rows, :] = head_lse(g, rr, hh)
    a_cols = []
    for hh in range(N_HEADS):
        os_ = [o0[0, 0, hh].astype(F32), oscr[0, hh], oscr[1, hh]]
        ls_ = [head_lse(0, 0, hh), lscr[0, hh], lscr[1, hh]]
        mx = jnp.maximum(jnp.maximum(ls_[0], ls_[1]), ls_[2])
        es = [jnp.exp(l_ - mx) for l_ in ls_]
        den = es[0] + es[1] + es[2]
        a_cols.append((es[0] * os_[0] + es[1] * os_[1] + es[2] * os_[2]) / den)
    ya_in = (jnp.concatenate(a_cols, axis=-1) * _silu(z_a)).astype(BF16)
    shift_substep(1)

    @pl.when(step == 0)
    def _():
        _new_rows_copy(new_ref, dst_b, nsem).wait()

    g_a = proj(3 * GROUP_W, D_MODEL)
    ext[16:16 + tm, :] = u
    pos = i * tm + lax.broadcasted_iota(jnp.int32, (tm, 1), 0)
    mixed = []
    for g, w in enumerate(POOL_WINDOWS):
        lanes = pl.ds(g * HEAD_DIM, HEAD_DIM)
        acc = ext[pl.ds(16, tm), lanes]
        tok = acc
        for k in range(1, w):
            acc = acc + ext[pl.ds(16 - k, tm), lanes]
        cnt = jnp.minimum(w, pos + 1).astype(F32)
        mixed.append(acc / cnt - tok)
    pool_ref[0, 0] = ext[pl.ds(tm + 1, POOL_STATE), :]
    ext[0:16, :] = jnp.where(i == n_tiles - 1, 0.0, ext[tm:tm + 16, :])
    yb_in = (_pool_project(mixed, wp_ref, bp_ref, ps_ref) * _silu(z_b)).astype(BF16)
    shift_substep(2)

    g_b = proj(3 * GROUP_W + D_MODEL, D_MODEL)
    ya = jnp.dot(ya_in, wa_ref[...], preferred_element_type=F32)
    yb = jnp.dot(yb_in, wb_ref[...], preferred_element_type=F32)
    m = (_sigmoid(g_a) * ya + _sigmoid(g_b) * yb).astype(BF16)
    shift_substep(3)
    y_ref[0] = x + gate * jnp.dot(m, wo_ref[...], preferred_element_type=F32)


def _prompt_out(x, mod_p, norm_g, w_rest, ol, w_pool, b_pool, pool_scale, wa, wb, wo,
                new_rows, state_a, state_b, shifted_a, shift):
    B, L, _ = x.shape
    tm = ROW_TILE
    n_tiles = L // tm
    const2 = lambda b, i: (0, 0)
    one = pl.Buffered(1)
    in_specs = [
        pl.BlockSpec((1, tm, D_MODEL), lambda b, i: (b, i, 0)),
        pl.BlockSpec((1, 1, 3 * D_MODEL), lambda b, i: (b, 0, 0)),
        pl.BlockSpec((1, D_MODEL), const2),
        pl.BlockSpec((D_MODEL, REST_W), const2, pipeline_mode=one),
    ]
    for window, d in ATTN_PATTERNS:
        for heads in (N_HEADS, N_HEADS // 2):
            in_specs.append(pl.BlockSpec((1, d, heads, tm // d, HEAD_DIM), lambda b, i: (b, 0, 0, i, 0)))
    in_specs += [
        pl.BlockSpec((len(POOL_WINDOWS), HEAD_DIM, HEAD_DIM), lambda b, i: (0, 0, 0)),
        pl.BlockSpec((len(POOL_WINDOWS), HEAD_DIM), const2),
        pl.BlockSpec((1, GROUP_W), const2),
        pl.BlockSpec((GROUP_W, D_MODEL), const2, pipeline_mode=one),
        pl.BlockSpec((GROUP_W, D_MODEL), const2, pipeline_mode=one),
        pl.BlockSpec((D_MODEL, D_MODEL), const2, pipeline_mode=one),
        pl.BlockSpec(new_rows.shape, lambda b, i: (0, 0, 0), pipeline_mode=one),
    ]
    any_spec = pl.BlockSpec(memory_space=pl.ANY)
    in_specs += [any_spec] * 3
    return pl.pallas_call(
        functools.partial(_out_body, n_tiles=n_tiles, shift=shift),
        grid=(B, n_tiles),
        in_specs=in_specs,
        out_specs=[pl.BlockSpec((1, tm, D_MODEL), lambda b, i: (b, i, 0)),
                   pl.BlockSpec((1, 1, POOL_STATE, GROUP_W), lambda b, i: (0, b, 0, 0)),
                   any_spec, any_spec],
        out_shape=[jax.ShapeDtypeStruct((B, L, D_MODEL), F32),
                   jax.ShapeDtypeStruct((1, B, POOL_STATE, GROUP_W), F32),
                   jax.ShapeDtypeStruct(state_a.shape, F32),
                   jax.ShapeDtypeStruct(state_b.shape, F32)],
        scratch_shapes=[pltpu.VMEM((tm + 16, GROUP_W), F32),
                        pltpu.VMEM((2, N_HEADS, tm, HEAD_DIM), F32),
                        pltpu.VMEM((2, N_HEADS, tm, HEAD_DIM), F32)] + _shift_scratch(OUT_SHIFT_LEAD),
        input_output_aliases={len(in_specs) - 1: 2},
        compiler_params=pltpu.CompilerParams(
            dimension_semantics=("arbitrary", "arbitrary"), vmem_limit_bytes=HOST_VMEM_LIMIT),
        name="prompt_out",
    )(x, mod_p, norm_g, w_rest, *ol, w_pool, b_pool, pool_scale, wa, wb, wo,
      new_rows, state_a, state_b, shifted_a)


def _sample_proj_body(x_ref, mod_ref, ng_ref, w_ref, o_ref):
    h = _modulated_norm(x_ref[...], ng_ref[...], mod_ref[:, 0:D_MODEL], mod_ref[:, D_MODEL:2 * D_MODEL])
    o_ref[...] = jnp.dot(h, w_ref[...], preferred_element_type=F32)


def _sample_proj(xs, mod_s, norm_g, w):
    n, width = xs.shape[0], w.shape[1]
    tn = max(t for t in range(HEAD_DIM, width // 2 + 1, HEAD_DIM) if width % t == 0)
    return pl.pallas_call(
        _sample_proj_body,
        grid=(width // tn,),
        in_specs=[pl.BlockSpec((n, D_MODEL), lambda j: (0, 0)),
                  pl.BlockSpec((n, 3 * D_MODEL), lambda j: (0, 0)),
                  pl.BlockSpec((1, D_MODEL), lambda j: (0, 0)),
                  pl.BlockSpec((D_MODEL, tn), lambda j: (0, j))],
        out_specs=pl.BlockSpec((n, tn), lambda j: (0, j)),
        out_shape=jax.ShapeDtypeStruct((n, width), F32),
        name="sample_proj",
    )(xs, mod_s, norm_g, w)


def _sample_attn_body(p_ref, s0, s1, s2, qg_ref, kg_ref, a_ref, shifted0, n1, n2):
    bt = p_ref.shape[0]
    s_refs = (s0, s1, s2)
    last = N_BACK - 1
    for b in range(bt):
        os_, ls_ = [], []
        shifted0[b, 0:last] = s0[b, 1:N_BACK]
        for g in range(N_GROUPS):
            row = lambda t: pl.ds((t * N_GROUPS + g) * N_HEADS, N_HEADS)
            q4 = _head_norm(p_ref[b, row(0), :], qg_ref[g:g + 1, :]) * Q_SCALE
            k4 = _head_norm(p_ref[b, row(1), :], kg_ref[g:g + 1, :])
            v4 = p_ref[b, row(2), :]
            if g == 0:
                shifted0[b, last, 0:N_HEADS, :] = k4
                shifted0[b, last, N_HEADS:2 * N_HEADS, :] = v4
            else:
                (n1, n2)[g - 1][b, 0:N_HEADS, :] = k4
                (n1, n2)[g - 1][b, N_HEADS:2 * N_HEADS, :] = v4
            keys = s_refs[g][b, :, 0:N_HEADS, :]
            vals = s_refs[g][b, :, N_HEADS:2 * N_HEADS, :]
            s_old = jnp.sum(keys * q4[None], axis=-1, keepdims=True)
            s_new = jnp.sum(k4 * q4, axis=-1, keepdims=True)
            m = jnp.maximum(jnp.max(s_old, axis=0), s_new)
            p_old = jnp.exp(s_old - m[None])
            p_new = jnp.exp(s_new - m)
            l = jnp.sum(p_old, axis=0) + p_new
            o = jnp.sum(p_old * vals, axis=0) + p_new * v4
            os_.append(o / l)
            ls_.append(m + jnp.log(l))
        mx = jnp.maximum(jnp.maximum(ls_[0], ls_[1]), ls_[2])
        es = [jnp.exp(l_ - mx) for l_ in ls_]
        a_ref[b] = (es[0] * os_[0] + es[1] * os_[1] + es[2] * os_[2]) / (es[0] + es[1] + es[2])


def _sample_attn(p3, views, q_gain, k_gain):
    n = p3.shape[0]
    bt = 8
    in_specs = [pl.BlockSpec((bt, p3.shape[1], HEAD_DIM), lambda j: (j, 0, 0))]
    for v in views:
        in_specs.append(pl.BlockSpec((bt, N_BACK, 2 * N_HEADS, HEAD_DIM), lambda j: (j, 0, 0, 0)))
    in_specs += [pl.BlockSpec((N_GROUPS, HEAD_DIM), lambda j: (0, 0))] * 2
    new_spec = pl.BlockSpec((bt, 2 * N_HEADS, HEAD_DIM), lambda j: (j, 0, 0))
    new_shape = jax.ShapeDtypeStruct((n, 2 * N_HEADS, HEAD_DIM), F32)
    state_spec = pl.BlockSpec((bt, N_BACK, 2 * N_HEADS, HEAD_DIM), lambda j: (j, 0, 0, 0))
    return pl.pallas_call(
        _sample_attn_body,
        grid=(n // bt,),
        in_specs=in_specs,
        out_specs=[pl.BlockSpec((bt, N_HEADS, HEAD_DIM), lambda j: (j, 0, 0)), state_spec, new_spec, new_spec],
        out_shape=[jax.ShapeDtypeStruct((n, N_HEADS, HEAD_DIM), F32),
                   jax.ShapeDtypeStruct(views[0].shape, F32), new_shape, new_shape],
        compiler_params=pltpu.CompilerParams(vmem_limit_bytes=VMEM_LIMIT),
        name="sample_attn",
    )(p3, *views, q_gain, k_gain)


def _sample_out_body(x_ref, mod_ref, r_ref, a_ref, sp_ref, wp_ref, bp_ref, ps_ref, wa_ref, wb_ref, wo_ref,
                     y_ref, pool_ref):
    u = r_ref[:, GROUP_W:2 * GROUP_W]
    mixed = []
    for g, w in enumerate(POOL_WINDOWS):
        lanes = pl.ds(g * HEAD_DIM, HEAD_DIM)
        tok = r_ref[:, pl.ds(GROUP_W + g * HEAD_DIM, HEAD_DIM)]
        acc = tok
        for k in range(1, w):
            acc = acc + sp_ref[POOL_STATE - k, :, lanes]
        mixed.append(acc / float(min(w, PAST_LEN + 1)) - tok)
    p = _pool_project(mixed, wp_ref, bp_ref, ps_ref)
    for k in range(POOL_STATE - 1):
        pool_ref[k] = sp_ref[k + 1]
    pool_ref[POOL_STATE - 1] = u
    y_ref[...] = _merge_out(
        x_ref[...], mod_ref[:, 2 * D_MODEL:3 * D_MODEL], a_ref[...], p,
        r_ref[:, 0:GROUP_W], r_ref[:, 2 * GROUP_W:3 * GROUP_W],
        r_ref[:, 3 * GROUP_W:3 * GROUP_W + D_MODEL], r_ref[:, 3 * GROUP_W + D_MODEL:REST_W],
        wa_ref, wb_ref, wo_ref)


def _sample_out(xs, mod_s, rest, a_s, sp_t, w_pool, b_pool, pool_scale, wa, wb, wo):
    n = xs.shape[0]
    return pl.pallas_call(
        _sample_out_body,
        out_shape=[jax.ShapeDtypeStruct((n, D_MODEL), F32),
                   jax.ShapeDtypeStruct((POOL_STATE, n, GROUP_W), F32)],
        compiler_params=pltpu.CompilerParams(vmem_limit_bytes=VMEM_LIMIT),
        name="sample_out",
    )(xs, mod_s, rest, a_s, sp_t, w_pool, b_pool, pool_scale, wa, wb, wo)


def kernel(x_prompt, x_sample, state_kv_w128, state_kv_w512, state_kv_w2048, state_pool, c_prompt, c_sample,
           norm_g, w_ada, b_ada, w_in, q_gain, k_gain, w_pool, b_pool, pool_scale, w_a_out, w_b_out, w_out):
    B, L, _ = x_prompt.shape
    n_s = x_sample.shape[0]
    w_qkv = w_in[0, :, :QKV_W].astype(BF16)
    w_rest = w_in[0, :, QKV_W:].astype(BF16)
    wa, wb, wo, wp = (w[0].astype(BF16) for w in (w_a_out, w_b_out, w_out, w_pool))
    qg, kg, bp = q_gain[0], k_gain[0], b_pool[0]

    mod = _ada(jnp.concatenate([c_prompt, c_sample], axis=0), w_ada[0], b_ada)
    mod_p = mod[:B].reshape(B, 1, 3 * D_MODEL)
    mod_s = mod[B:]

    xs = x_sample[:, 0, :]
    qkv_s = _sample_proj(xs, mod_s, norm_g, w_qkv)
    rest_s = _sample_proj(xs, mod_s, norm_g, w_rest)
    states = (state_kv_w128, state_kv_w512, state_kv_w2048)
    views = [s.reshape(n_s, N_BACK, d * ROWS_PER_POS, HEAD_DIM) for s, (_, d) in zip(states, ATTN_PATTERNS)]
    a_s, shifted_128, new_512, new_2048 = _sample_attn(
        qkv_s.reshape(n_s, QKV_W // HEAD_DIM, HEAD_DIM), views, qg, kg)
    y_s, pool_t = _sample_out(xs, mod_s, rest_s, a_s.reshape(n_s, GROUP_W), state_pool[0].transpose(1, 0, 2),
                              wp, bp, pool_scale, wa, wb, wo)

    flat_2048, flat_512 = (s.reshape(n_s, s.shape[2] * ROWS_PER_POS, HEAD_DIM) for s in states[:0:-1])
    kinds = _shift_plan([flat_2048.shape, flat_512.shape])
    n_sub = B * (L // ROW_TILE) * SHIFT_SUBSTEPS
    n_chunks = kinds[-1][-1]
    split = n_chunks - n_sub
    assert 0 < split <= min(n_sub, kinds[0][4])
    *qkv, t0, t1, t2, part_2048 = _prompt_qkv(x_prompt, mod_p, norm_g, w_qkv, qg, kg, new_2048, flat_2048,
                                              (kinds, 0, split))
    ol = []
    for g, (d_blk, h_blk) in enumerate(((1, 2), (2, 4), (8, 4))):
        ol += _prompt_attn(qkv[g], qkv[N_GROUPS + g], qkv[2 * N_GROUPS + g], d_blk, h_blk)
    y_p, pool_p, shifted_2048, shifted_512 = _prompt_out(
        x_prompt, mod_p, norm_g, w_rest, ol, wp, bp, pool_scale, wa, wb, wo,
        new_512, flat_2048, flat_512, part_2048, (kinds, split, n_chunks))
    kv_s = [o.reshape(s.shape) for o, s in zip((shifted_128, shifted_512, shifted_2048), states)]
    kv_p = [t.reshape(1, B, t.shape[1], 2, N_HEADS, HEAD_DIM) for t in (t0, t1, t2)]

    return (y_p, y_s.reshape(n_s, 1, D_MODEL), kv_p[0], kv_p[1], kv_p[2], pool_p,
            kv_s[0], kv_s[1], kv_s[2], pool_t.transpose(1, 0, 2)[None])
```
